```python
import jax, jax.numpy as jnp
from jax import lax
import numpy as np

D_MODEL = 2048
BATCH = 1
SEQ = 8192
DEPTH = 2

N_META = 16
Q_BLOCK = 128
MLA_HEADS = 8
MLA_Q_LORA = 512
MLA_KV_LORA = 512
MLA_NOPE_DIM = 128
MLA_ROPE_DIM = 64
MLA_V_DIM = 128
ROPE_THETA = 10000.0
FOX_HEADS = 8
FOX_HEAD_DIM = 128
_ATTN_SIZES = (MLA_Q_LORA, MLA_KV_LORA, MLA_ROPE_DIM,
               FOX_HEADS * FOX_HEAD_DIM, FOX_HEADS * FOX_HEAD_DIM, FOX_HEADS * FOX_HEAD_DIM,
               FOX_HEADS)
ATTN_IN = int(sum(_ATTN_SIZES))
ATTN_SPLITS = tuple(int(v) for v in np.cumsum(_ATTN_SIZES)[:-1])
ATTN_OUT = MLA_HEADS * MLA_V_DIM + FOX_HEADS * FOX_HEAD_DIM
POOL_WINDOWS = (2, 4, 8, 16)
POOL_WIDTH = D_MODEL
POOL_GROUP = POOL_WIDTH // len(POOL_WINDOWS)
FFN_DENSE = 5632
N_EXPERTS = 8
TOP_K = 2
FFN_EXPERT = 7168
MOE_BLOCK = 128
ALPHA = (2 * DEPTH) ** 0.25
BETA = (8 * DEPTH) ** -0.25
LN_EPS = 1e-5
RMS_EPS = 1e-6
NEG_INF = -1e30
N_EVEN = (DEPTH + 1) // 2
N_ODD = DEPTH // 2

kernel_name = "hybrid_mla_fox_pool_moe_deepnorm"


def layer_norm(x, g, b):
    xf = x.astype(jnp.float32)
    mu = jnp.mean(xf, axis=-1, keepdims=True)
    var = jnp.mean(jnp.square(xf - mu), axis=-1, keepdims=True)
    return ((xf - mu) * lax.rsqrt(var + LN_EPS) * g + b).astype(x.dtype)


def rms_norm(x, g):
    xf = x.astype(jnp.float32)
    return (xf * lax.rsqrt(jnp.mean(jnp.square(xf), axis=-1, keepdims=True) + RMS_EPS) * g).astype(x.dtype)


def rope(x, cos, sin):
    x1, x2 = jnp.split(x, 2, axis=-1)
    return jnp.concatenate([x1 * cos - x2 * sin, x2 * cos + x1 * sin], axis=-1)


def causal_block_attention(q, k, v, scale, log_decay_cum=None):
    B, H, L, dk = q.shape
    dv = v.shape[-1]
    lead = (-N_META) % Q_BLOCK
    lq = lead + L
    nb = -(-lq // Q_BLOCK)
    tail = nb * Q_BLOCK - lq
    qp = jnp.pad(q, ((0, 0), (0, 0), (lead, tail), (0, 0)))
    q_blocks = qp.reshape(B, H, nb, Q_BLOCK, dk).transpose(2, 0, 1, 3, 4)
    q_pos = (jnp.arange(nb * Q_BLOCK) - lead).reshape(nb, Q_BLOCK)
    k_pos = jnp.arange(L)
    xs = (q_blocks, q_pos)
    if log_decay_cum is not None:
        cp = jnp.pad(log_decay_cum, ((0, 0), (0, 0), (lead, tail)))
        xs = xs + (cp.reshape(B, H, nb, Q_BLOCK).transpose(2, 0, 1, 3),)
        k_is_meta = (k_pos < N_META)[None, None, None, :]

    def one_block(blk):
        qb, pos = blk[0], blk[1]
        s = jnp.einsum('bhtd,bhsd->bhts', qb, k).astype(jnp.float32) * scale
        if log_decay_cum is not None:
            decay = blk[2][..., :, None] - log_decay_cum[:, :, None, :]
            s = s + jnp.where(k_is_meta, 0.0, decay)
        mask = k_pos[None, :] <= pos[:, None]
        s = jnp.where(mask, s, NEG_INF)
        p = jax.nn.softmax(s, axis=-1).astype(v.dtype)
        return jnp.einsum('bhts,bhsd->bhtd', p, v)

    o = lax.map(one_block, xs)
    o = o.transpose(1, 2, 0, 3, 4).reshape(B, H, nb * Q_BLOCK, dv)
    return o[:, :, lead:lead + L]


def mla_fox_mixer(x, w_in, fox_b_f, q_norm, kv_norm, w_uq, w_ukv, w_out, cos, sin):
    B, L, _ = x.shape
    h = x @ w_in
    c_q, c_kv, k_r, q_f, k_f, v_f, f_logit = jnp.split(h, ATTN_SPLITS, axis=-1)
    q = (rms_norm(c_q, q_norm) @ w_uq).reshape(B, L, MLA_HEADS, MLA_NOPE_DIM + MLA_ROPE_DIM)
    q_nope, q_rot = q[..., :MLA_NOPE_DIM], q[..., MLA_NOPE_DIM:]
    q_rot = rope(q_rot, cos[:, None, :], sin[:, None, :])
    kv = (rms_norm(c_kv, kv_norm) @ w_ukv).reshape(B, L, MLA_HEADS, MLA_NOPE_DIM + MLA_V_DIM)
    k_nope, v_m = kv[..., :MLA_NOPE_DIM], kv[..., MLA_NOPE_DIM:]
    k_rot = rope(k_r, cos, sin)
    k_rot = jnp.broadcast_to(k_rot[:, :, None, :], (B, L, MLA_HEADS, MLA_ROPE_DIM))
    q_m = jnp.concatenate([q_nope, q_rot], axis=-1).transpose(0, 2, 1, 3)
    k_m = jnp.concatenate([k_nope, k_rot], axis=-1).transpose(0, 2, 1, 3)
    o_mla = causal_block_attention(q_m, k_m, v_m.transpose(0, 2, 1, 3),
                                   (MLA_NOPE_DIM + MLA_ROPE_DIM) ** -0.5)
    to_heads = lambda t: t.reshape(B, L, FOX_HEADS, FOX_HEAD_DIM).transpose(0, 2, 1, 3)
    log_f = jax.nn.log_sigmoid(f_logit.astype(jnp.float32) + fox_b_f)
    cum = jnp.cumsum(log_f, axis=1).transpose(0, 2, 1)
    o_fox = causal_block_attention(to_heads(q_f), to_heads(k_f), to_heads(v_f),
                                   FOX_HEAD_DIM ** -0.5, cum)
    o = jnp.concatenate([o_mla, o_fox], axis=1).transpose(0, 2, 1, 3).reshape(B, L, ATTN_OUT)
    return o @ w_out


def pool_mixer(x, w_in, w_group, scale, w_out):
    B, L, _ = x.shape
    h = (x @ w_in).reshape(B, L, len(POOL_WINDOWS), POOL_GROUP)
    count = jnp.arange(1, L + 1, dtype=jnp.float32)
    outs = []
    for g, w in enumerate(POOL_WINDOWS):
        hg = h[:, :, g].astype(jnp.float32)
        cs = jnp.cumsum(hg, axis=1)
        lower = jnp.pad(cs[:, :L - w], ((0, 0), (w, 0), (0, 0)))
        mean = (cs - lower) / jnp.minimum(count, float(w))[None, :, None]
        outs.append((mean - hg).astype(x.dtype) @ w_group[g])
    y = jnp.concatenate(outs, axis=-1) * scale
    return y @ w_out


def swiglu(x, w_gate, w_up, w_down):
    return (jax.nn.silu(x @ w_gate) * (x @ w_up)) @ w_down


def moe_swiglu(x, w_router, b_router, w_gate, w_up, w_down):
    B, L, D = x.shape
    n = B * L
    xt = x.reshape(n, D)
    logits = (xt @ w_router).astype(jnp.float32) + b_router
    top_val, top_idx = lax.top_k(logits, TOP_K)
    gates = jax.nn.softmax(top_val, axis=-1)
    a = n * TOP_K
    expert_of = top_idx.reshape(a)
    token_of = jnp.arange(a, dtype=jnp.int32) // TOP_K
    gate_of = gates.reshape(a)
    order = jnp.argsort(expert_of)
    e_sorted = expert_of[order]
    counts = jnp.bincount(expert_of, length=N_EXPERTS)
    padded = (counts + MOE_BLOCK - 1) // MOE_BLOCK * MOE_BLOCK
    seg_end = jnp.cumsum(padded)
    seg_start = seg_end - padded
    first = jnp.cumsum(counts) - counts
    dest = seg_start[e_sorted] + jnp.arange(a) - first[e_sorted]
    n_blocks = -(-a // MOE_BLOCK) + N_EXPERTS
    cap = n_blocks * MOE_BLOCK
    row_token = jnp.full((cap,), n, jnp.int32).at[dest].set(token_of[order])
    row_gate = jnp.zeros((cap,), jnp.float32).at[dest].set(gate_of[order])
    block_expert = jnp.minimum(
        jnp.searchsorted(seg_end, jnp.arange(n_blocks) * MOE_BLOCK, side='right'), N_EXPERTS - 1)
    x_rows = jnp.concatenate([xt, jnp.zeros((1, D), xt.dtype)], axis=0)[row_token]
    x_rows = x_rows.reshape(n_blocks, MOE_BLOCK, D)

    def expert_block(args):
        xb, e = args
        return (jax.nn.silu(xb @ w_gate[e]) * (xb @ w_up[e])) @ w_down[e]

    y_rows = lax.map(expert_block, (x_rows, block_expert)).reshape(cap, D)
    y = jax.ops.segment_sum(y_rows.astype(jnp.float32) * row_gate[:, None], row_token,
                            num_segments=n + 1)[:n]
    return y.astype(x.dtype).reshape(B, L, D)


def setup_inputs(seed: int = 0) -> dict:
    key = jax.random.key(seed)
    ks = iter(jax.random.split(key, 32))

    def nrm(shape, scale):
        return jax.random.normal(next(ks), shape, jnp.float32) * scale

    D = D_MODEL
    return {
        "x": nrm((BATCH, SEQ, D), 1.0),
        "meta_tokens": nrm((N_META, D), 1.0),
        "attn_w_in": nrm((N_EVEN, D, ATTN_IN), D ** -0.5),
        "fox_forget_bias": 3.0 + nrm((N_EVEN, FOX_HEADS), 0.5),
        "mla_q_norm": 1.0 + nrm((N_EVEN, MLA_Q_LORA), 0.1),
        "mla_kv_norm": 1.0 + nrm((N_EVEN, MLA_KV_LORA), 0.1),
        "mla_w_uq": nrm((N_EVEN, MLA_Q_LORA, MLA_HEADS * (MLA_NOPE_DIM + MLA_ROPE_DIM)), MLA_Q_LORA ** -0.5),
        "mla_w_ukv": nrm((N_EVEN, MLA_KV_LORA, MLA_HEADS * (MLA_NOPE_DIM + MLA_V_DIM)), MLA_KV_LORA ** -0.5),
        "attn_w_out": nrm((N_EVEN, ATTN_OUT, D), BETA * ATTN_OUT ** -0.5),
        "pool_w_in": nrm((N_ODD, D, POOL_WIDTH), D ** -0.5),
        "pool_w_group": nrm((N_ODD, len(POOL_WINDOWS), POOL_GROUP, POOL_GROUP), POOL_GROUP ** -0.5),
        "pool_scale": 1.0 + nrm((N_ODD, POOL_WIDTH), 0.1),
        "pool_w_out": nrm((N_ODD, POOL_WIDTH, D), BETA * POOL_WIDTH ** -0.5),
        "ffn_w_gate": nrm((N_EVEN, D, FFN_DENSE), D ** -0.5),
        "ffn_w_up": nrm((N_EVEN, D, FFN_DENSE), D ** -0.5),
        "ffn_w_down": nrm((N_EVEN, FFN_DENSE, D), BETA * FFN_DENSE ** -0.5),
        "moe_w_router": nrm((N_ODD, D, N_EXPERTS), D ** -0.5),
        "moe_b_router": nrm((N_ODD, N_EXPERTS), 0.01),
        "moe_w_gate": nrm((N_ODD, N_EXPERTS, D, FFN_EXPERT), D ** -0.5),
        "moe_w_up": nrm((N_ODD, N_EXPERTS, D, FFN_EXPERT), D ** -0.5),
        "moe_w_down": nrm((N_ODD, N_EXPERTS, FFN_EXPERT, D), BETA * FFN_EXPERT ** -0.5),
        "ln_mix_g": 1.0 + nrm((DEPTH, D), 0.1),
        "ln_mix_b": nrm((DEPTH, D), 0.02),
        "ln_ffn_g": 1.0 + nrm((DEPTH, D), 0.1),
        "ln_ffn_b": nrm((DEPTH, D), 0.02),
    }


def reference(x, meta_tokens, attn_w_in, fox_forget_bias, mla_q_norm, mla_kv_norm, mla_w_uq,
              mla_w_ukv, attn_w_out, pool_w_in, pool_w_group, pool_scale, pool_w_out,
              ffn_w_gate, ffn_w_up, ffn_w_down, moe_w_router, moe_b_router, moe_w_gate,
              moe_w_up, moe_w_down, ln_mix_g, ln_mix_b, ln_ffn_g, ln_ffn_b):
    B = x.shape[0]
    meta = jnp.broadcast_to(meta_tokens[None].astype(x.dtype), (B, N_META, D_MODEL))
    h = jnp.concatenate([meta, x], axis=1)
    L = h.shape[1]
    pos = jnp.arange(L, dtype=jnp.float32)
    inv_freq = ROPE_THETA ** (-jnp.arange(0, MLA_ROPE_DIM, 2, dtype=jnp.float32) / MLA_ROPE_DIM)
    ang = pos[:, None] * inv_freq[None, :]
    cos = jnp.cos(ang).astype(x.dtype)
    sin = jnp.sin(ang).astype(x.dtype)
    for i in range(DEPTH):
        j = i // 2
        if i % 2 == 0:
            m = mla_fox_mixer(h, attn_w_in[j], fox_forget_bias[j], mla_q_norm[j], mla_kv_norm[j],
                              mla_w_uq[j], mla_w_ukv[j], attn_w_out[j], cos, sin)
        else:
            m = pool_mixer(h, pool_w_in[j], pool_w_group[j], pool_scale[j], pool_w_out[j])
        h = layer_norm(ALPHA * h + m, ln_mix_g[i], ln_mix_b[i])
        if i % 2 == 0:
            f = swiglu(h, ffn_w_gate[j], ffn_w_up[j], ffn_w_down[j])
        else:
            f = moe_swiglu(h, moe_w_router[j], moe_b_router[j], moe_w_gate[j], moe_w_up[j], moe_w_down[j])
        h = layer_norm(ALPHA * h + f, ln_ffn_g[i], ln_ffn_b[i])
    return h[:, N_META:]
```

```python
import functools
import math

import jax
import jax.numpy as jnp
import numpy as np
from jax import lax
from jax.experimental import pallas as pl
from jax.experimental.pallas import tpu as pltpu

D = 2048
T = 8192
NM = 16
H = 8
QL = 512
KVL = 512
NOPE = 128
ROPE = 64
HD = 128
DQ_MLA = 256
THETA = 10000.0
POOL_W = (2, 4, 8, 16)
PG = D // 4
FD = 5632
NE = 8
FE = 7168
ALPHA = 4.0 ** 0.25
LN_EPS = 1e-5
RMS_EPS = 1e-6
NEG = -1e30
LOG2E = 1.4426950408889634

F32 = jnp.float32
BF16 = jnp.bfloat16

LANES_V7X = 128
VMEM_BYTES_V7X = 64 * 1024 * 1024
MIB = 1024 * 1024

TM = 512
TQ = 512
FFN_TF = 512
POOL_TM = 256
MOE_SUB = 256
MOE_TMAX = 5 * MOE_SUB
MOE_TF = 512
MOE_CAP = 2 * T + NE * MOE_SUB
MOE_NT = MOE_CAP // MOE_TMAX + NE + 1
CMB_TM = 256


def _params(sem, vmem_mib):
    return pltpu.CompilerParams(dimension_semantics=sem, vmem_limit_bytes=vmem_mib * MIB)


def _full(shape):
    n = len(shape)
    return pl.BlockSpec(shape, lambda *_: (0,) * n)


def _layer_norm(x, g, b):
    mu = jnp.mean(x, axis=-1, keepdims=True)
    xc = x - mu
    var = jnp.mean(xc * xc, axis=-1, keepdims=True)
    return xc * lax.rsqrt(var + LN_EPS) * g + b


def _rms_norm(x, g):
    return x * lax.rsqrt(jnp.mean(x * x, axis=-1, keepdims=True) + RMS_EPS) * g


def _dot(a, b):
    return jnp.dot(a, b, preferred_element_type=F32)


def _dot_nt(a, b):
    return lax.dot_general(a, b, (((1,), (1,)), ((), ())), preferred_element_type=F32)


def _mla_proj_kernel(x_ref, wlat_ref, qn_ref, kvn_ref, wuq_ref, wuqs_ref, wukv_ref, c_ref, s_ref,
                     q_ref, k_ref, v_ref, fl_ref, *, qscale):
    xb = x_ref[...].astype(BF16)
    lat = _dot(xb, wlat_ref[...])
    cq = lat[:, 0:QL]
    ckv = lat[:, QL:QL + KVL]
    kr = lat[:, 1024:1152]
    krs = lat[:, 1152:1280]
    fl_ref[...] = lat[:, 1280:1408]
    cos = c_ref[...]
    sin = s_ref[...]
    qn = _rms_norm(cq, qn_ref[...]).astype(BF16)
    qpre = _dot(qn, wuq_ref[...])
    qsw = _dot(qn, wuqs_ref[...])
    for h in range(H):
        nope = qpre[:, DQ_MLA * h:DQ_MLA * h + NOPE]
        rot = (qpre[:, DQ_MLA * h + NOPE:DQ_MLA * (h + 1)] * cos
               + qsw[:, HD * h:HD * (h + 1)] * sin)
        q_ref[:, DQ_MLA * h:DQ_MLA * h + NOPE] = (nope * qscale).astype(BF16)
        q_ref[:, DQ_MLA * h + NOPE:DQ_MLA * (h + 1)] = (rot * qscale).astype(BF16)
    kvn = _rms_norm(ckv, kvn_ref[...]).astype(BF16)
    kv = _dot(kvn, wukv_ref[...])
    krot = (kr * cos + krs * sin).astype(BF16)
    for h in range(H):
        k_ref[:, DQ_MLA * h:DQ_MLA * h + NOPE] = kv[:, HD * h:HD * (h + 1)].astype(BF16)
        k_ref[:, DQ_MLA * h + NOPE:DQ_MLA * (h + 1)] = krot
    v_ref[...] = kv[:, H * HD:].astype(BF16)


def _mla_proj(x, wlat, qn, kvn, wuq, wuqs, wukv, ctab, stab, tm):
    m = x.shape[0]
    qscale = (NOPE + ROPE) ** -0.5 * LOG2E
    row = lambda w: pl.BlockSpec((tm, w), lambda i: (i, 0))
    return pl.pallas_call(
        functools.partial(_mla_proj_kernel, qscale=qscale),
        grid=(m // tm,),
        in_specs=[row(D), _full(wlat.shape), _full(qn.shape), _full(kvn.shape), _full(wuq.shape),
                  _full(wuqs.shape), _full(wukv.shape), row(LANES_V7X), row(LANES_V7X)],
        out_specs=[row(H * DQ_MLA), row(H * DQ_MLA), row(H * HD), row(LANES_V7X)],
        out_shape=[jax.ShapeDtypeStruct((m, H * DQ_MLA), BF16),
                   jax.ShapeDtypeStruct((m, H * DQ_MLA), BF16),
                   jax.ShapeDtypeStruct((m, H * HD), BF16),
                   jax.ShapeDtypeStruct((m, LANES_V7X), F32)],
        compiler_params=_params(("parallel",), 56),
        name="mla_proj",
    )(x, wlat, qn, kvn, wuq, wuqs, wukv, ctab, stab)


def _mm_kernel(x_ref, w_ref, o_ref, *, scaled_blocks, scale):
    y = _dot(x_ref[...].astype(BF16), w_ref[...])
    if scaled_blocks:
        y = jnp.where(pl.program_id(1) < scaled_blocks, y * scale, y)
    o_ref[...] = y.astype(o_ref.dtype)


def _matmul(x, w, tm, tn, out_dtype, scaled_blocks=0, scale=1.0, name="matmul"):
    m, k = x.shape
    n = w.shape[1]
    return pl.pallas_call(
        functools.partial(_mm_kernel, scaled_blocks=scaled_blocks, scale=scale),
        grid=(m // tm, n // tn),
        in_specs=[pl.BlockSpec((tm, k), lambda i, j: (i, 0)),
                  pl.BlockSpec((k, tn), lambda i, j: (0, j))],
        out_specs=pl.BlockSpec((tm, tn), lambda i, j: (i, j)),
        out_shape=jax.ShapeDtypeStruct((m, n), out_dtype),
        compiler_params=_params(("parallel", "parallel"), 48),
        name=name,
    )(x, w)


def _split3(a):
    hi = a.astype(BF16)
    r = a - hi.astype(F32)
    mid = r.astype(BF16)
    lo = (r - mid.astype(F32)).astype(BF16)
    return hi, mid, lo


def _forget_cumsum_kernel(fl_ref, b_ref, o_ref):
    z = fl_ref[...] + b_ref[...]
    logf = jnp.minimum(z, 0.0) - jnp.log(1.0 + jnp.exp(-jnp.abs(z)))
    n = LANES_V7X
    ri = lax.broadcasted_iota(jnp.int32, (n, n), 0)
    ci = lax.broadcasted_iota(jnp.int32, (n, n), 1)
    upper = (ri <= ci).astype(BF16)
    ones = jnp.ones((n, n), BF16)
    within = jnp.zeros(logf.shape, F32)
    total = jnp.zeros(logf.shape, F32)
    for part in _split3(logf):
        within += _dot(part, upper)
        total += _dot(part, ones)
    r = logf.shape[0]
    lower = (lax.broadcasted_iota(jnp.int32, (r, r), 1)
             < lax.broadcasted_iota(jnp.int32, (r, r), 0)).astype(BF16)
    before = jnp.zeros(logf.shape, F32)
    for part in _split3(total):
        before += _dot(lower, part)
    o_ref[...] = (within + before) * LOG2E


def _forget_cumsum(fl_t, bias_b):
    spec = pl.BlockSpec((None, T // LANES_V7X, LANES_V7X), lambda h: (h, 0, 0))
    return pl.pallas_call(
        _forget_cumsum_kernel,
        grid=(H,),
        in_specs=[spec, pl.BlockSpec((None, 1, LANES_V7X), lambda h: (h, 0, 0))],
        out_specs=spec,
        out_shape=jax.ShapeDtypeStruct(fl_t.shape, F32),
        compiler_params=_params(("parallel",), 32),
        name="forget_cumsum",
    )(fl_t, bias_b)


def _attn_kernel(*refs, fox):
    if fox:
        q_ref, k_ref, v_ref, km_ref, vm_ref, crow_ref, ccol_ref, o_ref, m_scr, l_scr, acc_scr = refs
    else:
        q_ref, k_ref, v_ref, km_ref, vm_ref, o_ref, m_scr, l_scr, acc_scr = refs
    i = pl.program_id(1)
    q = q_ref[...]
    m_scr[...] = jnp.full(m_scr.shape, -jnp.inf, F32)
    l_scr[...] = jnp.zeros(l_scr.shape, F32)
    acc_scr[...] = jnp.zeros(acc_scr.shape, F32)

    def update(s, v):
        m_prev = m_scr[...]
        m_new = jnp.maximum(m_prev, jnp.max(s, axis=1, keepdims=True))
        alpha = jnp.exp2(m_prev - m_new)
        p = jnp.exp2(s - m_new)
        l_scr[...] = alpha * l_scr[...] + jnp.sum(p, axis=1, keepdims=True)
        acc_scr[...] = alpha * acc_scr[...] + _dot(p.astype(BF16), v)
        m_scr[...] = m_new

    def token_block(j, masked):
        off = pl.multiple_of(j * TQ, TQ)
        s = _dot_nt(q, k_ref[pl.ds(off, TQ), :])
        if fox:
            s = s - crow_ref[j]
        if masked:
            rows = lax.broadcasted_iota(jnp.int32, s.shape, 0)
            cols = lax.broadcasted_iota(jnp.int32, s.shape, 1)
            s = jnp.where(cols <= rows, s, NEG)
        update(s, v_ref[pl.ds(off, TQ), :])

    def body(j, carry):
        token_block(j, False)
        return carry

    lax.fori_loop(0, i, body, 0)
    token_block(i, True)
    s = _dot_nt(q, km_ref[...])
    if fox:
        s = s - ccol_ref[...]
    update(s, vm_ref[...])
    o_ref[...] = (acc_scr[...] / l_scr[...]).astype(o_ref.dtype)


def _attention(q, k, v, km, vm, q_col0, k_col0, v_col0, dq, crow=None, ccol=None, name="attn"):
    fox = crow is not None
    in_specs = [pl.BlockSpec((TQ, dq), lambda h, i: (i, q_col0 + h)),
                pl.BlockSpec((T, dq), lambda h, i: (0, k_col0 + h)),
                pl.BlockSpec((T, HD), lambda h, i: (0, v_col0 + h)),
                pl.BlockSpec((NM, dq), lambda h, i: (0, k_col0 + h)),
                pl.BlockSpec((NM, HD), lambda h, i: (0, v_col0 + h))]
    args = [q, k, v, km, vm]
    if fox:
        in_specs += [pl.BlockSpec((None, T // TQ, 1, TQ), lambda h, i: (h, 0, 0, 0)),
                     pl.BlockSpec((None, TQ, 1), lambda h, i: (h, i, 0))]
        args += [crow, ccol]
    return pl.pallas_call(
        functools.partial(_attn_kernel, fox=fox),
        grid=(H, T // TQ),
        in_specs=in_specs,
        out_specs=pl.BlockSpec((TQ, HD), lambda h, i: (i, h)),
        out_shape=jax.ShapeDtypeStruct((T, H * HD), BF16),
        scratch_shapes=[pltpu.VMEM((TQ, 1), F32), pltpu.VMEM((TQ, 1), F32),
                        pltpu.VMEM((TQ, HD), F32)],
        compiler_params=_params(("parallel", "arbitrary"), 48),
        name=name,
    )(*args)


def _meta_attn_kernel(q_ref, k_ref, v_ref, o_ref):
    s = _dot_nt(q_ref[...], k_ref[...])
    rows = lax.broadcasted_iota(jnp.int32, s.shape, 0)
    cols = lax.broadcasted_iota(jnp.int32, s.shape, 1)
    s = jnp.where(cols <= rows, s, NEG)
    p = jnp.exp2(s - jnp.max(s, axis=1, keepdims=True))
    o = _dot(p.astype(BF16), v_ref[...]) / jnp.sum(p, axis=1, keepdims=True)
    o_ref[...] = o.astype(o_ref.dtype)


def _meta_attention(q, k, v, q_col0, k_col0, v_col0, dq, name):
    return pl.pallas_call(
        _meta_attn_kernel,
        grid=(H,),
        in_specs=[pl.BlockSpec((NM, dq), lambda h: (0, q_col0 + h)),
                  pl.BlockSpec((NM, dq), lambda h: (0, k_col0 + h)),
                  pl.BlockSpec((NM, HD), lambda h: (0, v_col0 + h))],
        out_specs=pl.BlockSpec((NM, HD), lambda h: (0, h)),
        out_shape=jax.ShapeDtypeStruct((NM, H * HD), BF16),
        compiler_params=_params(("parallel",), 32),
        name=name,
    )(q, k, v)


def _attn_out_kernel(om_ref, of_ref, w_ref, h_ref, g_ref, b_ref, o_ref):
    m = _dot(om_ref[...], w_ref[0:H * HD, :]) + _dot(of_ref[...], w_ref[H * HD:, :])
    o_ref[...] = _layer_norm(ALPHA * h_ref[...] + m, g_ref[...], b_ref[...])


def _attn_out(om, of, w, h, g, b, tm):
    m = h.shape[0]
    row = lambda w_: pl.BlockSpec((tm, w_), lambda i: (i, 0))
    return pl.pallas_call(
        _attn_out_kernel,
        grid=(m // tm,),
        in_specs=[row(H * HD), row(H * HD), _full(w.shape), row(D), _full(g.shape), _full(b.shape)],
        out_specs=row(D),
        out_shape=jax.ShapeDtypeStruct((m, D), F32),
        compiler_params=_params(("parallel",), 48),
        name="attn_out_ln",
    )(om, of, w, h, g, b)


def _silu(x):
    return x / (1.0 + jnp.exp(-x))


def _ffn_kernel(h_ref, wg_ref, wu_ref, wd_ref, g_ref, b_ref, o_ref, xb_scr, acc_scr):
    f = pl.program_id(1)

    @pl.when(f == 0)
    def _():
        xb_scr[...] = h_ref[...].astype(BF16)
        acc_scr[...] = jnp.zeros(acc_scr.shape, F32)

    xb = xb_scr[...]
    hid = _silu(_dot(xb, wg_ref[...])) * _dot(xb, wu_ref[...])
    acc_scr[...] += _dot(hid.astype(BF16), wd_ref[...])

    @pl.when(f == pl.num_programs(1) - 1)
    def _():
        o_ref[...] = _layer_norm(ALPHA * h_ref[...] + acc_scr[...], g_ref[...], b_ref[...])


def _ffn(h, wg, wu, wd, g, b, tm):
    m = h.shape[0]
    return pl.pallas_call(
        _ffn_kernel,
        grid=(m // tm, FD // FFN_TF),
        in_specs=[pl.BlockSpec((tm, D), lambda i, f: (i, 0)),
                  pl.BlockSpec((D, FFN_TF), lambda i, f: (0, f)),
                  pl.BlockSpec((D, FFN_TF), lambda i, f: (0, f)),
                  pl.BlockSpec((FFN_TF, D), lambda i, f: (f, 0)),
                  pl.BlockSpec((1, D), lambda i, f: (0, 0)),
                  pl.BlockSpec((1, D), lambda i, f: (0, 0))],
        out_specs=pl.BlockSpec((tm, D), lambda i, f: (i, 0)),
        out_shape=jax.ShapeDtypeStruct((m, D), F32),
        scratch_shapes=[pltpu.VMEM((tm, D), BF16), pltpu.VMEM((tm, D), F32)],
        compiler_params=_params(("parallel", "arbitrary"), 52),
        name="ffn_ln",
    )(h, wg, wu, wd, g, b)


def _pool_kernel(hp_ref, prev_ref, meta_ref, h_ref, wg_ref, sc_ref, wo_ref, g_ref, b_ref, o_ref,
                 ext_scr, z_scr):
    tm = hp_ref.shape[0]
    halo = jnp.where(pl.program_id(0) == 0, meta_ref[...], prev_ref[...])
    ext_scr[0:NM, :] = halo
    ext_scr[NM:NM + tm, :] = hp_ref[...]
    for gi, w in enumerate(POOL_W):
        cols = slice(PG * gi, PG * (gi + 1))
        cur = ext_scr[NM:NM + tm, cols]
        tot = cur
        for j in range(1, w):
            tot = tot + ext_scr[NM - j:NM - j + tm, cols]
        diff = tot * (1.0 / w) - cur
        y = _dot(diff.astype(BF16), wg_ref[gi]) * sc_ref[:, cols]
        z_scr[:, cols] = y.astype(BF16)
    m = _dot(z_scr[...], wo_ref[...])
    o_ref[...] = _layer_norm(ALPHA * h_ref[...] + m, g_ref[...], b_ref[...])


def _pool_mixer(hp, hp_meta, h, wgroup, scale, wout, g, b):
    tm = POOL_TM
    row = pl.BlockSpec((tm, D), lambda i: (i, 0))
    prev = pl.BlockSpec((NM, D), lambda i: (jnp.maximum(i * (tm // NM) - 1, 0), 0))
    return pl.pallas_call(
        _pool_kernel,
        grid=(T // tm,),
        in_specs=[row, prev, _full(hp_meta.shape), row, _full(wgroup.shape), _full(scale.shape),
                  _full(wout.shape), _full(g.shape), _full(b.shape)],
        out_specs=row,
        out_shape=jax.ShapeDtypeStruct((T, D), F32),
        scratch_shapes=[pltpu.VMEM((NM + tm, D), F32), pltpu.VMEM((tm, D), BF16)],
        compiler_params=_params(("parallel",), 48),
        name="pool_ln",
    )(hp, hp, hp_meta, h, wgroup, scale, wout, g, b)


def _router_kernel(h_ref, w_ref, b_ref, info_ref, cnt_ref, carry_scr):
    tm = h_ref.shape[0]

    @pl.when(pl.program_id(0) == 0)
    def _():
        carry_scr[...] = jnp.zeros(carry_scr.shape, F32)

    logits = jnp.dot(h_ref[...], w_ref[...], precision=lax.Precision.HIGHEST,
                     preferred_element_type=F32) + b_ref[...]
    lane = lax.broadcasted_iota(jnp.int32, logits.shape, 1).astype(F32)
    nolane = float(LANES_V7X)
    m1 = jnp.max(logits, axis=1, keepdims=True)
    i1 = jnp.min(jnp.where(logits == m1, lane, nolane), axis=1, keepdims=True)
    rest = jnp.where(lane == i1, -jnp.inf, logits)
    m2 = jnp.max(rest, axis=1, keepdims=True)
    i2 = jnp.min(jnp.where(rest == m2, lane, nolane), axis=1, keepdims=True)
    e = jnp.exp(m2 - m1)
    g1 = 1.0 / (1.0 + e)
    g2 = e / (1.0 + e)
    hit1 = lane == i1
    hit2 = lane == i2
    member = jnp.logical_or(hit1, hit2).astype(BF16)
    earlier = (lax.broadcasted_iota(jnp.int32, (tm, tm), 1)
               < lax.broadcasted_iota(jnp.int32, (tm, tm), 0)).astype(BF16)
    before = _dot(earlier, member) + carry_scr[0:1, :]
    r1 = jnp.sum(jnp.where(hit1, before, 0.0), axis=1, keepdims=True)
    r2 = jnp.sum(jnp.where(hit2, before, 0.0), axis=1, keepdims=True)
    carry = carry_scr[0:1, :] + jnp.sum(member.astype(F32), axis=0, keepdims=True)
    carry_scr[...] = jnp.broadcast_to(carry, carry_scr.shape)
    cnt_ref[...] = jnp.broadcast_to(carry, cnt_ref.shape)
    info = jnp.where(lane == 0, g1, 0.0)
    info = jnp.where(lane == 1, g2, info)
    info = jnp.where(lane == 2, i1, info)
    info = jnp.where(lane == 3, i2, info)
    info = jnp.where(lane == 4, r1, info)
    info = jnp.where(lane == 5, r2, info)
    info_ref[...] = info


def _router(h, w_pad, b_pad):
    tm = TM
    return pl.pallas_call(
        _router_kernel,
        grid=(T // tm,),
        in_specs=[pl.BlockSpec((tm, D), lambda i: (i, 0)), _full(w_pad.shape), _full(b_pad.shape)],
        out_specs=[pl.BlockSpec((tm, LANES_V7X), lambda i: (i, 0)), _full((8, LANES_V7X))],
        out_shape=[jax.ShapeDtypeStruct((T, LANES_V7X), F32),
                   jax.ShapeDtypeStruct((8, LANES_V7X), F32)],
        scratch_shapes=[pltpu.VMEM((8, LANES_V7X), F32)],
        compiler_params=_params(("arbitrary",), 32),
        name="router",
    )(h, w_pad, b_pad)


def _moe_row_copy(h_hbm, stage, sem, tok, r):
    return pltpu.make_async_copy(h_hbm.at[pl.ds(tok, 1), :], stage.at[pl.ds(r, 1), :], sem)


def _moe_out_copy(acc_scr, y_hbm, sem, row0, c):
    off = pl.multiple_of(c * MOE_SUB, MOE_SUB)
    dst = pl.multiple_of(row0 + off, MOE_SUB)
    return pltpu.make_async_copy(acc_scr.at[pl.ds(off, MOE_SUB), :],
                                 y_hbm.at[pl.ds(dst, MOE_SUB), :], sem)


def _moe_kernel(te_ref, row0_ref, nsub_ref, nused_ref, dest_ref,
                h_hbm, wg_ref, wu_ref, wd_ref, y_hbm,
                slot_tok, stage, xb_scr, acc_scr, wgb, wub, wdb, gsem, osem):
    t = pl.program_id(0)
    f = pl.program_id(1)
    nf = pl.num_programs(1)

    @pl.when(jnp.logical_and(t == 0, f == 0))
    def _():
        def clear(s, c):
            slot_tok[s] = 0
            return c
        lax.fori_loop(0, MOE_CAP, clear, 0)

        def place(tok, c):
            slot_tok[dest_ref[2 * tok]] = tok
            slot_tok[dest_ref[2 * tok + 1]] = tok
            return c
        lax.fori_loop(0, T, place, 0)

    @pl.when(t < nused_ref[0])
    def _():
        row0 = row0_ref[t]
        nsub = nsub_ref[t]

        @pl.when(f == 0)
        def _():
            def gather(c, carry):
                base = row0 + c * MOE_SUB

                def issue(r, cc):
                    _moe_row_copy(h_hbm, stage, gsem, slot_tok[base + r], r).start()
                    return cc
                lax.fori_loop(0, MOE_SUB, issue, 0)

                def drain(r, cc):
                    _moe_row_copy(h_hbm, stage, gsem, 0, r).wait()
                    return cc
                lax.fori_loop(0, MOE_SUB, drain, 0)
                off = pl.multiple_of(c * MOE_SUB, MOE_SUB)
                xb_scr[pl.ds(off, MOE_SUB), :] = stage[...].astype(BF16)
                acc_scr[pl.ds(off, MOE_SUB), :] = jnp.zeros((MOE_SUB, D), F32)
                return carry
            lax.fori_loop(0, nsub, gather, 0)

        wgb[...] = wg_ref[...].astype(BF16)
        wub[...] = wu_ref[...].astype(BF16)
        wdb[...] = wd_ref[...].astype(BF16)

        def compute(c, carry):
            off = pl.multiple_of(c * MOE_SUB, MOE_SUB)
            xb = xb_scr[pl.ds(off, MOE_SUB), :]
            hid = _silu(_dot(xb, wgb[...])) * _dot(xb, wub[...])
            acc_scr[pl.ds(off, MOE_SUB), :] += _dot(hid.astype(BF16), wdb[...])
            return carry
        lax.fori_loop(0, nsub, compute, 0)

        @pl.when(f == nf - 1)
        def _():
            def start(c, carry):
                _moe_out_copy(acc_scr, y_hbm, osem, row0, c).start()
                return carry
            lax.fori_loop(0, nsub, start, 0)

            def finish(c, carry):
                _moe_out_copy(acc_scr, y_hbm, osem, row0, c).wait()
                return carry
            lax.fori_loop(0, nsub, finish, 0)

    @pl.when(jnp.logical_and(t == pl.num_programs(0) - 1, f == nf - 1))
    def _():
        acc_scr[0:MOE_SUB, :] = jnp.zeros((MOE_SUB, D), F32)
        first = nused_ref[1]

        def fill(c, carry):
            cp = _moe_out_copy(acc_scr, y_hbm, osem, c * MOE_SUB, 0)
            cp.start()
            cp.wait()
            return carry
        lax.fori_loop(first, MOE_CAP // MOE_SUB, fill, 0)


def _moe_experts(h, wg, wu, wd, tile_e, tile_row0, tile_nsub, n_used, dest):
    nf = FE // MOE_TF

    def fidx(t, f, nu):
        return jnp.where(t < nu[0], f, nf - 1)

    grid_spec = pltpu.PrefetchScalarGridSpec(
        num_scalar_prefetch=5,
        grid=(MOE_NT, nf),
        in_specs=[pl.BlockSpec(memory_space=pl.ANY),
                  pl.BlockSpec((None, D, MOE_TF), lambda t, f, te, r0, ns, nu, de: (te[t], 0, fidx(t, f, nu))),
                  pl.BlockSpec((None, D, MOE_TF), lambda t, f, te, r0, ns, nu, de: (te[t], 0, fidx(t, f, nu))),
                  pl.BlockSpec((None, MOE_TF, D), lambda t, f, te, r0, ns, nu, de: (te[t], fidx(t, f, nu), 0))],
        out_specs=pl.BlockSpec(memory_space=pl.ANY),
        scratch_shapes=[pltpu.SMEM((MOE_CAP,), jnp.int32),
                        pltpu.VMEM((MOE_SUB, D), F32),
                        pltpu.VMEM((MOE_TMAX, D), BF16),
                        pltpu.VMEM((MOE_TMAX, D), F32),
                        pltpu.VMEM((D, MOE_TF), BF16),
                        pltpu.VMEM((D, MOE_TF), BF16),
                        pltpu.VMEM((MOE_TF, D), BF16),
                        pltpu.SemaphoreType.DMA,
                        pltpu.SemaphoreType.DMA],
    )
    return pl.pallas_call(
        _moe_kernel,
        grid_spec=grid_spec,
        out_shape=jax.ShapeDtypeStruct((MOE_CAP, D), F32),
        compiler_params=_params(("arbitrary", "arbitrary"), 58),
        name="moe_experts",
    )(tile_e, tile_row0, tile_nsub, n_used, dest, h, wg, wu, wd)


def _cmb_copy(y_hbm, ybuf, sem, slot, k, r):
    return pltpu.make_async_copy(y_hbm.at[pl.ds(slot, 1), :], ybuf.at[k, pl.ds(r, 1), :], sem)


def _combine_kernel(dest_ref, h_ref, info_ref, y_hbm, g_ref, b_ref, o_ref, ybuf, sem):
    tm = h_ref.shape[0]
    base = pl.program_id(0) * tm

    def issue(r, c):
        _cmb_copy(y_hbm, ybuf, sem, dest_ref[2 * (base + r)], 0, r).start()
        _cmb_copy(y_hbm, ybuf, sem, dest_ref[2 * (base + r) + 1], 1, r).start()
        return c
    lax.fori_loop(0, tm, issue, 0)

    def drain(r, c):
        _cmb_copy(y_hbm, ybuf, sem, 0, 0, r).wait()
        _cmb_copy(y_hbm, ybuf, sem, 0, 1, r).wait()
        return c
    lax.fori_loop(0, tm, drain, 0)
    info = info_ref[...]
    moe = info[:, 0:1] * ybuf[0] + info[:, 1:2] * ybuf[1]
    o_ref[...] = _layer_norm(ALPHA * h_ref[...] + moe, g_ref[...], b_ref[...])


def _combine(dest, h, info, y_rows, g, b):
    tm = CMB_TM
    grid_spec = pltpu.PrefetchScalarGridSpec(
        num_scalar_prefetch=1,
        grid=(T // tm,),
        in_specs=[pl.BlockSpec((tm, D), lambda i, de: (i, 0)),
                  pl.BlockSpec((tm, LANES_V7X), lambda i, de: (i, 0)),
                  pl.BlockSpec(memory_space=pl.ANY),
                  pl.BlockSpec((1, D), lambda i, de: (0, 0)),
                  pl.BlockSpec((1, D), lambda i, de: (0, 0))],
        out_specs=pl.BlockSpec((tm, D), lambda i, de: (i, 0)),
        scratch_shapes=[pltpu.VMEM((2, tm, D), F32), pltpu.SemaphoreType.DMA],
    )
    return pl.pallas_call(
        _combine_kernel,
        grid_spec=grid_spec,
        out_shape=jax.ShapeDtypeStruct((T, D), F32),
        compiler_params=_params(("arbitrary",), 32),
        name="moe_combine_ln",
    )(dest, h, info, y_rows, g, b)


def _rope_tables(pos):
    inv_freq = THETA ** (-jnp.arange(0, ROPE, 2, dtype=F32) / ROPE)
    ang = pos[:, None] * inv_freq[None, :]
    cos = jnp.cos(ang).astype(F32)
    sin = jnp.sin(ang).astype(F32)
    zero = jnp.zeros((pos.shape[0], LANES_V7X - ROPE), F32)
    return (jnp.concatenate([cos, cos, zero], axis=1),
            jnp.concatenate([-sin, sin, zero], axis=1))


def _expert_tiles(counts):
    sub = (counts + MOE_SUB - 1) // MOE_SUB
    seg_sub = jnp.cumsum(sub) - sub
    max_sub = MOE_TMAX // MOE_SUB
    ntile = (sub + max_sub - 1) // max_sub
    tile_end = jnp.cumsum(ntile)
    n_used = tile_end[-1]
    tid = jnp.arange(MOE_NT, dtype=jnp.int32)
    e = jnp.minimum(jnp.searchsorted(tile_end, tid, side="right"), NE - 1).astype(jnp.int32)
    local = tid - (tile_end - ntile)[e]
    nt = jnp.maximum(ntile[e], 1)
    lo = (local * sub[e]) // nt
    hi = ((local + 1) * sub[e]) // nt
    used = tid < n_used
    last = jnp.maximum(n_used - 1, 0)
    e = jnp.where(used, e, e[last])
    row0 = jnp.where(used, (seg_sub[e] + lo) * MOE_SUB, 0)
    nsub = jnp.where(used, hi - lo, 0)
    used_info = jnp.stack([n_used, jnp.sum(sub)]).astype(jnp.int32)
    return (e.astype(jnp.int32), row0.astype(jnp.int32), nsub.astype(jnp.int32),
            used_info, (seg_sub * MOE_SUB).astype(jnp.int32))


def kernel(x, meta_tokens, attn_w_in, fox_forget_bias, mla_q_norm, mla_kv_norm, mla_w_uq, mla_w_ukv, attn_w_out, pool_w_in, pool_w_group, pool_scale, pool_w_out, ffn_w_gate, ffn_w_up, ffn_w_down, moe_w_router, moe_b_router, moe_w_gate, moe_w_up, moe_w_down, ln_mix_g, ln_mix_b, ln_ffn_g, ln_ffn_b):
    assert x.shape == (1, T, D) and meta_tokens.shape == (NM, D)
    xt = x[0]
    meta = meta_tokens.astype(x.dtype)

    w_in = attn_w_in[0]
    o_kr, o_qf, o_f = QL + KVL, QL + KVL + ROPE, QL + KVL + ROPE + 3 * H * HD
    kr1, kr2 = w_in[:, o_kr:o_kr + ROPE // 2], w_in[:, o_kr + ROPE // 2:o_qf]
    zpad = lambda n: jnp.zeros((D, n), F32)
    w_lat = jnp.concatenate(
        [w_in[:, :o_kr], kr1, kr2, zpad(LANES_V7X - ROPE), kr2, kr1, zpad(LANES_V7X - ROPE),
         w_in[:, o_f:o_f + H], zpad(LANES_V7X - H)], axis=1).astype(BF16)
    w_fox = w_in[:, o_qf:o_f].astype(BF16)
    wuq = mla_w_uq[0].reshape(QL, H, NOPE + ROPE)
    uq_n, uq_1, uq_2 = wuq[..., :NOPE], wuq[..., NOPE:NOPE + ROPE // 2], wuq[..., NOPE + ROPE // 2:]
    zq = jnp.zeros((QL, H, LANES_V7X - ROPE), F32)
    w_uq = jnp.concatenate([uq_n, uq_1, uq_2, zq], axis=-1).reshape(QL, H * DQ_MLA).astype(BF16)
    w_uqs = jnp.concatenate([uq_2, uq_1, zq], axis=-1).reshape(QL, H * HD).astype(BF16)
    wukv = mla_w_ukv[0].reshape(KVL, H, NOPE + HD)
    w_ukv = jnp.concatenate([wukv[..., :NOPE].reshape(KVL, H * NOPE),
                             wukv[..., NOPE:].reshape(KVL, H * HD)], axis=1).astype(BF16)
    w_ao = attn_w_out[0].astype(BF16)
    w_fg, w_fu, w_fd = ffn_w_gate[0].astype(BF16), ffn_w_up[0].astype(BF16), ffn_w_down[0].astype(BF16)
    w_pi, w_pg, w_po = pool_w_in[0].astype(BF16), pool_w_group[0].astype(BF16), pool_w_out[0].astype(BF16)
    qn, kvn = mla_q_norm[0][None, :], mla_kv_norm[0][None, :]
    row = lambda v: v[None, :]

    ctab_t, stab_t = _rope_tables(jnp.arange(NM, NM + T, dtype=F32))
    ctab_m, stab_m = _rope_tables(jnp.arange(NM, dtype=F32))

    fox_scale = HD ** -0.5 * LOG2E
    q_t, k_t, v_t, fl_t = _mla_proj(xt, w_lat, qn, kvn, w_uq, w_uqs, w_ukv, ctab_t, stab_t, TM)
    q_m, k_m, v_m, _ = _mla_proj(meta, w_lat, qn, kvn, w_uq, w_uqs, w_ukv, ctab_m, stab_m, NM)
    fox_t = _matmul(xt, w_fox, TM, H * HD, BF16, scaled_blocks=1, scale=fox_scale, name="fox_proj")
    fox_m = _matmul(meta, w_fox, NM, H * HD, BF16, scaled_blocks=1, scale=fox_scale, name="fox_proj_meta")

    fl = fl_t[:, :H].T.reshape(H, T // LANES_V7X, LANES_V7X)
    bias_b = jnp.broadcast_to(fox_forget_bias[0][:, None, None], (H, 1, LANES_V7X))
    c2 = _forget_cumsum(fl, bias_b).reshape(H, T)
    crow = c2.reshape(H, T // TQ, 1, TQ)
    ccol = c2.reshape(H, T, 1)

    o_mla = _attention(q_t, k_t, v_t, k_m, v_m, 0, 0, 0, DQ_MLA, name="mla_attn")
    o_fox = _attention(fox_t, fox_t, fox_t, fox_m, fox_m, 0, H, 2 * H, HD, crow, ccol, name="fox_attn")
    om_mla = _meta_attention(q_m, k_m, v_m, 0, 0, 0, DQ_MLA, "mla_attn_meta")
    om_fox = _meta_attention(fox_m, fox_m, fox_m, 0, H, 2 * H, HD, "fox_attn_meta")

    g0, b0, g1, b1 = row(ln_mix_g[0]), row(ln_mix_b[0]), row(ln_ffn_g[0]), row(ln_ffn_b[0])
    h_t = _attn_out(o_mla, o_fox, w_ao, xt, g0, b0, TM)
    h_m = _attn_out(om_mla, om_fox, w_ao, meta, g0, b0, NM)
    h_t = _ffn(h_t, w_fg, w_fu, w_fd, g1, b1, TM)
    h_m = _ffn(h_m, w_fg, w_fu, w_fd, g1, b1, NM)

    hp_t = _matmul(h_t, w_pi, TM, 1024, F32, name="pool_in")
    hp_m = _matmul(h_m, w_pi, NM, 1024, F32, name="pool_in_meta")
    h_t = _pool_mixer(hp_t, hp_m, h_t, w_pg, row(pool_scale[0]), w_po,
                      row(ln_mix_g[1]), row(ln_mix_b[1]))

    w_r = jnp.concatenate([moe_w_router[0], jnp.zeros((D, LANES_V7X - NE), F32)], axis=1)
    b_r = jnp.concatenate([moe_b_router[0], jnp.full((LANES_V7X - NE,), NEG, F32)])[None, :]
    info, cnt = _router(h_t, w_r, b_r)
    counts = cnt[0, :NE].astype(jnp.int32)
    tile_e, tile_row0, tile_nsub, n_used, seg_row = _expert_tiles(counts)
    experts = info[:, 2:4].astype(jnp.int32)
    ranks = info[:, 4:6].astype(jnp.int32)
    dest = (seg_row[experts] + ranks).reshape(2 * T)
    y_rows = _moe_experts(h_t, moe_w_gate[0], moe_w_up[0], moe_w_down[0],
                          tile_e, tile_row0, tile_nsub, n_used, dest)
    out = _combine(dest, h_t, info, y_rows, row(ln_ffn_g[1]), row(ln_ffn_b[1]))
    return out[None]
```

```python
import functools
import math

import jax
import jax.numpy as jnp
import numpy as np
from jax import lax
from jax.experimental import pallas as pl
from jax.experimental.pallas import tpu as pltpu

D = 2048
T = 8192
NM = 16
H = 8
QL = 512
KVL = 512
NOPE = 128
ROPE = 64
HD = 128
DQ_MLA = 256
THETA = 10000.0
POOL_W = (2, 4, 8, 16)
PG = D // 4
FD = 5632
NE = 8
FE = 7168
ALPHA = 4.0 ** 0.25
LN_EPS = 1e-5
RMS_EPS = 1e-6
NEG = -1e30
LOG2E = 1.4426950408889634

F32 = jnp.float32
BF16 = jnp.bfloat16

LANES_V7X = 128
VMEM_BYTES_V7X = 64 * 1024 * 1024
MIB = 1024 * 1024

TM = 512
TQ = 512
FFN_TF = 512
POOL_TM = 256
MOE_SUB = 256
MOE_TMAX = 5 * MOE_SUB
MOE_TF = 512
MOE_CAP = 2 * T + NE * MOE_SUB
MOE_NT = MOE_CAP // MOE_TMAX + NE + 1
CMB_TM = 256


def _params(sem, vmem_mib):
    return pltpu.CompilerParams(dimension_semantics=sem, vmem_limit_bytes=vmem_mib * MIB)


def _full(shape):
    n = len(shape)
    return pl.BlockSpec(shape, lambda *_: (0,) * n)


def _layer_norm(x, g, b):
    mu = jnp.mean(x, axis=-1, keepdims=True)
    xc = x - mu
    var = jnp.mean(xc * xc, axis=-1, keepdims=True)
    return xc * lax.rsqrt(var + LN_EPS) * g + b


def _rms_norm(x, g):
    return x * lax.rsqrt(jnp.mean(x * x, axis=-1, keepdims=True) + RMS_EPS) * g


def _dot(a, b):
    return jnp.dot(a, b, preferred_element_type=F32)


def _dot_nt(a, b):
    return lax.dot_general(a, b, (((1,), (1,)), ((), ())), preferred_element_type=F32)


def _mla_proj_kernel(x_ref, wlat_ref, qn_ref, kvn_ref, wuq_ref, wuqs_ref, wukv_ref, c_ref, s_ref,
                     q_ref, k_ref, v_ref, fl_ref, *, qscale):
    xb = x_ref[...].astype(BF16)
    lat = _dot(xb, wlat_ref[...])
    cq = lat[:, 0:QL]
    ckv = lat[:, QL:QL + KVL]
    kr = lat[:, 1024:1152]
    krs = lat[:, 1152:1280]
    fl_ref[...] = lat[:, 1280:1408]
    cos = c_ref[...]
    sin = s_ref[...]
    qn = _rms_norm(cq, qn_ref[...]).astype(BF16)
    qpre = _dot(qn, wuq_ref[...])
    qsw = _dot(qn, wuqs_ref[...])
    for h in range(H):
        nope = qpre[:, DQ_MLA * h:DQ_MLA * h + NOPE]
        rot = (qpre[:, DQ_MLA * h + NOPE:DQ_MLA * (h + 1)] * cos
               + qsw[:, HD * h:HD * (h + 1)] * sin)
        q_ref[:, DQ_MLA * h:DQ_MLA * h + NOPE] = (nope * qscale).astype(BF16)
        q_ref[:, DQ_MLA * h + NOPE:DQ_MLA * (h + 1)] = (rot * qscale).astype(BF16)
    kvn = _rms_norm(ckv, kvn_ref[...]).astype(BF16)
    kv = _dot(kvn, wukv_ref[...])
    krot = (kr * cos + krs * sin).astype(BF16)
    for h in range(H):
        k_ref[:, DQ_MLA * h:DQ_MLA * h + NOPE] = kv[:, HD * h:HD * (h + 1)].astype(BF16)
        k_ref[:, DQ_MLA * h + NOPE:DQ_MLA * (h + 1)] = krot
    v_ref[...] = kv[:, H * HD:].astype(BF16)


def _mla_proj(x, wlat, qn, kvn, wuq, wuqs, wukv, ctab, stab, tm):
    m = x.shape[0]
    qscale = (NOPE + ROPE) ** -0.5 * LOG2E
    row = lambda w: pl.BlockSpec((tm, w), lambda i: (i, 0))
    return pl.pallas_call(
        functools.partial(_mla_proj_kernel, qscale=qscale),
        grid=(m // tm,),
        in_specs=[row(D), _full(wlat.shape), _full(qn.shape), _full(kvn.shape), _full(wuq.shape),
                  _full(wuqs.shape), _full(wukv.shape), row(LANES_V7X), row(LANES_V7X)],
        out_specs=[row(H * DQ_MLA), row(H * DQ_MLA), row(H * HD), row(LANES_V7X)],
        out_shape=[jax.ShapeDtypeStruct((m, H * DQ_MLA), BF16),
                   jax.ShapeDtypeStruct((m, H * DQ_MLA), BF16),
                   jax.ShapeDtypeStruct((m, H * HD), BF16),
                   jax.ShapeDtypeStruct((m, LANES_V7X), F32)],
        compiler_params=_params(("parallel",), 56),
        name="mla_proj",
    )(x, wlat, qn, kvn, wuq, wuqs, wukv, ctab, stab)


def _mm_kernel(x_ref, w_ref, o_ref, *, scaled_blocks, scale):
    y = _dot(x_ref[...].astype(BF16), w_ref[...])
    if scaled_blocks:
        y = jnp.where(pl.program_id(1) < scaled_blocks, y * scale, y)
    o_ref[...] = y.astype(o_ref.dtype)


def _matmul(x, w, tm, tn, out_dtype, scaled_blocks=0, scale=1.0, name="matmul"):
    m, k = x.shape
    n = w.shape[1]
    return pl.pallas_call(
        functools.partial(_mm_kernel, scaled_blocks=scaled_blocks, scale=scale),
        grid=(m // tm, n // tn),
        in_specs=[pl.BlockSpec((tm, k), lambda i, j: (i, 0)),
                  pl.BlockSpec((k, tn), lambda i, j: (0, j))],
        out_specs=pl.BlockSpec((tm, tn), lambda i, j: (i, j)),
        out_shape=jax.ShapeDtypeStruct((m, n), out_dtype),
        compiler_params=_params(("parallel", "parallel"), 48),
        name=name,
    )(x, w)


def _split3(a):
    hi = a.astype(BF16)
    r = a - hi.astype(F32)
    mid = r.astype(BF16)
    lo = (r - mid.astype(F32)).astype(BF16)
    return hi, mid, lo


def _forget_cumsum_kernel(fl_ref, b_ref, o_ref):
    z = fl_ref[...] + b_ref[...]
    logf = jnp.minimum(z, 0.0) - jnp.log(1.0 + jnp.exp(-jnp.abs(z)))
    n = LANES_V7X
    ri = lax.broadcasted_iota(jnp.int32, (n, n), 0)
    ci = lax.broadcasted_iota(jnp.int32, (n, n), 1)
    upper = (ri <= ci).astype(BF16)
    ones = jnp.ones((n, n), BF16)
    within = jnp.zeros(logf.shape, F32)
    total = jnp.zeros(logf.shape, F32)
    for part in _split3(logf):
        within += _dot(part, upper)
        total += _dot(part, ones)
    r = logf.shape[0]
    lower = (lax.broadcasted_iota(jnp.int32, (r, r), 1)
             < lax.broadcasted_iota(jnp.int32, (r, r), 0)).astype(BF16)
    before = jnp.zeros(logf.shape, F32)
    for part in _split3(total):
        before += _dot(lower, part)
    c2 = (within + before) * LOG2E
    for p, part in enumerate(_split3(c2)):
        o_ref[p] = part
        o_ref[3 + p] = -part


def _forget_cumsum(fl_t, bias_b):
    r = T // LANES_V7X
    return pl.pallas_call(
        _forget_cumsum_kernel,
        grid=(H,),
        in_specs=[pl.BlockSpec((None, r, LANES_V7X), lambda h: (h, 0, 0)),
                  pl.BlockSpec((None, 1, LANES_V7X), lambda h: (h, 0, 0))],
        out_specs=pl.BlockSpec((None, 6, r, LANES_V7X), lambda h: (h, 0, 0, 0)),
        out_shape=jax.ShapeDtypeStruct((H, 6, r, LANES_V7X), BF16),
        compiler_params=_params(("parallel",), 32),
        name="forget_cumsum",
    )(fl_t, bias_b)


ATT_NH = 2
AUG_STRIDE = 16


def _attn_kernel(*refs, fox, dq):
    if fox:
        (q_ref, k_ref, vt_ref, km_ref, vmt_ref, qa_ref, ka_ref, kam_ref, o_ref,
         m_scr, l_scr, acc_scr, s0_scr, s1_scr, mb0_scr, mb1_scr, qf_scr) = refs
    else:
        (q_ref, k_ref, vt_ref, km_ref, vmt_ref, o_ref,
         m_scr, l_scr, acc_scr, s0_scr, s1_scr, mb0_scr, mb1_scr) = refs
    hp = pl.program_id(0)
    i = pl.program_id(1)
    m_scr[...] = jnp.full(m_scr.shape, -jnp.inf, F32)
    l_scr[...] = jnp.zeros(l_scr.shape, F32)
    acc_scr[...] = jnp.zeros(acc_scr.shape, F32)
    if fox:
        lane = lax.broadcasted_iota(jnp.int32, qa_ref.shape, 1)
        for g in range(ATT_NH):
            lo = (hp * ATT_NH + g) * AUG_STRIDE
            own = jnp.logical_and(lane >= lo, lane < lo + AUG_STRIDE)
            qf_scr[g, :, 0:dq] = q_ref[:, g * dq:(g + 1) * dq]
            qf_scr[g, :, dq:2 * dq] = jnp.where(own, qa_ref[...], jnp.zeros_like(qa_ref))

    def query(g):
        return qf_scr[g] if fox else q_ref[:, g * dq:(g + 1) * dq]

    def update(g, st, mb, vt):
        m_prev = m_scr[g]
        m_new = jnp.maximum(m_prev, mb)
        alpha = jnp.exp2(m_prev - m_new)
        p = jnp.exp2(st - m_new)
        l_scr[g] = alpha * l_scr[g] + jnp.sum(p, axis=0, keepdims=True)
        acc_scr[g] = alpha * acc_scr[g] + _dot(vt, p.astype(BF16))
        m_scr[g] = m_new

    def stage1(slot, j):
        s_ref, mb_ref = slots[slot]
        off = pl.multiple_of(j * TQ, TQ)
        for g in range(ATT_NH):
            kb = k_ref[pl.ds(off, TQ), g * dq:(g + 1) * dq]
            if fox:
                kb = jnp.concatenate([kb, ka_ref[pl.ds(off, TQ), :]], axis=1)
            st = _dot_nt(kb, query(g))
            s_ref[g] = st
            mb_ref[g] = jnp.max(st, axis=0, keepdims=True)

    def stage2(slot, j, masked=False):
        s_ref, mb_ref = slots[slot]
        for g in range(ATT_NH):
            st = s_ref[g]
            mb = mb_ref[g]
            if masked:
                keys = lax.broadcasted_iota(jnp.int32, st.shape, 0)
                qrys = lax.broadcasted_iota(jnp.int32, st.shape, 1)
                st = jnp.where(keys <= qrys, st, NEG)
                mb = jnp.max(st, axis=0, keepdims=True)
            update(g, st, mb, vt_ref[g, j])

    slots = ((s0_scr, mb0_scr), (s1_scr, mb1_scr))
    stage1(0, 0)

    def body(jj, carry):
        j = 2 * jj
        stage1(1, j + 1)
        stage2(0, j)
        stage1(0, j + 2)
        stage2(1, j + 1)
        return carry

    lax.fori_loop(0, i // 2, body, 0)

    @pl.when(i % 2 == 0)
    def _():
        stage2(0, i, masked=True)

    @pl.when(i % 2 == 1)
    def _():
        stage1(1, i)
        stage2(0, i - 1)
        stage2(1, i, masked=True)

    for g in range(ATT_NH):
        kb = km_ref[:, g * dq:(g + 1) * dq]
        if fox:
            kb = jnp.concatenate([kb, kam_ref[...]], axis=1)
        st = _dot_nt(kb, query(g))
        update(g, st, jnp.max(st, axis=0, keepdims=True), vmt_ref[g])
        o = (acc_scr[g] / l_scr[g]).T
        o_ref[:, g * HD:(g + 1) * HD] = o.astype(o_ref.dtype)


def _attention(q, k, vt, km, vmt, k_col0, dq, aug=None, name="attn"):
    fox = aug is not None
    nh = ATT_NH
    in_specs = [pl.BlockSpec((TQ, nh * dq), lambda h, i: (i, h)),
                pl.BlockSpec((T, nh * dq), lambda h, i: (0, k_col0 // nh + h)),
                pl.BlockSpec((nh, T // TQ, HD, TQ), lambda h, i: (h, 0, 0, 0)),
                pl.BlockSpec((NM, nh * dq), lambda h, i: (0, k_col0 // nh + h)),
                pl.BlockSpec((nh, HD, NM), lambda h, i: (h, 0, 0))]
    args = [q, k, vt, km, vmt]
    scratch = [pltpu.VMEM((nh, 1, TQ), F32), pltpu.VMEM((nh, 1, TQ), F32),
               pltpu.VMEM((nh, HD, TQ), F32),
               pltpu.VMEM((nh, TQ, TQ), F32), pltpu.VMEM((nh, TQ, TQ), F32),
               pltpu.VMEM((nh, 1, TQ), F32), pltpu.VMEM((nh, 1, TQ), F32)]
    if fox:
        in_specs += [pl.BlockSpec((TQ, LANES_V7X), lambda h, i: (i, 0)),
                     _full((T, LANES_V7X)), _full((NM, LANES_V7X))]
        args += list(aug)
        scratch.append(pltpu.VMEM((nh, TQ, 2 * dq), BF16))
    return pl.pallas_call(
        functools.partial(_attn_kernel, fox=fox, dq=dq),
        grid=(H // nh, T // TQ),
        in_specs=in_specs,
        out_specs=pl.BlockSpec((TQ, nh * HD), lambda h, i: (i, h)),
        out_shape=jax.ShapeDtypeStruct((T, H * HD), BF16),
        scratch_shapes=scratch,
        compiler_params=_params(("parallel", "arbitrary"), 56),
        name=name,
    )(*args)


def _meta_attn_kernel(q_ref, k_ref, v_ref, o_ref):
    s = _dot_nt(q_ref[...], k_ref[...])
    rows = lax.broadcasted_iota(jnp.int32, s.shape, 0)
    cols = lax.broadcasted_iota(jnp.int32, s.shape, 1)
    s = jnp.where(cols <= rows, s, NEG)
    p = jnp.exp2(s - jnp.max(s, axis=1, keepdims=True))
    o = _dot(p.astype(BF16), v_ref[...]) / jnp.sum(p, axis=1, keepdims=True)
    o_ref[...] = o.astype(o_ref.dtype)


def _meta_attention(q, k, v, q_col0, k_col0, v_col0, dq, name):
    return pl.pallas_call(
        _meta_attn_kernel,
        grid=(H,),
        in_specs=[pl.BlockSpec((NM, dq), lambda h: (0, q_col0 + h)),
                  pl.BlockSpec((NM, dq), lambda h: (0, k_col0 + h)),
                  pl.BlockSpec((NM, HD), lambda h: (0, v_col0 + h))],
        out_specs=pl.BlockSpec((NM, HD), lambda h: (0, h)),
        out_shape=jax.ShapeDtypeStruct((NM, H * HD), BF16),
        compiler_params=_params(("parallel",), 32),
        name=name,
    )(q, k, v)


def _attn_out_kernel(om_ref, of_ref, w_ref, h_ref, g_ref, b_ref, o_ref):
    m = _dot(om_ref[...], w_ref[0:H * HD, :]) + _dot(of_ref[...], w_ref[H * HD:, :])
    o_ref[...] = _layer_norm(ALPHA * h_ref[...] + m, g_ref[...], b_ref[...])


def _attn_out(om, of, w, h, g, b, tm):
    m = h.shape[0]
    row = lambda w_: pl.BlockSpec((tm, w_), lambda i: (i, 0))
    return pl.pallas_call(
        _attn_out_kernel,
        grid=(m // tm,),
        in_specs=[row(H * HD), row(H * HD), _full(w.shape), row(D), _full(g.shape), _full(b.shape)],
        out_specs=row(D),
        out_shape=jax.ShapeDtypeStruct((m, D), F32),
        compiler_params=_params(("parallel",), 48),
        name="attn_out_ln",
    )(om, of, w, h, g, b)


def _silu(x):
    return x / (1.0 + jnp.exp(-x))


def _ffn_kernel(h_ref, wg_ref, wu_ref, wd_ref, g_ref, b_ref, o_ref, xb_scr, acc_scr):
    f = pl.program_id(1)

    @pl.when(f == 0)
    def _():
        xb_scr[...] = h_ref[...].astype(BF16)
        acc_scr[...] = jnp.zeros(acc_scr.shape, F32)

    xb = xb_scr[...]
    hid = _silu(_dot(xb, wg_ref[...])) * _dot(xb, wu_ref[...])
    acc_scr[...] += _dot(hid.astype(BF16), wd_ref[...])

    @pl.when(f == pl.num_programs(1) - 1)
    def _():
        o_ref[...] = _layer_norm(ALPHA * h_ref[...] + acc_scr[...], g_ref[...], b_ref[...])


def _ffn(h, wg, wu, wd, g, b, tm):
    m = h.shape[0]
    return pl.pallas_call(
        _ffn_kernel,
        grid=(m // tm, FD // FFN_TF),
        in_specs=[pl.BlockSpec((tm, D), lambda i, f: (i, 0)),
                  pl.BlockSpec((D, FFN_TF), lambda i, f: (0, f)),
                  pl.BlockSpec((D, FFN_TF), lambda i, f: (0, f)),
                  pl.BlockSpec((FFN_TF, D), lambda i, f: (f, 0)),
                  pl.BlockSpec((1, D), lambda i, f: (0, 0)),
                  pl.BlockSpec((1, D), lambda i, f: (0, 0))],
        out_specs=pl.BlockSpec((tm, D), lambda i, f: (i, 0)),
        out_shape=jax.ShapeDtypeStruct((m, D), F32),
        scratch_shapes=[pltpu.VMEM((tm, D), BF16), pltpu.VMEM((tm, D), F32)],
        compiler_params=_params(("parallel", "arbitrary"), 52),
        name="ffn_ln",
    )(h, wg, wu, wd, g, b)


def _pool_kernel(hp_ref, prev_ref, meta_ref, h_ref, wg_ref, sc_ref, wo_ref, g_ref, b_ref, o_ref,
                 ext_scr, z_scr):
    tm = hp_ref.shape[0]
    halo = jnp.where(pl.program_id(0) == 0, meta_ref[...], prev_ref[...])
    ext_scr[0:NM, :] = halo
    ext_scr[NM:NM + tm, :] = hp_ref[...]
    for gi, w in enumerate(POOL_W):
        cols = slice(PG * gi, PG * (gi + 1))
        cur = ext_scr[NM:NM + tm, cols]
        tot = cur
        for j in range(1, w):
            tot = tot + ext_scr[NM - j:NM - j + tm, cols]
        diff = tot * (1.0 / w) - cur
        y = _dot(diff.astype(BF16), wg_ref[gi]) * sc_ref[:, cols]
        z_scr[:, cols] = y.astype(BF16)
    m = _dot(z_scr[...], wo_ref[...])
    o_ref[...] = _layer_norm(ALPHA * h_ref[...] + m, g_ref[...], b_ref[...])


def _pool_mixer(hp, hp_meta, h, wgroup, scale, wout, g, b):
    tm = POOL_TM
    row = pl.BlockSpec((tm, D), lambda i: (i, 0))
    prev = pl.BlockSpec((NM, D), lambda i: (jnp.maximum(i * (tm // NM) - 1, 0), 0))
    return pl.pallas_call(
        _pool_kernel,
        grid=(T // tm,),
        in_specs=[row, prev, _full(hp_meta.shape), row, _full(wgroup.shape), _full(scale.shape),
                  _full(wout.shape), _full(g.shape), _full(b.shape)],
        out_specs=row,
        out_shape=jax.ShapeDtypeStruct((T, D), F32),
        scratch_shapes=[pltpu.VMEM((NM + tm, D), F32), pltpu.VMEM((tm, D), BF16)],
        compiler_params=_params(("parallel",), 48),
        name="pool_ln",
    )(hp, hp, hp_meta, h, wgroup, scale, wout, g, b)


def _router_kernel(h_ref, w_ref, b_ref, info_ref, cnt_ref, carry_scr):
    tm = h_ref.shape[0]

    @pl.when(pl.program_id(0) == 0)
    def _():
        carry_scr[...] = jnp.zeros(carry_scr.shape, F32)

    logits = jnp.dot(h_ref[...], w_ref[...], precision=lax.Precision.HIGHEST,
                     preferred_element_type=F32) + b_ref[...]
    lane = lax.broadcasted_iota(jnp.int32, logits.shape, 1).astype(F32)
    nolane = float(LANES_V7X)
    m1 = jnp.max(logits, axis=1, keepdims=True)
    i1 = jnp.min(jnp.where(logits == m1, lane, nolane), axis=1, keepdims=True)
    rest = jnp.where(lane == i1, -jnp.inf, logits)
    m2 = jnp.max(rest, axis=1, keepdims=True)
    i2 = jnp.min(jnp.where(rest == m2, lane, nolane), axis=1, keepdims=True)
    e = jnp.exp(m2 - m1)
    g1 = 1.0 / (1.0 + e)
    g2 = e / (1.0 + e)
    hit1 = lane == i1
    hit2 = lane == i2
    member = jnp.logical_or(hit1, hit2).astype(BF16)
    earlier = (lax.broadcasted_iota(jnp.int32, (tm, tm), 1)
               < lax.broadcasted_iota(jnp.int32, (tm, tm), 0)).astype(BF16)
    before = _dot(earlier, member) + carry_scr[0:1, :]
    r1 = jnp.sum(jnp.where(hit1, before, 0.0), axis=1, keepdims=True)
    r2 = jnp.sum(jnp.where(hit2, before, 0.0), axis=1, keepdims=True)
    carry = carry_scr[0:1, :] + jnp.sum(member.astype(F32), axis=0, keepdims=True)
    carry_scr[...] = jnp.broadcast_to(carry, carry_scr.shape)
    cnt_ref[...] = jnp.broadcast_to(carry, cnt_ref.shape)
    info = jnp.where(lane == 0, g1, 0.0)
    info = jnp.where(lane == 1, g2, info)
    info = jnp.where(lane == 2, i1, info)
    info = jnp.where(lane == 3, i2, info)
    info = jnp.where(lane == 4, r1, info)
    info = jnp.where(lane == 5, r2, info)
    info_ref[...] = info


def _router(h, w_pad, b_pad):
    tm = TM
    return pl.pallas_call(
        _router_kernel,
        grid=(T // tm,),
        in_specs=[pl.BlockSpec((tm, D), lambda i: (i, 0)), _full(w_pad.shape), _full(b_pad.shape)],
        out_specs=[pl.BlockSpec((tm, LANES_V7X), lambda i: (i, 0)), _full((8, LANES_V7X))],
        out_shape=[jax.ShapeDtypeStruct((T, LANES_V7X), F32),
                   jax.ShapeDtypeStruct((8, LANES_V7X), F32)],
        scratch_shapes=[pltpu.VMEM((8, LANES_V7X), F32)],
        compiler_params=_params(("arbitrary",), 32),
        name="router",
    )(h, w_pad, b_pad)


def _moe_row_copy(h_hbm, stage, sem, tok, r):
    return pltpu.make_async_copy(h_hbm.at[pl.ds(tok, 1), :], stage.at[pl.ds(r, 1), :], sem)


def _moe_out_copy(acc_scr, y_hbm, sem, row0, c):
    off = pl.multiple_of(c * MOE_SUB, MOE_SUB)
    dst = pl.multiple_of(row0 + off, MOE_SUB)
    return pltpu.make_async_copy(acc_scr.at[pl.ds(off, MOE_SUB), :],
                                 y_hbm.at[pl.ds(dst, MOE_SUB), :], sem)


def _moe_kernel(te_ref, row0_ref, nsub_ref, nused_ref, dest_ref,
                h_hbm, wg_ref, wu_ref, wd_ref, y_hbm,
                slot_tok, stage, xb_scr, acc_scr, wgb, wub, wdb, gsem, osem):
    t = pl.program_id(0)
    f = pl.program_id(1)
    nf = pl.num_programs(1)

    @pl.when(jnp.logical_and(t == 0, f == 0))
    def _():
        def clear(s, c):
            slot_tok[s] = 0
            return c
        lax.fori_loop(0, MOE_CAP, clear, 0)

        def place(tok, c):
            slot_tok[dest_ref[2 * tok]] = tok
            slot_tok[dest_ref[2 * tok + 1]] = tok
            return c
        lax.fori_loop(0, T, place, 0)

    @pl.when(t < nused_ref[0])
    def _():
        row0 = row0_ref[t]
        nsub = nsub_ref[t]

        @pl.when(f == 0)
        def _():
            def gather(c, carry):
                base = row0 + c * MOE_SUB

                def issue(r, cc):
                    _moe_row_copy(h_hbm, stage, gsem, slot_tok[base + r], r).start()
                    return cc
                lax.fori_loop(0, MOE_SUB, issue, 0)

                def drain(r, cc):
                    _moe_row_copy(h_hbm, stage, gsem, 0, r).wait()
                    return cc
                lax.fori_loop(0, MOE_SUB, drain, 0)
                off = pl.multiple_of(c * MOE_SUB, MOE_SUB)
                xb_scr[pl.ds(off, MOE_SUB), :] = stage[...].astype(BF16)
                acc_scr[pl.ds(off, MOE_SUB), :] = jnp.zeros((MOE_SUB, D), F32)
                return carry
            lax.fori_loop(0, nsub, gather, 0)

        wgb[...] = wg_ref[...].astype(BF16)
        wub[...] = wu_ref[...].astype(BF16)
        wdb[...] = wd_ref[...].astype(BF16)

        def compute(c, carry):
            off = pl.multiple_of(c * MOE_SUB, MOE_SUB)
            xb = xb_scr[pl.ds(off, MOE_SUB), :]
            hid = _silu(_dot(xb, wgb[...])) * _dot(xb, wub[...])
            acc_scr[pl.ds(off, MOE_SUB), :] += _dot(hid.astype(BF16), wdb[...])
            return carry
        lax.fori_loop(0, nsub, compute, 0)

        @pl.when(f == nf - 1)
        def _():
            def start(c, carry):
                _moe_out_copy(acc_scr, y_hbm, osem, row0, c).start()
                return carry
            lax.fori_loop(0, nsub, start, 0)

            def finish(c, carry):
                _moe_out_copy(acc_scr, y_hbm, osem, row0, c).wait()
                return carry
            lax.fori_loop(0, nsub, finish, 0)

    @pl.when(jnp.logical_and(t == pl.num_programs(0) - 1, f == nf - 1))
    def _():
        acc_scr[0:MOE_SUB, :] = jnp.zeros((MOE_SUB, D), F32)
        first = nused_ref[1]

        def fill(c, carry):
            cp = _moe_out_copy(acc_scr, y_hbm, osem, c * MOE_SUB, 0)
            cp.start()
            cp.wait()
            return carry
        lax.fori_loop(first, MOE_CAP // MOE_SUB, fill, 0)


def _moe_experts(h, wg, wu, wd, tile_e, tile_row0, tile_nsub, n_used, dest):
    nf = FE // MOE_TF

    def fidx(t, f, nu):
        return jnp.where(t < nu[0], f, nf - 1)

    grid_spec = pltpu.PrefetchScalarGridSpec(
        num_scalar_prefetch=5,
        grid=(MOE_NT, nf),
        in_specs=[pl.BlockSpec(memory_space=pl.ANY),
                  pl.BlockSpec((None, D, MOE_TF), lambda t, f, te, r0, ns, nu, de: (te[t], 0, fidx(t, f, nu))),
                  pl.BlockSpec((None, D, MOE_TF), lambda t, f, te, r0, ns, nu, de: (te[t], 0, fidx(t, f, nu))),
                  pl.BlockSpec((None, MOE_TF, D), lambda t, f, te, r0, ns, nu, de: (te[t], fidx(t, f, nu), 0))],
        out_specs=pl.BlockSpec(memory_space=pl.ANY),
        scratch_shapes=[pltpu.SMEM((MOE_CAP,), jnp.int32),
                        pltpu.VMEM((MOE_SUB, D), F32),
                        pltpu.VMEM((MOE_TMAX, D), BF16),
                        pltpu.VMEM((MOE_TMAX, D), F32),
                        pltpu.VMEM((D, MOE_TF), BF16),
                        pltpu.VMEM((D, MOE_TF), BF16),
                        pltpu.VMEM((MOE_TF, D), BF16),
                        pltpu.SemaphoreType.DMA,
                        pltpu.SemaphoreType.DMA],
    )
    return pl.pallas_call(
        _moe_kernel,
        grid_spec=grid_spec,
        out_shape=jax.ShapeDtypeStruct((MOE_CAP, D), F32),
        compiler_params=_params(("arbitrary", "arbitrary"), 58),
        name="moe_experts",
    )(tile_e, tile_row0, tile_nsub, n_used, dest, h, wg, wu, wd)


def _cmb_copy(y_hbm, ybuf, sem, slot, k, r):
    return pltpu.make_async_copy(y_hbm.at[pl.ds(slot, 1), :], ybuf.at[k, pl.ds(r, 1), :], sem)


def _combine_kernel(dest_ref, h_ref, info_ref, y_hbm, g_ref, b_ref, o_ref, ybuf, sem):
    tm = h_ref.shape[0]
    base = pl.program_id(0) * tm

    def issue(r, c):
        _cmb_copy(y_hbm, ybuf, sem, dest_ref[2 * (base + r)], 0, r).start()
        _cmb_copy(y_hbm, ybuf, sem, dest_ref[2 * (base + r) + 1], 1, r).start()
        return c
    lax.fori_loop(0, tm, issue, 0)

    def drain(r, c):
        _cmb_copy(y_hbm, ybuf, sem, 0, 0, r).wait()
        _cmb_copy(y_hbm, ybuf, sem, 0, 1, r).wait()
        return c
    lax.fori_loop(0, tm, drain, 0)
    info = info_ref[...]
    moe = info[:, 0:1] * ybuf[0] + info[:, 1:2] * ybuf[1]
    o_ref[...] = _layer_norm(ALPHA * h_ref[...] + moe, g_ref[...], b_ref[...])


def _combine(dest, h, info, y_rows, g, b):
    tm = CMB_TM
    grid_spec = pltpu.PrefetchScalarGridSpec(
        num_scalar_prefetch=1,
        grid=(T // tm,),
        in_specs=[pl.BlockSpec((tm, D), lambda i, de: (i, 0)),
                  pl.BlockSpec((tm, LANES_V7X), lambda i, de: (i, 0)),
                  pl.BlockSpec(memory_space=pl.ANY),
                  pl.BlockSpec((1, D), lambda i, de: (0, 0)),
                  pl.BlockSpec((1, D), lambda i, de: (0, 0))],
        out_specs=pl.BlockSpec((tm, D), lambda i, de: (i, 0)),
        scratch_shapes=[pltpu.VMEM((2, tm, D), F32), pltpu.SemaphoreType.DMA],
    )
    return pl.pallas_call(
        _combine_kernel,
        grid_spec=grid_spec,
        out_shape=jax.ShapeDtypeStruct((T, D), F32),
        compiler_params=_params(("arbitrary",), 32),
        name="moe_combine_ln",
    )(dest, h, info, y_rows, g, b)


def _rope_tables(pos):
    inv_freq = THETA ** (-jnp.arange(0, ROPE, 2, dtype=F32) / ROPE)
    ang = pos[:, None] * inv_freq[None, :]
    cos = jnp.cos(ang).astype(F32)
    sin = jnp.sin(ang).astype(F32)
    zero = jnp.zeros((pos.shape[0], LANES_V7X - ROPE), F32)
    return (jnp.concatenate([cos, cos, zero], axis=1),
            jnp.concatenate([-sin, sin, zero], axis=1))


def _expert_tiles(counts):
    sub = (counts + MOE_SUB - 1) // MOE_SUB
    seg_sub = jnp.cumsum(sub) - sub
    max_sub = MOE_TMAX // MOE_SUB
    ntile = (sub + max_sub - 1) // max_sub
    tile_end = jnp.cumsum(ntile)
    n_used = tile_end[-1]
    tid = jnp.arange(MOE_NT, dtype=jnp.int32)
    e = jnp.minimum(jnp.sum(tid[:, None] >= tile_end[None, :], axis=1), NE - 1).astype(jnp.int32)
    local = tid - (tile_end - ntile)[e]
    nt = jnp.maximum(ntile[e], 1)
    lo = (local * sub[e]) // nt
    hi = ((local + 1) * sub[e]) // nt
    used = tid < n_used
    last = jnp.maximum(n_used - 1, 0)
    e = jnp.where(used, e, e[last])
    row0 = jnp.where(used, (seg_sub[e] + lo) * MOE_SUB, 0)
    nsub = jnp.where(used, hi - lo, 0)
    used_info = jnp.stack([n_used, jnp.sum(sub)]).astype(jnp.int32)
    return (e.astype(jnp.int32), row0.astype(jnp.int32), nsub.astype(jnp.int32),
            used_info, (seg_sub * MOE_SUB).astype(jnp.int32))


def kernel(x, meta_tokens, attn_w_in, fox_forget_bias, mla_q_norm, mla_kv_norm, mla_w_uq, mla_w_ukv, attn_w_out, pool_w_in, pool_w_group, pool_scale, pool_w_out, ffn_w_gate, ffn_w_up, ffn_w_down, moe_w_router, moe_b_router, moe_w_gate, moe_w_up, moe_w_down, ln_mix_g, ln_mix_b, ln_ffn_g, ln_ffn_b):
    assert x.shape == (1, T, D) and meta_tokens.shape == (NM, D)
    xt = x[0]
    meta = meta_tokens.astype(x.dtype)

    w_in = attn_w_in[0]
    o_kr, o_qf, o_f = QL + KVL, QL + KVL + ROPE, QL + KVL + ROPE + 3 * H * HD
    kr1, kr2 = w_in[:, o_kr:o_kr + ROPE // 2], w_in[:, o_kr + ROPE // 2:o_qf]
    zpad = lambda n: jnp.zeros((D, n), F32)
    w_lat = jnp.concatenate(
        [w_in[:, :o_kr], kr1, kr2, zpad(LANES_V7X - ROPE), kr2, kr1, zpad(LANES_V7X - ROPE),
         w_in[:, o_f:o_f + H], zpad(LANES_V7X - H)], axis=1).astype(BF16)
    w_fox = w_in[:, o_qf:o_f].astype(BF16)
    wuq = mla_w_uq[0].reshape(QL, H, NOPE + ROPE)
    uq_n, uq_1, uq_2 = wuq[..., :NOPE], wuq[..., NOPE:NOPE + ROPE // 2], wuq[..., NOPE + ROPE // 2:]
    zq = jnp.zeros((QL, H, LANES_V7X - ROPE), F32)
    w_uq = jnp.concatenate([uq_n, uq_1, uq_2, zq], axis=-1).reshape(QL, H * DQ_MLA).astype(BF16)
    w_uqs = jnp.concatenate([uq_2, uq_1, zq], axis=-1).reshape(QL, H * HD).astype(BF16)
    wukv = mla_w_ukv[0].reshape(KVL, H, NOPE + HD)
    w_ukv = jnp.concatenate([wukv[..., :NOPE].reshape(KVL, H * NOPE),
                             wukv[..., NOPE:].reshape(KVL, H * HD)], axis=1).astype(BF16)
    w_ao = attn_w_out[0].astype(BF16)
    w_fg, w_fu, w_fd = ffn_w_gate[0].astype(BF16), ffn_w_up[0].astype(BF16), ffn_w_down[0].astype(BF16)
    w_pi, w_pg, w_po = pool_w_in[0].astype(BF16), pool_w_group[0].astype(BF16), pool_w_out[0].astype(BF16)
    qn, kvn = mla_q_norm[0][None, :], mla_kv_norm[0][None, :]
    row = lambda v: v[None, :]

    ctab_t, stab_t = _rope_tables(jnp.arange(NM, NM + T, dtype=F32))
    ctab_m, stab_m = _rope_tables(jnp.arange(NM, dtype=F32))

    fox_scale = HD ** -0.5 * LOG2E
    q_t, k_t, v_t, fl_t = _mla_proj(xt, w_lat, qn, kvn, w_uq, w_uqs, w_ukv, ctab_t, stab_t, TM)
    q_m, k_m, v_m, _ = _mla_proj(meta, w_lat, qn, kvn, w_uq, w_uqs, w_ukv, ctab_m, stab_m, NM)
    fox_t = _matmul(xt, w_fox, TM, H * HD, BF16, scaled_blocks=1, scale=fox_scale, name="fox_proj")
    fox_m = _matmul(meta, w_fox, NM, H * HD, BF16, scaled_blocks=1, scale=fox_scale, name="fox_proj_meta")

    fl = fl_t[:, :H].T.reshape(H, T // LANES_V7X, LANES_V7X)
    bias_b = jnp.broadcast_to(fox_forget_bias[0][:, None, None], (H, 1, LANES_V7X))
    parts = _forget_cumsum(fl, bias_b).reshape(H, 6, T)
    pos = parts[:, 0:3].transpose(2, 0, 1)
    neg = parts[:, 3:6].transpose(2, 0, 1)
    lanes = lambda pieces: jnp.concatenate(pieces, axis=2).reshape(-1, H * AUG_STRIDE)
    cst = lambda rows, n, v: jnp.full((rows, H, n), v, BF16)
    aug = (lanes([cst(T, 3, 1.0), pos, cst(T, AUG_STRIDE - 6, 0.0)]),
           lanes([neg, cst(T, AUG_STRIDE - 3, 0.0)]),
           lanes([cst(NM, 3, 0.0), cst(NM, 3, -1.0), cst(NM, AUG_STRIDE - 6, 0.0)]))

    to_vt = lambda v: v.reshape(T // TQ, TQ, H, HD).transpose(2, 0, 3, 1)
    to_vmt = lambda v: v.reshape(NM, H, HD).transpose(1, 2, 0)
    o_mla = _attention(q_t, k_t, to_vt(v_t), k_m, to_vmt(v_m), 0, DQ_MLA, name="mla_attn")
    o_fox = _attention(fox_t, fox_t, to_vt(fox_t[:, 2 * H * HD:]), fox_m, to_vmt(fox_m[:, 2 * H * HD:]),
                       H, HD, aug, name="fox_attn")
    om_mla = _meta_attention(q_m, k_m, v_m, 0, 0, 0, DQ_MLA, "mla_attn_meta")
    om_fox = _meta_attention(fox_m, fox_m, fox_m, 0, H, 2 * H, HD, "fox_attn_meta")

    g0, b0, g1, b1 = row(ln_mix_g[0]), row(ln_mix_b[0]), row(ln_ffn_g[0]), row(ln_ffn_b[0])
    h_t = _attn_out(o_mla, o_fox, w_ao, xt, g0, b0, TM)
    h_m = _attn_out(om_mla, om_fox, w_ao, meta, g0, b0, NM)
    h_t = _ffn(h_t, w_fg, w_fu, w_fd, g1, b1, TM)
    h_m = _ffn(h_m, w_fg, w_fu, w_fd, g1, b1, NM)

    hp_t = _matmul(h_t, w_pi, TM, 1024, F32, name="pool_in")
    hp_m = _matmul(h_m, w_pi, NM, 1024, F32, name="pool_in_meta")
    h_t = _pool_mixer(hp_t, hp_m, h_t, w_pg, row(pool_scale[0]), w_po,
                      row(ln_mix_g[1]), row(ln_mix_b[1]))

    w_r = jnp.concatenate([moe_w_router[0], jnp.zeros((D, LANES_V7X - NE), F32)], axis=1)
    b_r = jnp.concatenate([moe_b_router[0], jnp.full((LANES_V7X - NE,), NEG, F32)])[None, :]
    info, cnt = _router(h_t, w_r, b_r)
    counts = cnt[0, :NE].astype(jnp.int32)
    tile_e, tile_row0, tile_nsub, n_used, seg_row = _expert_tiles(counts)
    experts = info[:, 2:4].astype(jnp.int32)
    ranks = info[:, 4:6].astype(jnp.int32)
    dest = (seg_row[experts] + ranks).reshape(2 * T)
    y_rows = _moe_experts(h_t, moe_w_gate[0], moe_w_up[0], moe_w_down[0],
                          tile_e, tile_row0, tile_nsub, n_used, dest)
    out = _combine(dest, h_t, info, y_rows, row(ln_ffn_g[1]), row(ln_ffn_b[1]))
    return out[None]
```

```python
import functools
import math

import jax
import jax.numpy as jnp
import numpy as np
from jax import lax
from jax.experimental import pallas as pl
from jax.experimental.pallas import tpu as pltpu

D = 2048
T = 8192
NM = 16
H = 8
QL = 512
KVL = 512
NOPE = 128
ROPE = 64
HD = 128
DQ_MLA = 256
THETA = 10000.0
POOL_W = (2, 4, 8, 16)
PG = D // 4
FD = 5632
NE = 8
FE = 7168
ALPHA = 4.0 ** 0.25
LN_EPS = 1e-5
RMS_EPS = 1e-6
NEG = -1e30
LOG2E = 1.4426950408889634

F32 = jnp.float32
BF16 = jnp.bfloat16

LANES_V7X = 128
VMEM_BYTES_V7X = 64 * 1024 * 1024
MIB = 1024 * 1024

TM = 512
TQ = 512
FFN_TF = 512
POOL_TM = 256
MOE_UNIT = 128
MOE_TUNITS = 10
MOE_TMAX = MOE_TUNITS * MOE_UNIT
MOE_TF = 512
MOE_CAP = 2 * T + NE * MOE_UNIT
MOE_NT = MOE_CAP // MOE_TMAX + NE + 1
CMB_TM = 256


def _params(sem, vmem_mib):
    return pltpu.CompilerParams(dimension_semantics=sem, vmem_limit_bytes=vmem_mib * MIB)


def _full(shape):
    n = len(shape)
    return pl.BlockSpec(shape, lambda *_: (0,) * n)


def _layer_norm(x, g, b):
    mu = jnp.mean(x, axis=-1, keepdims=True)
    xc = x - mu
    var = jnp.mean(xc * xc, axis=-1, keepdims=True)
    return xc * lax.rsqrt(var + LN_EPS) * g + b


def _rms_norm(x, g):
    return x * lax.rsqrt(jnp.mean(x * x, axis=-1, keepdims=True) + RMS_EPS) * g


def _dot(a, b):
    return jnp.dot(a, b, preferred_element_type=F32)


def _dot_nt(a, b):
    return lax.dot_general(a, b, (((1,), (1,)), ((), ())), preferred_element_type=F32)


def _mla_proj_kernel(x_ref, wlat_ref, qn_ref, kvn_ref, wuq_ref, wuqs_ref, wukv_ref, c_ref, s_ref,
                     q_ref, k_ref, v_ref, fl_ref, *, qscale):
    xb = x_ref[...].astype(BF16)
    lat = _dot(xb, wlat_ref[...])
    cq = lat[:, 0:QL]
    ckv = lat[:, QL:QL + KVL]
    kr = lat[:, 1024:1152]
    krs = lat[:, 1152:1280]
    fl_ref[...] = lat[:, 1280:1408]
    cos = c_ref[...]
    sin = s_ref[...]
    qn = _rms_norm(cq, qn_ref[...]).astype(BF16)
    qpre = _dot(qn, wuq_ref[...])
    qsw = _dot(qn, wuqs_ref[...])
    for h in range(H):
        nope = qpre[:, DQ_MLA * h:DQ_MLA * h + NOPE]
        rot = (qpre[:, DQ_MLA * h + NOPE:DQ_MLA * (h + 1)] * cos
               + qsw[:, HD * h:HD * (h + 1)] * sin)
        q_ref[:, DQ_MLA * h:DQ_MLA * h + NOPE] = (nope * qscale).astype(BF16)
        q_ref[:, DQ_MLA * h + NOPE:DQ_MLA * (h + 1)] = (rot * qscale).astype(BF16)
    kvn = _rms_norm(ckv, kvn_ref[...]).astype(BF16)
    kv = _dot(kvn, wukv_ref[...])
    krot = (kr * cos + krs * sin).astype(BF16)
    for h in range(H):
        k_ref[:, DQ_MLA * h:DQ_MLA * h + NOPE] = kv[:, HD * h:HD * (h + 1)].astype(BF16)
        k_ref[:, DQ_MLA * h + NOPE:DQ_MLA * (h + 1)] = krot
    v_ref[...] = kv[:, H * HD:].astype(BF16)


def _mla_proj(x, wlat, qn, kvn, wuq, wuqs, wukv, ctab, stab, tm):
    m = x.shape[0]
    qscale = (NOPE + ROPE) ** -0.5 * LOG2E
    row = lambda w: pl.BlockSpec((tm, w), lambda i: (i, 0))
    return pl.pallas_call(
        functools.partial(_mla_proj_kernel, qscale=qscale),
        grid=(m // tm,),
        in_specs=[row(D), _full(wlat.shape), _full(qn.shape), _full(kvn.shape), _full(wuq.shape),
                  _full(wuqs.shape), _full(wukv.shape), row(LANES_V7X), row(LANES_V7X)],
        out_specs=[row(H * DQ_MLA), row(H * DQ_MLA), row(H * HD), row(LANES_V7X)],
        out_shape=[jax.ShapeDtypeStruct((m, H * DQ_MLA), BF16),
                   jax.ShapeDtypeStruct((m, H * DQ_MLA), BF16),
                   jax.ShapeDtypeStruct((m, H * HD), BF16),
                   jax.ShapeDtypeStruct((m, LANES_V7X), F32)],
        compiler_params=_params(("parallel",), 56),
        name="mla_proj",
    )(x, wlat, qn, kvn, wuq, wuqs, wukv, ctab, stab)


def _mm_kernel(x_ref, w_ref, o_ref, *, scaled_blocks, scale):
    y = _dot(x_ref[...].astype(BF16), w_ref[...])
    if scaled_blocks:
        y = jnp.where(pl.program_id(1) < scaled_blocks, y * scale, y)
    o_ref[...] = y.astype(o_ref.dtype)


def _matmul(x, w, tm, tn, out_dtype, scaled_blocks=0, scale=1.0, name="matmul"):
    m, k = x.shape
    n = w.shape[1]
    return pl.pallas_call(
        functools.partial(_mm_kernel, scaled_blocks=scaled_blocks, scale=scale),
        grid=(m // tm, n // tn),
        in_specs=[pl.BlockSpec((tm, k), lambda i, j: (i, 0)),
                  pl.BlockSpec((k, tn), lambda i, j: (0, j))],
        out_specs=pl.BlockSpec((tm, tn), lambda i, j: (i, j)),
        out_shape=jax.ShapeDtypeStruct((m, n), out_dtype),
        compiler_params=_params(("parallel", "parallel"), 48),
        name=name,
    )(x, w)


def _split3(a):
    hi = a.astype(BF16)
    r = a - hi.astype(F32)
    mid = r.astype(BF16)
    lo = (r - mid.astype(F32)).astype(BF16)
    return hi, mid, lo


def _forget_cumsum_kernel(fl_ref, b_ref, o_ref):
    z = fl_ref[...] + b_ref[...]
    logf = jnp.minimum(z, 0.0) - jnp.log(1.0 + jnp.exp(-jnp.abs(z)))
    n = LANES_V7X
    ri = lax.broadcasted_iota(jnp.int32, (n, n), 0)
    ci = lax.broadcasted_iota(jnp.int32, (n, n), 1)
    upper = (ri <= ci).astype(BF16)
    ones = jnp.ones((n, n), BF16)
    within = jnp.zeros(logf.shape, F32)
    total = jnp.zeros(logf.shape, F32)
    for part in _split3(logf):
        within += _dot(part, upper)
        total += _dot(part, ones)
    r = logf.shape[0]
    lower = (lax.broadcasted_iota(jnp.int32, (r, r), 1)
             < lax.broadcasted_iota(jnp.int32, (r, r), 0)).astype(BF16)
    before = jnp.zeros(logf.shape, F32)
    for part in _split3(total):
        before += _dot(lower, part)
    c2 = (within + before) * LOG2E
    for p, part in enumerate(_split3(c2)):
        o_ref[p] = part
        o_ref[3 + p] = -part


def _forget_cumsum(fl_t, bias_b):
    r = T // LANES_V7X
    return pl.pallas_call(
        _forget_cumsum_kernel,
        grid=(H,),
        in_specs=[pl.BlockSpec((None, r, LANES_V7X), lambda h: (h, 0, 0)),
                  pl.BlockSpec((None, 1, LANES_V7X), lambda h: (h, 0, 0))],
        out_specs=pl.BlockSpec((None, 6, r, LANES_V7X), lambda h: (h, 0, 0, 0)),
        out_shape=jax.ShapeDtypeStruct((H, 6, r, LANES_V7X), BF16),
        compiler_params=_params(("parallel",), 32),
        name="forget_cumsum",
    )(fl_t, bias_b)


ATT_NH = 2
AUG_STRIDE = 16


def _attn_kernel(*refs, fox, dq):
    if fox:
        (q_ref, k_ref, vt_ref, km_ref, vmt_ref, qa_ref, ka_ref, kam_ref, o_ref,
         m_scr, l_scr, acc_scr, s0_scr, s1_scr, mb0_scr, mb1_scr, qf_scr) = refs
    else:
        (q_ref, k_ref, vt_ref, km_ref, vmt_ref, o_ref,
         m_scr, l_scr, acc_scr, s0_scr, s1_scr, mb0_scr, mb1_scr) = refs
    hp = pl.program_id(0)
    i = pl.program_id(1)
    m_scr[...] = jnp.full(m_scr.shape, -jnp.inf, F32)
    l_scr[...] = jnp.zeros(l_scr.shape, F32)
    acc_scr[...] = jnp.zeros(acc_scr.shape, F32)
    if fox:
        lane = lax.broadcasted_iota(jnp.int32, qa_ref.shape, 1)
        for g in range(ATT_NH):
            lo = (hp * ATT_NH + g) * AUG_STRIDE
            own = jnp.logical_and(lane >= lo, lane < lo + AUG_STRIDE)
            qf_scr[g, :, 0:dq] = q_ref[:, g * dq:(g + 1) * dq]
            qf_scr[g, :, dq:2 * dq] = jnp.where(own, qa_ref[...], jnp.zeros_like(qa_ref))

    def query(g):
        return qf_scr[g] if fox else q_ref[:, g * dq:(g + 1) * dq]

    def update(g, st, mb, vt):
        m_prev = m_scr[g]
        m_new = jnp.maximum(m_prev, mb)
        alpha = jnp.exp2(m_prev - m_new)
        p = jnp.exp2(st - m_new)
        l_scr[g] = alpha * l_scr[g] + jnp.sum(p, axis=0, keepdims=True)
        acc_scr[g] = alpha * acc_scr[g] + _dot(vt, p.astype(BF16))
        m_scr[g] = m_new

    def stage1(slot, j):
        s_ref, mb_ref = slots[slot]
        off = pl.multiple_of(j * TQ, TQ)
        for g in range(ATT_NH):
            kb = k_ref[pl.ds(off, TQ), g * dq:(g + 1) * dq]
            if fox:
                kb = jnp.concatenate([kb, ka_ref[pl.ds(off, TQ), :]], axis=1)
            st = _dot_nt(kb, query(g))
            s_ref[g] = st
            mb_ref[g] = jnp.max(st, axis=0, keepdims=True)

    def stage2(slot, j, masked=False):
        s_ref, mb_ref = slots[slot]
        for g in range(ATT_NH):
            st = s_ref[g]
            mb = mb_ref[g]
            if masked:
                keys = lax.broadcasted_iota(jnp.int32, st.shape, 0)
                qrys = lax.broadcasted_iota(jnp.int32, st.shape, 1)
                st = jnp.where(keys <= qrys, st, NEG)
                mb = jnp.max(st, axis=0, keepdims=True)
            update(g, st, mb, vt_ref[g, j])

    slots = ((s0_scr, mb0_scr), (s1_scr, mb1_scr))
    stage1(0, 0)

    def body(jj, carry):
        j = 2 * jj
        stage1(1, j + 1)
        stage2(0, j)
        stage1(0, j + 2)
        stage2(1, j + 1)
        return carry

    lax.fori_loop(0, i // 2, body, 0)

    @pl.when(i % 2 == 0)
    def _():
        stage2(0, i, masked=True)

    @pl.when(i % 2 == 1)
    def _():
        stage1(1, i)
        stage2(0, i - 1)
        stage2(1, i, masked=True)

    for g in range(ATT_NH):
        kb = km_ref[:, g * dq:(g + 1) * dq]
        if fox:
            kb = jnp.concatenate([kb, kam_ref[...]], axis=1)
        st = _dot_nt(kb, query(g))
        update(g, st, jnp.max(st, axis=0, keepdims=True), vmt_ref[g])
        o = (acc_scr[g] / l_scr[g]).T
        o_ref[:, g * HD:(g + 1) * HD] = o.astype(o_ref.dtype)


def _attention(q, k, vt, km, vmt, k_col0, dq, aug=None, name="attn"):
    fox = aug is not None
    nh = ATT_NH
    in_specs = [pl.BlockSpec((TQ, nh * dq), lambda h, i: (i, h)),
                pl.BlockSpec((T, nh * dq), lambda h, i: (0, k_col0 // nh + h)),
                pl.BlockSpec((nh, T // TQ, HD, TQ), lambda h, i: (h, 0, 0, 0)),
                pl.BlockSpec((NM, nh * dq), lambda h, i: (0, k_col0 // nh + h)),
                pl.BlockSpec((nh, HD, NM), lambda h, i: (h, 0, 0))]
    args = [q, k, vt, km, vmt]
    scratch = [pltpu.VMEM((nh, 1, TQ), F32), pltpu.VMEM((nh, 1, TQ), F32),
               pltpu.VMEM((nh, HD, TQ), F32),
               pltpu.VMEM((nh, TQ, TQ), F32), pltpu.VMEM((nh, TQ, TQ), F32),
               pltpu.VMEM((nh, 1, TQ), F32), pltpu.VMEM((nh, 1, TQ), F32)]
    if fox:
        in_specs += [pl.BlockSpec((TQ, LANES_V7X), lambda h, i: (i, 0)),
                     _full((T, LANES_V7X)), _full((NM, LANES_V7X))]
        args += list(aug)
        scratch.append(pltpu.VMEM((nh, TQ, 2 * dq), BF16))
    return pl.pallas_call(
        functools.partial(_attn_kernel, fox=fox, dq=dq),
        grid=(H // nh, T // TQ),
        in_specs=in_specs,
        out_specs=pl.BlockSpec((TQ, nh * HD), lambda h, i: (i, h)),
        out_shape=jax.ShapeDtypeStruct((T, H * HD), BF16),
        scratch_shapes=scratch,
        compiler_params=_params(("parallel", "arbitrary"), 56),
        name=name,
    )(*args)


def _meta_attn_kernel(q_ref, k_ref, v_ref, o_ref):
    s = _dot_nt(q_ref[...], k_ref[...])
    rows = lax.broadcasted_iota(jnp.int32, s.shape, 0)
    cols = lax.broadcasted_iota(jnp.int32, s.shape, 1)
    s = jnp.where(cols <= rows, s, NEG)
    p = jnp.exp2(s - jnp.max(s, axis=1, keepdims=True))
    o = _dot(p.astype(BF16), v_ref[...]) / jnp.sum(p, axis=1, keepdims=True)
    o_ref[...] = o.astype(o_ref.dtype)


def _meta_attention(q, k, v, q_col0, k_col0, v_col0, dq, name):
    return pl.pallas_call(
        _meta_attn_kernel,
        grid=(H,),
        in_specs=[pl.BlockSpec((NM, dq), lambda h: (0, q_col0 + h)),
                  pl.BlockSpec((NM, dq), lambda h: (0, k_col0 + h)),
                  pl.BlockSpec((NM, HD), lambda h: (0, v_col0 + h))],
        out_specs=pl.BlockSpec((NM, HD), lambda h: (0, h)),
        out_shape=jax.ShapeDtypeStruct((NM, H * HD), BF16),
        compiler_params=_params(("parallel",), 32),
        name=name,
    )(q, k, v)


def _attn_out_kernel(om_ref, of_ref, w_ref, h_ref, g_ref, b_ref, o_ref):
    m = _dot(om_ref[...], w_ref[0:H * HD, :]) + _dot(of_ref[...], w_ref[H * HD:, :])
    o_ref[...] = _layer_norm(ALPHA * h_ref[...] + m, g_ref[...], b_ref[...])


def _attn_out(om, of, w, h, g, b, tm):
    m = h.shape[0]
    row = lambda w_: pl.BlockSpec((tm, w_), lambda i: (i, 0))
    return pl.pallas_call(
        _attn_out_kernel,
        grid=(m // tm,),
        in_specs=[row(H * HD), row(H * HD), _full(w.shape), row(D), _full(g.shape), _full(b.shape)],
        out_specs=row(D),
        out_shape=jax.ShapeDtypeStruct((m, D), F32),
        compiler_params=_params(("parallel",), 48),
        name="attn_out_ln",
    )(om, of, w, h, g, b)


def _silu(x):
    return x / (1.0 + jnp.exp(-x))


def _ffn_kernel(h_ref, wg_ref, wu_ref, wd_ref, g_ref, b_ref, o_ref, xb_scr, acc_scr):
    f = pl.program_id(1)

    @pl.when(f == 0)
    def _():
        xb_scr[...] = h_ref[...].astype(BF16)
        acc_scr[...] = jnp.zeros(acc_scr.shape, F32)

    xb = xb_scr[...]
    hid = _silu(_dot(xb, wg_ref[...])) * _dot(xb, wu_ref[...])
    acc_scr[...] += _dot(hid.astype(BF16), wd_ref[...])

    @pl.when(f == pl.num_programs(1) - 1)
    def _():
        o_ref[...] = _layer_norm(ALPHA * h_ref[...] + acc_scr[...], g_ref[...], b_ref[...])


def _ffn(h, wg, wu, wd, g, b, tm):
    m = h.shape[0]
    return pl.pallas_call(
        _ffn_kernel,
        grid=(m // tm, FD // FFN_TF),
        in_specs=[pl.BlockSpec((tm, D), lambda i, f: (i, 0)),
                  pl.BlockSpec((D, FFN_TF), lambda i, f: (0, f)),
                  pl.BlockSpec((D, FFN_TF), lambda i, f: (0, f)),
                  pl.BlockSpec((FFN_TF, D), lambda i, f: (f, 0)),
                  pl.BlockSpec((1, D), lambda i, f: (0, 0)),
                  pl.BlockSpec((1, D), lambda i, f: (0, 0))],
        out_specs=pl.BlockSpec((tm, D), lambda i, f: (i, 0)),
        out_shape=jax.ShapeDtypeStruct((m, D), F32),
        scratch_shapes=[pltpu.VMEM((tm, D), BF16), pltpu.VMEM((tm, D), F32)],
        compiler_params=_params(("parallel", "arbitrary"), 52),
        name="ffn_ln",
    )(h, wg, wu, wd, g, b)


def _pool_kernel(hp_ref, prev_ref, meta_ref, h_ref, wg_ref, sc_ref, wo_ref, g_ref, b_ref, o_ref,
                 ext_scr, z_scr):
    tm = hp_ref.shape[0]
    halo = jnp.where(pl.program_id(0) == 0, meta_ref[...], prev_ref[...])
    ext_scr[0:NM, :] = halo
    ext_scr[NM:NM + tm, :] = hp_ref[...]
    for gi, w in enumerate(POOL_W):
        cols = slice(PG * gi, PG * (gi + 1))
        cur = ext_scr[NM:NM + tm, cols]
        tot = cur
        for j in range(1, w):
            tot = tot + ext_scr[NM - j:NM - j + tm, cols]
        diff = tot * (1.0 / w) - cur
        y = _dot(diff.astype(BF16), wg_ref[gi]) * sc_ref[:, cols]
        z_scr[:, cols] = y.astype(BF16)
    m = _dot(z_scr[...], wo_ref[...])
    o_ref[...] = _layer_norm(ALPHA * h_ref[...] + m, g_ref[...], b_ref[...])


def _pool_mixer(hp, hp_meta, h, wgroup, scale, wout, g, b):
    tm = POOL_TM
    row = pl.BlockSpec((tm, D), lambda i: (i, 0))
    prev = pl.BlockSpec((NM, D), lambda i: (jnp.maximum(i * (tm // NM) - 1, 0), 0))
    return pl.pallas_call(
        _pool_kernel,
        grid=(T // tm,),
        in_specs=[row, prev, _full(hp_meta.shape), row, _full(wgroup.shape), _full(scale.shape),
                  _full(wout.shape), _full(g.shape), _full(b.shape)],
        out_specs=row,
        out_shape=jax.ShapeDtypeStruct((T, D), F32),
        scratch_shapes=[pltpu.VMEM((NM + tm, D), F32), pltpu.VMEM((tm, D), BF16)],
        compiler_params=_params(("parallel",), 48),
        name="pool_ln",
    )(hp, hp, hp_meta, h, wgroup, scale, wout, g, b)


def _router_kernel(h_ref, w_ref, b_ref, info_ref, cnt_ref, carry_scr):
    tm = h_ref.shape[0]

    @pl.when(pl.program_id(0) == 0)
    def _():
        carry_scr[...] = jnp.zeros(carry_scr.shape, F32)

    logits = jnp.dot(h_ref[...], w_ref[...], precision=lax.Precision.HIGHEST,
                     preferred_element_type=F32) + b_ref[...]
    lane = lax.broadcasted_iota(jnp.int32, logits.shape, 1).astype(F32)
    nolane = float(LANES_V7X)
    m1 = jnp.max(logits, axis=1, keepdims=True)
    i1 = jnp.min(jnp.where(logits == m1, lane, nolane), axis=1, keepdims=True)
    rest = jnp.where(lane == i1, -jnp.inf, logits)
    m2 = jnp.max(rest, axis=1, keepdims=True)
    i2 = jnp.min(jnp.where(rest == m2, lane, nolane), axis=1, keepdims=True)
    e = jnp.exp(m2 - m1)
    g1 = 1.0 / (1.0 + e)
    g2 = e / (1.0 + e)
    hit1 = lane == i1
    hit2 = lane == i2
    member = jnp.logical_or(hit1, hit2).astype(BF16)
    earlier = (lax.broadcasted_iota(jnp.int32, (tm, tm), 1)
               < lax.broadcasted_iota(jnp.int32, (tm, tm), 0)).astype(BF16)
    before = _dot(earlier, member) + carry_scr[0:1, :]
    r1 = jnp.sum(jnp.where(hit1, before, 0.0), axis=1, keepdims=True)
    r2 = jnp.sum(jnp.where(hit2, before, 0.0), axis=1, keepdims=True)
    carry = carry_scr[0:1, :] + jnp.sum(member.astype(F32), axis=0, keepdims=True)
    carry_scr[...] = jnp.broadcast_to(carry, carry_scr.shape)
    cnt_ref[...] = jnp.broadcast_to(carry, cnt_ref.shape)
    info = jnp.where(lane == 0, g1, 0.0)
    info = jnp.where(lane == 1, g2, info)
    info = jnp.where(lane == 2, i1, info)
    info = jnp.where(lane == 3, i2, info)
    info = jnp.where(lane == 4, r1, info)
    info = jnp.where(lane == 5, r2, info)
    info_ref[...] = info


def _router(h, w_pad, b_pad):
    tm = TM
    return pl.pallas_call(
        _router_kernel,
        grid=(T // tm,),
        in_specs=[pl.BlockSpec((tm, D), lambda i: (i, 0)), _full(w_pad.shape), _full(b_pad.shape)],
        out_specs=[pl.BlockSpec((tm, LANES_V7X), lambda i: (i, 0)), _full((8, LANES_V7X))],
        out_shape=[jax.ShapeDtypeStruct((T, LANES_V7X), F32),
                   jax.ShapeDtypeStruct((8, LANES_V7X), F32)],
        scratch_shapes=[pltpu.VMEM((8, LANES_V7X), F32)],
        compiler_params=_params(("arbitrary",), 32),
        name="router",
    )(h, w_pad, b_pad)


def _moe_row_copy(h_hbm, x_scr, sem, tok, r):
    return pltpu.make_async_copy(h_hbm.at[pl.ds(tok, 1), :], x_scr.at[pl.ds(r, 1), :], sem)


def _moe_out_copy(acc_scr, y_hbm, sem, row0, u):
    off = pl.multiple_of(u * MOE_UNIT, MOE_UNIT)
    dst = pl.multiple_of(row0 + off, MOE_UNIT)
    return pltpu.make_async_copy(acc_scr.at[pl.ds(off, MOE_UNIT), :],
                                 y_hbm.at[pl.ds(dst, MOE_UNIT), :], sem)


def _moe_kernel(te_ref, row0_ref, nunit_ref, used_ref, pad_ref, dest_ref,
                h_hbm, wg_ref, wu_ref, wd_ref, y_hbm,
                slot_tok, x_scr, acc_scr, gsem, osem):
    t = pl.program_id(0)
    f = pl.program_id(1)
    nf = pl.num_programs(1)

    @pl.when(jnp.logical_and(t == 0, f == 0))
    def _():
        for e in range(NE):
            def clear(s, c):
                slot_tok[s] = 0
                return c
            lax.fori_loop(pad_ref[2 * e], pad_ref[2 * e + 1], clear, 0)

        def place(tok, c):
            slot_tok[dest_ref[2 * tok]] = tok
            slot_tok[dest_ref[2 * tok + 1]] = tok
            return c
        lax.fori_loop(0, T, place, 0, unroll=8)

    @pl.when(t < used_ref[0])
    def _():
        row0 = row0_ref[t]
        nunit = nunit_ref[t]
        nrows = nunit * MOE_UNIT

        @pl.when(f == 0)
        def _():
            def issue_unit(u, c):
                def issue(r, cc):
                    row = u * MOE_UNIT + r
                    _moe_row_copy(h_hbm, x_scr, gsem, slot_tok[row0 + row], row).start()
                    return cc
                return lax.fori_loop(0, MOE_UNIT, issue, c, unroll=8)
            lax.fori_loop(0, nunit, issue_unit, 0)

            def clear_acc(u, c):
                off = pl.multiple_of(u * MOE_UNIT, MOE_UNIT)
                acc_scr[pl.ds(off, MOE_UNIT), :] = jnp.zeros((MOE_UNIT, D), F32)
                return c
            lax.fori_loop(0, nunit, clear_acc, 0)

            def drain_unit(u, c):
                def drain(r, cc):
                    _moe_row_copy(h_hbm, x_scr, gsem, 0, u * MOE_UNIT + r).wait()
                    return cc
                return lax.fori_loop(0, MOE_UNIT, drain, c, unroll=8)
            lax.fori_loop(0, nunit, drain_unit, 0)

        def chunk(off, rows):
            x = x_scr[pl.ds(off, rows), :]
            hid = _silu(_dot(x, wg_ref[...])) * _dot(x, wu_ref[...])
            acc_scr[pl.ds(off, rows), :] += _dot(hid, wd_ref[...])

        def quad(c, carry):
            chunk(pl.multiple_of(c * 4 * MOE_UNIT, 4 * MOE_UNIT), 4 * MOE_UNIT)
            return carry
        lax.fori_loop(0, nunit // 4, quad, 0)

        @pl.when(nunit % 4 >= 2)
        def _():
            chunk(pl.multiple_of((nunit // 4) * 4 * MOE_UNIT, 2 * MOE_UNIT), 2 * MOE_UNIT)

        @pl.when(nunit % 2 == 1)
        def _():
            chunk(pl.multiple_of((nunit - 1) * MOE_UNIT, MOE_UNIT), MOE_UNIT)

        @pl.when(f == nf - 1)
        def _():
            def start(u, carry):
                _moe_out_copy(acc_scr, y_hbm, osem, row0, u).start()
                return carry
            lax.fori_loop(0, nunit, start, 0)

            def finish(u, carry):
                _moe_out_copy(acc_scr, y_hbm, osem, row0, u).wait()
                return carry
            lax.fori_loop(0, nunit, finish, 0)

    @pl.when(jnp.logical_and(t == pl.num_programs(0) - 1, f == nf - 1))
    def _():
        acc_scr[0:MOE_UNIT, :] = jnp.zeros((MOE_UNIT, D), F32)

        def fill(u, carry):
            cp = _moe_out_copy(acc_scr, y_hbm, osem, u * MOE_UNIT, 0)
            cp.start()
            cp.wait()
            return carry
        lax.fori_loop(used_ref[1], MOE_CAP // MOE_UNIT, fill, 0)


def _moe_experts(h, wg, wu, wd, tile_e, tile_row0, tile_nunit, used, pad, dest):
    nf = FE // MOE_TF

    def fidx(t, f, us):
        return jnp.where(t < us[0], f, nf - 1)

    grid_spec = pltpu.PrefetchScalarGridSpec(
        num_scalar_prefetch=6,
        grid=(MOE_NT, nf),
        in_specs=[pl.BlockSpec(memory_space=pl.ANY),
                  pl.BlockSpec((None, D, MOE_TF), lambda t, f, te, r0, nu, us, pa, de: (te[t], 0, fidx(t, f, us))),
                  pl.BlockSpec((None, D, MOE_TF), lambda t, f, te, r0, nu, us, pa, de: (te[t], 0, fidx(t, f, us))),
                  pl.BlockSpec((None, MOE_TF, D), lambda t, f, te, r0, nu, us, pa, de: (te[t], fidx(t, f, us), 0))],
        out_specs=pl.BlockSpec(memory_space=pl.ANY),
        scratch_shapes=[pltpu.SMEM((MOE_CAP,), jnp.int32),
                        pltpu.VMEM((MOE_TMAX, D), F32),
                        pltpu.VMEM((MOE_TMAX, D), F32),
                        pltpu.SemaphoreType.DMA,
                        pltpu.SemaphoreType.DMA],
    )
    return pl.pallas_call(
        _moe_kernel,
        grid_spec=grid_spec,
        out_shape=jax.ShapeDtypeStruct((MOE_CAP, D), F32),
        compiler_params=_params(("arbitrary", "arbitrary"), 58),
        name="moe_experts",
    )(tile_e, tile_row0, tile_nunit, used, pad, dest, h, wg, wu, wd)


def _cmb_copy(y_hbm, ybuf, sem, slot, k, r):
    return pltpu.make_async_copy(y_hbm.at[pl.ds(slot, 1), :], ybuf.at[k, pl.ds(r, 1), :], sem)


def _combine_kernel(dest_ref, h_ref, info_ref, y_hbm, g_ref, b_ref, o_ref, ybuf, sem):
    tm = h_ref.shape[0]
    base = pl.program_id(0) * tm

    def issue(r, c):
        _cmb_copy(y_hbm, ybuf, sem, dest_ref[2 * (base + r)], 0, r).start()
        _cmb_copy(y_hbm, ybuf, sem, dest_ref[2 * (base + r) + 1], 1, r).start()
        return c
    lax.fori_loop(0, tm, issue, 0)

    def drain(r, c):
        _cmb_copy(y_hbm, ybuf, sem, 0, 0, r).wait()
        _cmb_copy(y_hbm, ybuf, sem, 0, 1, r).wait()
        return c
    lax.fori_loop(0, tm, drain, 0)
    info = info_ref[...]
    moe = info[:, 0:1] * ybuf[0] + info[:, 1:2] * ybuf[1]
    o_ref[...] = _layer_norm(ALPHA * h_ref[...] + moe, g_ref[...], b_ref[...])


def _combine(dest, h, info, y_rows, g, b):
    tm = CMB_TM
    grid_spec = pltpu.PrefetchScalarGridSpec(
        num_scalar_prefetch=1,
        grid=(T // tm,),
        in_specs=[pl.BlockSpec((tm, D), lambda i, de: (i, 0)),
                  pl.BlockSpec((tm, LANES_V7X), lambda i, de: (i, 0)),
                  pl.BlockSpec(memory_space=pl.ANY),
                  pl.BlockSpec((1, D), lambda i, de: (0, 0)),
                  pl.BlockSpec((1, D), lambda i, de: (0, 0))],
        out_specs=pl.BlockSpec((tm, D), lambda i, de: (i, 0)),
        scratch_shapes=[pltpu.VMEM((2, tm, D), F32), pltpu.SemaphoreType.DMA],
    )
    return pl.pallas_call(
        _combine_kernel,
        grid_spec=grid_spec,
        out_shape=jax.ShapeDtypeStruct((T, D), F32),
        compiler_params=_params(("arbitrary",), 32),
        name="moe_combine_ln",
    )(dest, h, info, y_rows, g, b)


def _rope_tables(pos):
    inv_freq = THETA ** (-jnp.arange(0, ROPE, 2, dtype=F32) / ROPE)
    ang = pos[:, None] * inv_freq[None, :]
    cos = jnp.cos(ang).astype(F32)
    sin = jnp.sin(ang).astype(F32)
    zero = jnp.zeros((pos.shape[0], LANES_V7X - ROPE), F32)
    return (jnp.concatenate([cos, cos, zero], axis=1),
            jnp.concatenate([-sin, sin, zero], axis=1))


def _expert_tiles(counts):
    units = (counts + MOE_UNIT - 1) // MOE_UNIT
    seg_unit = jnp.cumsum(units) - units
    ntile = (units + MOE_TUNITS - 1) // MOE_TUNITS
    tile_end = jnp.cumsum(ntile)
    n_used = tile_end[-1]
    tid = jnp.arange(MOE_NT, dtype=jnp.int32)
    e = jnp.minimum(jnp.sum(tid[:, None] >= tile_end[None, :], axis=1), NE - 1).astype(jnp.int32)
    local = tid - (tile_end - ntile)[e]
    nt = jnp.maximum(ntile[e], 1)
    lo = (local * units[e]) // nt
    hi = ((local + 1) * units[e]) // nt
    used = tid < n_used
    last = jnp.maximum(n_used - 1, 0)
    e = jnp.where(used, e, e[last])
    row0 = jnp.where(used, (seg_unit[e] + lo) * MOE_UNIT, 0)
    nunit = jnp.where(used, hi - lo, 0)
    used_info = jnp.stack([n_used, jnp.sum(units)]).astype(jnp.int32)
    seg_row = seg_unit * MOE_UNIT
    pad = jnp.stack([seg_row + counts, seg_row + units * MOE_UNIT], axis=1).reshape(2 * NE)
    return (e.astype(jnp.int32), row0.astype(jnp.int32), nunit.astype(jnp.int32),
            used_info, pad.astype(jnp.int32), seg_row.astype(jnp.int32))


def kernel(x, meta_tokens, attn_w_in, fox_forget_bias, mla_q_norm, mla_kv_norm, mla_w_uq, mla_w_ukv, attn_w_out, pool_w_in, pool_w_group, pool_scale, pool_w_out, ffn_w_gate, ffn_w_up, ffn_w_down, moe_w_router, moe_b_router, moe_w_gate, moe_w_up, moe_w_down, ln_mix_g, ln_mix_b, ln_ffn_g, ln_ffn_b):
    assert x.shape == (1, T, D) and meta_tokens.shape == (NM, D)
    xt = x[0]
    meta = meta_tokens.astype(x.dtype)

    w_in = attn_w_in[0]
    o_kr, o_qf, o_f = QL + KVL, QL + KVL + ROPE, QL + KVL + ROPE + 3 * H * HD
    kr1, kr2 = w_in[:, o_kr:o_kr + ROPE // 2], w_in[:, o_kr + ROPE // 2:o_qf]
    zpad = lambda n: jnp.zeros((D, n), F32)
    w_lat = jnp.concatenate(
        [w_in[:, :o_kr], kr1, kr2, zpad(LANES_V7X - ROPE), kr2, kr1, zpad(LANES_V7X - ROPE),
         w_in[:, o_f:o_f + H], zpad(LANES_V7X - H)], axis=1).astype(BF16)
    w_fox = w_in[:, o_qf:o_f].astype(BF16)
    wuq = mla_w_uq[0].reshape(QL, H, NOPE + ROPE)
    uq_n, uq_1, uq_2 = wuq[..., :NOPE], wuq[..., NOPE:NOPE + ROPE // 2], wuq[..., NOPE + ROPE // 2:]
    zq = jnp.zeros((QL, H, LANES_V7X - ROPE), F32)
    w_uq = jnp.concatenate([uq_n, uq_1, uq_2, zq], axis=-1).reshape(QL, H * DQ_MLA).astype(BF16)
    w_uqs = jnp.concatenate([uq_2, uq_1, zq], axis=-1).reshape(QL, H * HD).astype(BF16)
    wukv = mla_w_ukv[0].reshape(KVL, H, NOPE + HD)
    w_ukv = jnp.concatenate([wukv[..., :NOPE].reshape(KVL, H * NOPE),
                             wukv[..., NOPE:].reshape(KVL, H * HD)], axis=1).astype(BF16)
    w_ao = attn_w_out[0].astype(BF16)
    w_fg, w_fu, w_fd = ffn_w_gate[0].astype(BF16), ffn_w_up[0].astype(BF16), ffn_w_down[0].astype(BF16)
    w_pi, w_pg, w_po = pool_w_in[0].astype(BF16), pool_w_group[0].astype(BF16), pool_w_out[0].astype(BF16)
    qn, kvn = mla_q_norm[0][None, :], mla_kv_norm[0][None, :]
    row = lambda v: v[None, :]

    ctab_t, stab_t = _rope_tables(jnp.arange(NM, NM + T, dtype=F32))
    ctab_m, stab_m = _rope_tables(jnp.arange(NM, dtype=F32))

    fox_scale = HD ** -0.5 * LOG2E
    q_t, k_t, v_t, fl_t = _mla_proj(xt, w_lat, qn, kvn, w_uq, w_uqs, w_ukv, ctab_t, stab_t, TM)
    q_m, k_m, v_m, _ = _mla_proj(meta, w_lat, qn, kvn, w_uq, w_uqs, w_ukv, ctab_m, stab_m, NM)
    fox_t = _matmul(xt, w_fox, TM, H * HD, BF16, scaled_blocks=1, scale=fox_scale, name="fox_proj")
    fox_m = _matmul(meta, w_fox, NM, H * HD, BF16, scaled_blocks=1, scale=fox_scale, name="fox_proj_meta")

    fl = fl_t[:, :H].T.reshape(H, T // LANES_V7X, LANES_V7X)
    bias_b = jnp.broadcast_to(fox_forget_bias[0][:, None, None], (H, 1, LANES_V7X))
    parts = _forget_cumsum(fl, bias_b).reshape(H, 6, T)
    pos = parts[:, 0:3].transpose(2, 0, 1)
    neg = parts[:, 3:6].transpose(2, 0, 1)
    lanes = lambda pieces: jnp.concatenate(pieces, axis=2).reshape(-1, H * AUG_STRIDE)
    cst = lambda rows, n, v: jnp.full((rows, H, n), v, BF16)
    aug = (lanes([cst(T, 3, 1.0), pos, cst(T, AUG_STRIDE - 6, 0.0)]),
           lanes([neg, cst(T, AUG_STRIDE - 3, 0.0)]),
           lanes([cst(NM, 3, 0.0), cst(NM, 3, -1.0), cst(NM, AUG_STRIDE - 6, 0.0)]))

    to_vt = lambda v: v.reshape(T // TQ, TQ, H, HD).transpose(2, 0, 3, 1)
    to_vmt = lambda v: v.reshape(NM, H, HD).transpose(1, 2, 0)
    o_mla = _attention(q_t, k_t, to_vt(v_t), k_m, to_vmt(v_m), 0, DQ_MLA, name="mla_attn")
    o_fox = _attention(fox_t, fox_t, to_vt(fox_t[:, 2 * H * HD:]), fox_m, to_vmt(fox_m[:, 2 * H * HD:]),
                       H, HD, aug, name="fox_attn")
    om_mla = _meta_attention(q_m, k_m, v_m, 0, 0, 0, DQ_MLA, "mla_attn_meta")
    om_fox = _meta_attention(fox_m, fox_m, fox_m, 0, H, 2 * H, HD, "fox_attn_meta")

    g0, b0, g1, b1 = row(ln_mix_g[0]), row(ln_mix_b[0]), row(ln_ffn_g[0]), row(ln_ffn_b[0])
    h_t = _attn_out(o_mla, o_fox, w_ao, xt, g0, b0, TM)
    h_m = _attn_out(om_mla, om_fox, w_ao, meta, g0, b0, NM)
    h_t = _ffn(h_t, w_fg, w_fu, w_fd, g1, b1, TM)
    h_m = _ffn(h_m, w_fg, w_fu, w_fd, g1, b1, NM)

    hp_t = _matmul(h_t, w_pi, TM, 1024, F32, name="pool_in")
    hp_m = _matmul(h_m, w_pi, NM, 1024, F32, name="pool_in_meta")
    h_t = _pool_mixer(hp_t, hp_m, h_t, w_pg, row(pool_scale[0]), w_po,
                      row(ln_mix_g[1]), row(ln_mix_b[1]))

    w_r = jnp.concatenate([moe_w_router[0], jnp.zeros((D, LANES_V7X - NE), F32)], axis=1)
    b_r = jnp.concatenate([moe_b_router[0], jnp.full((LANES_V7X - NE,), NEG, F32)])[None, :]
    info, cnt = _router(h_t, w_r, b_r)
    counts = cnt[0, :NE].astype(jnp.int32)
    tile_e, tile_row0, tile_nunit, used, pad, seg_row = _expert_tiles(counts)
    experts = info[:, 2:4].astype(jnp.int32)
    ranks = info[:, 4:6].astype(jnp.int32)
    dest = (seg_row[experts] + ranks).reshape(2 * T)
    y_rows = _moe_experts(h_t, moe_w_gate[0], moe_w_up[0], moe_w_down[0],
                          tile_e, tile_row0, tile_nunit, used, pad, dest)
    out = _combine(dest, h_t, info, y_rows, row(ln_ffn_g[1]), row(ln_ffn_b[1]))
    return out[None]
```

```python
import functools
import math

import jax
import jax.numpy as jnp
import numpy as np
from jax import lax
from jax.experimental import pallas as pl
from jax.experimental.pallas import tpu as pltpu

D = 2048
T = 8192
NM = 16
H = 8
QL = 512
KVL = 512
NOPE = 128
ROPE = 64
HD = 128
DQ_MLA = 256
THETA = 10000.0
POOL_W = (2, 4, 8, 16)
PG = D // 4
FD = 5632
NE = 8
FE = 7168
ALPHA = 4.0 ** 0.25
LN_EPS = 1e-5
RMS_EPS = 1e-6
NEG = -1e30
LOG2E = 1.4426950408889634

F32 = jnp.float32
BF16 = jnp.bfloat16

LANES_V7X = 128
VMEM_BYTES_V7X = 64 * 1024 * 1024
MIB = 1024 * 1024

TM = 512
TQ = 512
FFN_TF = 512
POOL_TM = 256
MOE_UNIT = 128
MOE_TUNITS = 10
MOE_TMAX = MOE_TUNITS * MOE_UNIT
MOE_TF = 512
MOE_CAP = 2 * T + NE * MOE_UNIT
MOE_NT = MOE_CAP // MOE_TMAX + NE + 1
CMB_TM = 256


def _params(sem, vmem_mib):
    return pltpu.CompilerParams(dimension_semantics=sem, vmem_limit_bytes=vmem_mib * MIB)


def _full(shape):
    n = len(shape)
    return pl.BlockSpec(shape, lambda *_: (0,) * n)


def _resident(shape):
    n = len(shape)
    return pl.BlockSpec(shape, lambda *_: (0,) * n, pipeline_mode=pl.Buffered(1))


def _layer_norm(x, g, b):
    mu = jnp.mean(x, axis=-1, keepdims=True)
    xc = x - mu
    var = jnp.mean(xc * xc, axis=-1, keepdims=True)
    return xc * lax.rsqrt(var + LN_EPS) * g + b


def _rms_norm(x, g):
    return x * lax.rsqrt(jnp.mean(x * x, axis=-1, keepdims=True) + RMS_EPS) * g


def _dot(a, b):
    return jnp.dot(a, b, preferred_element_type=F32)


def _dot_nt(a, b):
    return lax.dot_general(a, b, (((1,), (1,)), ((), ())), preferred_element_type=F32)


def _mla_proj_kernel(x_ref, wlat_ref, qn_ref, kvn_ref, wuq_ref, wuqs_ref, wukv_ref, c_ref, s_ref,
                     q_ref, k_ref, v_ref, fl_ref, *, qscale, transposed_v):
    xb = x_ref[...].astype(BF16)
    lat = _dot(xb, wlat_ref[...])
    cq = lat[:, 0:QL]
    ckv = lat[:, QL:QL + KVL]
    kr = lat[:, 1024:1152]
    krs = lat[:, 1152:1280]
    fl_ref[...] = lat[:, 1280:1408]
    cos = c_ref[...]
    sin = s_ref[...]
    qn = _rms_norm(cq, qn_ref[...]).astype(BF16)
    qpre = _dot(qn, wuq_ref[...])
    qsw = _dot(qn, wuqs_ref[...])
    for h in range(H):
        nope = qpre[:, DQ_MLA * h:DQ_MLA * h + NOPE]
        rot = (qpre[:, DQ_MLA * h + NOPE:DQ_MLA * (h + 1)] * cos
               + qsw[:, HD * h:HD * (h + 1)] * sin)
        q_ref[:, DQ_MLA * h:DQ_MLA * h + NOPE] = (nope * qscale).astype(BF16)
        q_ref[:, DQ_MLA * h + NOPE:DQ_MLA * (h + 1)] = (rot * qscale).astype(BF16)
    kvn = _rms_norm(ckv, kvn_ref[...]).astype(BF16)
    kv = _dot(kvn, wukv_ref[...])
    krot = (kr * cos + krs * sin).astype(BF16)
    for h in range(H):
        k_ref[:, DQ_MLA * h:DQ_MLA * h + NOPE] = kv[:, HD * h:HD * (h + 1)].astype(BF16)
        k_ref[:, DQ_MLA * h + NOPE:DQ_MLA * (h + 1)] = krot
    if transposed_v:
        for h in range(H):
            v_ref[h, 0] = kv[:, (H + h) * HD:(H + h + 1) * HD].T.astype(BF16)
    else:
        v_ref[...] = kv[:, H * HD:].astype(BF16)


def _mla_proj(x, wlat, qn, kvn, wuq, wuqs, wukv, ctab, stab, tm):
    m = x.shape[0]
    qscale = (NOPE + ROPE) ** -0.5 * LOG2E
    row = lambda w: pl.BlockSpec((tm, w), lambda i: (i, 0))
    transposed_v = tm == TQ
    if transposed_v:
        v_spec = pl.BlockSpec((H, 1, HD, TQ), lambda i: (0, i, 0, 0))
        v_shape = jax.ShapeDtypeStruct((H, m // TQ, HD, TQ), BF16)
    else:
        v_spec, v_shape = row(H * HD), jax.ShapeDtypeStruct((m, H * HD), BF16)
    return pl.pallas_call(
        functools.partial(_mla_proj_kernel, qscale=qscale, transposed_v=transposed_v),
        grid=(m // tm,),
        in_specs=[row(D), _full(wlat.shape), _full(qn.shape), _full(kvn.shape), _full(wuq.shape),
                  _full(wuqs.shape), _full(wukv.shape), row(LANES_V7X), row(LANES_V7X)],
        out_specs=[row(H * DQ_MLA), row(H * DQ_MLA), v_spec, row(LANES_V7X)],
        out_shape=[jax.ShapeDtypeStruct((m, H * DQ_MLA), BF16),
                   jax.ShapeDtypeStruct((m, H * DQ_MLA), BF16),
                   v_shape,
                   jax.ShapeDtypeStruct((m, LANES_V7X), F32)],
        compiler_params=_params(("parallel",), 56),
        name="mla_proj",
    )(x, wlat, qn, kvn, wuq, wuqs, wukv, ctab, stab)


def _mm_kernel(x_ref, w_ref, o_ref, *, scaled_blocks, scale):
    y = _dot(x_ref[...], w_ref[...])
    if scaled_blocks:
        y = jnp.where(pl.program_id(1) < scaled_blocks, y * scale, y)
    o_ref[...] = y.astype(o_ref.dtype)


def _matmul(x, w, tm, tn, out_dtype, scaled_blocks=0, scale=1.0, name="matmul"):
    m, k = x.shape
    n = w.shape[1]
    return pl.pallas_call(
        functools.partial(_mm_kernel, scaled_blocks=scaled_blocks, scale=scale),
        grid=(m // tm, n // tn),
        in_specs=[pl.BlockSpec((tm, k), lambda i, j: (i, 0)),
                  _resident((k, n)) if tn == n else pl.BlockSpec((k, tn), lambda i, j: (0, j))],
        out_specs=pl.BlockSpec((tm, tn), lambda i, j: (i, j)),
        out_shape=jax.ShapeDtypeStruct((m, n), out_dtype),
        compiler_params=_params(("parallel", "parallel"), 48),
        name=name,
    )(x, w)


def _fox_proj_kernel(x_ref, w_ref, qk_ref, vt_ref, *, scale):
    j = pl.program_id(1)
    y = _dot(x_ref[...].astype(BF16), w_ref[j])

    @pl.when(j == 0)
    def _():
        qk_ref[...] = (y * scale).astype(BF16)

    @pl.when(j == 1)
    def _():
        qk_ref[...] = y.astype(BF16)

    @pl.when(j == 2)
    def _():
        for h in range(H):
            vt_ref[h, 0] = y[:, h * HD:(h + 1) * HD].T.astype(BF16)


def _fox_proj(x, w, scale):
    n = H * HD
    return pl.pallas_call(
        functools.partial(_fox_proj_kernel, scale=scale),
        grid=(T // TQ, 3),
        in_specs=[pl.BlockSpec((TQ, D), lambda i, j: (i, 0)),
                  _resident((3, D, n))],
        out_specs=[pl.BlockSpec((TQ, n), lambda i, j: (i, jnp.minimum(j, 1))),
                   pl.BlockSpec((H, 1, HD, TQ), lambda i, j: (0, i, 0, 0))],
        out_shape=[jax.ShapeDtypeStruct((T, 2 * n), BF16),
                   jax.ShapeDtypeStruct((H, T // TQ, HD, TQ), BF16)],
        compiler_params=_params(("parallel", "arbitrary"), 48),
        name="fox_proj",
    )(x, w)


def _split3(a):
    hi = a.astype(BF16)
    r = a - hi.astype(F32)
    mid = r.astype(BF16)
    lo = (r - mid.astype(F32)).astype(BF16)
    return hi, mid, lo


def _forget_cumsum_kernel(fl_ref, b_ref, o_ref):
    z = fl_ref[...] + b_ref[...]
    logf = jnp.minimum(z, 0.0) - jnp.log(1.0 + jnp.exp(-jnp.abs(z)))
    n = LANES_V7X
    ri = lax.broadcasted_iota(jnp.int32, (n, n), 0)
    ci = lax.broadcasted_iota(jnp.int32, (n, n), 1)
    upper = (ri <= ci).astype(BF16)
    ones = jnp.ones((n, n), BF16)
    within = jnp.zeros(logf.shape, F32)
    total = jnp.zeros(logf.shape, F32)
    for part in _split3(logf):
        within += _dot(part, upper)
        total += _dot(part, ones)
    r = logf.shape[0]
    lower = (lax.broadcasted_iota(jnp.int32, (r, r), 1)
             < lax.broadcasted_iota(jnp.int32, (r, r), 0)).astype(BF16)
    before = jnp.zeros(logf.shape, F32)
    for part in _split3(total):
        before += _dot(lower, part)
    c2 = (within + before) * LOG2E
    for p, part in enumerate(_split3(c2)):
        o_ref[p] = part
        o_ref[3 + p] = -part


def _forget_cumsum(fl_t, bias_b):
    r = T // LANES_V7X
    return pl.pallas_call(
        _forget_cumsum_kernel,
        grid=(H,),
        in_specs=[pl.BlockSpec((None, r, LANES_V7X), lambda h: (h, 0, 0)),
                  pl.BlockSpec((None, 1, LANES_V7X), lambda h: (h, 0, 0))],
        out_specs=pl.BlockSpec((None, 6, r, LANES_V7X), lambda h: (h, 0, 0, 0)),
        out_shape=jax.ShapeDtypeStruct((H, 6, r, LANES_V7X), BF16),
        compiler_params=_params(("parallel",), 32),
        name="forget_cumsum",
    )(fl_t, bias_b)


ATT_NH = 2
AUG_STRIDE = 16


def _attn_kernel(*refs, fox, dq):
    if fox:
        (q_ref, k_ref, vt_ref, km_ref, vmt_ref, qa_ref, ka_ref, kam_ref, o_ref,
         m_scr, l_scr, acc_scr, s0_scr, s1_scr, mb0_scr, mb1_scr, qf_scr) = refs
    else:
        (q_ref, k_ref, vt_ref, km_ref, vmt_ref, o_ref,
         m_scr, l_scr, acc_scr, s0_scr, s1_scr, mb0_scr, mb1_scr) = refs
    hp = pl.program_id(0)
    i = pl.program_id(1)
    m_scr[...] = jnp.full(m_scr.shape, -jnp.inf, F32)
    l_scr[...] = jnp.zeros(l_scr.shape, F32)
    acc_scr[...] = jnp.zeros(acc_scr.shape, F32)
    if fox:
        lane = lax.broadcasted_iota(jnp.int32, qa_ref.shape, 1)
        for g in range(ATT_NH):
            lo = (hp * ATT_NH + g) * AUG_STRIDE
            own = jnp.logical_and(lane >= lo, lane < lo + AUG_STRIDE)
            qf_scr[g, :, 0:dq] = q_ref[:, g * dq:(g + 1) * dq]
            qf_scr[g, :, dq:2 * dq] = jnp.where(own, qa_ref[...], jnp.zeros_like(qa_ref))

    def query(g):
        return qf_scr[g] if fox else q_ref[:, g * dq:(g + 1) * dq]

    def update(g, st, mb, vt):
        m_prev = m_scr[g]
        m_new = jnp.maximum(m_prev, mb)
        alpha = jnp.exp2(m_prev - m_new)
        p = jnp.exp2(st - m_new)
        l_scr[g] = alpha * l_scr[g] + jnp.sum(p, axis=0, keepdims=True)
        acc_scr[g] = alpha * acc_scr[g] + _dot(vt, p.astype(BF16))
        m_scr[g] = m_new

    def stage1(slot, j):
        s_ref, mb_ref = slots[slot]
        off = pl.multiple_of(j * TQ, TQ)
        for g in range(ATT_NH):
            kb = k_ref[pl.ds(off, TQ), g * dq:(g + 1) * dq]
            if fox:
                kb = jnp.concatenate([kb, ka_ref[pl.ds(off, TQ), :]], axis=1)
            st = _dot_nt(kb, query(g))
            s_ref[g] = st
            mb_ref[g] = jnp.max(st, axis=0, keepdims=True)

    def stage2(slot, j, masked=False):
        s_ref, mb_ref = slots[slot]
        for g in range(ATT_NH):
            st = s_ref[g]
            mb = mb_ref[g]
            if masked:
                keys = lax.broadcasted_iota(jnp.int32, st.shape, 0)
                qrys = lax.broadcasted_iota(jnp.int32, st.shape, 1)
                st = jnp.where(keys <= qrys, st, NEG)
                mb = jnp.max(st, axis=0, keepdims=True)
            update(g, st, mb, vt_ref[g, j])

    slots = ((s0_scr, mb0_scr), (s1_scr, mb1_scr))
    stage1(0, 0)

    def body(jj, carry):
        j = 2 * jj
        stage1(1, j + 1)
        stage2(0, j)
        stage1(0, j + 2)
        stage2(1, j + 1)
        return carry

    lax.fori_loop(0, i // 2, body, 0)

    @pl.when(i % 2 == 0)
    def _():
        stage2(0, i, masked=True)

    @pl.when(i % 2 == 1)
    def _():
        stage1(1, i)
        stage2(0, i - 1)
        stage2(1, i, masked=True)

    for g in range(ATT_NH):
        kb = km_ref[:, g * dq:(g + 1) * dq]
        if fox:
            kb = jnp.concatenate([kb, kam_ref[...]], axis=1)
        st = _dot_nt(kb, query(g))
        update(g, st, jnp.max(st, axis=0, keepdims=True), vmt_ref[g])
        o = (acc_scr[g] / l_scr[g]).T
        o_ref[:, g * HD:(g + 1) * HD] = o.astype(o_ref.dtype)


def _attention(q, k, vt, km, vmt, k_col0, dq, aug=None, name="attn"):
    fox = aug is not None
    nh = ATT_NH
    in_specs = [pl.BlockSpec((TQ, nh * dq), lambda h, i: (i, h)),
                pl.BlockSpec((T, nh * dq), lambda h, i: (0, k_col0 // nh + h)),
                pl.BlockSpec((nh, T // TQ, HD, TQ), lambda h, i: (h, 0, 0, 0)),
                pl.BlockSpec((NM, nh * dq), lambda h, i: (0, k_col0 // nh + h)),
                pl.BlockSpec((nh, HD, NM), lambda h, i: (h, 0, 0))]
    args = [q, k, vt, km, vmt]
    scratch = [pltpu.VMEM((nh, 1, TQ), F32), pltpu.VMEM((nh, 1, TQ), F32),
               pltpu.VMEM((nh, HD, TQ), F32),
               pltpu.VMEM((nh, TQ, TQ), F32), pltpu.VMEM((nh, TQ, TQ), F32),
               pltpu.VMEM((nh, 1, TQ), F32), pltpu.VMEM((nh, 1, TQ), F32)]
    if fox:
        in_specs += [pl.BlockSpec((TQ, LANES_V7X), lambda h, i: (i, 0)),
                     _full((T, LANES_V7X)), _full((NM, LANES_V7X))]
        args += list(aug)
        scratch.append(pltpu.VMEM((nh, TQ, 2 * dq), BF16))
    return pl.pallas_call(
        functools.partial(_attn_kernel, fox=fox, dq=dq),
        grid=(H // nh, T // TQ),
        in_specs=in_specs,
        out_specs=pl.BlockSpec((TQ, nh * HD), lambda h, i: (i, h)),
        out_shape=jax.ShapeDtypeStruct((T, H * HD), BF16),
        scratch_shapes=scratch,
        compiler_params=_params(("parallel", "arbitrary"), 56),
        name=name,
    )(*args)


def _meta_attn_kernel(q_ref, k_ref, v_ref, o_ref):
    s = _dot_nt(q_ref[...], k_ref[...])
    rows = lax.broadcasted_iota(jnp.int32, s.shape, 0)
    cols = lax.broadcasted_iota(jnp.int32, s.shape, 1)
    s = jnp.where(cols <= rows, s, NEG)
    p = jnp.exp2(s - jnp.max(s, axis=1, keepdims=True))
    o = _dot(p.astype(BF16), v_ref[...]) / jnp.sum(p, axis=1, keepdims=True)
    o_ref[...] = o.astype(o_ref.dtype)


def _meta_attention(q, k, v, q_col0, k_col0, v_col0, dq, name):
    return pl.pallas_call(
        _meta_attn_kernel,
        grid=(H,),
        in_specs=[pl.BlockSpec((NM, dq), lambda h: (0, q_col0 + h)),
                  pl.BlockSpec((NM, dq), lambda h: (0, k_col0 + h)),
                  pl.BlockSpec((NM, HD), lambda h: (0, v_col0 + h))],
        out_specs=pl.BlockSpec((NM, HD), lambda h: (0, h)),
        out_shape=jax.ShapeDtypeStruct((NM, H * HD), BF16),
        compiler_params=_params(("parallel",), 32),
        name=name,
    )(q, k, v)


def _attn_out_kernel(om_ref, of_ref, w_ref, h_ref, g_ref, b_ref, o_ref):
    m = _dot(om_ref[...], w_ref[0:H * HD, :]) + _dot(of_ref[...], w_ref[H * HD:, :])
    o_ref[...] = _layer_norm(ALPHA * h_ref[...] + m, g_ref[...], b_ref[...])


def _attn_out(om, of, w, h, g, b, tm):
    m = h.shape[0]
    row = lambda w_: pl.BlockSpec((tm, w_), lambda i: (i, 0))
    return pl.pallas_call(
        _attn_out_kernel,
        grid=(m // tm,),
        in_specs=[row(H * HD), row(H * HD), _resident(w.shape), row(D), _full(g.shape), _full(b.shape)],
        out_specs=row(D),
        out_shape=jax.ShapeDtypeStruct((m, D), F32),
        compiler_params=_params(("parallel",), 48),
        name="attn_out_ln",
    )(om, of, w, h, g, b)


def _silu(x):
    return x / (1.0 + jnp.exp(-x))


def _ffn_kernel(h_ref, wg_ref, wu_ref, wd_ref, g_ref, b_ref, o_ref, xb_scr, acc_scr):
    f = pl.program_id(1)

    @pl.when(f == 0)
    def _():
        xb_scr[...] = h_ref[...].astype(BF16)
        acc_scr[...] = jnp.zeros(acc_scr.shape, F32)

    xb = xb_scr[...]
    hid = _silu(_dot(xb, wg_ref[...])) * _dot(xb, wu_ref[...])
    acc_scr[...] += _dot(hid.astype(BF16), wd_ref[...])

    @pl.when(f == pl.num_programs(1) - 1)
    def _():
        o_ref[...] = _layer_norm(ALPHA * h_ref[...] + acc_scr[...], g_ref[...], b_ref[...])


def _ffn(h, wg, wu, wd, g, b, tm):
    m = h.shape[0]
    return pl.pallas_call(
        _ffn_kernel,
        grid=(m // tm, FD // FFN_TF),
        in_specs=[pl.BlockSpec((tm, D), lambda i, f: (i, 0)),
                  pl.BlockSpec((D, FFN_TF), lambda i, f: (0, f)),
                  pl.BlockSpec((D, FFN_TF), lambda i, f: (0, f)),
                  pl.BlockSpec((FFN_TF, D), lambda i, f: (f, 0)),
                  pl.BlockSpec((1, D), lambda i, f: (0, 0)),
                  pl.BlockSpec((1, D), lambda i, f: (0, 0))],
        out_specs=pl.BlockSpec((tm, D), lambda i, f: (i, 0)),
        out_shape=jax.ShapeDtypeStruct((m, D), F32),
        scratch_shapes=[pltpu.VMEM((tm, D), BF16), pltpu.VMEM((tm, D), F32)],
        compiler_params=_params(("parallel", "arbitrary"), 52),
        name="ffn_ln",
    )(h, wg, wu, wd, g, b)


def _pool_kernel(hp_ref, prev_ref, meta_ref, h_ref, wg_ref, sc_ref, wo_ref, g_ref, b_ref, o_ref,
                 ext_scr, z_scr):
    tm = hp_ref.shape[0]
    halo = jnp.where(pl.program_id(0) == 0, meta_ref[...], prev_ref[...])
    ext_scr[0:NM, :] = halo
    ext_scr[NM:NM + tm, :] = hp_ref[...]
    for gi, w in enumerate(POOL_W):
        cols = slice(PG * gi, PG * (gi + 1))
        cur = ext_scr[NM:NM + tm, cols]
        tot = cur
        for j in range(1, w):
            tot = tot + ext_scr[NM - j:NM - j + tm, cols]
        diff = tot * (1.0 / w) - cur
        y = _dot(diff, wg_ref[gi]) * sc_ref[:, cols]
        z_scr[:, cols] = y
    m = _dot(z_scr[...], wo_ref[...])
    o_ref[...] = _layer_norm(ALPHA * h_ref[...] + m, g_ref[...], b_ref[...])


def _pool_mixer(hp, hp_meta, h, wgroup, scale, wout, g, b):
    tm = POOL_TM
    row = pl.BlockSpec((tm, D), lambda i: (i, 0))
    prev = pl.BlockSpec((NM, D), lambda i: (jnp.maximum(i * (tm // NM) - 1, 0), 0))
    return pl.pallas_call(
        _pool_kernel,
        grid=(T // tm,),
        in_specs=[row, prev, _full(hp_meta.shape), row, _resident(wgroup.shape), _full(scale.shape),
                  _resident(wout.shape), _full(g.shape), _full(b.shape)],
        out_specs=row,
        out_shape=jax.ShapeDtypeStruct((T, D), F32),
        scratch_shapes=[pltpu.VMEM((NM + tm, D), F32), pltpu.VMEM((tm, D), F32)],
        compiler_params=_params(("parallel",), 48),
        name="pool_ln",
    )(hp, hp, hp_meta, h, wgroup, scale, wout, g, b)


def _router_kernel(h_ref, w_ref, b_ref, info_ref, cnt_ref, carry_scr):
    tm = h_ref.shape[0]

    @pl.when(pl.program_id(0) == 0)
    def _():
        carry_scr[...] = jnp.zeros(carry_scr.shape, F32)

    logits = jnp.dot(h_ref[...], w_ref[...], precision=lax.Precision.HIGHEST,
                     preferred_element_type=F32) + b_ref[...]
    lane = lax.broadcasted_iota(jnp.int32, logits.shape, 1).astype(F32)
    nolane = float(LANES_V7X)
    m1 = jnp.max(logits, axis=1, keepdims=True)
    i1 = jnp.min(jnp.where(logits == m1, lane, nolane), axis=1, keepdims=True)
    rest = jnp.where(lane == i1, -jnp.inf, logits)
    m2 = jnp.max(rest, axis=1, keepdims=True)
    i2 = jnp.min(jnp.where(rest == m2, lane, nolane), axis=1, keepdims=True)
    e = jnp.exp(m2 - m1)
    g1 = 1.0 / (1.0 + e)
    g2 = e / (1.0 + e)
    hit1 = lane == i1
    hit2 = lane == i2
    member = jnp.logical_or(hit1, hit2).astype(BF16)
    earlier = (lax.broadcasted_iota(jnp.int32, (tm, tm), 1)
               < lax.broadcasted_iota(jnp.int32, (tm, tm), 0)).astype(BF16)
    before = _dot(earlier, member) + carry_scr[0:1, :]
    r1 = jnp.sum(jnp.where(hit1, before, 0.0), axis=1, keepdims=True)
    r2 = jnp.sum(jnp.where(hit2, before, 0.0), axis=1, keepdims=True)
    carry = carry_scr[0:1, :] + jnp.sum(member.astype(F32), axis=0, keepdims=True)
    carry_scr[...] = jnp.broadcast_to(carry, carry_scr.shape)
    cnt_ref[...] = jnp.broadcast_to(carry, cnt_ref.shape)
    info = jnp.where(lane == 0, g1, 0.0)
    info = jnp.where(lane == 1, g2, info)
    info = jnp.where(lane == 2, i1, info)
    info = jnp.where(lane == 3, i2, info)
    info = jnp.where(lane == 4, r1, info)
    info = jnp.where(lane == 5, r2, info)
    info_ref[...] = info


def _router(h, w_pad, b_pad):
    tm = TM
    return pl.pallas_call(
        _router_kernel,
        grid=(T // tm,),
        in_specs=[pl.BlockSpec((tm, D), lambda i: (i, 0)), _full(w_pad.shape), _full(b_pad.shape)],
        out_specs=[pl.BlockSpec((tm, LANES_V7X), lambda i: (i, 0)), _full((8, LANES_V7X))],
        out_shape=[jax.ShapeDtypeStruct((T, LANES_V7X), F32),
                   jax.ShapeDtypeStruct((8, LANES_V7X), F32)],
        scratch_shapes=[pltpu.VMEM((8, LANES_V7X), F32)],
        compiler_params=_params(("arbitrary",), 32),
        name="router",
    )(h, w_pad, b_pad)


def _moe_row_copy(h_hbm, x_scr, sem, tok, r):
    return pltpu.make_async_copy(h_hbm.at[pl.ds(tok, 1), :], x_scr.at[pl.ds(r, 1), :], sem)


def _moe_out_copy(acc_scr, y_hbm, sem, row0, u):
    off = pl.multiple_of(u * MOE_UNIT, MOE_UNIT)
    dst = pl.multiple_of(row0 + off, MOE_UNIT)
    return pltpu.make_async_copy(acc_scr.at[pl.ds(off, MOE_UNIT), :],
                                 y_hbm.at[pl.ds(dst, MOE_UNIT), :], sem)


def _moe_kernel(te_ref, row0_ref, nunit_ref, used_ref, pad_ref, dest_ref,
                h_hbm, wg_ref, wu_ref, wd_ref, y_hbm,
                slot_tok, x_scr, acc_scr, gsem, osem):
    t = pl.program_id(0)
    f = pl.program_id(1)
    nf = pl.num_programs(1)

    @pl.when(jnp.logical_and(t == 0, f == 0))
    def _():
        for e in range(NE):
            def clear(s, c):
                slot_tok[s] = 0
                return c
            lax.fori_loop(pad_ref[2 * e], pad_ref[2 * e + 1], clear, 0)

        def place(tok, c):
            slot_tok[dest_ref[2 * tok]] = tok
            slot_tok[dest_ref[2 * tok + 1]] = tok
            return c
        lax.fori_loop(0, T, place, 0, unroll=8)

    @pl.when(t < used_ref[0])
    def _():
        row0 = row0_ref[t]
        nunit = nunit_ref[t]
        nrows = nunit * MOE_UNIT

        @pl.when(f == 0)
        def _():
            def issue_unit(u, c):
                def issue(r, cc):
                    row = u * MOE_UNIT + r
                    _moe_row_copy(h_hbm, x_scr, gsem, slot_tok[row0 + row], row).start()
                    return cc
                return lax.fori_loop(0, MOE_UNIT, issue, c, unroll=8)
            lax.fori_loop(0, nunit, issue_unit, 0)

            @pl.when(t > 0)
            def _():
                def finish(u, c):
                    _moe_out_copy(acc_scr, y_hbm, osem, row0_ref[t - 1], u).wait()
                    return c
                lax.fori_loop(0, nunit_ref[t - 1], finish, 0)

            def clear_acc(u, c):
                off = pl.multiple_of(u * MOE_UNIT, MOE_UNIT)
                acc_scr[pl.ds(off, MOE_UNIT), :] = jnp.zeros((MOE_UNIT, D), F32)
                return c
            lax.fori_loop(0, nunit, clear_acc, 0)

            def drain_unit(u, c):
                def drain(r, cc):
                    _moe_row_copy(h_hbm, x_scr, gsem, 0, u * MOE_UNIT + r).wait()
                    return cc
                return lax.fori_loop(0, MOE_UNIT, drain, c, unroll=8)
            lax.fori_loop(0, nunit, drain_unit, 0)

        def chunk(off, rows):
            x = x_scr[pl.ds(off, rows), :]
            hid = _silu(_dot(x, wg_ref[...])) * _dot(x, wu_ref[...])
            acc_scr[pl.ds(off, rows), :] += _dot(hid, wd_ref[...])

        def quad(c, carry):
            chunk(pl.multiple_of(c * 4 * MOE_UNIT, 4 * MOE_UNIT), 4 * MOE_UNIT)
            return carry
        lax.fori_loop(0, nunit // 4, quad, 0)

        @pl.when(nunit % 4 >= 2)
        def _():
            chunk(pl.multiple_of((nunit // 4) * 4 * MOE_UNIT, 2 * MOE_UNIT), 2 * MOE_UNIT)

        @pl.when(nunit % 2 == 1)
        def _():
            chunk(pl.multiple_of((nunit - 1) * MOE_UNIT, MOE_UNIT), MOE_UNIT)

        @pl.when(f == nf - 1)
        def _():
            def start(u, carry):
                _moe_out_copy(acc_scr, y_hbm, osem, row0, u).start()
                return carry
            lax.fori_loop(0, nunit, start, 0)

            @pl.when(t == used_ref[0] - 1)
            def _():
                def finish(u, carry):
                    _moe_out_copy(acc_scr, y_hbm, osem, row0, u).wait()
                    return carry
                lax.fori_loop(0, nunit, finish, 0)

    @pl.when(jnp.logical_and(t == pl.num_programs(0) - 1, f == nf - 1))
    def _():
        acc_scr[0:MOE_UNIT, :] = jnp.zeros((MOE_UNIT, D), F32)

        def fill(u, carry):
            cp = _moe_out_copy(acc_scr, y_hbm, osem, u * MOE_UNIT, 0)
            cp.start()
            cp.wait()
            return carry
        lax.fori_loop(used_ref[1], MOE_CAP // MOE_UNIT, fill, 0)


def _moe_experts(h, wg, wu, wd, tile_e, tile_row0, tile_nunit, used, pad, dest):
    nf = FE // MOE_TF

    def fidx(t, f, us):
        return jnp.where(t < us[0], f, nf - 1)

    grid_spec = pltpu.PrefetchScalarGridSpec(
        num_scalar_prefetch=6,
        grid=(MOE_NT, nf),
        in_specs=[pl.BlockSpec(memory_space=pl.ANY),
                  pl.BlockSpec((None, D, MOE_TF), lambda t, f, te, r0, nu, us, pa, de: (te[t], 0, fidx(t, f, us))),
                  pl.BlockSpec((None, D, MOE_TF), lambda t, f, te, r0, nu, us, pa, de: (te[t], 0, fidx(t, f, us))),
                  pl.BlockSpec((None, MOE_TF, D), lambda t, f, te, r0, nu, us, pa, de: (te[t], fidx(t, f, us), 0))],
        out_specs=pl.BlockSpec(memory_space=pl.ANY),
        scratch_shapes=[pltpu.SMEM((MOE_CAP,), jnp.int32),
                        pltpu.VMEM((MOE_TMAX, D), F32),
                        pltpu.VMEM((MOE_TMAX, D), F32),
                        pltpu.SemaphoreType.DMA,
                        pltpu.SemaphoreType.DMA],
    )
    return pl.pallas_call(
        _moe_kernel,
        grid_spec=grid_spec,
        out_shape=jax.ShapeDtypeStruct((MOE_CAP, D), F32),
        compiler_params=_params(("arbitrary", "arbitrary"), 58),
        name="moe_experts",
    )(tile_e, tile_row0, tile_nunit, used, pad, dest, h, wg, wu, wd)


def _cmb_copy(y_hbm, ybuf, sem, buf, slot, k, r):
    return pltpu.make_async_copy(y_hbm.at[pl.ds(slot, 1), :], ybuf.at[buf, k, pl.ds(r, 1), :],
                                 sem.at[buf])


def _combine_kernel(dest_ref, h_ref, info_ref, y_hbm, g_ref, b_ref, o_ref, ybuf, sem):
    tm = h_ref.shape[0]
    i = pl.program_id(0)
    cur = lax.rem(i, 2)

    def gather(tile, buf):
        def issue(r, c):
            tok = tile * tm + r
            _cmb_copy(y_hbm, ybuf, sem, buf, dest_ref[2 * tok], 0, r).start()
            _cmb_copy(y_hbm, ybuf, sem, buf, dest_ref[2 * tok + 1], 1, r).start()
            return c
        lax.fori_loop(0, tm, issue, 0, unroll=8)

    @pl.when(i == 0)
    def _():
        gather(0, 0)

    @pl.when(i + 1 < pl.num_programs(0))
    def _():
        gather(i + 1, 1 - cur)

    def drain(r, c):
        _cmb_copy(y_hbm, ybuf, sem, cur, 0, 0, r).wait()
        _cmb_copy(y_hbm, ybuf, sem, cur, 0, 1, r).wait()
        return c
    lax.fori_loop(0, tm, drain, 0, unroll=8)
    info = info_ref[...]
    moe = info[:, 0:1] * ybuf[cur, 0] + info[:, 1:2] * ybuf[cur, 1]
    o_ref[...] = _layer_norm(ALPHA * h_ref[...] + moe, g_ref[...], b_ref[...])


def _combine(dest, h, info, y_rows, g, b):
    tm = CMB_TM
    grid_spec = pltpu.PrefetchScalarGridSpec(
        num_scalar_prefetch=1,
        grid=(T // tm,),
        in_specs=[pl.BlockSpec((tm, D), lambda i, de: (i, 0)),
                  pl.BlockSpec((tm, LANES_V7X), lambda i, de: (i, 0)),
                  pl.BlockSpec(memory_space=pl.ANY),
                  pl.BlockSpec((1, D), lambda i, de: (0, 0)),
                  pl.BlockSpec((1, D), lambda i, de: (0, 0))],
        out_specs=pl.BlockSpec((tm, D), lambda i, de: (i, 0)),
        scratch_shapes=[pltpu.VMEM((2, 2, tm, D), F32), pltpu.SemaphoreType.DMA((2,))],
    )
    return pl.pallas_call(
        _combine_kernel,
        grid_spec=grid_spec,
        out_shape=jax.ShapeDtypeStruct((T, D), F32),
        compiler_params=_params(("arbitrary",), 32),
        name="moe_combine_ln",
    )(dest, h, info, y_rows, g, b)


def _rope_tables(pos):
    inv_freq = THETA ** (-jnp.arange(0, ROPE, 2, dtype=F32) / ROPE)
    ang = pos[:, None] * inv_freq[None, :]
    cos = jnp.cos(ang).astype(F32)
    sin = jnp.sin(ang).astype(F32)
    zero = jnp.zeros((pos.shape[0], LANES_V7X - ROPE), F32)
    return (jnp.concatenate([cos, cos, zero], axis=1),
            jnp.concatenate([-sin, sin, zero], axis=1))


def _expert_tiles(counts):
    units = (counts + MOE_UNIT - 1) // MOE_UNIT
    seg_unit = jnp.cumsum(units) - units
    ntile = (units + MOE_TUNITS - 1) // MOE_TUNITS
    tile_end = jnp.cumsum(ntile)
    n_used = tile_end[-1]
    tid = jnp.arange(MOE_NT, dtype=jnp.int32)
    e = jnp.minimum(jnp.sum(tid[:, None] >= tile_end[None, :], axis=1), NE - 1).astype(jnp.int32)
    local = tid - (tile_end - ntile)[e]
    nt = jnp.maximum(ntile[e], 1)
    lo = (local * units[e]) // nt
    hi = ((local + 1) * units[e]) // nt
    used = tid < n_used
    last = jnp.maximum(n_used - 1, 0)
    e = jnp.where(used, e, e[last])
    row0 = jnp.where(used, (seg_unit[e] + lo) * MOE_UNIT, 0)
    nunit = jnp.where(used, hi - lo, 0)
    used_info = jnp.stack([n_used, jnp.sum(units)]).astype(jnp.int32)
    seg_row = seg_unit * MOE_UNIT
    pad = jnp.stack([seg_row + counts, seg_row + units * MOE_UNIT], axis=1).reshape(2 * NE)
    return (e.astype(jnp.int32), row0.astype(jnp.int32), nunit.astype(jnp.int32),
            used_info, pad.astype(jnp.int32), seg_row.astype(jnp.int32))


def kernel(x, meta_tokens, attn_w_in, fox_forget_bias, mla_q_norm, mla_kv_norm, mla_w_uq, mla_w_ukv, attn_w_out, pool_w_in, pool_w_group, pool_scale, pool_w_out, ffn_w_gate, ffn_w_up, ffn_w_down, moe_w_router, moe_b_router, moe_w_gate, moe_w_up, moe_w_down, ln_mix_g, ln_mix_b, ln_ffn_g, ln_ffn_b):
    assert x.shape == (1, T, D) and meta_tokens.shape == (NM, D)
    xt = x[0]
    meta = meta_tokens.astype(x.dtype)

    w_in = attn_w_in[0]
    o_kr, o_qf, o_f = QL + KVL, QL + KVL + ROPE, QL + KVL + ROPE + 3 * H * HD
    kr1, kr2 = w_in[:, o_kr:o_kr + ROPE // 2], w_in[:, o_kr + ROPE // 2:o_qf]
    zpad = lambda n: jnp.zeros((D, n), F32)
    w_lat = jnp.concatenate(
        [w_in[:, :o_kr], kr1, kr2, zpad(LANES_V7X - ROPE), kr2, kr1, zpad(LANES_V7X - ROPE),
         w_in[:, o_f:o_f + H], zpad(LANES_V7X - H)], axis=1).astype(BF16)
    w_fox = w_in[:, o_qf:o_f].astype(BF16)
    wuq = mla_w_uq[0].reshape(QL, H, NOPE + ROPE)
    uq_n, uq_1, uq_2 = wuq[..., :NOPE], wuq[..., NOPE:NOPE + ROPE // 2], wuq[..., NOPE + ROPE // 2:]
    zq = jnp.zeros((QL, H, LANES_V7X - ROPE), F32)
    w_uq = jnp.concatenate([uq_n, uq_1, uq_2, zq], axis=-1).reshape(QL, H * DQ_MLA).astype(BF16)
    w_uqs = jnp.concatenate([uq_2, uq_1, zq], axis=-1).reshape(QL, H * HD).astype(BF16)
    wukv = mla_w_ukv[0].reshape(KVL, H, NOPE + HD)
    w_ukv = jnp.concatenate([wukv[..., :NOPE].reshape(KVL, H * NOPE),
                             wukv[..., NOPE:].reshape(KVL, H * HD)], axis=1).astype(BF16)
    w_ao = attn_w_out[0]
    w_fg, w_fu, w_fd = ffn_w_gate[0].astype(BF16), ffn_w_up[0].astype(BF16), ffn_w_down[0].astype(BF16)
    w_pi, w_pg, w_po = pool_w_in[0], pool_w_group[0], pool_w_out[0]
    qn, kvn = mla_q_norm[0][None, :], mla_kv_norm[0][None, :]
    row = lambda v: v[None, :]

    ctab_t, stab_t = _rope_tables(jnp.arange(NM, NM + T, dtype=F32))
    ctab_m, stab_m = _rope_tables(jnp.arange(NM, dtype=F32))

    fox_scale = HD ** -0.5 * LOG2E
    q_t, k_t, vt_t, fl_t = _mla_proj(xt, w_lat, qn, kvn, w_uq, w_uqs, w_ukv, ctab_t, stab_t, TQ)
    q_m, k_m, v_m, _ = _mla_proj(meta, w_lat, qn, kvn, w_uq, w_uqs, w_ukv, ctab_m, stab_m, NM)
    fox_qk, fox_vt = _fox_proj(xt, w_fox.reshape(D, 3, H * HD).transpose(1, 0, 2), fox_scale)
    fox_m = _matmul(meta, w_fox, NM, H * HD, BF16, scaled_blocks=1, scale=fox_scale, name="fox_proj_meta")

    fl = fl_t[:, :H].T.reshape(H, T // LANES_V7X, LANES_V7X)
    bias_b = jnp.broadcast_to(fox_forget_bias[0][:, None, None], (H, 1, LANES_V7X))
    parts = _forget_cumsum(fl, bias_b).reshape(H, 6, T)
    pos = parts[:, 0:3].transpose(2, 0, 1)
    neg = parts[:, 3:6].transpose(2, 0, 1)
    lanes = lambda pieces: jnp.concatenate(pieces, axis=2).reshape(-1, H * AUG_STRIDE)
    cst = lambda rows, n, v: jnp.full((rows, H, n), v, BF16)
    aug = (lanes([cst(T, 3, 1.0), pos, cst(T, AUG_STRIDE - 6, 0.0)]),
           lanes([neg, cst(T, AUG_STRIDE - 3, 0.0)]),
           lanes([cst(NM, 3, 0.0), cst(NM, 3, -1.0), cst(NM, AUG_STRIDE - 6, 0.0)]))

    to_vmt = lambda v: v.reshape(NM, H, HD).transpose(1, 2, 0)
    o_mla = _attention(q_t, k_t, vt_t, k_m, to_vmt(v_m), 0, DQ_MLA, name="mla_attn")
    o_fox = _attention(fox_qk, fox_qk, fox_vt, fox_m, to_vmt(fox_m[:, 2 * H * HD:]),
                       H, HD, aug, name="fox_attn")
    om_mla = _meta_attention(q_m, k_m, v_m, 0, 0, 0, DQ_MLA, "mla_attn_meta")
    om_fox = _meta_attention(fox_m, fox_m, fox_m, 0, H, 2 * H, HD, "fox_attn_meta")

    g0, b0, g1, b1 = row(ln_mix_g[0]), row(ln_mix_b[0]), row(ln_ffn_g[0]), row(ln_ffn_b[0])
    h_t = _attn_out(o_mla, o_fox, w_ao, xt, g0, b0, TM)
    h_m = _attn_out(om_mla, om_fox, w_ao, meta, g0, b0, NM)
    h_t = _ffn(h_t, w_fg, w_fu, w_fd, g1, b1, TM)
    h_m = _ffn(h_m, w_fg, w_fu, w_fd, g1, b1, NM)

    hp_t = _matmul(h_t, w_pi, TM, D, F32, name="pool_in")
    hp_m = _matmul(h_m, w_pi, NM, D, F32, name="pool_in_meta")
    h_t = _pool_mixer(hp_t, hp_m, h_t, w_pg, row(pool_scale[0]), w_po,
                      row(ln_mix_g[1]), row(ln_mix_b[1]))

    w_r = jnp.concatenate([moe_w_router[0], jnp.zeros((D, LANES_V7X - NE), F32)], axis=1)
    b_r = jnp.concatenate([moe_b_router[0], jnp.full((LANES_V7X - NE,), NEG, F32)])[None, :]
    info, cnt = _router(h_t, w_r, b_r)
    counts = cnt[0, :NE].astype(jnp.int32)
    tile_e, tile_row0, tile_nunit, used, pad, seg_row = _expert_tiles(counts)
    experts = info[:, 2:4].astype(jnp.int32)
    ranks = info[:, 4:6].astype(jnp.int32)
    dest = (seg_row[experts] + ranks).reshape(2 * T)
    y_rows = _moe_experts(h_t, moe_w_gate[0], moe_w_up[0], moe_w_down[0],
                          tile_e, tile_row0, tile_nunit, used, pad, dest)
    out = _combine(dest, h_t, info, y_rows, row(ln_ffn_g[1]), row(ln_ffn_b[1]))
    return out[None]
```

```python
import functools
import math

import jax
import jax.numpy as jnp
import numpy as np
from jax import lax
from jax.experimental import pallas as pl
from jax.experimental.pallas import tpu as pltpu

D = 2048
T = 8192
NM = 16
H = 8
QL = 512
KVL = 512
NOPE = 128
ROPE = 64
HD = 128
DQ_MLA = 256
THETA = 10000.0
POOL_W = (2, 4, 8, 16)
PG = D // 4
FD = 5632
NE = 8
FE = 7168
ALPHA = 4.0 ** 0.25
LN_EPS = 1e-5
RMS_EPS = 1e-6
NEG = -1e30
LOG2E = 1.4426950408889634

F32 = jnp.float32
BF16 = jnp.bfloat16

LANES_V7X = 128
VMEM_BYTES_V7X = 64 * 1024 * 1024
MIB = 1024 * 1024

TM = 512
TK = 512
ATT_TQ = 2 * TK
FFN_TM = 1024
FFN_CHUNK = 512
FFN_TF = 512
POOL_TM = 256
MOE_UNIT = 128
MOE_TUNITS = 10
MOE_TMAX = MOE_TUNITS * MOE_UNIT
MOE_TF = 512
MOE_CAP = 2 * T + NE * MOE_UNIT
MOE_NT = MOE_CAP // MOE_TMAX + NE + 1
CMB_TM = 256


def _params(sem, vmem_mib):
    return pltpu.CompilerParams(dimension_semantics=sem, vmem_limit_bytes=vmem_mib * MIB)


def _full(shape):
    n = len(shape)
    return pl.BlockSpec(shape, lambda *_: (0,) * n)


def _resident(shape):
    n = len(shape)
    return pl.BlockSpec(shape, lambda *_: (0,) * n, pipeline_mode=pl.Buffered(1))


def _layer_norm(x, g, b):
    mu = jnp.mean(x, axis=-1, keepdims=True)
    xc = x - mu
    var = jnp.mean(xc * xc, axis=-1, keepdims=True)
    return xc * lax.rsqrt(var + LN_EPS) * g + b


def _rms_norm(x, g):
    return x * lax.rsqrt(jnp.mean(x * x, axis=-1, keepdims=True) + RMS_EPS) * g


def _dot(a, b):
    return jnp.dot(a, b, preferred_element_type=F32)


def _dot_nt(a, b):
    return lax.dot_general(a, b, (((1,), (1,)), ((), ())), preferred_element_type=F32)


def _mla_proj_kernel(x_ref, wlat_ref, qn_ref, kvn_ref, wuq_ref, wuqs_ref, wukv_ref, c_ref, s_ref,
                     q_ref, k_ref, v_ref, fl_ref, *, qscale, transposed_v):
    xb = x_ref[...].astype(BF16)
    lat = _dot(xb, wlat_ref[...])
    cq = lat[:, 0:QL]
    ckv = lat[:, QL:QL + KVL]
    kr = lat[:, 1024:1152]
    krs = lat[:, 1152:1280]
    fl_ref[...] = lat[:, 1280:1408]
    cos = c_ref[...]
    sin = s_ref[...]
    qn = _rms_norm(cq, qn_ref[...]).astype(BF16)
    qpre = _dot(qn, wuq_ref[...])
    qsw = _dot(qn, wuqs_ref[...])
    for h in range(H):
        nope = qpre[:, DQ_MLA * h:DQ_MLA * h + NOPE]
        rot = (qpre[:, DQ_MLA * h + NOPE:DQ_MLA * (h + 1)] * cos
               + qsw[:, HD * h:HD * (h + 1)] * sin)
        q_ref[:, DQ_MLA * h:DQ_MLA * h + NOPE] = (nope * qscale).astype(BF16)
        q_ref[:, DQ_MLA * h + NOPE:DQ_MLA * (h + 1)] = (rot * qscale).astype(BF16)
    kvn = _rms_norm(ckv, kvn_ref[...]).astype(BF16)
    kv = _dot(kvn, wukv_ref[...])
    krot = (kr * cos + krs * sin).astype(BF16)
    for h in range(H):
        k_ref[:, DQ_MLA * h:DQ_MLA * h + NOPE] = kv[:, HD * h:HD * (h + 1)].astype(BF16)
        k_ref[:, DQ_MLA * h + NOPE:DQ_MLA * (h + 1)] = krot
    if transposed_v:
        for h in range(H):
            v_ref[h, 0] = kv[:, (H + h) * HD:(H + h + 1) * HD].T.astype(BF16)
    else:
        v_ref[...] = kv[:, H * HD:].astype(BF16)


def _mla_proj(x, wlat, qn, kvn, wuq, wuqs, wukv, ctab, stab, tm):
    m = x.shape[0]
    qscale = (NOPE + ROPE) ** -0.5 * LOG2E
    row = lambda w: pl.BlockSpec((tm, w), lambda i: (i, 0))
    transposed_v = tm == TK
    if transposed_v:
        v_spec = pl.BlockSpec((H, 1, HD, TK), lambda i: (0, i, 0, 0))
        v_shape = jax.ShapeDtypeStruct((H, m // TK, HD, TK), BF16)
    else:
        v_spec, v_shape = row(H * HD), jax.ShapeDtypeStruct((m, H * HD), BF16)
    return pl.pallas_call(
        functools.partial(_mla_proj_kernel, qscale=qscale, transposed_v=transposed_v),
        grid=(m // tm,),
        in_specs=[row(D), _full(wlat.shape), _full(qn.shape), _full(kvn.shape), _full(wuq.shape),
                  _full(wuqs.shape), _full(wukv.shape), row(LANES_V7X), row(LANES_V7X)],
        out_specs=[row(H * DQ_MLA), row(H * DQ_MLA), v_spec, row(LANES_V7X)],
        out_shape=[jax.ShapeDtypeStruct((m, H * DQ_MLA), BF16),
                   jax.ShapeDtypeStruct((m, H * DQ_MLA), BF16),
                   v_shape,
                   jax.ShapeDtypeStruct((m, LANES_V7X), F32)],
        compiler_params=_params(("parallel",), 56),
        name="mla_proj",
    )(x, wlat, qn, kvn, wuq, wuqs, wukv, ctab, stab)


def _mm_kernel(x_ref, w_ref, o_ref, *, scaled_blocks, scale):
    y = _dot(x_ref[...], w_ref[...])
    if scaled_blocks:
        y = jnp.where(pl.program_id(1) < scaled_blocks, y * scale, y)
    o_ref[...] = y.astype(o_ref.dtype)


def _matmul(x, w, tm, tn, out_dtype, scaled_blocks=0, scale=1.0, name="matmul"):
    m, k = x.shape
    n = w.shape[1]
    return pl.pallas_call(
        functools.partial(_mm_kernel, scaled_blocks=scaled_blocks, scale=scale),
        grid=(m // tm, n // tn),
        in_specs=[pl.BlockSpec((tm, k), lambda i, j: (i, 0)),
                  _resident((k, n)) if tn == n else pl.BlockSpec((k, tn), lambda i, j: (0, j))],
        out_specs=pl.BlockSpec((tm, tn), lambda i, j: (i, j)),
        out_shape=jax.ShapeDtypeStruct((m, n), out_dtype),
        compiler_params=_params(("parallel", "parallel"), 48),
        name=name,
    )(x, w)


def _fox_proj_kernel(x_ref, w_ref, qk_ref, vt_ref, *, scale):
    j = pl.program_id(1)
    y = _dot(x_ref[...].astype(BF16), w_ref[j])

    @pl.when(j == 0)
    def _():
        qk_ref[...] = (y * scale).astype(BF16)

    @pl.when(j == 1)
    def _():
        qk_ref[...] = y.astype(BF16)

    @pl.when(j == 2)
    def _():
        for h in range(H):
            vt_ref[h, 0] = y[:, h * HD:(h + 1) * HD].T.astype(BF16)


def _fox_proj(x, w, scale):
    n = H * HD
    return pl.pallas_call(
        functools.partial(_fox_proj_kernel, scale=scale),
        grid=(T // TK, 3),
        in_specs=[pl.BlockSpec((TK, D), lambda i, j: (i, 0)),
                  _resident((3, D, n))],
        out_specs=[pl.BlockSpec((TK, n), lambda i, j: (i, jnp.minimum(j, 1))),
                   pl.BlockSpec((H, 1, HD, TK), lambda i, j: (0, i, 0, 0))],
        out_shape=[jax.ShapeDtypeStruct((T, 2 * n), BF16),
                   jax.ShapeDtypeStruct((H, T // TK, HD, TK), BF16)],
        compiler_params=_params(("parallel", "arbitrary"), 48),
        name="fox_proj",
    )(x, w)


def _split3(a):
    hi = a.astype(BF16)
    r = a - hi.astype(F32)
    mid = r.astype(BF16)
    lo = (r - mid.astype(F32)).astype(BF16)
    return hi, mid, lo


def _forget_cumsum_kernel(fl_ref, b_ref, o_ref):
    z = fl_ref[...] + b_ref[...]
    logf = jnp.minimum(z, 0.0) - jnp.log(1.0 + jnp.exp(-jnp.abs(z)))
    n = LANES_V7X
    ri = lax.broadcasted_iota(jnp.int32, (n, n), 0)
    ci = lax.broadcasted_iota(jnp.int32, (n, n), 1)
    upper = (ri <= ci).astype(BF16)
    ones = jnp.ones((n, n), BF16)
    within = jnp.zeros(logf.shape, F32)
    total = jnp.zeros(logf.shape, F32)
    for part in _split3(logf):
        within += _dot(part, upper)
        total += _dot(part, ones)
    r = logf.shape[0]
    lower = (lax.broadcasted_iota(jnp.int32, (r, r), 1)
             < lax.broadcasted_iota(jnp.int32, (r, r), 0)).astype(BF16)
    before = jnp.zeros(logf.shape, F32)
    for part in _split3(total):
        before += _dot(lower, part)
    c2 = (within + before) * LOG2E
    for p, part in enumerate(_split3(c2)):
        o_ref[p] = part
        o_ref[3 + p] = -part


def _forget_cumsum(fl_t, bias_b):
    r = T // LANES_V7X
    return pl.pallas_call(
        _forget_cumsum_kernel,
        grid=(H,),
        in_specs=[pl.BlockSpec((None, r, LANES_V7X), lambda h: (h, 0, 0)),
                  pl.BlockSpec((None, 1, LANES_V7X), lambda h: (h, 0, 0))],
        out_specs=pl.BlockSpec((None, 6, r, LANES_V7X), lambda h: (h, 0, 0, 0)),
        out_shape=jax.ShapeDtypeStruct((H, 6, r, LANES_V7X), BF16),
        compiler_params=_params(("parallel",), 32),
        name="forget_cumsum",
    )(fl_t, bias_b)


ATT_NH = 2
AUG_STRIDE = 16


def _attn_kernel(*refs, fox, dq):
    if fox:
        (q_ref, k_ref, vt_ref, km_ref, vmt_ref, qa_ref, ka_ref, kam_ref, o_ref,
         m_scr, l_scr, acc_scr, s0_scr, s1_scr, mb0_scr, mb1_scr, qf_scr) = refs
    else:
        (q_ref, k_ref, vt_ref, km_ref, vmt_ref, o_ref,
         m_scr, l_scr, acc_scr, s0_scr, s1_scr, mb0_scr, mb1_scr) = refs
    hp = pl.program_id(0)
    i = pl.program_id(1)
    m_scr[...] = jnp.full(m_scr.shape, -jnp.inf, F32)
    l_scr[...] = jnp.zeros(l_scr.shape, F32)
    acc_scr[...] = jnp.zeros(acc_scr.shape, F32)
    if fox:
        lane = lax.broadcasted_iota(jnp.int32, qa_ref.shape, 1)
        for g in range(ATT_NH):
            lo = (hp * ATT_NH + g) * AUG_STRIDE
            own = jnp.logical_and(lane >= lo, lane < lo + AUG_STRIDE)
            qf_scr[g, :, 0:dq] = q_ref[:, g * dq:(g + 1) * dq]
            qf_scr[g, :, dq:2 * dq] = jnp.where(own, qa_ref[...], jnp.zeros_like(qa_ref))

    def query(g):
        return qf_scr[g] if fox else q_ref[:, g * dq:(g + 1) * dq]

    def update(g, st, mb, vt):
        m_prev = m_scr[g]
        m_new = jnp.maximum(m_prev, mb)
        alpha = jnp.exp2(m_prev - m_new)
        p = jnp.exp2(st - m_new)
        l_scr[g] = alpha * l_scr[g] + jnp.sum(p, axis=0, keepdims=True)
        acc_scr[g] = alpha * acc_scr[g] + _dot(vt, p.astype(BF16))
        m_scr[g] = m_new

    def stage1(slot, j):
        s_ref, mb_ref = slots[slot]
        off = pl.multiple_of(j * TK, TK)
        for g in range(ATT_NH):
            kb = k_ref[pl.ds(off, TK), g * dq:(g + 1) * dq]
            if fox:
                kb = jnp.concatenate([kb, ka_ref[pl.ds(off, TK), :]], axis=1)
            st = _dot_nt(kb, query(g))
            s_ref[g] = st
            mb_ref[g] = jnp.max(st, axis=0, keepdims=True)

    def stage2(slot, j, diag_key0=None):
        s_ref, mb_ref = slots[slot]
        for g in range(ATT_NH):
            st = s_ref[g]
            mb = mb_ref[g]
            if diag_key0 is not None:
                keys = lax.broadcasted_iota(jnp.int32, st.shape, 0) + diag_key0
                qrys = lax.broadcasted_iota(jnp.int32, st.shape, 1)
                st = jnp.where(keys <= qrys, st, NEG)
                mb = jnp.max(st, axis=0, keepdims=True)
            update(g, st, mb, vt_ref[g, j])

    slots = ((s0_scr, mb0_scr), (s1_scr, mb1_scr))
    stage1(0, 0)

    def body(jj, carry):
        j = 2 * jj
        stage1(1, j + 1)
        stage2(0, j)
        stage1(0, j + 2)
        stage2(1, j + 1)
        return carry

    lax.fori_loop(0, i, body, 0)
    stage1(1, 2 * i + 1)
    stage2(0, 2 * i, diag_key0=0)
    stage2(1, 2 * i + 1, diag_key0=TK)

    for g in range(ATT_NH):
        kb = km_ref[:, g * dq:(g + 1) * dq]
        if fox:
            kb = jnp.concatenate([kb, kam_ref[...]], axis=1)
        st = _dot_nt(kb, query(g))
        update(g, st, jnp.max(st, axis=0, keepdims=True), vmt_ref[g])
        o = (acc_scr[g] / l_scr[g]).T
        o_ref[:, g * HD:(g + 1) * HD] = o.astype(o_ref.dtype)


def _attention(q, k, vt, km, vmt, k_col0, dq, aug=None, name="attn"):
    fox = aug is not None
    nh, tq = ATT_NH, ATT_TQ
    in_specs = [pl.BlockSpec((tq, nh * dq), lambda h, i: (i, h)),
                pl.BlockSpec((T, nh * dq), lambda h, i: (0, k_col0 // nh + h)),
                pl.BlockSpec((nh, T // TK, HD, TK), lambda h, i: (h, 0, 0, 0)),
                pl.BlockSpec((NM, nh * dq), lambda h, i: (0, k_col0 // nh + h)),
                pl.BlockSpec((nh, HD, NM), lambda h, i: (h, 0, 0))]
    args = [q, k, vt, km, vmt]
    scratch = [pltpu.VMEM((nh, 1, tq), F32), pltpu.VMEM((nh, 1, tq), F32),
               pltpu.VMEM((nh, HD, tq), F32),
               pltpu.VMEM((nh, TK, tq), F32), pltpu.VMEM((nh, TK, tq), F32),
               pltpu.VMEM((nh, 1, tq), F32), pltpu.VMEM((nh, 1, tq), F32)]
    if fox:
        in_specs += [pl.BlockSpec((tq, LANES_V7X), lambda h, i: (i, 0)),
                     _full((T, LANES_V7X)), _full((NM, LANES_V7X))]
        args += list(aug)
        scratch.append(pltpu.VMEM((nh, tq, 2 * dq), BF16))
    return pl.pallas_call(
        functools.partial(_attn_kernel, fox=fox, dq=dq),
        grid=(H // nh, T // tq),
        in_specs=in_specs,
        out_specs=pl.BlockSpec((tq, nh * HD), lambda h, i: (i, h)),
        out_shape=jax.ShapeDtypeStruct((T, H * HD), BF16),
        scratch_shapes=scratch,
        compiler_params=_params(("parallel", "arbitrary"), 56),
        name=name,
    )(*args)


def _meta_attn_kernel(q_ref, k_ref, v_ref, o_ref):
    s = _dot_nt(q_ref[...], k_ref[...])
    rows = lax.broadcasted_iota(jnp.int32, s.shape, 0)
    cols = lax.broadcasted_iota(jnp.int32, s.shape, 1)
    s = jnp.where(cols <= rows, s, NEG)
    p = jnp.exp2(s - jnp.max(s, axis=1, keepdims=True))
    o = _dot(p.astype(BF16), v_ref[...]) / jnp.sum(p, axis=1, keepdims=True)
    o_ref[...] = o.astype(o_ref.dtype)


def _meta_attention(q, k, v, q_col0, k_col0, v_col0, dq, name):
    return pl.pallas_call(
        _meta_attn_kernel,
        grid=(H,),
        in_specs=[pl.BlockSpec((NM, dq), lambda h: (0, q_col0 + h)),
                  pl.BlockSpec((NM, dq), lambda h: (0, k_col0 + h)),
                  pl.BlockSpec((NM, HD), lambda h: (0, v_col0 + h))],
        out_specs=pl.BlockSpec((NM, HD), lambda h: (0, h)),
        out_shape=jax.ShapeDtypeStruct((NM, H * HD), BF16),
        compiler_params=_params(("parallel",), 32),
        name=name,
    )(q, k, v)


def _attn_out_kernel(om_ref, of_ref, w_ref, h_ref, g_ref, b_ref, o_ref):
    m = _dot(om_ref[...], w_ref[0:H * HD, :]) + _dot(of_ref[...], w_ref[H * HD:, :])
    o_ref[...] = _layer_norm(ALPHA * h_ref[...] + m, g_ref[...], b_ref[...])


def _attn_out(om, of, w, h, g, b, tm):
    m = h.shape[0]
    row = lambda w_: pl.BlockSpec((tm, w_), lambda i: (i, 0))
    return pl.pallas_call(
        _attn_out_kernel,
        grid=(m // tm,),
        in_specs=[row(H * HD), row(H * HD), _resident(w.shape), row(D), _full(g.shape), _full(b.shape)],
        out_specs=row(D),
        out_shape=jax.ShapeDtypeStruct((m, D), F32),
        compiler_params=_params(("parallel",), 48),
        name="attn_out_ln",
    )(om, of, w, h, g, b)


def _silu(x):
    return x / (1.0 + jnp.exp(-x))


def _ffn_kernel(h_ref, wg_ref, wu_ref, wd_ref, g_ref, b_ref, o_ref):
    f = pl.program_id(1)
    rows = h_ref.shape[0]
    chunk = min(rows, FFN_CHUNK)

    @pl.when(f == 0)
    def _():
        o_ref[...] = jnp.zeros(o_ref.shape, F32)

    for c in range(rows // chunk):
        sl = slice(c * chunk, (c + 1) * chunk)
        x = h_ref[sl, :]
        hid = _silu(_dot(x, wg_ref[...])) * _dot(x, wu_ref[...])
        o_ref[sl, :] += _dot(hid, wd_ref[...])

    @pl.when(f == pl.num_programs(1) - 1)
    def _():
        o_ref[...] = _layer_norm(ALPHA * h_ref[...] + o_ref[...], g_ref[...], b_ref[...])


def _ffn(h, wg, wu, wd, g, b, tm):
    m = h.shape[0]
    once = pl.Buffered(1)
    return pl.pallas_call(
        _ffn_kernel,
        grid=(m // tm, FD // FFN_TF),
        in_specs=[pl.BlockSpec((tm, D), lambda i, f: (i, 0), pipeline_mode=once),
                  pl.BlockSpec((D, FFN_TF), lambda i, f: (0, f)),
                  pl.BlockSpec((D, FFN_TF), lambda i, f: (0, f)),
                  pl.BlockSpec((FFN_TF, D), lambda i, f: (f, 0)),
                  pl.BlockSpec((1, D), lambda i, f: (0, 0)),
                  pl.BlockSpec((1, D), lambda i, f: (0, 0))],
        out_specs=pl.BlockSpec((tm, D), lambda i, f: (i, 0)),
        out_shape=jax.ShapeDtypeStruct((m, D), F32),
        compiler_params=_params(("parallel", "arbitrary"), 58),
        name="ffn_ln",
    )(h, wg, wu, wd, g, b)


def _pool_kernel(hp_ref, prev_ref, meta_ref, h_ref, wg_ref, sc_ref, wo_ref, g_ref, b_ref, o_ref,
                 ext_scr, z_scr):
    tm = hp_ref.shape[0]
    halo = jnp.where(pl.program_id(0) == 0, meta_ref[...], prev_ref[...])
    ext_scr[0:NM, :] = halo
    ext_scr[NM:NM + tm, :] = hp_ref[...]
    for gi, w in enumerate(POOL_W):
        cols = slice(PG * gi, PG * (gi + 1))
        cur = ext_scr[NM:NM + tm, cols]
        tot = cur
        for j in range(1, w):
            tot = tot + ext_scr[NM - j:NM - j + tm, cols]
        diff = tot * (1.0 / w) - cur
        y = _dot(diff, wg_ref[gi]) * sc_ref[:, cols]
        z_scr[:, cols] = y
    m = _dot(z_scr[...], wo_ref[...])
    o_ref[...] = _layer_norm(ALPHA * h_ref[...] + m, g_ref[...], b_ref[...])


def _pool_mixer(hp, hp_meta, h, wgroup, scale, wout, g, b):
    tm = POOL_TM
    row = pl.BlockSpec((tm, D), lambda i: (i, 0))
    prev = pl.BlockSpec((NM, D), lambda i: (jnp.maximum(i * (tm // NM) - 1, 0), 0))
    return pl.pallas_call(
        _pool_kernel,
        grid=(T // tm,),
        in_specs=[row, prev, _full(hp_meta.shape), row, _resident(wgroup.shape), _full(scale.shape),
                  _resident(wout.shape), _full(g.shape), _full(b.shape)],
        out_specs=row,
        out_shape=jax.ShapeDtypeStruct((T, D), F32),
        scratch_shapes=[pltpu.VMEM((NM + tm, D), F32), pltpu.VMEM((tm, D), F32)],
        compiler_params=_params(("parallel",), 48),
        name="pool_ln",
    )(hp, hp, hp_meta, h, wgroup, scale, wout, g, b)


def _router_kernel(h_ref, w_ref, b_ref, info_ref, cnt_ref, carry_scr):
    tm = h_ref.shape[0]

    @pl.when(pl.program_id(0) == 0)
    def _():
        carry_scr[...] = jnp.zeros(carry_scr.shape, F32)

    logits = jnp.dot(h_ref[...], w_ref[...], precision=lax.Precision.HIGHEST,
                     preferred_element_type=F32) + b_ref[...]
    lane = lax.broadcasted_iota(jnp.int32, logits.shape, 1).astype(F32)
    nolane = float(LANES_V7X)
    m1 = jnp.max(logits, axis=1, keepdims=True)
    i1 = jnp.min(jnp.where(logits == m1, lane, nolane), axis=1, keepdims=True)
    rest = jnp.where(lane == i1, -jnp.inf, logits)
    m2 = jnp.max(rest, axis=1, keepdims=True)
    i2 = jnp.min(jnp.where(rest == m2, lane, nolane), axis=1, keepdims=True)
    e = jnp.exp(m2 - m1)
    g1 = 1.0 / (1.0 + e)
    g2 = e / (1.0 + e)
    hit1 = lane == i1
    hit2 = lane == i2
    member = jnp.logical_or(hit1, hit2).astype(BF16)
    earlier = (lax.broadcasted_iota(jnp.int32, (tm, tm), 1)
               < lax.broadcasted_iota(jnp.int32, (tm, tm), 0)).astype(BF16)
    before = _dot(earlier, member) + carry_scr[0:1, :]
    r1 = jnp.sum(jnp.where(hit1, before, 0.0), axis=1, keepdims=True)
    r2 = jnp.sum(jnp.where(hit2, before, 0.0), axis=1, keepdims=True)
    carry = carry_scr[0:1, :] + jnp.sum(member.astype(F32), axis=0, keepdims=True)
    carry_scr[...] = jnp.broadcast_to(carry, carry_scr.shape)
    cnt_ref[...] = jnp.broadcast_to(carry, cnt_ref.shape)
    info = jnp.where(lane == 0, g1, 0.0)
    info = jnp.where(lane == 1, g2, info)
    info = jnp.where(lane == 2, i1, info)
    info = jnp.where(lane == 3, i2, info)
    info = jnp.where(lane == 4, r1, info)
    info = jnp.where(lane == 5, r2, info)
    info_ref[...] = info


def _router(h, w_pad, b_pad):
    tm = TM
    return pl.pallas_call(
        _router_kernel,
        grid=(T // tm,),
        in_specs=[pl.BlockSpec((tm, D), lambda i: (i, 0)), _full(w_pad.shape), _full(b_pad.shape)],
        out_specs=[pl.BlockSpec((tm, LANES_V7X), lambda i: (i, 0)), _full((8, LANES_V7X))],
        out_shape=[jax.ShapeDtypeStruct((T, LANES_V7X), F32),
                   jax.ShapeDtypeStruct((8, LANES_V7X), F32)],
        scratch_shapes=[pltpu.VMEM((8, LANES_V7X), F32)],
        compiler_params=_params(("arbitrary",), 32),
        name="router",
    )(h, w_pad, b_pad)


def _moe_row_copy(h_hbm, x_scr, sem, tok, r):
    return pltpu.make_async_copy(h_hbm.at[pl.ds(tok, 1), :], x_scr.at[pl.ds(r, 1), :], sem)


def _moe_out_copy(acc_scr, y_hbm, sem, row0, u):
    off = pl.multiple_of(u * MOE_UNIT, MOE_UNIT)
    dst = pl.multiple_of(row0 + off, MOE_UNIT)
    return pltpu.make_async_copy(acc_scr.at[pl.ds(off, MOE_UNIT), :],
                                 y_hbm.at[pl.ds(dst, MOE_UNIT), :], sem)


def _moe_kernel(te_ref, row0_ref, nunit_ref, used_ref, pad_ref, dest_ref,
                h_hbm, wg_ref, wu_ref, wd_ref, y_hbm,
                slot_tok, x_scr, acc_scr, gsem, osem):
    t = pl.program_id(0)
    f = pl.program_id(1)
    nf = pl.num_programs(1)

    @pl.when(jnp.logical_and(t == 0, f == 0))
    def _():
        for e in range(NE):
            def clear(s, c):
                slot_tok[s] = 0
                return c
            lax.fori_loop(pad_ref[2 * e], pad_ref[2 * e + 1], clear, 0)

        def place(tok, c):
            slot_tok[dest_ref[2 * tok]] = tok
            slot_tok[dest_ref[2 * tok + 1]] = tok
            return c
        lax.fori_loop(0, T, place, 0, unroll=8)

    @pl.when(t < used_ref[0])
    def _():
        row0 = row0_ref[t]
        nunit = nunit_ref[t]
        nrows = nunit * MOE_UNIT

        @pl.when(f == 0)
        def _():
            def issue_unit(u, c):
                def issue(r, cc):
                    row = u * MOE_UNIT + r
                    _moe_row_copy(h_hbm, x_scr, gsem, slot_tok[row0 + row], row).start()
                    return cc
                return lax.fori_loop(0, MOE_UNIT, issue, c, unroll=8)
            lax.fori_loop(0, nunit, issue_unit, 0)

            @pl.when(t > 0)
            def _():
                def finish(u, c):
                    _moe_out_copy(acc_scr, y_hbm, osem, row0_ref[t - 1], u).wait()
                    return c
                lax.fori_loop(0, nunit_ref[t - 1], finish, 0)

            def clear_acc(u, c):
                off = pl.multiple_of(u * MOE_UNIT, MOE_UNIT)
                acc_scr[pl.ds(off, MOE_UNIT), :] = jnp.zeros((MOE_UNIT, D), F32)
                return c
            lax.fori_loop(0, nunit, clear_acc, 0)

            def drain_unit(u, c):
                def drain(r, cc):
                    _moe_row_copy(h_hbm, x_scr, gsem, 0, u * MOE_UNIT + r).wait()
                    return cc
                return lax.fori_loop(0, MOE_UNIT, drain, c, unroll=8)
            lax.fori_loop(0, nunit, drain_unit, 0)

        def chunk(off, rows):
            x = x_scr[pl.ds(off, rows), :]
            hid = _silu(_dot(x, wg_ref[...])) * _dot(x, wu_ref[...])
            acc_scr[pl.ds(off, rows), :] += _dot(hid, wd_ref[...])

        def quad(c, carry):
            chunk(pl.multiple_of(c * 4 * MOE_UNIT, 4 * MOE_UNIT), 4 * MOE_UNIT)
            return carry
        lax.fori_loop(0, nunit // 4, quad, 0)

        @pl.when(nunit % 4 >= 2)
        def _():
            chunk(pl.multiple_of((nunit // 4) * 4 * MOE_UNIT, 2 * MOE_UNIT), 2 * MOE_UNIT)

        @pl.when(nunit % 2 == 1)
        def _():
            chunk(pl.multiple_of((nunit - 1) * MOE_UNIT, MOE_UNIT), MOE_UNIT)

        @pl.when(f == nf - 1)
        def _():
            def start(u, carry):
                _moe_out_copy(acc_scr, y_hbm, osem, row0, u).start()
                return carry
            lax.fori_loop(0, nunit, start, 0)

            @pl.when(t == used_ref[0] - 1)
            def _():
                def finish(u, carry):
                    _moe_out_copy(acc_scr, y_hbm, osem, row0, u).wait()
                    return carry
                lax.fori_loop(0, nunit, finish, 0)

    @pl.when(jnp.logical_and(t == pl.num_programs(0) - 1, f == nf - 1))
    def _():
        acc_scr[0:MOE_UNIT, :] = jnp.zeros((MOE_UNIT, D), F32)

        def fill(u, carry):
            cp = _moe_out_copy(acc_scr, y_hbm, osem, u * MOE_UNIT, 0)
            cp.start()
            cp.wait()
            return carry
        lax.fori_loop(used_ref[1], MOE_CAP // MOE_UNIT, fill, 0)


def _moe_experts(h, wg, wu, wd, tile_e, tile_row0, tile_nunit, used, pad, dest):
    nf = FE // MOE_TF

    def fidx(t, f, us):
        return jnp.where(t < us[0], f, nf - 1)

    grid_spec = pltpu.PrefetchScalarGridSpec(
        num_scalar_prefetch=6,
        grid=(MOE_NT, nf),
        in_specs=[pl.BlockSpec(memory_space=pl.ANY),
                  pl.BlockSpec((None, D, MOE_TF), lambda t, f, te, r0, nu, us, pa, de: (te[t], 0, fidx(t, f, us))),
                  pl.BlockSpec((None, D, MOE_TF), lambda t, f, te, r0, nu, us, pa, de: (te[t], 0, fidx(t, f, us))),
                  pl.BlockSpec((None, MOE_TF, D), lambda t, f, te, r0, nu, us, pa, de: (te[t], fidx(t, f, us), 0))],
        out_specs=pl.BlockSpec(memory_space=pl.ANY),
        scratch_shapes=[pltpu.SMEM((MOE_CAP,), jnp.int32),
                        pltpu.VMEM((MOE_TMAX, D), F32),
                        pltpu.VMEM((MOE_TMAX, D), F32),
                        pltpu.SemaphoreType.DMA,
                        pltpu.SemaphoreType.DMA],
    )
    return pl.pallas_call(
        _moe_kernel,
        grid_spec=grid_spec,
        out_shape=jax.ShapeDtypeStruct((MOE_CAP, D), F32),
        compiler_params=_params(("arbitrary", "arbitrary"), 58),
        name="moe_experts",
    )(tile_e, tile_row0, tile_nunit, used, pad, dest, h, wg, wu, wd)


def _cmb_copy(y_hbm, ybuf, sem, buf, slot, k, r):
    return pltpu.make_async_copy(y_hbm.at[pl.ds(slot, 1), :], ybuf.at[buf, k, pl.ds(r, 1), :],
                                 sem.at[buf])


def _combine_kernel(dest_ref, h_ref, info_ref, y_hbm, g_ref, b_ref, o_ref, ybuf, sem):
    tm = h_ref.shape[0]
    i = pl.program_id(0)
    cur = lax.rem(i, 2)

    def gather(tile, buf):
        def issue(r, c):
            tok = tile * tm + r
            _cmb_copy(y_hbm, ybuf, sem, buf, dest_ref[2 * tok], 0, r).start()
            _cmb_copy(y_hbm, ybuf, sem, buf, dest_ref[2 * tok + 1], 1, r).start()
            return c
        lax.fori_loop(0, tm, issue, 0, unroll=8)

    @pl.when(i == 0)
    def _():
        gather(0, 0)

    @pl.when(i + 1 < pl.num_programs(0))
    def _():
        gather(i + 1, 1 - cur)

    def drain(r, c):
        _cmb_copy(y_hbm, ybuf, sem, cur, 0, 0, r).wait()
        _cmb_copy(y_hbm, ybuf, sem, cur, 0, 1, r).wait()
        return c
    lax.fori_loop(0, tm, drain, 0, unroll=8)
    info = info_ref[...]
    moe = info[:, 0:1] * ybuf[cur, 0] + info[:, 1:2] * ybuf[cur, 1]
    o_ref[...] = _layer_norm(ALPHA * h_ref[...] + moe, g_ref[...], b_ref[...])


def _combine(dest, h, info, y_rows, g, b):
    tm = CMB_TM
    grid_spec = pltpu.PrefetchScalarGridSpec(
        num_scalar_prefetch=1,
        grid=(T // tm,),
        in_specs=[pl.BlockSpec((tm, D), lambda i, de: (i, 0)),
                  pl.BlockSpec((tm, LANES_V7X), lambda i, de: (i, 0)),
                  pl.BlockSpec(memory_space=pl.ANY),
                  pl.BlockSpec((1, D), lambda i, de: (0, 0)),
                  pl.BlockSpec((1, D), lambda i, de: (0, 0))],
        out_specs=pl.BlockSpec((tm, D), lambda i, de: (i, 0)),
        scratch_shapes=[pltpu.VMEM((2, 2, tm, D), F32), pltpu.SemaphoreType.DMA((2,))],
    )
    return pl.pallas_call(
        _combine_kernel,
        grid_spec=grid_spec,
        out_shape=jax.ShapeDtypeStruct((T, D), F32),
        compiler_params=_params(("arbitrary",), 32),
        name="moe_combine_ln",
    )(dest, h, info, y_rows, g, b)


def _rope_tables(pos):
    inv_freq = THETA ** (-jnp.arange(0, ROPE, 2, dtype=F32) / ROPE)
    ang = pos[:, None] * inv_freq[None, :]
    cos = jnp.cos(ang).astype(F32)
    sin = jnp.sin(ang).astype(F32)
    zero = jnp.zeros((pos.shape[0], LANES_V7X - ROPE), F32)
    return (jnp.concatenate([cos, cos, zero], axis=1),
            jnp.concatenate([-sin, sin, zero], axis=1))


def _expert_tiles(counts):
    units = (counts + MOE_UNIT - 1) // MOE_UNIT
    seg_unit = jnp.cumsum(units) - units
    ntile = (units + MOE_TUNITS - 1) // MOE_TUNITS
    tile_end = jnp.cumsum(ntile)
    n_used = tile_end[-1]
    tid = jnp.arange(MOE_NT, dtype=jnp.int32)
    e = jnp.minimum(jnp.sum(tid[:, None] >= tile_end[None, :], axis=1), NE - 1).astype(jnp.int32)
    local = tid - (tile_end - ntile)[e]
    nt = jnp.maximum(ntile[e], 1)
    lo = (local * units[e]) // nt
    hi = ((local + 1) * units[e]) // nt
    used = tid < n_used
    last = jnp.maximum(n_used - 1, 0)
    e = jnp.where(used, e, e[last])
    row0 = jnp.where(used, (seg_unit[e] + lo) * MOE_UNIT, 0)
    nunit = jnp.where(used, hi - lo, 0)
    used_info = jnp.stack([n_used, jnp.sum(units)]).astype(jnp.int32)
    seg_row = seg_unit * MOE_UNIT
    pad = jnp.stack([seg_row + counts, seg_row + units * MOE_UNIT], axis=1).reshape(2 * NE)
    return (e.astype(jnp.int32), row0.astype(jnp.int32), nunit.astype(jnp.int32),
            used_info, pad.astype(jnp.int32), seg_row.astype(jnp.int32))


def kernel(x, meta_tokens, attn_w_in, fox_forget_bias, mla_q_norm, mla_kv_norm, mla_w_uq, mla_w_ukv, attn_w_out, pool_w_in, pool_w_group, pool_scale, pool_w_out, ffn_w_gate, ffn_w_up, ffn_w_down, moe_w_router, moe_b_router, moe_w_gate, moe_w_up, moe_w_down, ln_mix_g, ln_mix_b, ln_ffn_g, ln_ffn_b):
    assert x.shape == (1, T, D) and meta_tokens.shape == (NM, D)
    xt = x[0]
    meta = meta_tokens.astype(x.dtype)

    w_in = attn_w_in[0]
    o_kr, o_qf, o_f = QL + KVL, QL + KVL + ROPE, QL + KVL + ROPE + 3 * H * HD
    kr1, kr2 = w_in[:, o_kr:o_kr + ROPE // 2], w_in[:, o_kr + ROPE // 2:o_qf]
    zpad = lambda n: jnp.zeros((D, n), F32)
    w_lat = jnp.concatenate(
        [w_in[:, :o_kr], kr1, kr2, zpad(LANES_V7X - ROPE), kr2, kr1, zpad(LANES_V7X - ROPE),
         w_in[:, o_f:o_f + H], zpad(LANES_V7X - H)], axis=1).astype(BF16)
    w_fox = w_in[:, o_qf:o_f].astype(BF16)
    wuq = mla_w_uq[0].reshape(QL, H, NOPE + ROPE)
    uq_n, uq_1, uq_2 = wuq[..., :NOPE], wuq[..., NOPE:NOPE + ROPE // 2], wuq[..., NOPE + ROPE // 2:]
    zq = jnp.zeros((QL, H, LANES_V7X - ROPE), F32)
    w_uq = jnp.concatenate([uq_n, uq_1, uq_2, zq], axis=-1).reshape(QL, H * DQ_MLA).astype(BF16)
    w_uqs = jnp.concatenate([uq_2, uq_1, zq], axis=-1).reshape(QL, H * HD).astype(BF16)
    wukv = mla_w_ukv[0].reshape(KVL, H, NOPE + HD)
    w_ukv = jnp.concatenate([wukv[..., :NOPE].reshape(KVL, H * NOPE),
                             wukv[..., NOPE:].reshape(KVL, H * HD)], axis=1).astype(BF16)
    w_ao = attn_w_out[0]
    w_fg, w_fu, w_fd = ffn_w_gate[0], ffn_w_up[0], ffn_w_down[0]
    w_pi, w_pg, w_po = pool_w_in[0], pool_w_group[0], pool_w_out[0]
    qn, kvn = mla_q_norm[0][None, :], mla_kv_norm[0][None, :]
    row = lambda v: v[None, :]

    ctab_t, stab_t = _rope_tables(jnp.arange(NM, NM + T, dtype=F32))
    ctab_m, stab_m = _rope_tables(jnp.arange(NM, dtype=F32))

    fox_scale = HD ** -0.5 * LOG2E
    q_t, k_t, vt_t, fl_t = _mla_proj(xt, w_lat, qn, kvn, w_uq, w_uqs, w_ukv, ctab_t, stab_t, TK)
    q_m, k_m, v_m, _ = _mla_proj(meta, w_lat, qn, kvn, w_uq, w_uqs, w_ukv, ctab_m, stab_m, NM)
    fox_qk, fox_vt = _fox_proj(xt, w_fox.reshape(D, 3, H * HD).transpose(1, 0, 2), fox_scale)
    fox_m = _matmul(meta, w_fox, NM, H * HD, BF16, scaled_blocks=1, scale=fox_scale, name="fox_proj_meta")

    fl = fl_t[:, :H].T.reshape(H, T // LANES_V7X, LANES_V7X)
    bias_b = jnp.broadcast_to(fox_forget_bias[0][:, None, None], (H, 1, LANES_V7X))
    parts = _forget_cumsum(fl, bias_b).reshape(H, 6, T)
    pos = parts[:, 0:3].transpose(2, 0, 1)
    neg = parts[:, 3:6].transpose(2, 0, 1)
    lanes = lambda pieces: jnp.concatenate(pieces, axis=2).reshape(-1, H * AUG_STRIDE)
    cst = lambda rows, n, v: jnp.full((rows, H, n), v, BF16)
    aug = (lanes([cst(T, 3, 1.0), pos, cst(T, AUG_STRIDE - 6, 0.0)]),
           lanes([neg, cst(T, AUG_STRIDE - 3, 0.0)]),
           lanes([cst(NM, 3, 0.0), cst(NM, 3, -1.0), cst(NM, AUG_STRIDE - 6, 0.0)]))

    to_vmt = lambda v: v.reshape(NM, H, HD).transpose(1, 2, 0)
    o_mla = _attention(q_t, k_t, vt_t, k_m, to_vmt(v_m), 0, DQ_MLA, name="mla_attn")
    o_fox = _attention(fox_qk, fox_qk, fox_vt, fox_m, to_vmt(fox_m[:, 2 * H * HD:]),
                       H, HD, aug, name="fox_attn")
    om_mla = _meta_attention(q_m, k_m, v_m, 0, 0, 0, DQ_MLA, "mla_attn_meta")
    om_fox = _meta_attention(fox_m, fox_m, fox_m, 0, H, 2 * H, HD, "fox_attn_meta")

    g0, b0, g1, b1 = row(ln_mix_g[0]), row(ln_mix_b[0]), row(ln_ffn_g[0]), row(ln_ffn_b[0])
    h_t = _attn_out(o_mla, o_fox, w_ao, xt, g0, b0, TM)
    h_m = _attn_out(om_mla, om_fox, w_ao, meta, g0, b0, NM)
    h_t = _ffn(h_t, w_fg, w_fu, w_fd, g1, b1, FFN_TM)
    h_m = _ffn(h_m, w_fg, w_fu, w_fd, g1, b1, NM)

    hp_t = _matmul(h_t, w_pi, TM, D, F32, name="pool_in")
    hp_m = _matmul(h_m, w_pi, NM, D, F32, name="pool_in_meta")
    h_t = _pool_mixer(hp_t, hp_m, h_t, w_pg, row(pool_scale[0]), w_po,
                      row(ln_mix_g[1]), row(ln_mix_b[1]))

    w_r = jnp.concatenate([moe_w_router[0], jnp.zeros((D, LANES_V7X - NE), F32)], axis=1)
    b_r = jnp.concatenate([moe_b_router[0], jnp.full((LANES_V7X - NE,), NEG, F32)])[None, :]
    info, cnt = _router(h_t, w_r, b_r)
    counts = cnt[0, :NE].astype(jnp.int32)
    tile_e, tile_row0, tile_nunit, used, pad, seg_row = _expert_tiles(counts)
    experts = info[:, 2:4].astype(jnp.int32)
    ranks = info[:, 4:6].astype(jnp.int32)
    dest = (seg_row[experts] + ranks).reshape(2 * T)
    y_rows = _moe_experts(h_t, moe_w_gate[0], moe_w_up[0], moe_w_down[0],
                          tile_e, tile_row0, tile_nunit, used, pad, dest)
    out = _combine(dest, h_t, info, y_rows, row(ln_ffn_g[1]), row(ln_ffn_b[1]))
    return out[None]
```

```python
import functools
import math

import jax
import jax.numpy as jnp
import numpy as np
from jax import lax
from jax.experimental import pallas as pl
from jax.experimental.pallas import tpu as pltpu

D = 2048
T = 8192
NM = 16
H = 8
QL = 512
KVL = 512
NOPE = 128
ROPE = 64
HD = 128
DQ_MLA = 256
LAT_COLS = QL + KVL + 128
VROWS = HD + 16
THETA = 10000.0
POOL_W = (2, 4, 8, 16)
PG = D // 4
FD = 5632
NE = 8
FE = 7168
ALPHA = 4.0 ** 0.25
LN_EPS = 1e-5
RMS_EPS = 1e-6
NEG = -1e30
LOG2E = 1.4426950408889634

F32 = jnp.float32
BF16 = jnp.bfloat16

LANES_V7X = 128
VMEM_BYTES_V7X = 64 * 1024 * 1024
MIB = 1024 * 1024

TM = 512
TK = 512
ATT_TQ = 2 * TK
FFN_TM = 1024
FFN_CHUNK = 512
FFN_TF = 512
POOL_TM = 256
MOE_UNIT = 128
MOE_TUNITS = 10
MOE_TMAX = MOE_TUNITS * MOE_UNIT
MOE_TF = 512
MOE_CAP = 2 * T + NE * MOE_UNIT
MOE_NT = MOE_CAP // MOE_TMAX + NE + 1
CMB_TM = 256


def _params(sem, vmem_mib):
    return pltpu.CompilerParams(dimension_semantics=sem, vmem_limit_bytes=vmem_mib * MIB)


def _full(shape):
    n = len(shape)
    return pl.BlockSpec(shape, lambda *_: (0,) * n)


def _resident(shape):
    n = len(shape)
    return pl.BlockSpec(shape, lambda *_: (0,) * n, pipeline_mode=pl.Buffered(1))


def _layer_norm(x, g, b):
    mu = jnp.mean(x, axis=-1, keepdims=True)
    xc = x - mu
    var = jnp.mean(xc * xc, axis=-1, keepdims=True)
    return xc * lax.rsqrt(var + LN_EPS) * g + b


def _rms_norm(x, g):
    return x * lax.rsqrt(jnp.mean(x * x, axis=-1, keepdims=True) + RMS_EPS) * g


def _dot(a, b):
    return jnp.dot(a, b, preferred_element_type=F32)


def _dot_nt(a, b):
    return lax.dot_general(a, b, (((1,), (1,)), ((), ())), preferred_element_type=F32)


def _vt_store(v_ref, h, v):
    v_ref[h, 0, 0:HD, :] = v.T.astype(BF16)
    rows = lax.broadcasted_iota(jnp.int32, (VROWS - HD, v.shape[0]), 0)
    v_ref[h, 0, HD:VROWS, :] = (rows == 0).astype(BF16)


def _mla_proj_kernel(x_ref, win_ref, wsm_ref, qn_ref, kvn_ref, wuq_ref, wuqs_ref, wukv_ref,
                     c_ref, s_ref, q_ref, k_ref, v_ref, fl_ref, *, qscale, transposed_v):
    x = x_ref[...]
    lat = _dot(x, win_ref[...])
    small = _dot(x, wsm_ref[...])
    cq = lat[:, 0:QL]
    ckv = lat[:, QL:QL + KVL]
    kr = lat[:, QL + KVL:LAT_COLS]
    krs = small[:, 0:LANES_V7X]
    fl_ref[...] = small[:, LANES_V7X:]
    cos = c_ref[...]
    sin = s_ref[...]
    qn = _rms_norm(cq, qn_ref[...]).astype(BF16)
    qpre = _dot(qn, wuq_ref[...])
    qsw = _dot(qn, wuqs_ref[...])
    for h in range(H):
        nope = qpre[:, DQ_MLA * h:DQ_MLA * h + NOPE]
        rot = (qpre[:, DQ_MLA * h + NOPE:DQ_MLA * (h + 1)] * cos
               + qsw[:, HD * h:HD * (h + 1)] * sin)
        q_ref[:, DQ_MLA * h:DQ_MLA * h + NOPE] = (nope * qscale).astype(BF16)
        q_ref[:, DQ_MLA * h + NOPE:DQ_MLA * (h + 1)] = (rot * qscale).astype(BF16)
    kvn = _rms_norm(ckv, kvn_ref[...]).astype(BF16)
    kv = _dot(kvn, wukv_ref[...])
    krot = (kr * cos + krs * sin).astype(BF16)
    for h in range(H):
        k_ref[:, DQ_MLA * h:DQ_MLA * h + NOPE] = kv[:, HD * h:HD * (h + 1)].astype(BF16)
        k_ref[:, DQ_MLA * h + NOPE:DQ_MLA * (h + 1)] = krot
    if transposed_v:
        for h in range(H):
            _vt_store(v_ref, h, kv[:, (H + h) * HD:(H + h + 1) * HD])
    else:
        v_ref[...] = kv[:, H * HD:].astype(BF16)


def _mla_proj(x, w_in, wsm, qn, kvn, wuq, wuqs, wukv, ctab, stab, tm):
    m = x.shape[0]
    qscale = (NOPE + ROPE) ** -0.5 * LOG2E
    row = lambda w: pl.BlockSpec((tm, w), lambda i: (i, 0))
    transposed_v = tm == TK
    if transposed_v:
        v_spec = pl.BlockSpec((H, 1, VROWS, TK), lambda i: (0, i, 0, 0))
        v_shape = jax.ShapeDtypeStruct((H, m // TK, VROWS, TK), BF16)
    else:
        v_spec, v_shape = row(H * HD), jax.ShapeDtypeStruct((m, H * HD), BF16)
    lat_spec = pl.BlockSpec((D, LAT_COLS), lambda i: (0, 0), pipeline_mode=pl.Buffered(1))
    return pl.pallas_call(
        functools.partial(_mla_proj_kernel, qscale=qscale, transposed_v=transposed_v),
        grid=(m // tm,),
        in_specs=[row(D), lat_spec, _full(wsm.shape), _full(qn.shape), _full(kvn.shape),
                  _resident(wuq.shape), _resident(wuqs.shape), _resident(wukv.shape),
                  row(LANES_V7X), row(LANES_V7X)],
        out_specs=[row(H * DQ_MLA), row(H * DQ_MLA), v_spec, row(LANES_V7X)],
        out_shape=[jax.ShapeDtypeStruct((m, H * DQ_MLA), BF16),
                   jax.ShapeDtypeStruct((m, H * DQ_MLA), BF16),
                   v_shape,
                   jax.ShapeDtypeStruct((m, LANES_V7X), F32)],
        compiler_params=_params(("parallel",), 56),
        name="mla_proj",
    )(x, w_in, wsm, qn, kvn, wuq, wuqs, wukv, ctab, stab)


def _mm_kernel(x_ref, w_ref, o_ref, *, scaled_blocks, scale):
    y = _dot(x_ref[...], w_ref[...])
    if scaled_blocks:
        y = jnp.where(pl.program_id(1) < scaled_blocks, y * scale, y)
    o_ref[...] = y.astype(o_ref.dtype)


def _matmul(x, w, tm, tn, out_dtype, scaled_blocks=0, scale=1.0, name="matmul"):
    m, k = x.shape
    n = w.shape[1]
    return pl.pallas_call(
        functools.partial(_mm_kernel, scaled_blocks=scaled_blocks, scale=scale),
        grid=(m // tm, n // tn),
        in_specs=[pl.BlockSpec((tm, k), lambda i, j: (i, 0)),
                  _resident((k, n)) if tn == n else pl.BlockSpec((k, tn), lambda i, j: (0, j))],
        out_specs=pl.BlockSpec((tm, tn), lambda i, j: (i, j)),
        out_shape=jax.ShapeDtypeStruct((m, n), out_dtype),
        compiler_params=_params(("parallel", "parallel"), 48),
        name=name,
    )(x, w)


def _fox_proj_kernel(x_ref, w_ref, qk_ref, vt_ref, *, scale):
    j = pl.program_id(1)
    n = H * HD
    part = lambda c: _dot(x_ref[...].astype(BF16), w_ref[:, c * n:(c + 1) * n])

    @pl.when(j == 0)
    def _():
        qk_ref[...] = (part(0) * scale).astype(BF16)

    @pl.when(j == 1)
    def _():
        qk_ref[...] = part(1).astype(BF16)

    @pl.when(j == 2)
    def _():
        y = part(2)
        for h in range(H):
            _vt_store(vt_ref, h, y[:, h * HD:(h + 1) * HD])


def _fox_proj(x, w, scale):
    n = H * HD
    return pl.pallas_call(
        functools.partial(_fox_proj_kernel, scale=scale),
        grid=(T // TK, 3),
        in_specs=[pl.BlockSpec((TK, D), lambda i, j: (i, 0)),
                  _resident((D, 3 * n))],
        out_specs=[pl.BlockSpec((TK, n), lambda i, j: (i, jnp.minimum(j, 1))),
                   pl.BlockSpec((H, 1, VROWS, TK), lambda i, j: (0, i, 0, 0))],
        out_shape=[jax.ShapeDtypeStruct((T, 2 * n), BF16),
                   jax.ShapeDtypeStruct((H, T // TK, VROWS, TK), BF16)],
        compiler_params=_params(("parallel", "arbitrary"), 48),
        name="fox_proj",
    )(x, w)


def _split3(a):
    hi = a.astype(BF16)
    r = a - hi.astype(F32)
    mid = r.astype(BF16)
    lo = (r - mid.astype(F32)).astype(BF16)
    return hi, mid, lo


def _forget_cumsum_kernel(fl_ref, b_ref, o_ref):
    z = fl_ref[...] + b_ref[...]
    logf = jnp.minimum(z, 0.0) - jnp.log(1.0 + jnp.exp(-jnp.abs(z)))
    n = LANES_V7X
    ri = lax.broadcasted_iota(jnp.int32, (n, n), 0)
    ci = lax.broadcasted_iota(jnp.int32, (n, n), 1)
    upper = (ri <= ci).astype(BF16)
    ones = jnp.ones((n, n), BF16)
    within = jnp.zeros(logf.shape, F32)
    total = jnp.zeros(logf.shape, F32)
    for part in _split3(logf):
        within += _dot(part, upper)
        total += _dot(part, ones)
    r = logf.shape[0]
    lower = (lax.broadcasted_iota(jnp.int32, (r, r), 1)
             < lax.broadcasted_iota(jnp.int32, (r, r), 0)).astype(BF16)
    before = jnp.zeros(logf.shape, F32)
    for part in _split3(total):
        before += _dot(lower, part)
    c2 = (within + before) * LOG2E
    for p, part in enumerate(_split3(c2)):
        o_ref[p] = part
        o_ref[3 + p] = -part


def _forget_cumsum(fl_t, bias_b):
    r = T // LANES_V7X
    return pl.pallas_call(
        _forget_cumsum_kernel,
        grid=(H,),
        in_specs=[pl.BlockSpec((None, r, LANES_V7X), lambda h: (h, 0, 0)),
                  pl.BlockSpec((None, 1, LANES_V7X), lambda h: (h, 0, 0))],
        out_specs=pl.BlockSpec((None, 6, r, LANES_V7X), lambda h: (h, 0, 0, 0)),
        out_shape=jax.ShapeDtypeStruct((H, 6, r, LANES_V7X), BF16),
        compiler_params=_params(("parallel",), 32),
        name="forget_cumsum",
    )(fl_t, bias_b)


ATT_NH = 2
AUG_STRIDE = 16


def _attn_kernel(*refs, fox, dq):
    if fox:
        (q_ref, k_ref, vt_ref, km_ref, vmt_ref, qa_ref, ka_ref, kam_ref, o_ref,
         m_scr, acc_scr, s0_scr, s1_scr, mb0_scr, mb1_scr, qf_scr) = refs
    else:
        (q_ref, k_ref, vt_ref, km_ref, vmt_ref, o_ref,
         m_scr, acc_scr, s0_scr, s1_scr, mb0_scr, mb1_scr) = refs
    hp = pl.program_id(0)
    i = pl.program_id(1)
    m_scr[...] = jnp.full(m_scr.shape, -jnp.inf, F32)
    acc_scr[...] = jnp.zeros(acc_scr.shape, F32)
    if fox:
        lane = lax.broadcasted_iota(jnp.int32, qa_ref.shape, 1)
        for g in range(ATT_NH):
            lo = (hp * ATT_NH + g) * AUG_STRIDE
            own = jnp.logical_and(lane >= lo, lane < lo + AUG_STRIDE)
            qf_scr[g, :, 0:dq] = q_ref[:, g * dq:(g + 1) * dq]
            qf_scr[g, :, dq:2 * dq] = jnp.where(own, qa_ref[...], jnp.zeros_like(qa_ref))

    def query(g):
        return qf_scr[g] if fox else q_ref[:, g * dq:(g + 1) * dq]

    def update(g, st, mb, vt):
        m_prev = m_scr[g]
        m_new = jnp.maximum(m_prev, mb)
        alpha = jnp.exp2(m_prev - m_new)
        p = jnp.exp2(st - m_new)
        acc_scr[g] = alpha * acc_scr[g] + _dot(vt, p.astype(BF16))
        m_scr[g] = m_new

    def stage1(slot, j):
        s_ref, mb_ref = slots[slot]
        off = pl.multiple_of(j * TK, TK)
        for g in range(ATT_NH):
            kb = k_ref[pl.ds(off, TK), g * dq:(g + 1) * dq]
            if fox:
                kb = jnp.concatenate([kb, ka_ref[pl.ds(off, TK), :]], axis=1)
            st = _dot_nt(kb, query(g))
            s_ref[g] = st
            mb_ref[g] = jnp.max(st, axis=0, keepdims=True)

    def stage2(slot, j, diag_key0=None):
        s_ref, mb_ref = slots[slot]
        for g in range(ATT_NH):
            st = s_ref[g]
            mb = mb_ref[g]
            if diag_key0 is not None:
                keys = lax.broadcasted_iota(jnp.int32, st.shape, 0) + diag_key0
                qrys = lax.broadcasted_iota(jnp.int32, st.shape, 1)
                st = jnp.where(keys <= qrys, st, NEG)
                mb = jnp.max(st, axis=0, keepdims=True)
            update(g, st, mb, vt_ref[g, j])

    slots = ((s0_scr, mb0_scr), (s1_scr, mb1_scr))
    stage1(0, 0)

    def body(jj, carry):
        j = 2 * jj
        stage1(1, j + 1)
        stage2(0, j)
        stage1(0, j + 2)
        stage2(1, j + 1)
        return carry

    lax.fori_loop(0, i, body, 0)
    stage1(1, 2 * i + 1)
    stage2(0, 2 * i, diag_key0=0)
    stage2(1, 2 * i + 1, diag_key0=TK)

    for g in range(ATT_NH):
        kb = km_ref[:, g * dq:(g + 1) * dq]
        if fox:
            kb = jnp.concatenate([kb, kam_ref[...]], axis=1)
        st = _dot_nt(kb, query(g))
        update(g, st, jnp.max(st, axis=0, keepdims=True), vmt_ref[g])
        acc = acc_scr[g]
        o = (acc[0:HD] / acc[HD:HD + 1]).T
        o_ref[:, g * HD:(g + 1) * HD] = o.astype(o_ref.dtype)


def _attention(q, k, vt, km, vmt, k_col0, dq, aug=None, name="attn"):
    fox = aug is not None
    nh, tq = ATT_NH, ATT_TQ
    in_specs = [pl.BlockSpec((tq, nh * dq), lambda h, i: (i, h)),
                pl.BlockSpec((T, nh * dq), lambda h, i: (0, k_col0 // nh + h)),
                pl.BlockSpec((nh, T // TK, VROWS, TK), lambda h, i: (h, 0, 0, 0)),
                pl.BlockSpec((NM, nh * dq), lambda h, i: (0, k_col0 // nh + h)),
                pl.BlockSpec((nh, VROWS, NM), lambda h, i: (h, 0, 0))]
    args = [q, k, vt, km, vmt]
    scratch = [pltpu.VMEM((nh, 1, tq), F32), pltpu.VMEM((nh, VROWS, tq), F32),
               pltpu.VMEM((nh, TK, tq), F32), pltpu.VMEM((nh, TK, tq), F32),
               pltpu.VMEM((nh, 1, tq), F32), pltpu.VMEM((nh, 1, tq), F32)]
    if fox:
        in_specs += [pl.BlockSpec((tq, LANES_V7X), lambda h, i: (i, 0)),
                     _full((T, LANES_V7X)), _full((NM, LANES_V7X))]
        args += list(aug)
        scratch.append(pltpu.VMEM((nh, tq, 2 * dq), BF16))
    return pl.pallas_call(
        functools.partial(_attn_kernel, fox=fox, dq=dq),
        grid=(H // nh, T // tq),
        in_specs=in_specs,
        out_specs=pl.BlockSpec((tq, nh * HD), lambda h, i: (i, h)),
        out_shape=jax.ShapeDtypeStruct((T, H * HD), BF16),
        scratch_shapes=scratch,
        compiler_params=_params(("parallel", "arbitrary"), 56),
        name=name,
    )(*args)


def _meta_attn_kernel(q_ref, k_ref, v_ref, o_ref):
    s = _dot_nt(q_ref[...], k_ref[...])
    rows = lax.broadcasted_iota(jnp.int32, s.shape, 0)
    cols = lax.broadcasted_iota(jnp.int32, s.shape, 1)
    s = jnp.where(cols <= rows, s, NEG)
    p = jnp.exp2(s - jnp.max(s, axis=1, keepdims=True))
    o = _dot(p.astype(BF16), v_ref[...]) / jnp.sum(p, axis=1, keepdims=True)
    o_ref[...] = o.astype(o_ref.dtype)


def _meta_attention(q, k, v, q_col0, k_col0, v_col0, dq, name):
    return pl.pallas_call(
        _meta_attn_kernel,
        grid=(H,),
        in_specs=[pl.BlockSpec((NM, dq), lambda h: (0, q_col0 + h)),
                  pl.BlockSpec((NM, dq), lambda h: (0, k_col0 + h)),
                  pl.BlockSpec((NM, HD), lambda h: (0, v_col0 + h))],
        out_specs=pl.BlockSpec((NM, HD), lambda h: (0, h)),
        out_shape=jax.ShapeDtypeStruct((NM, H * HD), BF16),
        compiler_params=_params(("parallel",), 32),
        name=name,
    )(q, k, v)


def _attn_out_kernel(om_ref, of_ref, w_ref, h_ref, g_ref, b_ref, o_ref):
    m = _dot(om_ref[...], w_ref[0:H * HD, :]) + _dot(of_ref[...], w_ref[H * HD:, :])
    o_ref[...] = _layer_norm(ALPHA * h_ref[...] + m, g_ref[...], b_ref[...])


def _attn_out(om, of, w, h, g, b, tm):
    m = h.shape[0]
    row = lambda w_: pl.BlockSpec((tm, w_), lambda i: (i, 0))
    return pl.pallas_call(
        _attn_out_kernel,
        grid=(m // tm,),
        in_specs=[row(H * HD), row(H * HD), _resident(w.shape), row(D), _full(g.shape), _full(b.shape)],
        out_specs=row(D),
        out_shape=jax.ShapeDtypeStruct((m, D), F32),
        compiler_params=_params(("parallel",), 48),
        name="attn_out_ln",
    )(om, of, w, h, g, b)


def _silu(x):
    return x / (1.0 + jnp.exp(-x))


def _ffn_kernel(h_ref, wg_ref, wu_ref, wd_ref, g_ref, b_ref, o_ref):
    f = pl.program_id(1)
    rows = h_ref.shape[0]
    chunk = min(rows, FFN_CHUNK)

    @pl.when(f == 0)
    def _():
        o_ref[...] = jnp.zeros(o_ref.shape, F32)

    for c in range(rows // chunk):
        sl = slice(c * chunk, (c + 1) * chunk)
        x = h_ref[sl, :].astype(wg_ref.dtype)
        hid = _silu(_dot(x, wg_ref[...])) * _dot(x, wu_ref[...])
        o_ref[sl, :] += _dot(hid.astype(wd_ref.dtype), wd_ref[...])

    @pl.when(f == pl.num_programs(1) - 1)
    def _():
        o_ref[...] = _layer_norm(ALPHA * h_ref[...] + o_ref[...], g_ref[...], b_ref[...])


def _ffn(h, wg, wu, wd, g, b, tm):
    m = h.shape[0]
    once = pl.Buffered(1)
    return pl.pallas_call(
        _ffn_kernel,
        grid=(m // tm, FD // FFN_TF),
        in_specs=[pl.BlockSpec((tm, D), lambda i, f: (i, 0), pipeline_mode=once),
                  pl.BlockSpec((D, FFN_TF), lambda i, f: (0, f)),
                  pl.BlockSpec((D, FFN_TF), lambda i, f: (0, f)),
                  pl.BlockSpec((FFN_TF, D), lambda i, f: (f, 0)),
                  pl.BlockSpec((1, D), lambda i, f: (0, 0)),
                  pl.BlockSpec((1, D), lambda i, f: (0, 0))],
        out_specs=pl.BlockSpec((tm, D), lambda i, f: (i, 0)),
        out_shape=jax.ShapeDtypeStruct((m, D), F32),
        compiler_params=_params(("parallel", "arbitrary"), 58),
        name="ffn_ln",
    )(h, wg, wu, wd, g, b)


def _pool_kernel(hp_ref, prev_ref, meta_ref, h_ref, wg_ref, sc_ref, wo_ref, g_ref, b_ref, o_ref,
                 ext_scr, z_scr):
    tm = hp_ref.shape[0]
    halo = jnp.where(pl.program_id(0) == 0, meta_ref[...], prev_ref[...])
    ext_scr[0:NM, :] = halo
    ext_scr[NM:NM + tm, :] = hp_ref[...]
    for gi, w in enumerate(POOL_W):
        cols = slice(PG * gi, PG * (gi + 1))
        cur = ext_scr[NM:NM + tm, cols]
        tot = cur
        for j in range(1, w):
            tot = tot + ext_scr[NM - j:NM - j + tm, cols]
        diff = tot * (1.0 / w) - cur
        y = _dot(diff, wg_ref[gi]) * sc_ref[:, cols]
        z_scr[:, cols] = y
    m = _dot(z_scr[...], wo_ref[...])
    o_ref[...] = _layer_norm(ALPHA * h_ref[...] + m, g_ref[...], b_ref[...])


def _pool_mixer(hp, hp_meta, h, wgroup, scale, wout, g, b):
    tm = POOL_TM
    row = pl.BlockSpec((tm, D), lambda i: (i, 0))
    prev = pl.BlockSpec((NM, D), lambda i: (jnp.maximum(i * (tm // NM) - 1, 0), 0))
    return pl.pallas_call(
        _pool_kernel,
        grid=(T // tm,),
        in_specs=[row, prev, _full(hp_meta.shape), row, _resident(wgroup.shape), _full(scale.shape),
                  _resident(wout.shape), _full(g.shape), _full(b.shape)],
        out_specs=row,
        out_shape=jax.ShapeDtypeStruct((T, D), F32),
        scratch_shapes=[pltpu.VMEM((NM + tm, D), F32), pltpu.VMEM((tm, D), F32)],
        compiler_params=_params(("parallel",), 48),
        name="pool_ln",
    )(hp, hp, hp_meta, h, wgroup, scale, wout, g, b)


def _router_kernel(h_ref, w_ref, b_ref, info_ref, cnt_ref, carry_scr):
    tm = h_ref.shape[0]

    @pl.when(pl.program_id(0) == 0)
    def _():
        carry_scr[...] = jnp.zeros(carry_scr.shape, F32)

    logits = jnp.dot(h_ref[...], w_ref[...], precision=lax.Precision.HIGHEST,
                     preferred_element_type=F32) + b_ref[...]
    lane = lax.broadcasted_iota(jnp.int32, logits.shape, 1).astype(F32)
    nolane = float(LANES_V7X)
    m1 = jnp.max(logits, axis=1, keepdims=True)
    i1 = jnp.min(jnp.where(logits == m1, lane, nolane), axis=1, keepdims=True)
    rest = jnp.where(lane == i1, -jnp.inf, logits)
    m2 = jnp.max(rest, axis=1, keepdims=True)
    i2 = jnp.min(jnp.where(rest == m2, lane, nolane), axis=1, keepdims=True)
    e = jnp.exp(m2 - m1)
    g1 = 1.0 / (1.0 + e)
    g2 = e / (1.0 + e)
    hit1 = lane == i1
    hit2 = lane == i2
    member = jnp.logical_or(hit1, hit2).astype(BF16)
    earlier = (lax.broadcasted_iota(jnp.int32, (tm, tm), 1)
               < lax.broadcasted_iota(jnp.int32, (tm, tm), 0)).astype(BF16)
    before = _dot(earlier, member) + carry_scr[0:1, :]
    r1 = jnp.sum(jnp.where(hit1, before, 0.0), axis=1, keepdims=True)
    r2 = jnp.sum(jnp.where(hit2, before, 0.0), axis=1, keepdims=True)
    carry = carry_scr[0:1, :] + jnp.sum(member.astype(F32), axis=0, keepdims=True)
    carry_scr[...] = jnp.broadcast_to(carry, carry_scr.shape)
    cnt_ref[...] = jnp.broadcast_to(carry, cnt_ref.shape)
    info = jnp.where(lane == 0, g1, 0.0)
    info = jnp.where(lane == 1, g2, info)
    info = jnp.where(lane == 2, i1, info)
    info = jnp.where(lane == 3, i2, info)
    info = jnp.where(lane == 4, r1, info)
    info = jnp.where(lane == 5, r2, info)
    info_ref[...] = info


def _router(h, w_pad, b_pad):
    tm = TM
    return pl.pallas_call(
        _router_kernel,
        grid=(T // tm,),
        in_specs=[pl.BlockSpec((tm, D), lambda i: (i, 0)), _full(w_pad.shape), _full(b_pad.shape)],
        out_specs=[pl.BlockSpec((tm, LANES_V7X), lambda i: (i, 0)), _full((8, LANES_V7X))],
        out_shape=[jax.ShapeDtypeStruct((T, LANES_V7X), F32),
                   jax.ShapeDtypeStruct((8, LANES_V7X), F32)],
        scratch_shapes=[pltpu.VMEM((8, LANES_V7X), F32)],
        compiler_params=_params(("arbitrary",), 32),
        name="router",
    )(h, w_pad, b_pad)


def _moe_row_copy(h_hbm, x_scr, sem, tok, r):
    return pltpu.make_async_copy(h_hbm.at[pl.ds(tok, 1), :], x_scr.at[pl.ds(r, 1), :], sem)


def _moe_out_copy(acc_scr, y_hbm, sem, row0, u):
    off = pl.multiple_of(u * MOE_UNIT, MOE_UNIT)
    dst = pl.multiple_of(row0 + off, MOE_UNIT)
    return pltpu.make_async_copy(acc_scr.at[pl.ds(off, MOE_UNIT), :],
                                 y_hbm.at[pl.ds(dst, MOE_UNIT), :], sem)


def _moe_kernel(te_ref, row0_ref, nunit_ref, used_ref, pad_ref, dest_ref,
                h_hbm, wg_ref, wu_ref, wd_ref, y_hbm,
                slot_tok, x_scr, acc_scr, gsem, osem):
    t = pl.program_id(0)
    f = pl.program_id(1)
    nf = pl.num_programs(1)

    @pl.when(jnp.logical_and(t == 0, f == 0))
    def _():
        for e in range(NE):
            def clear(s, c):
                slot_tok[s] = 0
                return c
            lax.fori_loop(pad_ref[2 * e], pad_ref[2 * e + 1], clear, 0)

        def place(tok, c):
            slot_tok[dest_ref[2 * tok]] = tok
            slot_tok[dest_ref[2 * tok + 1]] = tok
            return c
        lax.fori_loop(0, T, place, 0, unroll=8)

    @pl.when(t < used_ref[0])
    def _():
        row0 = row0_ref[t]
        nunit = nunit_ref[t]
        nrows = nunit * MOE_UNIT

        @pl.when(f == 0)
        def _():
            def issue_unit(u, c):
                def issue(r, cc):
                    row = u * MOE_UNIT + r
                    _moe_row_copy(h_hbm, x_scr, gsem, slot_tok[row0 + row], row).start()
                    return cc
                return lax.fori_loop(0, MOE_UNIT, issue, c, unroll=8)
            lax.fori_loop(0, nunit, issue_unit, 0)

            @pl.when(t > 0)
            def _():
                def finish(u, c):
                    _moe_out_copy(acc_scr, y_hbm, osem, row0_ref[t - 1], u).wait()
                    return c
                lax.fori_loop(0, nunit_ref[t - 1], finish, 0)

            def clear_acc(u, c):
                off = pl.multiple_of(u * MOE_UNIT, MOE_UNIT)
                acc_scr[pl.ds(off, MOE_UNIT), :] = jnp.zeros((MOE_UNIT, D), F32)
                return c
            lax.fori_loop(0, nunit, clear_acc, 0)

            def drain_unit(u, c):
                def drain(r, cc):
                    _moe_row_copy(h_hbm, x_scr, gsem, 0, u * MOE_UNIT + r).wait()
                    return cc
                return lax.fori_loop(0, MOE_UNIT, drain, c, unroll=8)
            lax.fori_loop(0, nunit, drain_unit, 0)

        def chunk(off, rows):
            x = x_scr[pl.ds(off, rows), :]
            hid = _silu(_dot(x, wg_ref[...])) * _dot(x, wu_ref[...])
            acc_scr[pl.ds(off, rows), :] += _dot(hid, wd_ref[...])

        def quad(c, carry):
            chunk(pl.multiple_of(c * 4 * MOE_UNIT, 4 * MOE_UNIT), 4 * MOE_UNIT)
            return carry
        lax.fori_loop(0, nunit // 4, quad, 0)

        @pl.when(nunit % 4 >= 2)
        def _():
            chunk(pl.multiple_of((nunit // 4) * 4 * MOE_UNIT, 2 * MOE_UNIT), 2 * MOE_UNIT)

        @pl.when(nunit % 2 == 1)
        def _():
            chunk(pl.multiple_of((nunit - 1) * MOE_UNIT, MOE_UNIT), MOE_UNIT)

        @pl.when(f == nf - 1)
        def _():
            def start(u, carry):
                _moe_out_copy(acc_scr, y_hbm, osem, row0, u).start()
                return carry
            lax.fori_loop(0, nunit, start, 0)

            @pl.when(t == used_ref[0] - 1)
            def _():
                def finish(u, carry):
                    _moe_out_copy(acc_scr, y_hbm, osem, row0, u).wait()
                    return carry
                lax.fori_loop(0, nunit, finish, 0)

    @pl.when(jnp.logical_and(t == pl.num_programs(0) - 1, f == nf - 1))
    def _():
        acc_scr[0:MOE_UNIT, :] = jnp.zeros((MOE_UNIT, D), F32)

        def fill(u, carry):
            cp = _moe_out_copy(acc_scr, y_hbm, osem, u * MOE_UNIT, 0)
            cp.start()
            cp.wait()
            return carry
        lax.fori_loop(used_ref[1], MOE_CAP // MOE_UNIT, fill, 0)


def _moe_experts(h, wg, wu, wd, tile_e, tile_row0, tile_nunit, used, pad, dest):
    nf = FE // MOE_TF

    def fidx(t, f, us):
        return jnp.where(t < us[0], f, nf - 1)

    grid_spec = pltpu.PrefetchScalarGridSpec(
        num_scalar_prefetch=6,
        grid=(MOE_NT, nf),
        in_specs=[pl.BlockSpec(memory_space=pl.ANY),
                  pl.BlockSpec((None, D, MOE_TF), lambda t, f, te, r0, nu, us, pa, de: (te[t], 0, fidx(t, f, us))),
                  pl.BlockSpec((None, D, MOE_TF), lambda t, f, te, r0, nu, us, pa, de: (te[t], 0, fidx(t, f, us))),
                  pl.BlockSpec((None, MOE_TF, D), lambda t, f, te, r0, nu, us, pa, de: (te[t], fidx(t, f, us), 0))],
        out_specs=pl.BlockSpec(memory_space=pl.ANY),
        scratch_shapes=[pltpu.SMEM((MOE_CAP,), jnp.int32),
                        pltpu.VMEM((MOE_TMAX, D), F32),
                        pltpu.VMEM((MOE_TMAX, D), F32),
                        pltpu.SemaphoreType.DMA,
                        pltpu.SemaphoreType.DMA],
    )
    return pl.pallas_call(
        _moe_kernel,
        grid_spec=grid_spec,
        out_shape=jax.ShapeDtypeStruct((MOE_CAP, D), F32),
        compiler_params=_params(("arbitrary", "arbitrary"), 58),
        name="moe_experts",
    )(tile_e, tile_row0, tile_nunit, used, pad, dest, h, wg, wu, wd)


def _cmb_copy(y_hbm, ybuf, sem, buf, slot, k, r):
    return pltpu.make_async_copy(y_hbm.at[pl.ds(slot, 1), :], ybuf.at[buf, k, pl.ds(r, 1), :],
                                 sem.at[buf])


def _combine_kernel(dest_ref, h_ref, info_ref, y_hbm, g_ref, b_ref, o_ref, ybuf, sem):
    tm = h_ref.shape[0]
    i = pl.program_id(0)
    cur = lax.rem(i, 2)

    def gather(tile, buf):
        def issue(r, c):
            tok = tile * tm + r
            _cmb_copy(y_hbm, ybuf, sem, buf, dest_ref[2 * tok], 0, r).start()
            _cmb_copy(y_hbm, ybuf, sem, buf, dest_ref[2 * tok + 1], 1, r).start()
            return c
        lax.fori_loop(0, tm, issue, 0, unroll=8)

    @pl.when(i == 0)
    def _():
        gather(0, 0)

    @pl.when(i + 1 < pl.num_programs(0))
    def _():
        gather(i + 1, 1 - cur)

    def drain(r, c):
        _cmb_copy(y_hbm, ybuf, sem, cur, 0, 0, r).wait()
        _cmb_copy(y_hbm, ybuf, sem, cur, 0, 1, r).wait()
        return c
    lax.fori_loop(0, tm, drain, 0, unroll=8)
    info = info_ref[...]
    moe = info[:, 0:1] * ybuf[cur, 0] + info[:, 1:2] * ybuf[cur, 1]
    o_ref[...] = _layer_norm(ALPHA * h_ref[...] + moe, g_ref[...], b_ref[...])


def _combine(dest, h, info, y_rows, g, b):
    tm = CMB_TM
    grid_spec = pltpu.PrefetchScalarGridSpec(
        num_scalar_prefetch=1,
        grid=(T // tm,),
        in_specs=[pl.BlockSpec((tm, D), lambda i, de: (i, 0)),
                  pl.BlockSpec((tm, LANES_V7X), lambda i, de: (i, 0)),
                  pl.BlockSpec(memory_space=pl.ANY),
                  pl.BlockSpec((1, D), lambda i, de: (0, 0)),
                  pl.BlockSpec((1, D), lambda i, de: (0, 0))],
        out_specs=pl.BlockSpec((tm, D), lambda i, de: (i, 0)),
        scratch_shapes=[pltpu.VMEM((2, 2, tm, D), F32), pltpu.SemaphoreType.DMA((2,))],
    )
    return pl.pallas_call(
        _combine_kernel,
        grid_spec=grid_spec,
        out_shape=jax.ShapeDtypeStruct((T, D), F32),
        compiler_params=_params(("arbitrary",), 32),
        name="moe_combine_ln",
    )(dest, h, info, y_rows, g, b)


def _rope_tables(pos):
    inv_freq = THETA ** (-jnp.arange(0, ROPE, 2, dtype=F32) / ROPE)
    ang = pos[:, None] * inv_freq[None, :]
    cos = jnp.cos(ang).astype(F32)
    sin = jnp.sin(ang).astype(F32)
    zero = jnp.zeros((pos.shape[0], LANES_V7X - ROPE), F32)
    return (jnp.concatenate([cos, cos, zero], axis=1),
            jnp.concatenate([-sin, sin, zero], axis=1))


def _expert_tiles(counts):
    units = (counts + MOE_UNIT - 1) // MOE_UNIT
    seg_unit = jnp.cumsum(units) - units
    ntile = (units + MOE_TUNITS - 1) // MOE_TUNITS
    tile_end = jnp.cumsum(ntile)
    n_used = tile_end[-1]
    tid = jnp.arange(MOE_NT, dtype=jnp.int32)
    e = jnp.minimum(jnp.sum(tid[:, None] >= tile_end[None, :], axis=1), NE - 1).astype(jnp.int32)
    local = tid - (tile_end - ntile)[e]
    nt = jnp.maximum(ntile[e], 1)
    lo = (local * units[e]) // nt
    hi = ((local + 1) * units[e]) // nt
    used = tid < n_used
    last = jnp.maximum(n_used - 1, 0)
    e = jnp.where(used, e, e[last])
    row0 = jnp.where(used, (seg_unit[e] + lo) * MOE_UNIT, 0)
    nunit = jnp.where(used, hi - lo, 0)
    used_info = jnp.stack([n_used, jnp.sum(units)]).astype(jnp.int32)
    seg_row = seg_unit * MOE_UNIT
    pad = jnp.stack([seg_row + counts, seg_row + units * MOE_UNIT], axis=1).reshape(2 * NE)
    return (e.astype(jnp.int32), row0.astype(jnp.int32), nunit.astype(jnp.int32),
            used_info, pad.astype(jnp.int32), seg_row.astype(jnp.int32))


def kernel(x, meta_tokens, attn_w_in, fox_forget_bias, mla_q_norm, mla_kv_norm, mla_w_uq, mla_w_ukv, attn_w_out, pool_w_in, pool_w_group, pool_scale, pool_w_out, ffn_w_gate, ffn_w_up, ffn_w_down, moe_w_router, moe_b_router, moe_w_gate, moe_w_up, moe_w_down, ln_mix_g, ln_mix_b, ln_ffn_g, ln_ffn_b):
    assert x.shape == (1, T, D) and meta_tokens.shape == (NM, D)
    xt = x[0]
    meta = meta_tokens.astype(x.dtype)

    w_in = attn_w_in[0]
    o_kr, o_qf, o_f = QL + KVL, QL + KVL + ROPE, QL + KVL + ROPE + 3 * H * HD
    kr1, kr2 = w_in[:, o_kr:o_kr + ROPE // 2], w_in[:, o_kr + ROPE // 2:o_qf]
    zpad = lambda n: jnp.zeros((D, n), F32)
    w_small = jnp.concatenate([kr2, kr1, zpad(LANES_V7X - ROPE),
                               w_in[:, o_f:o_f + H], zpad(LANES_V7X - H)], axis=1)
    w_fox = w_in[:, o_qf:o_f].astype(BF16)
    wuq = mla_w_uq[0].reshape(QL, H, NOPE + ROPE)
    uq_n, uq_1, uq_2 = wuq[..., :NOPE], wuq[..., NOPE:NOPE + ROPE // 2], wuq[..., NOPE + ROPE // 2:]
    zq = jnp.zeros((QL, H, LANES_V7X - ROPE), F32)
    w_uq = jnp.concatenate([uq_n, uq_1, uq_2, zq], axis=-1).reshape(QL, H * DQ_MLA).astype(BF16)
    w_uqs = jnp.concatenate([uq_2, uq_1, zq], axis=-1).reshape(QL, H * HD).astype(BF16)
    wukv = mla_w_ukv[0].reshape(KVL, H, NOPE + HD)
    w_ukv = jnp.concatenate([wukv[..., :NOPE].reshape(KVL, H * NOPE),
                             wukv[..., NOPE:].reshape(KVL, H * HD)], axis=1).astype(BF16)
    w_ao = attn_w_out[0]
    w_fg, w_fu, w_fd = ffn_w_gate[0].astype(BF16), ffn_w_up[0].astype(BF16), ffn_w_down[0].astype(BF16)
    w_pi, w_pg, w_po = pool_w_in[0], pool_w_group[0], pool_w_out[0]
    qn, kvn = mla_q_norm[0][None, :], mla_kv_norm[0][None, :]
    row = lambda v: v[None, :]

    ctab_t, stab_t = _rope_tables(jnp.arange(NM, NM + T, dtype=F32))
    ctab_m, stab_m = _rope_tables(jnp.arange(NM, dtype=F32))

    fox_scale = HD ** -0.5 * LOG2E
    q_t, k_t, vt_t, fl_t = _mla_proj(xt, w_in, w_small, qn, kvn, w_uq, w_uqs, w_ukv, ctab_t, stab_t, TK)
    q_m, k_m, v_m, _ = _mla_proj(meta, w_in, w_small, qn, kvn, w_uq, w_uqs, w_ukv, ctab_m, stab_m, NM)
    fox_qk, fox_vt = _fox_proj(xt, w_fox, fox_scale)
    fox_m = _matmul(meta, w_fox, NM, H * HD, BF16, scaled_blocks=1, scale=fox_scale, name="fox_proj_meta")

    fl = fl_t[:, :H].T.reshape(H, T // LANES_V7X, LANES_V7X)
    bias_b = jnp.broadcast_to(fox_forget_bias[0][:, None, None], (H, 1, LANES_V7X))
    parts = _forget_cumsum(fl, bias_b).reshape(H, 6, T)
    pos = parts[:, 0:3].transpose(2, 0, 1)
    neg = parts[:, 3:6].transpose(2, 0, 1)
    lanes = lambda pieces: jnp.concatenate(pieces, axis=2).reshape(-1, H * AUG_STRIDE)
    cst = lambda rows, n, v: jnp.full((rows, H, n), v, BF16)
    aug = (lanes([cst(T, 3, 1.0), pos, cst(T, AUG_STRIDE - 6, 0.0)]),
           lanes([neg, cst(T, AUG_STRIDE - 3, 0.0)]),
           lanes([cst(NM, 3, 0.0), cst(NM, 3, -1.0), cst(NM, AUG_STRIDE - 6, 0.0)]))

    ones_rows = jnp.concatenate([jnp.ones((H, 1, NM), BF16), jnp.zeros((H, VROWS - HD - 1, NM), BF16)], axis=1)
    to_vmt = lambda v: jnp.concatenate([v.reshape(NM, H, HD).transpose(1, 2, 0), ones_rows], axis=1)
    o_mla = _attention(q_t, k_t, vt_t, k_m, to_vmt(v_m), 0, DQ_MLA, name="mla_attn")
    o_fox = _attention(fox_qk, fox_qk, fox_vt, fox_m, to_vmt(fox_m[:, 2 * H * HD:]),
                       H, HD, aug, name="fox_attn")
    om_mla = _meta_attention(q_m, k_m, v_m, 0, 0, 0, DQ_MLA, "mla_attn_meta")
    om_fox = _meta_attention(fox_m, fox_m, fox_m, 0, H, 2 * H, HD, "fox_attn_meta")

    g0, b0, g1, b1 = row(ln_mix_g[0]), row(ln_mix_b[0]), row(ln_ffn_g[0]), row(ln_ffn_b[0])
    h_t = _attn_out(o_mla, o_fox, w_ao, xt, g0, b0, TM)
    h_m = _attn_out(om_mla, om_fox, w_ao, meta, g0, b0, NM)
    h_t = _ffn(h_t, w_fg, w_fu, w_fd, g1, b1, FFN_TM)
    h_m = _ffn(h_m, w_fg, w_fu, w_fd, g1, b1, NM)

    hp_t = _matmul(h_t, w_pi, TM, D, F32, name="pool_in")
    hp_m = _matmul(h_m, w_pi, NM, D, F32, name="pool_in_meta")
    h_t = _pool_mixer(hp_t, hp_m, h_t, w_pg, row(pool_scale[0]), w_po,
                      row(ln_mix_g[1]), row(ln_mix_b[1]))

    w_r = jnp.concatenate([moe_w_router[0], jnp.zeros((D, LANES_V7X - NE), F32)], axis=1)
    b_r = jnp.concatenate([moe_b_router[0], jnp.full((LANES_V7X - NE,), NEG, F32)])[None, :]
    info, cnt = _router(h_t, w_r, b_r)
    counts = cnt[0, :NE].astype(jnp.int32)
    tile_e, tile_row0, tile_nunit, used, pad, seg_row = _expert_tiles(counts)
    experts = info[:, 2:4].astype(jnp.int32)
    ranks = info[:, 4:6].astype(jnp.int32)
    dest = (seg_row[experts] + ranks).reshape(2 * T)
    y_rows = _moe_experts(h_t, moe_w_gate[0], moe_w_up[0], moe_w_down[0],
                          tile_e, tile_row0, tile_nunit, used, pad, dest)
    out = _combine(dest, h_t, info, y_rows, row(ln_ffn_g[1]), row(ln_ffn_b[1]))
    return out[None]
```

```python
import functools
import math

import jax
import jax.numpy as jnp
import numpy as np
from jax import lax
from jax.experimental import pallas as pl
from jax.experimental.pallas import tpu as pltpu

D = 2048
T = 8192
NM = 16
H = 8
QL = 512
KVL = 512
NOPE = 128
ROPE = 64
HD = 128
DQ_MLA = 256
LAT_COLS = QL + KVL + 128
VROWS = HD + 16
THETA = 10000.0
POOL_W = (2, 4, 8, 16)
PG = D // 4
FD = 5632
NE = 8
FE = 7168
ALPHA = 4.0 ** 0.25
LN_EPS = 1e-5
RMS_EPS = 1e-6
NEG = -1e30
LOG2E = 1.4426950408889634

F32 = jnp.float32
BF16 = jnp.bfloat16

LANES_V7X = 128
VMEM_BYTES_V7X = 64 * 1024 * 1024
MIB = 1024 * 1024

TM = 512
TK = 512
ATT_TQ = 2 * TK
FFN_TM = 1024
FFN_CHUNK = 512
FFN_TF = 512
POOL_TM = 256
MOE_UNIT = 128
MOE_TUNITS = 10
MOE_TMAX = MOE_TUNITS * MOE_UNIT
MOE_TF = 512
MOE_CAP = 2 * T + NE * MOE_UNIT
MOE_NT = MOE_CAP // MOE_TMAX + NE + 1
CMB_TM = 256


def _params(sem, vmem_mib):
    return pltpu.CompilerParams(dimension_semantics=sem, vmem_limit_bytes=vmem_mib * MIB)


def _full(shape):
    n = len(shape)
    return pl.BlockSpec(shape, lambda *_: (0,) * n)


def _resident(shape):
    n = len(shape)
    return pl.BlockSpec(shape, lambda *_: (0,) * n, pipeline_mode=pl.Buffered(1))


def _layer_norm(x, g, b):
    mu = jnp.mean(x, axis=-1, keepdims=True)
    xc = x - mu
    var = jnp.mean(xc * xc, axis=-1, keepdims=True)
    return xc * lax.rsqrt(var + LN_EPS) * g + b


def _rms_norm(x, g):
    return x * lax.rsqrt(jnp.mean(x * x, axis=-1, keepdims=True) + RMS_EPS) * g


def _dot(a, b):
    return jnp.dot(a, b, preferred_element_type=F32)


def _dot_nt(a, b):
    return lax.dot_general(a, b, (((1,), (1,)), ((), ())), preferred_element_type=F32)


def _vt_store(v_ref, h, v):
    v_ref[h, 0, 0:HD, :] = v.T.astype(BF16)
    rows = lax.broadcasted_iota(jnp.int32, (VROWS - HD, v.shape[0]), 0)
    v_ref[h, 0, HD:VROWS, :] = (rows == 0).astype(BF16)


def _mla_proj_kernel(x_ref, win_ref, wsm_ref, qn_ref, kvn_ref, wuq_ref, wuqs_ref, wukv_ref,
                     c_ref, s_ref, q_ref, k_ref, v_ref, fl_ref, *, qscale, transposed_v):
    x = x_ref[...]
    lat = _dot(x, win_ref[...])
    small = _dot(x, wsm_ref[...])
    cq = lat[:, 0:QL]
    ckv = lat[:, QL:QL + KVL]
    kr = lat[:, QL + KVL:LAT_COLS]
    krs = small[:, 0:LANES_V7X]
    fl_ref[...] = small[:, LANES_V7X:]
    cos = c_ref[...]
    sin = s_ref[...]
    qn = _rms_norm(cq, qn_ref[...]).astype(BF16)
    qpre = _dot(qn, wuq_ref[...])
    qsw = _dot(qn, wuqs_ref[...])
    for h in range(H):
        nope = qpre[:, DQ_MLA * h:DQ_MLA * h + NOPE]
        rot = (qpre[:, DQ_MLA * h + NOPE:DQ_MLA * (h + 1)] * cos
               + qsw[:, HD * h:HD * (h + 1)] * sin)
        q_ref[:, DQ_MLA * h:DQ_MLA * h + NOPE] = (nope * qscale).astype(BF16)
        q_ref[:, DQ_MLA * h + NOPE:DQ_MLA * (h + 1)] = (rot * qscale).astype(BF16)
    kvn = _rms_norm(ckv, kvn_ref[...]).astype(BF16)
    kv = _dot(kvn, wukv_ref[...])
    krot = (kr * cos + krs * sin).astype(BF16)
    for h in range(H):
        k_ref[:, DQ_MLA * h:DQ_MLA * h + NOPE] = kv[:, HD * h:HD * (h + 1)].astype(BF16)
        k_ref[:, DQ_MLA * h + NOPE:DQ_MLA * (h + 1)] = krot
    if transposed_v:
        for h in range(H):
            _vt_store(v_ref, h, kv[:, (H + h) * HD:(H + h + 1) * HD])
    else:
        v_ref[...] = kv[:, H * HD:].astype(BF16)


def _mla_proj(x, w_in, wsm, qn, kvn, wuq, wuqs, wukv, ctab, stab, tm):
    m = x.shape[0]
    qscale = (NOPE + ROPE) ** -0.5 * LOG2E
    row = lambda w: pl.BlockSpec((tm, w), lambda i: (i, 0))
    transposed_v = tm == TK
    if transposed_v:
        v_spec = pl.BlockSpec((H, 1, VROWS, TK), lambda i: (0, i, 0, 0))
        v_shape = jax.ShapeDtypeStruct((H, m // TK, VROWS, TK), BF16)
    else:
        v_spec, v_shape = row(H * HD), jax.ShapeDtypeStruct((m, H * HD), BF16)
    lat_spec = pl.BlockSpec((D, LAT_COLS), lambda i: (0, 0), pipeline_mode=pl.Buffered(1))
    return pl.pallas_call(
        functools.partial(_mla_proj_kernel, qscale=qscale, transposed_v=transposed_v),
        grid=(m // tm,),
        in_specs=[row(D), lat_spec, _full(wsm.shape), _full(qn.shape), _full(kvn.shape),
                  _resident(wuq.shape), _resident(wuqs.shape), _resident(wukv.shape),
                  row(LANES_V7X), row(LANES_V7X)],
        out_specs=[row(H * DQ_MLA), row(H * DQ_MLA), v_spec, row(LANES_V7X)],
        out_shape=[jax.ShapeDtypeStruct((m, H * DQ_MLA), BF16),
                   jax.ShapeDtypeStruct((m, H * DQ_MLA), BF16),
                   v_shape,
                   jax.ShapeDtypeStruct((m, LANES_V7X), F32)],
        compiler_params=_params(("parallel",), 56),
        name="mla_proj",
    )(x, w_in, wsm, qn, kvn, wuq, wuqs, wukv, ctab, stab)


def _mm_kernel(x_ref, w_ref, o_ref, *, scaled_blocks, scale):
    y = _dot(x_ref[...], w_ref[...])
    if scaled_blocks:
        y = jnp.where(pl.program_id(1) < scaled_blocks, y * scale, y)
    o_ref[...] = y.astype(o_ref.dtype)


def _matmul(x, w, tm, tn, out_dtype, scaled_blocks=0, scale=1.0, name="matmul"):
    m, k = x.shape
    n = w.shape[1]
    return pl.pallas_call(
        functools.partial(_mm_kernel, scaled_blocks=scaled_blocks, scale=scale),
        grid=(m // tm, n // tn),
        in_specs=[pl.BlockSpec((tm, k), lambda i, j: (i, 0)),
                  _resident((k, n)) if tn == n else pl.BlockSpec((k, tn), lambda i, j: (0, j))],
        out_specs=pl.BlockSpec((tm, tn), lambda i, j: (i, j)),
        out_shape=jax.ShapeDtypeStruct((m, n), out_dtype),
        compiler_params=_params(("parallel", "parallel"), 48),
        name=name,
    )(x, w)


def _fox_proj_kernel(x_ref, w_ref, qk_ref, vt_ref, *, scale):
    j = pl.program_id(1)
    n = H * HD
    part = lambda c: _dot(x_ref[...].astype(BF16), w_ref[:, c * n:(c + 1) * n])

    @pl.when(j == 0)
    def _():
        qk_ref[...] = (part(0) * scale).astype(BF16)

    @pl.when(j == 1)
    def _():
        qk_ref[...] = part(1).astype(BF16)

    @pl.when(j == 2)
    def _():
        y = part(2)
        for h in range(H):
            _vt_store(vt_ref, h, y[:, h * HD:(h + 1) * HD])


def _fox_proj(x, w, scale):
    n = H * HD
    return pl.pallas_call(
        functools.partial(_fox_proj_kernel, scale=scale),
        grid=(T // TK, 3),
        in_specs=[pl.BlockSpec((TK, D), lambda i, j: (i, 0)),
                  _resident((D, 3 * n))],
        out_specs=[pl.BlockSpec((TK, n), lambda i, j: (i, jnp.minimum(j, 1))),
                   pl.BlockSpec((H, 1, VROWS, TK), lambda i, j: (0, i, 0, 0))],
        out_shape=[jax.ShapeDtypeStruct((T, 2 * n), BF16),
                   jax.ShapeDtypeStruct((H, T // TK, VROWS, TK), BF16)],
        compiler_params=_params(("parallel", "arbitrary"), 48),
        name="fox_proj",
    )(x, w)


def _split3(a):
    hi = a.astype(BF16)
    r = a - hi.astype(F32)
    mid = r.astype(BF16)
    lo = (r - mid.astype(F32)).astype(BF16)
    return hi, mid, lo


def _forget_cumsum_kernel(fl_ref, b_ref, o_ref):
    z = fl_ref[...] + b_ref[...]
    logf = jnp.minimum(z, 0.0) - jnp.log(1.0 + jnp.exp(-jnp.abs(z)))
    n = LANES_V7X
    ri = lax.broadcasted_iota(jnp.int32, (n, n), 0)
    ci = lax.broadcasted_iota(jnp.int32, (n, n), 1)
    upper = (ri <= ci).astype(BF16)
    ones = jnp.ones((n, n), BF16)
    within = jnp.zeros(logf.shape, F32)
    total = jnp.zeros(logf.shape, F32)
    for part in _split3(logf):
        within += _dot(part, upper)
        total += _dot(part, ones)
    r = logf.shape[0]
    lower = (lax.broadcasted_iota(jnp.int32, (r, r), 1)
             < lax.broadcasted_iota(jnp.int32, (r, r), 0)).astype(BF16)
    before = jnp.zeros(logf.shape, F32)
    for part in _split3(total):
        before += _dot(lower, part)
    c2 = (within + before) * LOG2E
    for p, part in enumerate(_split3(c2)):
        o_ref[p] = part
        o_ref[3 + p] = -part


def _forget_cumsum(fl_t, bias_b):
    r = T // LANES_V7X
    return pl.pallas_call(
        _forget_cumsum_kernel,
        grid=(H,),
        in_specs=[pl.BlockSpec((None, r, LANES_V7X), lambda h: (h, 0, 0)),
                  pl.BlockSpec((None, 1, LANES_V7X), lambda h: (h, 0, 0))],
        out_specs=pl.BlockSpec((None, 6, r, LANES_V7X), lambda h: (h, 0, 0, 0)),
        out_shape=jax.ShapeDtypeStruct((H, 6, r, LANES_V7X), BF16),
        compiler_params=_params(("parallel",), 32),
        name="forget_cumsum",
    )(fl_t, bias_b)


ATT_NH = 2
AUG_STRIDE = 16


def _attn_kernel(*refs, fox, dq):
    if fox:
        (q_ref, k_ref, vt_ref, km_ref, vmt_ref, qa_ref, ka_ref, kam_ref, o_ref,
         m_scr, acc_scr, s0_scr, s1_scr, mb0_scr, mb1_scr, qf_scr) = refs
    else:
        (q_ref, k_ref, vt_ref, km_ref, vmt_ref, o_ref,
         m_scr, acc_scr, s0_scr, s1_scr, mb0_scr, mb1_scr) = refs
    hp = pl.program_id(0)
    i = pl.program_id(1)
    m_scr[...] = jnp.full(m_scr.shape, -jnp.inf, F32)
    acc_scr[...] = jnp.zeros(acc_scr.shape, F32)
    if fox:
        lane = lax.broadcasted_iota(jnp.int32, qa_ref.shape, 1)
        for g in range(ATT_NH):
            lo = (hp * ATT_NH + g) * AUG_STRIDE
            own = jnp.logical_and(lane >= lo, lane < lo + AUG_STRIDE)
            qf_scr[g, :, 0:dq] = q_ref[:, g * dq:(g + 1) * dq]
            qf_scr[g, :, dq:2 * dq] = jnp.where(own, qa_ref[...], jnp.zeros_like(qa_ref))

    def query(g):
        return qf_scr[g] if fox else q_ref[:, g * dq:(g + 1) * dq]

    def update(g, st, mb, vt):
        m_prev = m_scr[g]
        m_new = jnp.maximum(m_prev, mb)
        alpha = jnp.exp2(m_prev - m_new)
        p = jnp.exp2(st - m_new)
        acc_scr[g] = alpha * acc_scr[g] + _dot(vt, p.astype(BF16))
        m_scr[g] = m_new

    def stage1(slot, j):
        s_ref, mb_ref = slots[slot]
        off = pl.multiple_of(j * TK, TK)
        for g in range(ATT_NH):
            kb = k_ref[pl.ds(off, TK), g * dq:(g + 1) * dq]
            if fox:
                kb = jnp.concatenate([kb, ka_ref[pl.ds(off, TK), :]], axis=1)
            st = _dot_nt(kb, query(g))
            s_ref[g] = st
            mb_ref[g] = jnp.max(st, axis=0, keepdims=True)

    def stage2(slot, j, diag_key0=None):
        s_ref, mb_ref = slots[slot]
        for g in range(ATT_NH):
            st = s_ref[g]
            mb = mb_ref[g]
            if diag_key0 is not None:
                keys = lax.broadcasted_iota(jnp.int32, st.shape, 0) + diag_key0
                qrys = lax.broadcasted_iota(jnp.int32, st.shape, 1)
                st = jnp.where(keys <= qrys, st, NEG)
                mb = jnp.max(st, axis=0, keepdims=True)
            update(g, st, mb, vt_ref[g, j])

    slots = ((s0_scr, mb0_scr), (s1_scr, mb1_scr))
    stage1(0, 0)

    def body(jj, carry):
        j = 2 * jj
        stage1(1, j + 1)
        stage2(0, j)
        stage1(0, j + 2)
        stage2(1, j + 1)
        return carry

    lax.fori_loop(0, i, body, 0)
    stage1(1, 2 * i + 1)
    stage2(0, 2 * i, diag_key0=0)
    stage2(1, 2 * i + 1, diag_key0=TK)

    for g in range(ATT_NH):
        kb = km_ref[:, g * dq:(g + 1) * dq]
        if fox:
            kb = jnp.concatenate([kb, kam_ref[...]], axis=1)
        st = _dot_nt(kb, query(g))
        update(g, st, jnp.max(st, axis=0, keepdims=True), vmt_ref[g])
        acc = acc_scr[g]
        o = (acc[0:HD] / acc[HD:HD + 1]).T
        o_ref[:, g * HD:(g + 1) * HD] = o.astype(o_ref.dtype)


def _attention(q, k, vt, km, vmt, k_col0, dq, aug=None, name="attn"):
    fox = aug is not None
    nh, tq = ATT_NH, ATT_TQ
    in_specs = [pl.BlockSpec((tq, nh * dq), lambda h, i: (i, h)),
                pl.BlockSpec((T, nh * dq), lambda h, i: (0, k_col0 // nh + h)),
                pl.BlockSpec((nh, T // TK, VROWS, TK), lambda h, i: (h, 0, 0, 0)),
                pl.BlockSpec((NM, nh * dq), lambda h, i: (0, k_col0 // nh + h)),
                pl.BlockSpec((nh, VROWS, NM), lambda h, i: (h, 0, 0))]
    args = [q, k, vt, km, vmt]
    scratch = [pltpu.VMEM((nh, 1, tq), F32), pltpu.VMEM((nh, VROWS, tq), F32),
               pltpu.VMEM((nh, TK, tq), F32), pltpu.VMEM((nh, TK, tq), F32),
               pltpu.VMEM((nh, 1, tq), F32), pltpu.VMEM((nh, 1, tq), F32)]
    if fox:
        in_specs += [pl.BlockSpec((tq, LANES_V7X), lambda h, i: (i, 0)),
                     _full((T, LANES_V7X)), _full((NM, LANES_V7X))]
        args += list(aug)
        scratch.append(pltpu.VMEM((nh, tq, 2 * dq), BF16))
    return pl.pallas_call(
        functools.partial(_attn_kernel, fox=fox, dq=dq),
        grid=(H // nh, T // tq),
        in_specs=in_specs,
        out_specs=pl.BlockSpec((tq, nh * HD), lambda h, i: (i, h)),
        out_shape=jax.ShapeDtypeStruct((T, H * HD), BF16),
        scratch_shapes=scratch,
        compiler_params=_params(("parallel", "arbitrary"), 56),
        name=name,
    )(*args)


def _meta_attn_kernel(q_ref, k_ref, v_ref, o_ref):
    s = _dot_nt(q_ref[...], k_ref[...])
    rows = lax.broadcasted_iota(jnp.int32, s.shape, 0)
    cols = lax.broadcasted_iota(jnp.int32, s.shape, 1)
    s = jnp.where(cols <= rows, s, NEG)
    p = jnp.exp2(s - jnp.max(s, axis=1, keepdims=True))
    o = _dot(p.astype(BF16), v_ref[...]) / jnp.sum(p, axis=1, keepdims=True)
    o_ref[...] = o.astype(o_ref.dtype)


def _meta_attention(q, k, v, q_col0, k_col0, v_col0, dq, name):
    return pl.pallas_call(
        _meta_attn_kernel,
        grid=(H,),
        in_specs=[pl.BlockSpec((NM, dq), lambda h: (0, q_col0 + h)),
                  pl.BlockSpec((NM, dq), lambda h: (0, k_col0 + h)),
                  pl.BlockSpec((NM, HD), lambda h: (0, v_col0 + h))],
        out_specs=pl.BlockSpec((NM, HD), lambda h: (0, h)),
        out_shape=jax.ShapeDtypeStruct((NM, H * HD), BF16),
        compiler_params=_params(("parallel",), 32),
        name=name,
    )(q, k, v)


def _attn_out_kernel(om_ref, of_ref, w_ref, h_ref, g_ref, b_ref, o_ref):
    m = _dot(om_ref[...], w_ref[0:H * HD, :]) + _dot(of_ref[...], w_ref[H * HD:, :])
    o_ref[...] = _layer_norm(ALPHA * h_ref[...] + m, g_ref[...], b_ref[...])


def _attn_out(om, of, w, h, g, b, tm):
    m = h.shape[0]
    row = lambda w_: pl.BlockSpec((tm, w_), lambda i: (i, 0))
    return pl.pallas_call(
        _attn_out_kernel,
        grid=(m // tm,),
        in_specs=[row(H * HD), row(H * HD), _resident(w.shape), row(D), _full(g.shape), _full(b.shape)],
        out_specs=row(D),
        out_shape=jax.ShapeDtypeStruct((m, D), F32),
        compiler_params=_params(("parallel",), 48),
        name="attn_out_ln",
    )(om, of, w, h, g, b)


def _silu(x):
    return x / (1.0 + jnp.exp(-x))


def _ffn_kernel(h_ref, wg_ref, wu_ref, wd_ref, g_ref, b_ref, o_ref):
    f = pl.program_id(1)
    rows = h_ref.shape[0]
    chunk = min(rows, FFN_CHUNK)

    @pl.when(f == 0)
    def _():
        o_ref[...] = jnp.zeros(o_ref.shape, F32)

    for c in range(rows // chunk):
        sl = slice(c * chunk, (c + 1) * chunk)
        x = h_ref[sl, :].astype(wg_ref.dtype)
        hid = _silu(_dot(x, wg_ref[...])) * _dot(x, wu_ref[...])
        o_ref[sl, :] += _dot(hid.astype(wd_ref.dtype), wd_ref[...])

    @pl.when(f == pl.num_programs(1) - 1)
    def _():
        o_ref[...] = _layer_norm(ALPHA * h_ref[...] + o_ref[...], g_ref[...], b_ref[...])


def _ffn(h, wg, wu, wd, g, b, tm):
    m = h.shape[0]
    once = pl.Buffered(1)
    return pl.pallas_call(
        _ffn_kernel,
        grid=(m // tm, FD // FFN_TF),
        in_specs=[pl.BlockSpec((tm, D), lambda i, f: (i, 0), pipeline_mode=once),
                  pl.BlockSpec((D, FFN_TF), lambda i, f: (0, f)),
                  pl.BlockSpec((D, FFN_TF), lambda i, f: (0, f)),
                  pl.BlockSpec((FFN_TF, D), lambda i, f: (f, 0)),
                  pl.BlockSpec((1, D), lambda i, f: (0, 0)),
                  pl.BlockSpec((1, D), lambda i, f: (0, 0))],
        out_specs=pl.BlockSpec((tm, D), lambda i, f: (i, 0)),
        out_shape=jax.ShapeDtypeStruct((m, D), F32),
        compiler_params=_params(("parallel", "arbitrary"), 58),
        name="ffn_ln",
    )(h, wg, wu, wd, g, b)


def _pool_kernel(hp_ref, prev_ref, meta_ref, h_ref, wg_ref, sc_ref, wo_ref, g_ref, b_ref, o_ref,
                 ext_scr, z_scr):
    tm = hp_ref.shape[0]
    halo = jnp.where(pl.program_id(0) == 0, meta_ref[...], prev_ref[...])
    ext_scr[0:NM, :] = halo
    ext_scr[NM:NM + tm, :] = hp_ref[...]
    for gi, w in enumerate(POOL_W):
        cols = slice(PG * gi, PG * (gi + 1))
        cur = ext_scr[NM:NM + tm, cols]
        tot = cur
        for j in range(1, w):
            tot = tot + ext_scr[NM - j:NM - j + tm, cols]
        diff = tot * (1.0 / w) - cur
        y = _dot(diff, wg_ref[gi]) * sc_ref[:, cols]
        z_scr[:, cols] = y
    m = _dot(z_scr[...], wo_ref[...])
    o_ref[...] = _layer_norm(ALPHA * h_ref[...] + m, g_ref[...], b_ref[...])


def _pool_mixer(hp, hp_meta, h, wgroup, scale, wout, g, b):
    tm = POOL_TM
    row = pl.BlockSpec((tm, D), lambda i: (i, 0))
    prev = pl.BlockSpec((NM, D), lambda i: (jnp.maximum(i * (tm // NM) - 1, 0), 0))
    return pl.pallas_call(
        _pool_kernel,
        grid=(T // tm,),
        in_specs=[row, prev, _full(hp_meta.shape), row, _resident(wgroup.shape), _full(scale.shape),
                  _resident(wout.shape), _full(g.shape), _full(b.shape)],
        out_specs=row,
        out_shape=jax.ShapeDtypeStruct((T, D), F32),
        scratch_shapes=[pltpu.VMEM((NM + tm, D), F32), pltpu.VMEM((tm, D), F32)],
        compiler_params=_params(("parallel",), 48),
        name="pool_ln",
    )(hp, hp, hp_meta, h, wgroup, scale, wout, g, b)


def _router_kernel(h_ref, w_ref, b_ref, info_ref, cnt_ref, carry_scr):
    tm = h_ref.shape[0]

    @pl.when(pl.program_id(0) == 0)
    def _():
        carry_scr[...] = jnp.zeros(carry_scr.shape, F32)

    x = h_ref[...]
    w = w_ref[...]
    xh, wh = x.astype(BF16), w.astype(BF16)
    xl = (x - xh.astype(F32)).astype(BF16)
    wl = (w - wh.astype(F32)).astype(BF16)
    logits = _dot(xh, wh) + _dot(xh, wl) + _dot(xl, wh) + b_ref[...]
    lane = lax.broadcasted_iota(jnp.int32, logits.shape, 1).astype(F32)
    nolane = float(LANES_V7X)
    m1 = jnp.max(logits, axis=1, keepdims=True)
    i1 = jnp.min(jnp.where(logits == m1, lane, nolane), axis=1, keepdims=True)
    rest = jnp.where(lane == i1, -jnp.inf, logits)
    m2 = jnp.max(rest, axis=1, keepdims=True)
    i2 = jnp.min(jnp.where(rest == m2, lane, nolane), axis=1, keepdims=True)
    e = jnp.exp(m2 - m1)
    g1 = 1.0 / (1.0 + e)
    g2 = e / (1.0 + e)
    hit1 = lane == i1
    hit2 = lane == i2
    member = jnp.logical_or(hit1, hit2).astype(BF16)
    earlier = (lax.broadcasted_iota(jnp.int32, (tm, tm), 1)
               < lax.broadcasted_iota(jnp.int32, (tm, tm), 0)).astype(BF16)
    before = _dot(earlier, member) + carry_scr[0:1, :]
    r1 = jnp.sum(jnp.where(hit1, before, 0.0), axis=1, keepdims=True)
    r2 = jnp.sum(jnp.where(hit2, before, 0.0), axis=1, keepdims=True)
    carry = carry_scr[0:1, :] + jnp.sum(member.astype(F32), axis=0, keepdims=True)
    carry_scr[...] = jnp.broadcast_to(carry, carry_scr.shape)
    cnt_ref[...] = jnp.broadcast_to(carry, cnt_ref.shape)
    info = jnp.where(lane == 0, g1, 0.0)
    info = jnp.where(lane == 1, g2, info)
    info = jnp.where(lane == 2, i1, info)
    info = jnp.where(lane == 3, i2, info)
    info = jnp.where(lane == 4, r1, info)
    info = jnp.where(lane == 5, r2, info)
    info_ref[...] = info


def _router(h, w_pad, b_pad):
    tm = TM
    return pl.pallas_call(
        _router_kernel,
        grid=(T // tm,),
        in_specs=[pl.BlockSpec((tm, D), lambda i: (i, 0)), _full(w_pad.shape), _full(b_pad.shape)],
        out_specs=[pl.BlockSpec((tm, LANES_V7X), lambda i: (i, 0)), _full((8, LANES_V7X))],
        out_shape=[jax.ShapeDtypeStruct((T, LANES_V7X), F32),
                   jax.ShapeDtypeStruct((8, LANES_V7X), F32)],
        scratch_shapes=[pltpu.VMEM((8, LANES_V7X), F32)],
        compiler_params=_params(("arbitrary",), 32),
        name="router",
    )(h, w_pad, b_pad)


def _moe_row_copy(h_hbm, x_scr, sem, tok, r):
    return pltpu.make_async_copy(h_hbm.at[pl.ds(tok, 1), :], x_scr.at[pl.ds(r, 1), :], sem)


def _moe_out_copy(acc_scr, y_hbm, sem, row0, u):
    off = pl.multiple_of(u * MOE_UNIT, MOE_UNIT)
    dst = pl.multiple_of(row0 + off, MOE_UNIT)
    return pltpu.make_async_copy(acc_scr.at[pl.ds(off, MOE_UNIT), :],
                                 y_hbm.at[pl.ds(dst, MOE_UNIT), :], sem)


def _moe_kernel(te_ref, row0_ref, nunit_ref, used_ref, pad_ref, dest_ref,
                h_hbm, wg_ref, wu_ref, wd_ref, y_hbm,
                slot_tok, x_scr, acc_scr, gsem, osem):
    t = pl.program_id(0)
    f = pl.program_id(1)
    nf = pl.num_programs(1)

    @pl.when(jnp.logical_and(t == 0, f == 0))
    def _():
        for e in range(NE):
            def clear(s, c):
                slot_tok[s] = 0
                return c
            lax.fori_loop(pad_ref[2 * e], pad_ref[2 * e + 1], clear, 0)

        def place(tok, c):
            slot_tok[dest_ref[2 * tok]] = tok
            slot_tok[dest_ref[2 * tok + 1]] = tok
            return c
        lax.fori_loop(0, T, place, 0, unroll=8)

    @pl.when(t < used_ref[0])
    def _():
        row0 = row0_ref[t]
        nunit = nunit_ref[t]
        nrows = nunit * MOE_UNIT

        @pl.when(f == 0)
        def _():
            def issue_group(r8, c):
                base = pl.multiple_of(r8 * 8, 8)
                for k in range(8):
                    _moe_row_copy(h_hbm, x_scr, gsem, slot_tok[row0 + base + k], base + k).start()
                return c
            lax.fori_loop(0, nunit * (MOE_UNIT // 8), issue_group, 0)

            @pl.when(t > 0)
            def _():
                def finish(u, c):
                    _moe_out_copy(acc_scr, y_hbm, osem, row0_ref[t - 1], u).wait()
                    return c
                lax.fori_loop(0, nunit_ref[t - 1], finish, 0)

            def clear_acc(u, c):
                off = pl.multiple_of(u * MOE_UNIT, MOE_UNIT)
                acc_scr[pl.ds(off, MOE_UNIT), :] = jnp.zeros((MOE_UNIT, D), F32)
                return c
            lax.fori_loop(0, nunit, clear_acc, 0)

            def drain_group(r8, c):
                base = pl.multiple_of(r8 * 8, 8)
                for k in range(8):
                    _moe_row_copy(h_hbm, x_scr, gsem, 0, base + k).wait()
                return c
            lax.fori_loop(0, nunit * (MOE_UNIT // 8), drain_group, 0)

        def chunk(off, rows):
            x = x_scr[pl.ds(off, rows), :]
            hid = _silu(_dot(x, wg_ref[...])) * _dot(x, wu_ref[...])
            acc_scr[pl.ds(off, rows), :] += _dot(hid, wd_ref[...])

        def quad(c, carry):
            chunk(pl.multiple_of(c * 4 * MOE_UNIT, 4 * MOE_UNIT), 4 * MOE_UNIT)
            return carry
        lax.fori_loop(0, nunit // 4, quad, 0)

        @pl.when(nunit % 4 >= 2)
        def _():
            chunk(pl.multiple_of((nunit // 4) * 4 * MOE_UNIT, 2 * MOE_UNIT), 2 * MOE_UNIT)

        @pl.when(nunit % 2 == 1)
        def _():
            chunk(pl.multiple_of((nunit - 1) * MOE_UNIT, MOE_UNIT), MOE_UNIT)

        @pl.when(f == nf - 1)
        def _():
            def start(u, carry):
                _moe_out_copy(acc_scr, y_hbm, osem, row0, u).start()
                return carry
            lax.fori_loop(0, nunit, start, 0)

            @pl.when(t == used_ref[0] - 1)
            def _():
                def finish(u, carry):
                    _moe_out_copy(acc_scr, y_hbm, osem, row0, u).wait()
                    return carry
                lax.fori_loop(0, nunit, finish, 0)

    @pl.when(jnp.logical_and(t == pl.num_programs(0) - 1, f == nf - 1))
    def _():
        acc_scr[0:MOE_UNIT, :] = jnp.zeros((MOE_UNIT, D), F32)

        def fill(u, carry):
            cp = _moe_out_copy(acc_scr, y_hbm, osem, u * MOE_UNIT, 0)
            cp.start()
            cp.wait()
            return carry
        lax.fori_loop(used_ref[1], MOE_CAP // MOE_UNIT, fill, 0)


def _moe_experts(h, wg, wu, wd, tile_e, tile_row0, tile_nunit, used, pad, dest):
    nf = FE // MOE_TF

    def fidx(t, f, us):
        return jnp.where(t < us[0], f, nf - 1)

    grid_spec = pltpu.PrefetchScalarGridSpec(
        num_scalar_prefetch=6,
        grid=(MOE_NT, nf),
        in_specs=[pl.BlockSpec(memory_space=pl.ANY),
                  pl.BlockSpec((None, D, MOE_TF), lambda t, f, te, r0, nu, us, pa, de: (te[t], 0, fidx(t, f, us))),
                  pl.BlockSpec((None, D, MOE_TF), lambda t, f, te, r0, nu, us, pa, de: (te[t], 0, fidx(t, f, us))),
                  pl.BlockSpec((None, MOE_TF, D), lambda t, f, te, r0, nu, us, pa, de: (te[t], fidx(t, f, us), 0))],
        out_specs=pl.BlockSpec(memory_space=pl.ANY),
        scratch_shapes=[pltpu.SMEM((MOE_CAP,), jnp.int32),
                        pltpu.VMEM((MOE_TMAX, D), F32),
                        pltpu.VMEM((MOE_TMAX, D), F32),
                        pltpu.SemaphoreType.DMA,
                        pltpu.SemaphoreType.DMA],
    )
    return pl.pallas_call(
        _moe_kernel,
        grid_spec=grid_spec,
        out_shape=jax.ShapeDtypeStruct((MOE_CAP, D), F32),
        compiler_params=_params(("arbitrary", "arbitrary"), 58),
        name="moe_experts",
    )(tile_e, tile_row0, tile_nunit, used, pad, dest, h, wg, wu, wd)


def _cmb_copy(y_hbm, ybuf, sem, buf, slot, k, r):
    return pltpu.make_async_copy(y_hbm.at[pl.ds(slot, 1), :], ybuf.at[buf, k, pl.ds(r, 1), :],
                                 sem.at[buf])


def _combine_kernel(dest_ref, h_ref, info_ref, y_hbm, g_ref, b_ref, o_ref, ybuf, sem):
    tm = h_ref.shape[0]
    i = pl.program_id(0)
    cur = lax.rem(i, 2)

    def gather(tile, buf):
        def issue_group(r8, c):
            base = pl.multiple_of(r8 * 8, 8)
            for k in range(8):
                tok = tile * tm + base + k
                _cmb_copy(y_hbm, ybuf, sem, buf, dest_ref[2 * tok], 0, base + k).start()
                _cmb_copy(y_hbm, ybuf, sem, buf, dest_ref[2 * tok + 1], 1, base + k).start()
            return c
        lax.fori_loop(0, tm // 8, issue_group, 0)

    @pl.when(i == 0)
    def _():
        gather(0, 0)

    @pl.when(i + 1 < pl.num_programs(0))
    def _():
        gather(i + 1, 1 - cur)

    def drain_group(r8, c):
        base = pl.multiple_of(r8 * 8, 8)
        for k in range(8):
            _cmb_copy(y_hbm, ybuf, sem, cur, 0, 0, base + k).wait()
            _cmb_copy(y_hbm, ybuf, sem, cur, 0, 1, base + k).wait()
        return c
    lax.fori_loop(0, tm // 8, drain_group, 0)
    info = info_ref[...]
    moe = info[:, 0:1] * ybuf[cur, 0] + info[:, 1:2] * ybuf[cur, 1]
    o_ref[...] = _layer_norm(ALPHA * h_ref[...] + moe, g_ref[...], b_ref[...])


def _combine(dest, h, info, y_rows, g, b):
    tm = CMB_TM
    grid_spec = pltpu.PrefetchScalarGridSpec(
        num_scalar_prefetch=1,
        grid=(T // tm,),
        in_specs=[pl.BlockSpec((tm, D), lambda i, de: (i, 0)),
                  pl.BlockSpec((tm, LANES_V7X), lambda i, de: (i, 0)),
                  pl.BlockSpec(memory_space=pl.ANY),
                  pl.BlockSpec((1, D), lambda i, de: (0, 0)),
                  pl.BlockSpec((1, D), lambda i, de: (0, 0))],
        out_specs=pl.BlockSpec((tm, D), lambda i, de: (i, 0)),
        scratch_shapes=[pltpu.VMEM((2, 2, tm, D), F32), pltpu.SemaphoreType.DMA((2,))],
    )
    return pl.pallas_call(
        _combine_kernel,
        grid_spec=grid_spec,
        out_shape=jax.ShapeDtypeStruct((T, D), F32),
        compiler_params=_params(("arbitrary",), 32),
        name="moe_combine_ln",
    )(dest, h, info, y_rows, g, b)


def _rope_tables(pos):
    inv_freq = THETA ** (-jnp.arange(0, ROPE, 2, dtype=F32) / ROPE)
    ang = pos[:, None] * inv_freq[None, :]
    cos = jnp.cos(ang).astype(F32)
    sin = jnp.sin(ang).astype(F32)
    zero = jnp.zeros((pos.shape[0], LANES_V7X - ROPE), F32)
    return (jnp.concatenate([cos, cos, zero], axis=1),
            jnp.concatenate([-sin, sin, zero], axis=1))


def _expert_tiles(counts):
    units = (counts + MOE_UNIT - 1) // MOE_UNIT
    seg_unit = jnp.cumsum(units) - units
    ntile = (units + MOE_TUNITS - 1) // MOE_TUNITS
    tile_end = jnp.cumsum(ntile)
    n_used = tile_end[-1]
    tid = jnp.arange(MOE_NT, dtype=jnp.int32)
    e = jnp.minimum(jnp.sum(tid[:, None] >= tile_end[None, :], axis=1), NE - 1).astype(jnp.int32)
    local = tid - (tile_end - ntile)[e]
    nt = jnp.maximum(ntile[e], 1)
    lo = (local * units[e]) // nt
    hi = ((local + 1) * units[e]) // nt
    used = tid < n_used
    last = jnp.maximum(n_used - 1, 0)
    e = jnp.where(used, e, e[last])
    row0 = jnp.where(used, (seg_unit[e] + lo) * MOE_UNIT, 0)
    nunit = jnp.where(used, hi - lo, 0)
    used_info = jnp.stack([n_used, jnp.sum(units)]).astype(jnp.int32)
    seg_row = seg_unit * MOE_UNIT
    pad = jnp.stack([seg_row + counts, seg_row + units * MOE_UNIT], axis=1).reshape(2 * NE)
    return (e.astype(jnp.int32), row0.astype(jnp.int32), nunit.astype(jnp.int32),
            used_info, pad.astype(jnp.int32), seg_row.astype(jnp.int32))


def kernel(x, meta_tokens, attn_w_in, fox_forget_bias, mla_q_norm, mla_kv_norm, mla_w_uq, mla_w_ukv, attn_w_out, pool_w_in, pool_w_group, pool_scale, pool_w_out, ffn_w_gate, ffn_w_up, ffn_w_down, moe_w_router, moe_b_router, moe_w_gate, moe_w_up, moe_w_down, ln_mix_g, ln_mix_b, ln_ffn_g, ln_ffn_b):
    assert x.shape == (1, T, D) and meta_tokens.shape == (NM, D)
    xt = x[0]
    meta = meta_tokens.astype(x.dtype)

    w_in = attn_w_in[0]
    o_kr, o_qf, o_f = QL + KVL, QL + KVL + ROPE, QL + KVL + ROPE + 3 * H * HD
    kr1, kr2 = w_in[:, o_kr:o_kr + ROPE // 2], w_in[:, o_kr + ROPE // 2:o_qf]
    zpad = lambda n: jnp.zeros((D, n), F32)
    w_small = jnp.concatenate([kr2, kr1, zpad(LANES_V7X - ROPE),
                               w_in[:, o_f:o_f + H], zpad(LANES_V7X - H)], axis=1)
    w_fox = w_in[:, o_qf:o_f].astype(BF16)
    wuq = mla_w_uq[0].reshape(QL, H, NOPE + ROPE)
    uq_n, uq_1, uq_2 = wuq[..., :NOPE], wuq[..., NOPE:NOPE + ROPE // 2], wuq[..., NOPE + ROPE // 2:]
    zq = jnp.zeros((QL, H, LANES_V7X - ROPE), F32)
    w_uq = jnp.concatenate([uq_n, uq_1, uq_2, zq], axis=-1).reshape(QL, H * DQ_MLA).astype(BF16)
    w_uqs = jnp.concatenate([uq_2, uq_1, zq], axis=-1).reshape(QL, H * HD).astype(BF16)
    wukv = mla_w_ukv[0].reshape(KVL, H, NOPE + HD)
    w_ukv = jnp.concatenate([wukv[..., :NOPE].reshape(KVL, H * NOPE),
                             wukv[..., NOPE:].reshape(KVL, H * HD)], axis=1).astype(BF16)
    w_ao = attn_w_out[0]
    w_fg, w_fu, w_fd = ffn_w_gate[0], ffn_w_up[0], ffn_w_down[0]
    w_pi, w_pg, w_po = pool_w_in[0], pool_w_group[0], pool_w_out[0]
    qn, kvn = mla_q_norm[0][None, :], mla_kv_norm[0][None, :]
    row = lambda v: v[None, :]

    ctab_t, stab_t = _rope_tables(jnp.arange(NM, NM + T, dtype=F32))
    ctab_m, stab_m = _rope_tables(jnp.arange(NM, dtype=F32))

    fox_scale = HD ** -0.5 * LOG2E
    q_t, k_t, vt_t, fl_t = _mla_proj(xt, w_in, w_small, qn, kvn, w_uq, w_uqs, w_ukv, ctab_t, stab_t, TK)
    q_m, k_m, v_m, _ = _mla_proj(meta, w_in, w_small, qn, kvn, w_uq, w_uqs, w_ukv, ctab_m, stab_m, NM)
    fox_qk, fox_vt = _fox_proj(xt, w_fox, fox_scale)
    fox_m = _matmul(meta, w_fox, NM, H * HD, BF16, scaled_blocks=1, scale=fox_scale, name="fox_proj_meta")

    fl = fl_t[:, :H].T.reshape(H, T // LANES_V7X, LANES_V7X)
    bias_b = jnp.broadcast_to(fox_forget_bias[0][:, None, None], (H, 1, LANES_V7X))
    parts = _forget_cumsum(fl, bias_b).reshape(H, 6, T)
    pos = parts[:, 0:3].transpose(2, 0, 1)
    neg = parts[:, 3:6].transpose(2, 0, 1)
    lanes = lambda pieces: jnp.concatenate(pieces, axis=2).reshape(-1, H * AUG_STRIDE)
    cst = lambda rows, n, v: jnp.full((rows, H, n), v, BF16)
    aug = (lanes([cst(T, 3, 1.0), pos, cst(T, AUG_STRIDE - 6, 0.0)]),
           lanes([neg, cst(T, AUG_STRIDE - 3, 0.0)]),
           lanes([cst(NM, 3, 0.0), cst(NM, 3, -1.0), cst(NM, AUG_STRIDE - 6, 0.0)]))

    ones_rows = jnp.concatenate([jnp.ones((H, 1, NM), BF16), jnp.zeros((H, VROWS - HD - 1, NM), BF16)], axis=1)
    to_vmt = lambda v: jnp.concatenate([v.reshape(NM, H, HD).transpose(1, 2, 0), ones_rows], axis=1)
    o_mla = _attention(q_t, k_t, vt_t, k_m, to_vmt(v_m), 0, DQ_MLA, name="mla_attn")
    o_fox = _attention(fox_qk, fox_qk, fox_vt, fox_m, to_vmt(fox_m[:, 2 * H * HD:]),
                       H, HD, aug, name="fox_attn")
    om_mla = _meta_attention(q_m, k_m, v_m, 0, 0, 0, DQ_MLA, "mla_attn_meta")
    om_fox = _meta_attention(fox_m, fox_m, fox_m, 0, H, 2 * H, HD, "fox_attn_meta")

    g0, b0, g1, b1 = row(ln_mix_g[0]), row(ln_mix_b[0]), row(ln_ffn_g[0]), row(ln_ffn_b[0])
    h_t = _attn_out(o_mla, o_fox, w_ao, xt, g0, b0, TM)
    h_m = _attn_out(om_mla, om_fox, w_ao, meta, g0, b0, NM)
    h_t = _ffn(h_t, w_fg, w_fu, w_fd, g1, b1, FFN_TM)
    h_m = _ffn(h_m, w_fg, w_fu, w_fd, g1, b1, NM)

    hp_t = _matmul(h_t, w_pi, TM, D, F32, name="pool_in")
    hp_m = _matmul(h_m, w_pi, NM, D, F32, name="pool_in_meta")
    h_t = _pool_mixer(hp_t, hp_m, h_t, w_pg, row(pool_scale[0]), w_po,
                      row(ln_mix_g[1]), row(ln_mix_b[1]))

    w_r = jnp.concatenate([moe_w_router[0], jnp.zeros((D, LANES_V7X - NE), F32)], axis=1)
    b_r = jnp.concatenate([moe_b_router[0], jnp.full((LANES_V7X - NE,), NEG, F32)])[None, :]
    info, cnt = _router(h_t, w_r, b_r)
    counts = cnt[0, :NE].astype(jnp.int32)
    tile_e, tile_row0, tile_nunit, used, pad, seg_row = _expert_tiles(counts)
    experts = info[:, 2:4].astype(jnp.int32)
    ranks = info[:, 4:6].astype(jnp.int32)
    dest = (seg_row[experts] + ranks).reshape(2 * T)
    y_rows = _moe_experts(h_t, moe_w_gate[0], moe_w_up[0], moe_w_down[0],
                          tile_e, tile_row0, tile_nunit, used, pad, dest)
    out = _combine(dest, h_t, info, y_rows, row(ln_ffn_g[1]), row(ln_ffn_b[1]))
    return out[None]
```

```python
import functools
import math

import jax
import jax.numpy as jnp
import numpy as np
from jax import lax
from jax.experimental import pallas as pl
from jax.experimental.pallas import tpu as pltpu

D = 2048
T = 8192
NM = 16
H = 8
QL = 512
KVL = 512
NOPE = 128
ROPE = 64
HD = 128
DQ_MLA = 256
LAT_COLS = QL + KVL + 128
VROWS = HD + 16
THETA = 10000.0
POOL_W = (2, 4, 8, 16)
PG = D // 4
FD = 5632
NE = 8
FE = 7168
ALPHA = 4.0 ** 0.25
LN_EPS = 1e-5
RMS_EPS = 1e-6
NEG = -1e30
LOG2E = 1.4426950408889634

F32 = jnp.float32
BF16 = jnp.bfloat16

LANES_V7X = 128
VMEM_BYTES_V7X = 64 * 1024 * 1024
MIB = 1024 * 1024

TM = 512
TK = 512
ATT_TQ = 2 * TK
FFN_TM = 1024
FFN_CHUNK = 512
FFN_TF = 512
POOL_TM = 256
MOE_UNIT = 128
MOE_TUNITS = 10
MOE_TMAX = MOE_TUNITS * MOE_UNIT
MOE_TF = 512
MOE_CAP = 2 * T + NE * MOE_UNIT
MOE_NT = MOE_CAP // MOE_TMAX + NE + 1
CMB_TM = 256


def _params(sem, vmem_mib):
    return pltpu.CompilerParams(dimension_semantics=sem, vmem_limit_bytes=vmem_mib * MIB)


def _full(shape):
    n = len(shape)
    return pl.BlockSpec(shape, lambda *_: (0,) * n)


def _resident(shape):
    n = len(shape)
    return pl.BlockSpec(shape, lambda *_: (0,) * n, pipeline_mode=pl.Buffered(1))


def _layer_norm(x, g, b):
    mu = jnp.mean(x, axis=-1, keepdims=True)
    xc = x - mu
    var = jnp.mean(xc * xc, axis=-1, keepdims=True)
    return xc * lax.rsqrt(var + LN_EPS) * g + b


def _rms_norm(x, g):
    return x * lax.rsqrt(jnp.mean(x * x, axis=-1, keepdims=True) + RMS_EPS) * g


def _dot(a, b):
    return jnp.dot(a, b, preferred_element_type=F32)


def _dot_nt(a, b):
    return lax.dot_general(a, b, (((1,), (1,)), ((), ())), preferred_element_type=F32)


def _vt_store(v_ref, h, v):
    v_ref[h, 0, 0:HD, :] = v.T.astype(BF16)
    rows = lax.broadcasted_iota(jnp.int32, (VROWS - HD, v.shape[0]), 0)
    v_ref[h, 0, HD:VROWS, :] = (rows == 0).astype(BF16)


def _mla_proj_kernel(x_ref, win_ref, wsm_ref, qn_ref, kvn_ref, wuq_ref, wuqs_ref, wukv_ref,
                     c_ref, s_ref, q_ref, k_ref, v_ref, fl_ref, *, qscale, transposed_v):
    x = x_ref[...]
    lat = _dot(x, win_ref[...])
    small = _dot(x, wsm_ref[...])
    cq = lat[:, 0:QL]
    ckv = lat[:, QL:QL + KVL]
    kr = lat[:, QL + KVL:LAT_COLS]
    krs = small[:, 0:LANES_V7X]
    fl_ref[...] = small[:, LANES_V7X:]
    cos = c_ref[...]
    sin = s_ref[...]
    qn = _rms_norm(cq, qn_ref[...]).astype(BF16)
    qpre = _dot(qn, wuq_ref[...])
    qsw = _dot(qn, wuqs_ref[...])
    for h in range(H):
        nope = qpre[:, DQ_MLA * h:DQ_MLA * h + NOPE]
        rot = (qpre[:, DQ_MLA * h + NOPE:DQ_MLA * (h + 1)] * cos
               + qsw[:, HD * h:HD * (h + 1)] * sin)
        q_ref[:, DQ_MLA * h:DQ_MLA * h + NOPE] = (nope * qscale).astype(BF16)
        q_ref[:, DQ_MLA * h + NOPE:DQ_MLA * (h + 1)] = (rot * qscale).astype(BF16)
    kvn = _rms_norm(ckv, kvn_ref[...]).astype(BF16)
    kv = _dot(kvn, wukv_ref[...])
    krot = (kr * cos + krs * sin).astype(BF16)
    for h in range(H):
        k_ref[:, DQ_MLA * h:DQ_MLA * h + NOPE] = kv[:, HD * h:HD * (h + 1)].astype(BF16)
        k_ref[:, DQ_MLA * h + NOPE:DQ_MLA * (h + 1)] = krot
    if transposed_v:
        for h in range(H):
            _vt_store(v_ref, h, kv[:, (H + h) * HD:(H + h + 1) * HD])
    else:
        v_ref[...] = kv[:, H * HD:].astype(BF16)


def _mla_proj(x, w_in, wsm, qn, kvn, wuq, wuqs, wukv, ctab, stab, tm):
    m = x.shape[0]
    qscale = (NOPE + ROPE) ** -0.5 * LOG2E
    row = lambda w: pl.BlockSpec((tm, w), lambda i: (i, 0))
    transposed_v = tm == TK
    if transposed_v:
        v_spec = pl.BlockSpec((H, 1, VROWS, TK), lambda i: (0, i, 0, 0))
        v_shape = jax.ShapeDtypeStruct((H, m // TK, VROWS, TK), BF16)
    else:
        v_spec, v_shape = row(H * HD), jax.ShapeDtypeStruct((m, H * HD), BF16)
    lat_spec = pl.BlockSpec((D, LAT_COLS), lambda i: (0, 0), pipeline_mode=pl.Buffered(1))
    return pl.pallas_call(
        functools.partial(_mla_proj_kernel, qscale=qscale, transposed_v=transposed_v),
        grid=(m // tm,),
        in_specs=[row(D), lat_spec, _full(wsm.shape), _full(qn.shape), _full(kvn.shape),
                  _resident(wuq.shape), _resident(wuqs.shape), _resident(wukv.shape),
                  row(LANES_V7X), row(LANES_V7X)],
        out_specs=[row(H * DQ_MLA), row(H * DQ_MLA), v_spec, row(LANES_V7X)],
        out_shape=[jax.ShapeDtypeStruct((m, H * DQ_MLA), BF16),
                   jax.ShapeDtypeStruct((m, H * DQ_MLA), BF16),
                   v_shape,
                   jax.ShapeDtypeStruct((m, LANES_V7X), F32)],
        compiler_params=_params(("parallel",), 56),
        name="mla_proj",
    )(x, w_in, wsm, qn, kvn, wuq, wuqs, wukv, ctab, stab)


def _mm_kernel(x_ref, w_ref, o_ref, *, scaled_blocks, scale):
    y = _dot(x_ref[...], w_ref[...])
    if scaled_blocks:
        y = jnp.where(pl.program_id(1) < scaled_blocks, y * scale, y)
    o_ref[...] = y.astype(o_ref.dtype)


def _matmul(x, w, tm, tn, out_dtype, scaled_blocks=0, scale=1.0, name="matmul"):
    m, k = x.shape
    n = w.shape[1]
    return pl.pallas_call(
        functools.partial(_mm_kernel, scaled_blocks=scaled_blocks, scale=scale),
        grid=(m // tm, n // tn),
        in_specs=[pl.BlockSpec((tm, k), lambda i, j: (i, 0)),
                  _resident((k, n)) if tn == n else pl.BlockSpec((k, tn), lambda i, j: (0, j))],
        out_specs=pl.BlockSpec((tm, tn), lambda i, j: (i, j)),
        out_shape=jax.ShapeDtypeStruct((m, n), out_dtype),
        compiler_params=_params(("parallel", "parallel"), 48),
        name=name,
    )(x, w)


def _fox_proj_kernel(x_ref, w_ref, qk_ref, vt_ref, *, scale):
    j = pl.program_id(1)
    n = H * HD
    part = lambda c: _dot(x_ref[...].astype(BF16), w_ref[:, c * n:(c + 1) * n])

    @pl.when(j == 0)
    def _():
        qk_ref[...] = (part(0) * scale).astype(BF16)

    @pl.when(j == 1)
    def _():
        qk_ref[...] = part(1).astype(BF16)

    @pl.when(j == 2)
    def _():
        y = part(2)
        for h in range(H):
            _vt_store(vt_ref, h, y[:, h * HD:(h + 1) * HD])


def _fox_proj(x, w, scale):
    n = H * HD
    return pl.pallas_call(
        functools.partial(_fox_proj_kernel, scale=scale),
        grid=(T // TK, 3),
        in_specs=[pl.BlockSpec((TK, D), lambda i, j: (i, 0)),
                  _resident((D, 3 * n))],
        out_specs=[pl.BlockSpec((TK, n), lambda i, j: (i, jnp.minimum(j, 1))),
                   pl.BlockSpec((H, 1, VROWS, TK), lambda i, j: (0, i, 0, 0))],
        out_shape=[jax.ShapeDtypeStruct((T, 2 * n), BF16),
                   jax.ShapeDtypeStruct((H, T // TK, VROWS, TK), BF16)],
        compiler_params=_params(("parallel", "arbitrary"), 48),
        name="fox_proj",
    )(x, w)


def _split3(a):
    hi = a.astype(BF16)
    r = a - hi.astype(F32)
    mid = r.astype(BF16)
    lo = (r - mid.astype(F32)).astype(BF16)
    return hi, mid, lo


def _forget_cumsum_kernel(fl_ref, b_ref, o_ref):
    z = fl_ref[...] + b_ref[...]
    logf = jnp.minimum(z, 0.0) - jnp.log(1.0 + jnp.exp(-jnp.abs(z)))
    n = LANES_V7X
    ri = lax.broadcasted_iota(jnp.int32, (n, n), 0)
    ci = lax.broadcasted_iota(jnp.int32, (n, n), 1)
    upper = (ri <= ci).astype(BF16)
    ones = jnp.ones((n, n), BF16)
    within = jnp.zeros(logf.shape, F32)
    total = jnp.zeros(logf.shape, F32)
    for part in _split3(logf):
        within += _dot(part, upper)
        total += _dot(part, ones)
    r = logf.shape[0]
    lower = (lax.broadcasted_iota(jnp.int32, (r, r), 1)
             < lax.broadcasted_iota(jnp.int32, (r, r), 0)).astype(BF16)
    before = jnp.zeros(logf.shape, F32)
    for part in _split3(total):
        before += _dot(lower, part)
    c2 = (within + before) * LOG2E
    for p, part in enumerate(_split3(c2)):
        o_ref[p] = part
        o_ref[3 + p] = -part


def _forget_cumsum(fl_t, bias_b):
    r = T // LANES_V7X
    return pl.pallas_call(
        _forget_cumsum_kernel,
        grid=(H,),
        in_specs=[pl.BlockSpec((None, r, LANES_V7X), lambda h: (h, 0, 0)),
                  pl.BlockSpec((None, 1, LANES_V7X), lambda h: (h, 0, 0))],
        out_specs=pl.BlockSpec((None, 6, r, LANES_V7X), lambda h: (h, 0, 0, 0)),
        out_shape=jax.ShapeDtypeStruct((H, 6, r, LANES_V7X), BF16),
        compiler_params=_params(("parallel",), 32),
        name="forget_cumsum",
    )(fl_t, bias_b)


ATT_NH = 2
AUG_STRIDE = 16


def _attn_kernel(*refs, fox, dq):
    if fox:
        (q_ref, k_ref, vt_ref, km_ref, vmt_ref, qa_ref, ka_ref, kam_ref, o_ref,
         m_scr, acc_scr, s0_scr, s1_scr, mb0_scr, mb1_scr, qf_scr) = refs
    else:
        (q_ref, k_ref, vt_ref, km_ref, vmt_ref, o_ref,
         m_scr, acc_scr, s0_scr, s1_scr, mb0_scr, mb1_scr) = refs
    hp = pl.program_id(0)
    i = pl.program_id(1)
    m_scr[...] = jnp.full(m_scr.shape, -jnp.inf, F32)
    acc_scr[...] = jnp.zeros(acc_scr.shape, F32)
    if fox:
        lane = lax.broadcasted_iota(jnp.int32, qa_ref.shape, 1)
        for g in range(ATT_NH):
            lo = (hp * ATT_NH + g) * AUG_STRIDE
            own = jnp.logical_and(lane >= lo, lane < lo + AUG_STRIDE)
            qf_scr[g, :, 0:dq] = q_ref[:, g * dq:(g + 1) * dq]
            qf_scr[g, :, dq:2 * dq] = jnp.where(own, qa_ref[...], jnp.zeros_like(qa_ref))

    def query(g):
        return qf_scr[g] if fox else q_ref[:, g * dq:(g + 1) * dq]

    def update(g, st, mb, vt):
        m_prev = m_scr[g]
        m_new = jnp.maximum(m_prev, mb)
        alpha = jnp.exp2(m_prev - m_new)
        p = jnp.exp2(st - m_new)
        acc_scr[g] = alpha * acc_scr[g] + _dot(vt, p.astype(BF16))
        m_scr[g] = m_new

    def stage1(slot, j):
        s_ref, mb_ref = slots[slot]
        off = pl.multiple_of(j * TK, TK)
        for g in range(ATT_NH):
            kb = k_ref[pl.ds(off, TK), g * dq:(g + 1) * dq]
            if fox:
                kb = jnp.concatenate([kb, ka_ref[pl.ds(off, TK), :]], axis=1)
            st = _dot_nt(kb, query(g))
            s_ref[g] = st
            mb_ref[g] = jnp.max(st, axis=0, keepdims=True)

    def stage2(slot, j, diag_key0=None):
        s_ref, mb_ref = slots[slot]
        for g in range(ATT_NH):
            st = s_ref[g]
            mb = mb_ref[g]
            if diag_key0 is not None:
                keys = lax.broadcasted_iota(jnp.int32, st.shape, 0) + diag_key0
                qrys = lax.broadcasted_iota(jnp.int32, st.shape, 1)
                st = jnp.where(keys <= qrys, st, NEG)
                mb = jnp.max(st, axis=0, keepdims=True)
            update(g, st, mb, vt_ref[g, j])

    slots = ((s0_scr, mb0_scr), (s1_scr, mb1_scr))
    stage1(0, 0)

    def body(jj, carry):
        j = 2 * jj
        stage1(1, j + 1)
        stage2(0, j)
        stage1(0, j + 2)
        stage2(1, j + 1)
        return carry

    lax.fori_loop(0, i, body, 0)
    stage1(1, 2 * i + 1)
    stage2(0, 2 * i, diag_key0=0)
    stage2(1, 2 * i + 1, diag_key0=TK)

    for g in range(ATT_NH):
        kb = km_ref[:, g * dq:(g + 1) * dq]
        if fox:
            kb = jnp.concatenate([kb, kam_ref[...]], axis=1)
        st = _dot_nt(kb, query(g))
        update(g, st, jnp.max(st, axis=0, keepdims=True), vmt_ref[g])
        acc = acc_scr[g]
        o = (acc[0:HD] / acc[HD:HD + 1]).T
        o_ref[:, g * HD:(g + 1) * HD] = o.astype(o_ref.dtype)


def _attention(q, k, vt, km, vmt, k_col0, dq, aug=None, name="attn"):
    fox = aug is not None
    nh, tq = ATT_NH, ATT_TQ
    in_specs = [pl.BlockSpec((tq, nh * dq), lambda h, i: (i, h)),
                pl.BlockSpec((T, nh * dq), lambda h, i: (0, k_col0 // nh + h)),
                pl.BlockSpec((nh, T // TK, VROWS, TK), lambda h, i: (h, 0, 0, 0)),
                pl.BlockSpec((NM, nh * dq), lambda h, i: (0, k_col0 // nh + h)),
                pl.BlockSpec((nh, VROWS, NM), lambda h, i: (h, 0, 0))]
    args = [q, k, vt, km, vmt]
    scratch = [pltpu.VMEM((nh, 1, tq), F32), pltpu.VMEM((nh, VROWS, tq), F32),
               pltpu.VMEM((nh, TK, tq), F32), pltpu.VMEM((nh, TK, tq), F32),
               pltpu.VMEM((nh, 1, tq), F32), pltpu.VMEM((nh, 1, tq), F32)]
    if fox:
        in_specs += [pl.BlockSpec((tq, LANES_V7X), lambda h, i: (i, 0)),
                     _full((T, LANES_V7X)), _full((NM, LANES_V7X))]
        args += list(aug)
        scratch.append(pltpu.VMEM((nh, tq, 2 * dq), BF16))
    return pl.pallas_call(
        functools.partial(_attn_kernel, fox=fox, dq=dq),
        grid=(H // nh, T // tq),
        in_specs=in_specs,
        out_specs=pl.BlockSpec((tq, nh * HD), lambda h, i: (i, h)),
        out_shape=jax.ShapeDtypeStruct((T, H * HD), BF16),
        scratch_shapes=scratch,
        compiler_params=_params(("parallel", "arbitrary"), 56),
        name=name,
    )(*args)


def _meta_attn_kernel(q_ref, k_ref, v_ref, o_ref):
    s = _dot_nt(q_ref[...], k_ref[...])
    rows = lax.broadcasted_iota(jnp.int32, s.shape, 0)
    cols = lax.broadcasted_iota(jnp.int32, s.shape, 1)
    s = jnp.where(cols <= rows, s, NEG)
    p = jnp.exp2(s - jnp.max(s, axis=1, keepdims=True))
    o = _dot(p.astype(BF16), v_ref[...]) / jnp.sum(p, axis=1, keepdims=True)
    o_ref[...] = o.astype(o_ref.dtype)


def _meta_attention(q, k, v, q_col0, k_col0, v_col0, dq, name):
    return pl.pallas_call(
        _meta_attn_kernel,
        grid=(H,),
        in_specs=[pl.BlockSpec((NM, dq), lambda h: (0, q_col0 + h)),
                  pl.BlockSpec((NM, dq), lambda h: (0, k_col0 + h)),
                  pl.BlockSpec((NM, HD), lambda h: (0, v_col0 + h))],
        out_specs=pl.BlockSpec((NM, HD), lambda h: (0, h)),
        out_shape=jax.ShapeDtypeStruct((NM, H * HD), BF16),
        compiler_params=_params(("parallel",), 32),
        name=name,
    )(q, k, v)


def _attn_out_kernel(om_ref, of_ref, w_ref, h_ref, g_ref, b_ref, o_ref):
    m = _dot(om_ref[...], w_ref[0:H * HD, :]) + _dot(of_ref[...], w_ref[H * HD:, :])
    o_ref[...] = _layer_norm(ALPHA * h_ref[...] + m, g_ref[...], b_ref[...])


def _attn_out(om, of, w, h, g, b, tm):
    m = h.shape[0]
    row = lambda w_: pl.BlockSpec((tm, w_), lambda i: (i, 0))
    return pl.pallas_call(
        _attn_out_kernel,
        grid=(m // tm,),
        in_specs=[row(H * HD), row(H * HD), _resident(w.shape), row(D), _full(g.shape), _full(b.shape)],
        out_specs=row(D),
        out_shape=jax.ShapeDtypeStruct((m, D), F32),
        compiler_params=_params(("parallel",), 48),
        name="attn_out_ln",
    )(om, of, w, h, g, b)


def _silu(x):
    return x / (1.0 + jnp.exp(-x))


def _ffn_kernel(h_ref, hm_ref, wg_ref, wu_ref, wd_ref, g_ref, b_ref, o_ref, om_ref):
    i = pl.program_id(0)
    f = pl.program_id(1)
    last = pl.num_programs(1) - 1
    rows = h_ref.shape[0]
    chunk = FFN_CHUNK

    def chain(x):
        hid = _silu(_dot(x, wg_ref[...])) * _dot(x, wu_ref[...])
        return _dot(hid, wd_ref[...])

    @pl.when(f == 0)
    def _():
        o_ref[...] = jnp.zeros(o_ref.shape, F32)

    @pl.when(i == 0)
    def _():
        @pl.when(f == 0)
        def _():
            om_ref[...] = jnp.zeros(om_ref.shape, F32)
        part = chain(jnp.concatenate([h_ref[0:chunk, :], hm_ref[...]], axis=0))
        o_ref[0:chunk, :] += part[0:chunk]
        om_ref[...] += part[chunk:]

        @pl.when(f == last)
        def _():
            om_ref[...] = _layer_norm(ALPHA * hm_ref[...] + om_ref[...], g_ref[...], b_ref[...])

    @pl.when(i != 0)
    def _():
        o_ref[0:chunk, :] += chain(h_ref[0:chunk, :])

    for c in range(1, rows // chunk):
        sl = slice(c * chunk, (c + 1) * chunk)
        o_ref[sl, :] += chain(h_ref[sl, :])

    @pl.when(f == last)
    def _():
        o_ref[...] = _layer_norm(ALPHA * h_ref[...] + o_ref[...], g_ref[...], b_ref[...])


def _ffn(h, hm, wg, wu, wd, g, b):
    tm = FFN_TM
    once = pl.Buffered(1)
    return pl.pallas_call(
        _ffn_kernel,
        grid=(T // tm, FD // FFN_TF),
        in_specs=[pl.BlockSpec((tm, D), lambda i, f: (i, 0), pipeline_mode=once),
                  pl.BlockSpec((NM, D), lambda i, f: (0, 0)),
                  pl.BlockSpec((D, FFN_TF), lambda i, f: (0, f)),
                  pl.BlockSpec((D, FFN_TF), lambda i, f: (0, f)),
                  pl.BlockSpec((FFN_TF, D), lambda i, f: (f, 0)),
                  pl.BlockSpec((1, D), lambda i, f: (0, 0)),
                  pl.BlockSpec((1, D), lambda i, f: (0, 0))],
        out_specs=[pl.BlockSpec((tm, D), lambda i, f: (i, 0)),
                   pl.BlockSpec((NM, D), lambda i, f: (0, 0))],
        out_shape=[jax.ShapeDtypeStruct((T, D), F32), jax.ShapeDtypeStruct((NM, D), F32)],
        compiler_params=_params(("arbitrary", "arbitrary"), 58),
        name="ffn_ln",
    )(h, hm, wg, wu, wd, g, b)


def _pool_kernel(hp_ref, prev_ref, meta_ref, h_ref, wg_ref, sc_ref, wo_ref, g_ref, b_ref, o_ref,
                 ext_scr, z_scr):
    tm = hp_ref.shape[0]
    halo = jnp.where(pl.program_id(0) == 0, meta_ref[...], prev_ref[...])
    ext_scr[0:NM, :] = halo
    ext_scr[NM:NM + tm, :] = hp_ref[...]
    for gi, w in enumerate(POOL_W):
        cols = slice(PG * gi, PG * (gi + 1))
        cur = ext_scr[NM:NM + tm, cols]
        tot = cur
        for j in range(1, w):
            tot = tot + ext_scr[NM - j:NM - j + tm, cols]
        diff = tot * (1.0 / w) - cur
        y = _dot(diff, wg_ref[gi]) * sc_ref[:, cols]
        z_scr[:, cols] = y
    m = _dot(z_scr[...], wo_ref[...])
    o_ref[...] = _layer_norm(ALPHA * h_ref[...] + m, g_ref[...], b_ref[...])


def _pool_mixer(hp, hp_meta, h, wgroup, scale, wout, g, b):
    tm = POOL_TM
    row = pl.BlockSpec((tm, D), lambda i: (i, 0))
    prev = pl.BlockSpec((NM, D), lambda i: (jnp.maximum(i * (tm // NM) - 1, 0), 0))
    return pl.pallas_call(
        _pool_kernel,
        grid=(T // tm,),
        in_specs=[row, prev, _full(hp_meta.shape), row, _resident(wgroup.shape), _full(scale.shape),
                  _resident(wout.shape), _full(g.shape), _full(b.shape)],
        out_specs=row,
        out_shape=jax.ShapeDtypeStruct((T, D), F32),
        scratch_shapes=[pltpu.VMEM((NM + tm, D), F32), pltpu.VMEM((tm, D), F32)],
        compiler_params=_params(("parallel",), 48),
        name="pool_ln",
    )(hp, hp, hp_meta, h, wgroup, scale, wout, g, b)


def _router_kernel(h_ref, w_ref, b_ref, info_ref, cnt_ref, carry_scr):
    tm = h_ref.shape[0]

    @pl.when(pl.program_id(0) == 0)
    def _():
        carry_scr[...] = jnp.zeros(carry_scr.shape, F32)

    x = h_ref[...]
    w = w_ref[...]
    xh, wh = x.astype(BF16), w.astype(BF16)
    xl = (x - xh.astype(F32)).astype(BF16)
    wl = (w - wh.astype(F32)).astype(BF16)
    logits = _dot(xh, wh) + _dot(xh, wl) + _dot(xl, wh) + b_ref[...]
    lane = lax.broadcasted_iota(jnp.int32, logits.shape, 1).astype(F32)
    nolane = float(LANES_V7X)
    m1 = jnp.max(logits, axis=1, keepdims=True)
    i1 = jnp.min(jnp.where(logits == m1, lane, nolane), axis=1, keepdims=True)
    rest = jnp.where(lane == i1, -jnp.inf, logits)
    m2 = jnp.max(rest, axis=1, keepdims=True)
    i2 = jnp.min(jnp.where(rest == m2, lane, nolane), axis=1, keepdims=True)
    e = jnp.exp(m2 - m1)
    g1 = 1.0 / (1.0 + e)
    g2 = e / (1.0 + e)
    hit1 = lane == i1
    hit2 = lane == i2
    member = jnp.logical_or(hit1, hit2).astype(BF16)
    earlier = (lax.broadcasted_iota(jnp.int32, (tm, tm), 1)
               < lax.broadcasted_iota(jnp.int32, (tm, tm), 0)).astype(BF16)
    before = _dot(earlier, member) + carry_scr[0:1, :]
    r1 = jnp.sum(jnp.where(hit1, before, 0.0), axis=1, keepdims=True)
    r2 = jnp.sum(jnp.where(hit2, before, 0.0), axis=1, keepdims=True)
    carry = carry_scr[0:1, :] + jnp.sum(member.astype(F32), axis=0, keepdims=True)
    carry_scr[...] = jnp.broadcast_to(carry, carry_scr.shape)
    cnt_ref[...] = jnp.broadcast_to(carry, cnt_ref.shape)
    info = jnp.where(lane == 0, g1, 0.0)
    info = jnp.where(lane == 1, g2, info)
    info = jnp.where(lane == 2, i1, info)
    info = jnp.where(lane == 3, i2, info)
    info = jnp.where(lane == 4, r1, info)
    info = jnp.where(lane == 5, r2, info)
    info_ref[...] = info


def _router(h, w_pad, b_pad):
    tm = TM
    return pl.pallas_call(
        _router_kernel,
        grid=(T // tm,),
        in_specs=[pl.BlockSpec((tm, D), lambda i: (i, 0)), _full(w_pad.shape), _full(b_pad.shape)],
        out_specs=[pl.BlockSpec((tm, LANES_V7X), lambda i: (i, 0)), _full((8, LANES_V7X))],
        out_shape=[jax.ShapeDtypeStruct((T, LANES_V7X), F32),
                   jax.ShapeDtypeStruct((8, LANES_V7X), F32)],
        scratch_shapes=[pltpu.VMEM((8, LANES_V7X), F32)],
        compiler_params=_params(("arbitrary",), 32),
        name="router",
    )(h, w_pad, b_pad)


def _moe_row_copy(h_hbm, x_scr, sem, tok, r):
    return pltpu.make_async_copy(h_hbm.at[pl.ds(tok, 1), :], x_scr.at[pl.ds(r, 1), :], sem)


def _moe_out_copy(acc_scr, y_hbm, sem, row0, u):
    off = pl.multiple_of(u * MOE_UNIT, MOE_UNIT)
    dst = pl.multiple_of(row0 + off, MOE_UNIT)
    return pltpu.make_async_copy(acc_scr.at[pl.ds(off, MOE_UNIT), :],
                                 y_hbm.at[pl.ds(dst, MOE_UNIT), :], sem)


def _moe_kernel(te_ref, row0_ref, nunit_ref, used_ref, pad_ref, dest_ref,
                h_hbm, wg_ref, wu_ref, wd_ref, y_hbm,
                slot_tok, x_scr, acc_scr, gsem, osem):
    t = pl.program_id(0)
    f = pl.program_id(1)
    nf = pl.num_programs(1)

    @pl.when(jnp.logical_and(t == 0, f == 0))
    def _():
        for e in range(NE):
            def clear(s, c):
                slot_tok[s] = 0
                return c
            lax.fori_loop(pad_ref[2 * e], pad_ref[2 * e + 1], clear, 0)

        def place(tok, c):
            slot_tok[dest_ref[2 * tok]] = tok
            slot_tok[dest_ref[2 * tok + 1]] = tok
            return c
        lax.fori_loop(0, T, place, 0, unroll=8)

    @pl.when(t < used_ref[0])
    def _():
        row0 = row0_ref[t]
        nunit = nunit_ref[t]
        nrows = nunit * MOE_UNIT

        @pl.when(f == 0)
        def _():
            def issue_group(r8, c):
                base = pl.multiple_of(r8 * 8, 8)
                for k in range(8):
                    _moe_row_copy(h_hbm, x_scr, gsem, slot_tok[row0 + base + k], base + k).start()
                return c
            lax.fori_loop(0, nunit * (MOE_UNIT // 8), issue_group, 0)

            @pl.when(t > 0)
            def _():
                def finish(u, c):
                    _moe_out_copy(acc_scr, y_hbm, osem, row0_ref[t - 1], u).wait()
                    return c
                lax.fori_loop(0, nunit_ref[t - 1], finish, 0)

            def clear_acc(u, c):
                off = pl.multiple_of(u * MOE_UNIT, MOE_UNIT)
                acc_scr[pl.ds(off, MOE_UNIT), :] = jnp.zeros((MOE_UNIT, D), F32)
                return c
            lax.fori_loop(0, nunit, clear_acc, 0)

            def drain_group(r8, c):
                base = pl.multiple_of(r8 * 8, 8)
                for k in range(8):
                    _moe_row_copy(h_hbm, x_scr, gsem, 0, base + k).wait()
                return c
            lax.fori_loop(0, nunit * (MOE_UNIT // 8), drain_group, 0)

        def chunk(off, rows):
            x = x_scr[pl.ds(off, rows), :]
            hid = _silu(_dot(x, wg_ref[...])) * _dot(x, wu_ref[...])
            acc_scr[pl.ds(off, rows), :] += _dot(hid, wd_ref[...])

        def quad(c, carry):
            chunk(pl.multiple_of(c * 4 * MOE_UNIT, 4 * MOE_UNIT), 4 * MOE_UNIT)
            return carry
        lax.fori_loop(0, nunit // 4, quad, 0)

        @pl.when(nunit % 4 >= 2)
        def _():
            chunk(pl.multiple_of((nunit // 4) * 4 * MOE_UNIT, 2 * MOE_UNIT), 2 * MOE_UNIT)

        @pl.when(nunit % 2 == 1)
        def _():
            chunk(pl.multiple_of((nunit - 1) * MOE_UNIT, MOE_UNIT), MOE_UNIT)

        @pl.when(f == nf - 1)
        def _():
            def start(u, carry):
                _moe_out_copy(acc_scr, y_hbm, osem, row0, u).start()
                return carry
            lax.fori_loop(0, nunit, start, 0)

            @pl.when(t == used_ref[0] - 1)
            def _():
                def finish(u, carry):
                    _moe_out_copy(acc_scr, y_hbm, osem, row0, u).wait()
                    return carry
                lax.fori_loop(0, nunit, finish, 0)

    @pl.when(jnp.logical_and(t == pl.num_programs(0) - 1, f == nf - 1))
    def _():
        acc_scr[0:MOE_UNIT, :] = jnp.zeros((MOE_UNIT, D), F32)

        def fill(u, carry):
            cp = _moe_out_copy(acc_scr, y_hbm, osem, u * MOE_UNIT, 0)
            cp.start()
            cp.wait()
            return carry
        lax.fori_loop(used_ref[1], MOE_CAP // MOE_UNIT, fill, 0)


def _moe_experts(h, wg, wu, wd, tile_e, tile_row0, tile_nunit, used, pad, dest):
    nf = FE // MOE_TF

    def fidx(t, f, us):
        return jnp.where(t < us[0], f, nf - 1)

    grid_spec = pltpu.PrefetchScalarGridSpec(
        num_scalar_prefetch=6,
        grid=(MOE_NT, nf),
        in_specs=[pl.BlockSpec(memory_space=pl.ANY),
                  pl.BlockSpec((None, D, MOE_TF), lambda t, f, te, r0, nu, us, pa, de: (te[t], 0, fidx(t, f, us))),
                  pl.BlockSpec((None, D, MOE_TF), lambda t, f, te, r0, nu, us, pa, de: (te[t], 0, fidx(t, f, us))),
                  pl.BlockSpec((None, MOE_TF, D), lambda t, f, te, r0, nu, us, pa, de: (te[t], fidx(t, f, us), 0))],
        out_specs=pl.BlockSpec(memory_space=pl.ANY),
        scratch_shapes=[pltpu.SMEM((MOE_CAP,), jnp.int32),
                        pltpu.VMEM((MOE_TMAX, D), F32),
                        pltpu.VMEM((MOE_TMAX, D), F32),
                        pltpu.SemaphoreType.DMA,
                        pltpu.SemaphoreType.DMA],
    )
    return pl.pallas_call(
        _moe_kernel,
        grid_spec=grid_spec,
        out_shape=jax.ShapeDtypeStruct((MOE_CAP, D), F32),
        compiler_params=_params(("arbitrary", "arbitrary"), 58),
        name="moe_experts",
    )(tile_e, tile_row0, tile_nunit, used, pad, dest, h, wg, wu, wd)


def _cmb_copy(y_hbm, ybuf, sem, buf, slot, k, r):
    return pltpu.make_async_copy(y_hbm.at[pl.ds(slot, 1), :], ybuf.at[buf, k, pl.ds(r, 1), :],
                                 sem.at[buf])


def _combine_kernel(dest_ref, h_ref, info_ref, y_hbm, g_ref, b_ref, o_ref, ybuf, sem):
    tm = h_ref.shape[0]
    i = pl.program_id(0)
    cur = lax.rem(i, 2)

    def gather(tile, buf):
        def issue_group(r8, c):
            base = pl.multiple_of(r8 * 8, 8)
            for k in range(8):
                tok = tile * tm + base + k
                _cmb_copy(y_hbm, ybuf, sem, buf, dest_ref[2 * tok], 0, base + k).start()
                _cmb_copy(y_hbm, ybuf, sem, buf, dest_ref[2 * tok + 1], 1, base + k).start()
            return c
        lax.fori_loop(0, tm // 8, issue_group, 0)

    @pl.when(i == 0)
    def _():
        gather(0, 0)

    @pl.when(i + 1 < pl.num_programs(0))
    def _():
        gather(i + 1, 1 - cur)

    def drain_group(r8, c):
        base = pl.multiple_of(r8 * 8, 8)
        for k in range(8):
            _cmb_copy(y_hbm, ybuf, sem, cur, 0, 0, base + k).wait()
            _cmb_copy(y_hbm, ybuf, sem, cur, 0, 1, base + k).wait()
        return c
    lax.fori_loop(0, tm // 8, drain_group, 0)
    info = info_ref[...]
    moe = info[:, 0:1] * ybuf[cur, 0] + info[:, 1:2] * ybuf[cur, 1]
    o_ref[...] = _layer_norm(ALPHA * h_ref[...] + moe, g_ref[...], b_ref[...])


def _combine(dest, h, info, y_rows, g, b):
    tm = CMB_TM
    grid_spec = pltpu.PrefetchScalarGridSpec(
        num_scalar_prefetch=1,
        grid=(T // tm,),
        in_specs=[pl.BlockSpec((tm, D), lambda i, de: (i, 0)),
                  pl.BlockSpec((tm, LANES_V7X), lambda i, de: (i, 0)),
                  pl.BlockSpec(memory_space=pl.ANY),
                  pl.BlockSpec((1, D), lambda i, de: (0, 0)),
                  pl.BlockSpec((1, D), lambda i, de: (0, 0))],
        out_specs=pl.BlockSpec((tm, D), lambda i, de: (i, 0)),
        scratch_shapes=[pltpu.VMEM((2, 2, tm, D), F32), pltpu.SemaphoreType.DMA((2,))],
    )
    return pl.pallas_call(
        _combine_kernel,
        grid_spec=grid_spec,
        out_shape=jax.ShapeDtypeStruct((T, D), F32),
        compiler_params=_params(("arbitrary",), 32),
        name="moe_combine_ln",
    )(dest, h, info, y_rows, g, b)


def _rope_tables(pos):
    inv_freq = THETA ** (-jnp.arange(0, ROPE, 2, dtype=F32) / ROPE)
    ang = pos[:, None] * inv_freq[None, :]
    cos = jnp.cos(ang).astype(F32)
    sin = jnp.sin(ang).astype(F32)
    zero = jnp.zeros((pos.shape[0], LANES_V7X - ROPE), F32)
    return (jnp.concatenate([cos, cos, zero], axis=1),
            jnp.concatenate([-sin, sin, zero], axis=1))


def _expert_tiles(counts):
    units = (counts + MOE_UNIT - 1) // MOE_UNIT
    seg_unit = jnp.cumsum(units) - units
    ntile = (units + MOE_TUNITS - 1) // MOE_TUNITS
    tile_end = jnp.cumsum(ntile)
    n_used = tile_end[-1]
    tid = jnp.arange(MOE_NT, dtype=jnp.int32)
    e = jnp.minimum(jnp.sum(tid[:, None] >= tile_end[None, :], axis=1), NE - 1).astype(jnp.int32)
    local = tid - (tile_end - ntile)[e]
    nt = jnp.maximum(ntile[e], 1)
    lo = (local * units[e]) // nt
    hi = ((local + 1) * units[e]) // nt
    used = tid < n_used
    last = jnp.maximum(n_used - 1, 0)
    e = jnp.where(used, e, e[last])
    row0 = jnp.where(used, (seg_unit[e] + lo) * MOE_UNIT, 0)
    nunit = jnp.where(used, hi - lo, 0)
    used_info = jnp.stack([n_used, jnp.sum(units)]).astype(jnp.int32)
    seg_row = seg_unit * MOE_UNIT
    pad = jnp.stack([seg_row + counts, seg_row + units * MOE_UNIT], axis=1).reshape(2 * NE)
    return (e.astype(jnp.int32), row0.astype(jnp.int32), nunit.astype(jnp.int32),
            used_info, pad.astype(jnp.int32), seg_row.astype(jnp.int32))


def kernel(x, meta_tokens, attn_w_in, fox_forget_bias, mla_q_norm, mla_kv_norm, mla_w_uq, mla_w_ukv, attn_w_out, pool_w_in, pool_w_group, pool_scale, pool_w_out, ffn_w_gate, ffn_w_up, ffn_w_down, moe_w_router, moe_b_router, moe_w_gate, moe_w_up, moe_w_down, ln_mix_g, ln_mix_b, ln_ffn_g, ln_ffn_b):
    assert x.shape == (1, T, D) and meta_tokens.shape == (NM, D)
    xt = x[0]
    meta = meta_tokens.astype(x.dtype)

    w_in = attn_w_in[0]
    o_kr, o_qf, o_f = QL + KVL, QL + KVL + ROPE, QL + KVL + ROPE + 3 * H * HD
    kr1, kr2 = w_in[:, o_kr:o_kr + ROPE // 2], w_in[:, o_kr + ROPE // 2:o_qf]
    zpad = lambda n: jnp.zeros((D, n), F32)
    w_small = jnp.concatenate([kr2, kr1, zpad(LANES_V7X - ROPE),
                               w_in[:, o_f:o_f + H], zpad(LANES_V7X - H)], axis=1)
    w_fox = w_in[:, o_qf:o_f].astype(BF16)
    w_in = w_in[:, :LAT_COLS]
    wuq = mla_w_uq[0].reshape(QL, H, NOPE + ROPE)
    uq_n, uq_1, uq_2 = wuq[..., :NOPE], wuq[..., NOPE:NOPE + ROPE // 2], wuq[..., NOPE + ROPE // 2:]
    zq = jnp.zeros((QL, H, LANES_V7X - ROPE), F32)
    w_uq = jnp.concatenate([uq_n, uq_1, uq_2, zq], axis=-1).reshape(QL, H * DQ_MLA).astype(BF16)
    w_uqs = jnp.concatenate([uq_2, uq_1, zq], axis=-1).reshape(QL, H * HD).astype(BF16)
    wukv = mla_w_ukv[0].reshape(KVL, H, NOPE + HD)
    w_ukv = jnp.concatenate([wukv[..., :NOPE].reshape(KVL, H * NOPE),
                             wukv[..., NOPE:].reshape(KVL, H * HD)], axis=1).astype(BF16)
    w_ao = attn_w_out[0]
    w_fg, w_fu, w_fd = ffn_w_gate[0], ffn_w_up[0], ffn_w_down[0]
    w_pi, w_pg, w_po = pool_w_in[0], pool_w_group[0], pool_w_out[0]
    qn, kvn = mla_q_norm[0][None, :], mla_kv_norm[0][None, :]
    row = lambda v: v[None, :]

    ctab_t, stab_t = _rope_tables(jnp.arange(NM, NM + T, dtype=F32))
    ctab_m, stab_m = _rope_tables(jnp.arange(NM, dtype=F32))

    fox_scale = HD ** -0.5 * LOG2E
    q_t, k_t, vt_t, fl_t = _mla_proj(xt, w_in, w_small, qn, kvn, w_uq, w_uqs, w_ukv, ctab_t, stab_t, TK)
    q_m, k_m, v_m, _ = _mla_proj(meta, w_in, w_small, qn, kvn, w_uq, w_uqs, w_ukv, ctab_m, stab_m, NM)
    fox_qk, fox_vt = _fox_proj(xt, w_fox, fox_scale)
    fox_m = _matmul(meta, w_fox, NM, H * HD, BF16, scaled_blocks=1, scale=fox_scale, name="fox_proj_meta")

    fl = fl_t[:, :H].T.reshape(H, T // LANES_V7X, LANES_V7X)
    bias_b = jnp.broadcast_to(fox_forget_bias[0][:, None, None], (H, 1, LANES_V7X))
    parts = _forget_cumsum(fl, bias_b).reshape(H, 6, T)
    pos = parts[:, 0:3].transpose(2, 0, 1)
    neg = parts[:, 3:6].transpose(2, 0, 1)
    lanes = lambda pieces: jnp.concatenate(pieces, axis=2).reshape(-1, H * AUG_STRIDE)
    cst = lambda rows, n, v: jnp.full((rows, H, n), v, BF16)
    aug = (lanes([cst(T, 3, 1.0), pos, cst(T, AUG_STRIDE - 6, 0.0)]),
           lanes([neg, cst(T, AUG_STRIDE - 3, 0.0)]),
           lanes([cst(NM, 3, 0.0), cst(NM, 3, -1.0), cst(NM, AUG_STRIDE - 6, 0.0)]))

    ones_rows = jnp.concatenate([jnp.ones((H, 1, NM), BF16), jnp.zeros((H, VROWS - HD - 1, NM), BF16)], axis=1)
    to_vmt = lambda v: jnp.concatenate([v.reshape(NM, H, HD).transpose(1, 2, 0), ones_rows], axis=1)
    o_mla = _attention(q_t, k_t, vt_t, k_m, to_vmt(v_m), 0, DQ_MLA, name="mla_attn")
    o_fox = _attention(fox_qk, fox_qk, fox_vt, fox_m, to_vmt(fox_m[:, 2 * H * HD:]),
                       H, HD, aug, name="fox_attn")
    om_mla = _meta_attention(q_m, k_m, v_m, 0, 0, 0, DQ_MLA, "mla_attn_meta")
    om_fox = _meta_attention(fox_m, fox_m, fox_m, 0, H, 2 * H, HD, "fox_attn_meta")

    g0, b0, g1, b1 = row(ln_mix_g[0]), row(ln_mix_b[0]), row(ln_ffn_g[0]), row(ln_ffn_b[0])
    h_t = _attn_out(o_mla, o_fox, w_ao, xt, g0, b0, TM)
    h_m = _attn_out(om_mla, om_fox, w_ao, meta, g0, b0, NM)
    h_t, h_m = _ffn(h_t, h_m, w_fg, w_fu, w_fd, g1, b1)

    hp_t = _matmul(h_t, w_pi, TM, D, F32, name="pool_in")
    hp_m = _matmul(h_m, w_pi, NM, D, F32, name="pool_in_meta")
    h_t = _pool_mixer(hp_t, hp_m, h_t, w_pg, row(pool_scale[0]), w_po,
                      row(ln_mix_g[1]), row(ln_mix_b[1]))

    w_r = jnp.concatenate([moe_w_router[0], jnp.zeros((D, LANES_V7X - NE), F32)], axis=1)
    b_r = jnp.concatenate([moe_b_router[0], jnp.full((LANES_V7X - NE,), NEG, F32)])[None, :]
    info, cnt = _router(h_t, w_r, b_r)
    counts = cnt[0, :NE].astype(jnp.int32)
    tile_e, tile_row0, tile_nunit, used, pad, seg_row = _expert_tiles(counts)
    experts = info[:, 2:4].astype(jnp.int32)
    ranks = info[:, 4:6].astype(jnp.int32)
    dest = (seg_row[experts] + ranks).reshape(2 * T)
    y_rows = _moe_experts(h_t, moe_w_gate[0], moe_w_up[0], moe_w_down[0],
                          tile_e, tile_row0, tile_nunit, used, pad, dest)
    out = _combine(dest, h_t, info, y_rows, row(ln_ffn_g[1]), row(ln_ffn_b[1]))
    return out[None]
```

```python
import functools
import math

import jax
import jax.numpy as jnp
import numpy as np
from jax import lax
from jax.experimental import pallas as pl
from jax.experimental.pallas import tpu as pltpu

D = 2048
T = 8192
NM = 16
H = 8
QL = 512
KVL = 512
NOPE = 128
ROPE = 64
HD = 128
DQ_MLA = 256
LAT_COLS = QL + KVL + 128
VROWS = HD + 16
THETA = 10000.0
POOL_W = (2, 4, 8, 16)
PG = D // 4
FD = 5632
NE = 8
FE = 7168
ALPHA = 4.0 ** 0.25
LN_EPS = 1e-5
RMS_EPS = 1e-6
NEG = -1e30
LOG2E = 1.4426950408889634

F32 = jnp.float32
BF16 = jnp.bfloat16

LANES_V7X = 128
VMEM_BYTES_V7X = 64 * 1024 * 1024
MIB = 1024 * 1024

TM = 512
TK = 512
ATT_TQ = 2 * TK
FFN_TM = 1024
FFN_CHUNK = 512
FFN_TF = 256
POOL_TM = 256
MOE_UNIT = 128
MOE_TUNITS = 10
MOE_TMAX = MOE_TUNITS * MOE_UNIT
MOE_TF = 512
MOE_CAP = 2 * T + NE * MOE_UNIT
MOE_NT = MOE_CAP // MOE_TMAX + NE + 1
CMB_TM = 256


def _params(sem, vmem_mib):
    return pltpu.CompilerParams(dimension_semantics=sem, vmem_limit_bytes=vmem_mib * MIB)


def _full(shape):
    n = len(shape)
    return pl.BlockSpec(shape, lambda *_: (0,) * n)


def _resident(shape):
    n = len(shape)
    return pl.BlockSpec(shape, lambda *_: (0,) * n, pipeline_mode=pl.Buffered(1))


def _layer_norm(x, g, b):
    mu = jnp.mean(x, axis=-1, keepdims=True)
    xc = x - mu
    var = jnp.mean(xc * xc, axis=-1, keepdims=True)
    return xc * lax.rsqrt(var + LN_EPS) * g + b


def _rms_norm(x, g):
    return x * lax.rsqrt(jnp.mean(x * x, axis=-1, keepdims=True) + RMS_EPS) * g


def _dot(a, b):
    return jnp.dot(a, b, preferred_element_type=F32)


def _dot_nt(a, b):
    return lax.dot_general(a, b, (((1,), (1,)), ((), ())), preferred_element_type=F32)


def _vt_store(v_ref, h, v):
    v_ref[h, 0, 0:HD, :] = v.T.astype(BF16)
    rows = lax.broadcasted_iota(jnp.int32, (VROWS - HD, v.shape[0]), 0)
    v_ref[h, 0, HD:VROWS, :] = (rows == 0).astype(BF16)


def _mla_proj_kernel(x_ref, win_ref, wsm_ref, qn_ref, kvn_ref, wuq_ref, wuqs_ref, wukv_ref,
                     c_ref, s_ref, q_ref, k_ref, v_ref, fl_ref, *, qscale, transposed_v):
    x = x_ref[...]
    lat = _dot(x, win_ref[...])
    small = _dot(x, wsm_ref[...])
    cq = lat[:, 0:QL]
    ckv = lat[:, QL:QL + KVL]
    kr = lat[:, QL + KVL:LAT_COLS]
    krs = small[:, 0:LANES_V7X]
    fl_ref[...] = small[:, LANES_V7X:]
    cos = c_ref[...]
    sin = s_ref[...]
    qn = _rms_norm(cq, qn_ref[...]).astype(BF16)
    qpre = _dot(qn, wuq_ref[...])
    qsw = _dot(qn, wuqs_ref[...])
    for h in range(H):
        nope = qpre[:, DQ_MLA * h:DQ_MLA * h + NOPE]
        rot = (qpre[:, DQ_MLA * h + NOPE:DQ_MLA * (h + 1)] * cos
               + qsw[:, HD * h:HD * (h + 1)] * sin)
        q_ref[:, DQ_MLA * h:DQ_MLA * h + NOPE] = (nope * qscale).astype(BF16)
        q_ref[:, DQ_MLA * h + NOPE:DQ_MLA * (h + 1)] = (rot * qscale).astype(BF16)
    kvn = _rms_norm(ckv, kvn_ref[...]).astype(BF16)
    kv = _dot(kvn, wukv_ref[...])
    krot = (kr * cos + krs * sin).astype(BF16)
    for h in range(H):
        k_ref[:, DQ_MLA * h:DQ_MLA * h + NOPE] = kv[:, HD * h:HD * (h + 1)].astype(BF16)
        k_ref[:, DQ_MLA * h + NOPE:DQ_MLA * (h + 1)] = krot
    if transposed_v:
        for h in range(H):
            _vt_store(v_ref, h, kv[:, (H + h) * HD:(H + h + 1) * HD])
    else:
        v_ref[...] = kv[:, H * HD:].astype(BF16)


def _mla_proj(x, w_in, wsm, qn, kvn, wuq, wuqs, wukv, ctab, stab, tm):
    m = x.shape[0]
    qscale = (NOPE + ROPE) ** -0.5 * LOG2E
    row = lambda w: pl.BlockSpec((tm, w), lambda i: (i, 0))
    transposed_v = tm == TK
    if transposed_v:
        v_spec = pl.BlockSpec((H, 1, VROWS, TK), lambda i: (0, i, 0, 0))
        v_shape = jax.ShapeDtypeStruct((H, m // TK, VROWS, TK), BF16)
    else:
        v_spec, v_shape = row(H * HD), jax.ShapeDtypeStruct((m, H * HD), BF16)
    lat_spec = pl.BlockSpec((D, LAT_COLS), lambda i: (0, 0), pipeline_mode=pl.Buffered(1))
    return pl.pallas_call(
        functools.partial(_mla_proj_kernel, qscale=qscale, transposed_v=transposed_v),
        grid=(m // tm,),
        in_specs=[row(D), lat_spec, _full(wsm.shape), _full(qn.shape), _full(kvn.shape),
                  _resident(wuq.shape), _resident(wuqs.shape), _resident(wukv.shape),
                  row(LANES_V7X), row(LANES_V7X)],
        out_specs=[row(H * DQ_MLA), row(H * DQ_MLA), v_spec, row(LANES_V7X)],
        out_shape=[jax.ShapeDtypeStruct((m, H * DQ_MLA), BF16),
                   jax.ShapeDtypeStruct((m, H * DQ_MLA), BF16),
                   v_shape,
                   jax.ShapeDtypeStruct((m, LANES_V7X), F32)],
        compiler_params=_params(("parallel",), 56),
        name="mla_proj",
    )(x, w_in, wsm, qn, kvn, wuq, wuqs, wukv, ctab, stab)


def _mm_kernel(x_ref, w_ref, o_ref, *, scaled_blocks, scale):
    y = _dot(x_ref[...], w_ref[...])
    if scaled_blocks:
        y = jnp.where(pl.program_id(1) < scaled_blocks, y * scale, y)
    o_ref[...] = y.astype(o_ref.dtype)


def _matmul(x, w, tm, tn, out_dtype, scaled_blocks=0, scale=1.0, name="matmul"):
    m, k = x.shape
    n = w.shape[1]
    return pl.pallas_call(
        functools.partial(_mm_kernel, scaled_blocks=scaled_blocks, scale=scale),
        grid=(m // tm, n // tn),
        in_specs=[pl.BlockSpec((tm, k), lambda i, j: (i, 0)),
                  _resident((k, n)) if tn == n else pl.BlockSpec((k, tn), lambda i, j: (0, j))],
        out_specs=pl.BlockSpec((tm, tn), lambda i, j: (i, j)),
        out_shape=jax.ShapeDtypeStruct((m, n), out_dtype),
        compiler_params=_params(("parallel", "parallel"), 48),
        name=name,
    )(x, w)


def _fox_proj_kernel(x_ref, w_ref, qk_ref, vt_ref, *, scale):
    j = pl.program_id(1)
    n = H * HD
    part = lambda c: _dot(x_ref[...].astype(BF16), w_ref[:, c * n:(c + 1) * n])

    @pl.when(j == 0)
    def _():
        qk_ref[...] = (part(0) * scale).astype(BF16)

    @pl.when(j == 1)
    def _():
        qk_ref[...] = part(1).astype(BF16)

    @pl.when(j == 2)
    def _():
        y = part(2)
        for h in range(H):
            _vt_store(vt_ref, h, y[:, h * HD:(h + 1) * HD])


def _fox_proj(x, w, scale):
    n = H * HD
    return pl.pallas_call(
        functools.partial(_fox_proj_kernel, scale=scale),
        grid=(T // TK, 3),
        in_specs=[pl.BlockSpec((TK, D), lambda i, j: (i, 0)),
                  _resident((D, 3 * n))],
        out_specs=[pl.BlockSpec((TK, n), lambda i, j: (i, jnp.minimum(j, 1))),
                   pl.BlockSpec((H, 1, VROWS, TK), lambda i, j: (0, i, 0, 0))],
        out_shape=[jax.ShapeDtypeStruct((T, 2 * n), BF16),
                   jax.ShapeDtypeStruct((H, T // TK, VROWS, TK), BF16)],
        compiler_params=_params(("parallel", "arbitrary"), 48),
        name="fox_proj",
    )(x, w)


def _split3(a):
    hi = a.astype(BF16)
    r = a - hi.astype(F32)
    mid = r.astype(BF16)
    lo = (r - mid.astype(F32)).astype(BF16)
    return hi, mid, lo


def _forget_cumsum_kernel(fl_ref, b_ref, o_ref):
    z = fl_ref[...] + b_ref[...]
    logf = jnp.minimum(z, 0.0) - jnp.log(1.0 + jnp.exp(-jnp.abs(z)))
    n = LANES_V7X
    ri = lax.broadcasted_iota(jnp.int32, (n, n), 0)
    ci = lax.broadcasted_iota(jnp.int32, (n, n), 1)
    upper = (ri <= ci).astype(BF16)
    ones = jnp.ones((n, n), BF16)
    within = jnp.zeros(logf.shape, F32)
    total = jnp.zeros(logf.shape, F32)
    for part in _split3(logf):
        within += _dot(part, upper)
        total += _dot(part, ones)
    r = logf.shape[0]
    lower = (lax.broadcasted_iota(jnp.int32, (r, r), 1)
             < lax.broadcasted_iota(jnp.int32, (r, r), 0)).astype(BF16)
    before = jnp.zeros(logf.shape, F32)
    for part in _split3(total):
        before += _dot(lower, part)
    c2 = (within + before) * LOG2E
    for p, part in enumerate(_split3(c2)):
        o_ref[p] = part
        o_ref[3 + p] = -part


def _forget_cumsum(fl_t, bias_b):
    r = T // LANES_V7X
    return pl.pallas_call(
        _forget_cumsum_kernel,
        grid=(H,),
        in_specs=[pl.BlockSpec((None, r, LANES_V7X), lambda h: (h, 0, 0)),
                  pl.BlockSpec((None, 1, LANES_V7X), lambda h: (h, 0, 0))],
        out_specs=pl.BlockSpec((None, 6, r, LANES_V7X), lambda h: (h, 0, 0, 0)),
        out_shape=jax.ShapeDtypeStruct((H, 6, r, LANES_V7X), BF16),
        compiler_params=_params(("parallel",), 32),
        name="forget_cumsum",
    )(fl_t, bias_b)


ATT_NH = 2
AUG_STRIDE = 16


def _attn_kernel(*refs, fox, dq):
    if fox:
        (q_ref, k_ref, vt_ref, km_ref, vmt_ref, qa_ref, ka_ref, kam_ref, o_ref,
         m_scr, acc_scr, s0_scr, s1_scr, mb0_scr, mb1_scr, qf_scr) = refs
    else:
        (q_ref, k_ref, vt_ref, km_ref, vmt_ref, o_ref,
         m_scr, acc_scr, s0_scr, s1_scr, mb0_scr, mb1_scr) = refs
    hp = pl.program_id(0)
    i = pl.program_id(1)
    m_scr[...] = jnp.full(m_scr.shape, -jnp.inf, F32)
    acc_scr[...] = jnp.zeros(acc_scr.shape, F32)
    if fox:
        lane = lax.broadcasted_iota(jnp.int32, qa_ref.shape, 1)
        for g in range(ATT_NH):
            lo = (hp * ATT_NH + g) * AUG_STRIDE
            own = jnp.logical_and(lane >= lo, lane < lo + AUG_STRIDE)
            qf_scr[g, :, 0:dq] = q_ref[:, g * dq:(g + 1) * dq]
            qf_scr[g, :, dq:2 * dq] = jnp.where(own, qa_ref[...], jnp.zeros_like(qa_ref))

    def query(g):
        return qf_scr[g] if fox else q_ref[:, g * dq:(g + 1) * dq]

    def update(g, st, mb, vt):
        m_prev = m_scr[g]
        m_new = jnp.maximum(m_prev, mb)
        alpha = jnp.exp2(m_prev - m_new)
        p = jnp.exp2(st - m_new)
        acc_scr[g] = alpha * acc_scr[g] + _dot(vt, p.astype(BF16))
        m_scr[g] = m_new

    def stage1(slot, j):
        s_ref, mb_ref = slots[slot]
        off = pl.multiple_of(j * TK, TK)
        for g in range(ATT_NH):
            kb = k_ref[pl.ds(off, TK), g * dq:(g + 1) * dq]
            if fox:
                kb = jnp.concatenate([kb, ka_ref[pl.ds(off, TK), :]], axis=1)
            st = _dot_nt(kb, query(g))
            s_ref[g] = st
            mb_ref[g] = jnp.max(st, axis=0, keepdims=True)

    def stage2(slot, j, diag_key0=None):
        s_ref, mb_ref = slots[slot]
        for g in range(ATT_NH):
            st = s_ref[g]
            mb = mb_ref[g]
            if diag_key0 is not None:
                keys = lax.broadcasted_iota(jnp.int32, st.shape, 0) + diag_key0
                qrys = lax.broadcasted_iota(jnp.int32, st.shape, 1)
                st = jnp.where(keys <= qrys, st, NEG)
                mb = jnp.max(st, axis=0, keepdims=True)
            update(g, st, mb, vt_ref[g, j])

    slots = ((s0_scr, mb0_scr), (s1_scr, mb1_scr))
    stage1(0, 0)

    def body(jj, carry):
        j = 2 * jj
        stage1(1, j + 1)
        stage2(0, j)
        stage1(0, j + 2)
        stage2(1, j + 1)
        return carry

    lax.fori_loop(0, i, body, 0)
    stage1(1, 2 * i + 1)
    stage2(0, 2 * i, diag_key0=0)
    stage2(1, 2 * i + 1, diag_key0=TK)

    for g in range(ATT_NH):
        kb = km_ref[:, g * dq:(g + 1) * dq]
        if fox:
            kb = jnp.concatenate([kb, kam_ref[...]], axis=1)
        st = _dot_nt(kb, query(g))
        update(g, st, jnp.max(st, axis=0, keepdims=True), vmt_ref[g])
        acc = acc_scr[g]
        o = (acc[0:HD] / acc[HD:HD + 1]).T
        o_ref[:, g * HD:(g + 1) * HD] = o.astype(o_ref.dtype)


def _attention(q, k, vt, km, vmt, k_col0, dq, aug=None, name="attn"):
    fox = aug is not None
    nh, tq = ATT_NH, ATT_TQ
    in_specs = [pl.BlockSpec((tq, nh * dq), lambda h, i: (i, h)),
                pl.BlockSpec((T, nh * dq), lambda h, i: (0, k_col0 // nh + h)),
                pl.BlockSpec((nh, T // TK, VROWS, TK), lambda h, i: (h, 0, 0, 0)),
                pl.BlockSpec((NM, nh * dq), lambda h, i: (0, k_col0 // nh + h)),
                pl.BlockSpec((nh, VROWS, NM), lambda h, i: (h, 0, 0))]
    args = [q, k, vt, km, vmt]
    scratch = [pltpu.VMEM((nh, 1, tq), F32), pltpu.VMEM((nh, VROWS, tq), F32),
               pltpu.VMEM((nh, TK, tq), F32), pltpu.VMEM((nh, TK, tq), F32),
               pltpu.VMEM((nh, 1, tq), F32), pltpu.VMEM((nh, 1, tq), F32)]
    if fox:
        in_specs += [pl.BlockSpec((tq, LANES_V7X), lambda h, i: (i, 0)),
                     _full((T, LANES_V7X)), _full((NM, LANES_V7X))]
        args += list(aug)
        scratch.append(pltpu.VMEM((nh, tq, 2 * dq), BF16))
    return pl.pallas_call(
        functools.partial(_attn_kernel, fox=fox, dq=dq),
        grid=(H // nh, T // tq),
        in_specs=in_specs,
        out_specs=pl.BlockSpec((tq, nh * HD), lambda h, i: (i, h)),
        out_shape=jax.ShapeDtypeStruct((T, H * HD), BF16),
        scratch_shapes=scratch,
        compiler_params=_params(("parallel", "arbitrary"), 56),
        name=name,
    )(*args)


def _meta_attn_kernel(q_ref, k_ref, v_ref, o_ref):
    s = _dot_nt(q_ref[...], k_ref[...])
    rows = lax.broadcasted_iota(jnp.int32, s.shape, 0)
    cols = lax.broadcasted_iota(jnp.int32, s.shape, 1)
    s = jnp.where(cols <= rows, s, NEG)
    p = jnp.exp2(s - jnp.max(s, axis=1, keepdims=True))
    o = _dot(p.astype(BF16), v_ref[...]) / jnp.sum(p, axis=1, keepdims=True)
    o_ref[...] = o.astype(o_ref.dtype)


def _meta_attention(q, k, v, q_col0, k_col0, v_col0, dq, name):
    return pl.pallas_call(
        _meta_attn_kernel,
        grid=(H,),
        in_specs=[pl.BlockSpec((NM, dq), lambda h: (0, q_col0 + h)),
                  pl.BlockSpec((NM, dq), lambda h: (0, k_col0 + h)),
                  pl.BlockSpec((NM, HD), lambda h: (0, v_col0 + h))],
        out_specs=pl.BlockSpec((NM, HD), lambda h: (0, h)),
        out_shape=jax.ShapeDtypeStruct((NM, H * HD), BF16),
        compiler_params=_params(("parallel",), 32),
        name=name,
    )(q, k, v)


def _attn_out_kernel(om_ref, of_ref, w_ref, h_ref, g_ref, b_ref, o_ref):
    m = _dot(om_ref[...], w_ref[0:H * HD, :]) + _dot(of_ref[...], w_ref[H * HD:, :])
    o_ref[...] = _layer_norm(ALPHA * h_ref[...] + m, g_ref[...], b_ref[...])


def _attn_out(om, of, w, h, g, b, tm):
    m = h.shape[0]
    row = lambda w_: pl.BlockSpec((tm, w_), lambda i: (i, 0))
    return pl.pallas_call(
        _attn_out_kernel,
        grid=(m // tm,),
        in_specs=[row(H * HD), row(H * HD), _resident(w.shape), row(D), _full(g.shape), _full(b.shape)],
        out_specs=row(D),
        out_shape=jax.ShapeDtypeStruct((m, D), F32),
        compiler_params=_params(("parallel",), 48),
        name="attn_out_ln",
    )(om, of, w, h, g, b)


def _silu(x):
    return x / (1.0 + jnp.exp(-x))


def _ffn_kernel(h_ref, hm_ref, wg_ref, wu_ref, wd_ref, g_ref, b_ref, o_ref, om_ref):
    i = pl.program_id(0)
    f = pl.program_id(1)
    last = pl.num_programs(1) - 1
    rows = h_ref.shape[0]
    chunk = FFN_CHUNK

    def chain(x):
        hid = _silu(_dot(x, wg_ref[...])) * _dot(x, wu_ref[...])
        return _dot(hid, wd_ref[...])

    @pl.when(f == 0)
    def _():
        o_ref[...] = jnp.zeros(o_ref.shape, F32)

    @pl.when(i == 0)
    def _():
        @pl.when(f == 0)
        def _():
            om_ref[...] = jnp.zeros(om_ref.shape, F32)
        part = chain(jnp.concatenate([h_ref[0:chunk, :], hm_ref[...]], axis=0))
        o_ref[0:chunk, :] += part[0:chunk]
        om_ref[...] += part[chunk:]

        @pl.when(f == last)
        def _():
            om_ref[...] = _layer_norm(ALPHA * hm_ref[...] + om_ref[...], g_ref[...], b_ref[...])

    @pl.when(i != 0)
    def _():
        o_ref[0:chunk, :] += chain(h_ref[0:chunk, :])

    for c in range(1, rows // chunk):
        sl = slice(c * chunk, (c + 1) * chunk)
        o_ref[sl, :] += chain(h_ref[sl, :])

    @pl.when(f == last)
    def _():
        o_ref[...] = _layer_norm(ALPHA * h_ref[...] + o_ref[...], g_ref[...], b_ref[...])


def _ffn(h, hm, wg, wu, wd, g, b):
    tm = FFN_TM
    once = pl.Buffered(1)
    return pl.pallas_call(
        _ffn_kernel,
        grid=(T // tm, FD // FFN_TF),
        in_specs=[pl.BlockSpec((tm, D), lambda i, f: (i, 0), pipeline_mode=once),
                  pl.BlockSpec((NM, D), lambda i, f: (0, 0)),
                  pl.BlockSpec((D, FFN_TF), lambda i, f: (0, f)),
                  pl.BlockSpec((D, FFN_TF), lambda i, f: (0, f)),
                  pl.BlockSpec((FFN_TF, D), lambda i, f: (f, 0)),
                  pl.BlockSpec((1, D), lambda i, f: (0, 0)),
                  pl.BlockSpec((1, D), lambda i, f: (0, 0))],
        out_specs=[pl.BlockSpec((tm, D), lambda i, f: (i, 0)),
                   pl.BlockSpec((NM, D), lambda i, f: (0, 0))],
        out_shape=[jax.ShapeDtypeStruct((T, D), F32), jax.ShapeDtypeStruct((NM, D), F32)],
        compiler_params=_params(("arbitrary", "arbitrary"), 58),
        name="ffn_ln",
    )(h, hm, wg, wu, wd, g, b)


def _pool_kernel(hp_ref, prev_ref, meta_ref, h_ref, wg_ref, sc_ref, wo_ref, g_ref, b_ref, o_ref,
                 ext_scr, z_scr):
    tm = hp_ref.shape[0]
    halo = jnp.where(pl.program_id(0) == 0, meta_ref[...], prev_ref[...])
    ext_scr[0:NM, :] = halo
    ext_scr[NM:NM + tm, :] = hp_ref[...]
    for gi, w in enumerate(POOL_W):
        cols = slice(PG * gi, PG * (gi + 1))
        cur = ext_scr[NM:NM + tm, cols]
        tot = cur
        for j in range(1, w):
            tot = tot + ext_scr[NM - j:NM - j + tm, cols]
        diff = tot * (1.0 / w) - cur
        y = _dot(diff, wg_ref[gi]) * sc_ref[:, cols]
        z_scr[:, cols] = y
    m = _dot(z_scr[...], wo_ref[...])
    o_ref[...] = _layer_norm(ALPHA * h_ref[...] + m, g_ref[...], b_ref[...])


def _pool_mixer(hp, hp_meta, h, wgroup, scale, wout, g, b):
    tm = POOL_TM
    row = pl.BlockSpec((tm, D), lambda i: (i, 0))
    prev = pl.BlockSpec((NM, D), lambda i: (jnp.maximum(i * (tm // NM) - 1, 0), 0))
    return pl.pallas_call(
        _pool_kernel,
        grid=(T // tm,),
        in_specs=[row, prev, _full(hp_meta.shape), row, _resident(wgroup.shape), _full(scale.shape),
                  _resident(wout.shape), _full(g.shape), _full(b.shape)],
        out_specs=row,
        out_shape=jax.ShapeDtypeStruct((T, D), F32),
        scratch_shapes=[pltpu.VMEM((NM + tm, D), F32), pltpu.VMEM((tm, D), F32)],
        compiler_params=_params(("parallel",), 48),
        name="pool_ln",
    )(hp, hp, hp_meta, h, wgroup, scale, wout, g, b)


def _router_kernel(h_ref, w_ref, b_ref, info_ref, cnt_ref, carry_scr):
    tm = h_ref.shape[0]

    @pl.when(pl.program_id(0) == 0)
    def _():
        carry_scr[...] = jnp.zeros(carry_scr.shape, F32)

    x = h_ref[...]
    w = w_ref[...]
    xh, wh = x.astype(BF16), w.astype(BF16)
    xl = (x - xh.astype(F32)).astype(BF16)
    wl = (w - wh.astype(F32)).astype(BF16)
    logits = _dot(xh, wh) + _dot(xh, wl) + _dot(xl, wh) + b_ref[...]
    lane = lax.broadcasted_iota(jnp.int32, logits.shape, 1).astype(F32)
    nolane = float(LANES_V7X)
    m1 = jnp.max(logits, axis=1, keepdims=True)
    i1 = jnp.min(jnp.where(logits == m1, lane, nolane), axis=1, keepdims=True)
    rest = jnp.where(lane == i1, -jnp.inf, logits)
    m2 = jnp.max(rest, axis=1, keepdims=True)
    i2 = jnp.min(jnp.where(rest == m2, lane, nolane), axis=1, keepdims=True)
    e = jnp.exp(m2 - m1)
    g1 = 1.0 / (1.0 + e)
    g2 = e / (1.0 + e)
    hit1 = lane == i1
    hit2 = lane == i2
    member = jnp.logical_or(hit1, hit2).astype(BF16)
    earlier = (lax.broadcasted_iota(jnp.int32, (tm, tm), 1)
               < lax.broadcasted_iota(jnp.int32, (tm, tm), 0)).astype(BF16)
    before = _dot(earlier, member) + carry_scr[0:1, :]
    r1 = jnp.sum(jnp.where(hit1, before, 0.0), axis=1, keepdims=True)
    r2 = jnp.sum(jnp.where(hit2, before, 0.0), axis=1, keepdims=True)
    carry = carry_scr[0:1, :] + jnp.sum(member.astype(F32), axis=0, keepdims=True)
    carry_scr[...] = jnp.broadcast_to(carry, carry_scr.shape)
    cnt_ref[...] = jnp.broadcast_to(carry, cnt_ref.shape)
    info = jnp.where(lane == 0, g1, 0.0)
    info = jnp.where(lane == 1, g2, info)
    info = jnp.where(lane == 2, i1, info)
    info = jnp.where(lane == 3, i2, info)
    info = jnp.where(lane == 4, r1, info)
    info = jnp.where(lane == 5, r2, info)
    info_ref[...] = info


def _router(h, w_pad, b_pad):
    tm = TM
    return pl.pallas_call(
        _router_kernel,
        grid=(T // tm,),
        in_specs=[pl.BlockSpec((tm, D), lambda i: (i, 0)), _full(w_pad.shape), _full(b_pad.shape)],
        out_specs=[pl.BlockSpec((tm, LANES_V7X), lambda i: (i, 0)), _full((8, LANES_V7X))],
        out_shape=[jax.ShapeDtypeStruct((T, LANES_V7X), F32),
                   jax.ShapeDtypeStruct((8, LANES_V7X), F32)],
        scratch_shapes=[pltpu.VMEM((8, LANES_V7X), F32)],
        compiler_params=_params(("arbitrary",), 32),
        name="router",
    )(h, w_pad, b_pad)


def _moe_row_copy(h_hbm, x_scr, sem, tok, r):
    return pltpu.make_async_copy(h_hbm.at[pl.ds(tok, 1), :], x_scr.at[pl.ds(r, 1), :], sem)


def _moe_out_copy(acc_scr, y_hbm, sem, row0, u):
    off = pl.multiple_of(u * MOE_UNIT, MOE_UNIT)
    dst = pl.multiple_of(row0 + off, MOE_UNIT)
    return pltpu.make_async_copy(acc_scr.at[pl.ds(off, MOE_UNIT), :],
                                 y_hbm.at[pl.ds(dst, MOE_UNIT), :], sem)


def _moe_kernel(te_ref, row0_ref, nunit_ref, used_ref, pad_ref, dest_ref,
                h_hbm, wg_ref, wu_ref, wd_ref, y_hbm,
                slot_tok, x_scr, acc_scr, gsem, osem):
    t = pl.program_id(0)
    f = pl.program_id(1)
    nf = pl.num_programs(1)

    @pl.when(jnp.logical_and(t == 0, f == 0))
    def _():
        for e in range(NE):
            def clear(s, c):
                slot_tok[s] = 0
                return c
            lax.fori_loop(pad_ref[2 * e], pad_ref[2 * e + 1], clear, 0)

        def place(tok, c):
            slot_tok[dest_ref[2 * tok]] = tok
            slot_tok[dest_ref[2 * tok + 1]] = tok
            return c
        lax.fori_loop(0, T, place, 0, unroll=8)

    @pl.when(t < used_ref[0])
    def _():
        row0 = row0_ref[t]
        nunit = nunit_ref[t]
        nrows = nunit * MOE_UNIT

        @pl.when(f == 0)
        def _():
            def issue_group(r8, c):
                base = pl.multiple_of(r8 * 8, 8)
                for k in range(8):
                    _moe_row_copy(h_hbm, x_scr, gsem, slot_tok[row0 + base + k], base + k).start()
                return c
            lax.fori_loop(0, nunit * (MOE_UNIT // 8), issue_group, 0)

            @pl.when(t > 0)
            def _():
                def finish(u, c):
                    _moe_out_copy(acc_scr, y_hbm, osem, row0_ref[t - 1], u).wait()
                    return c
                lax.fori_loop(0, nunit_ref[t - 1], finish, 0)

            def clear_acc(u, c):
                off = pl.multiple_of(u * MOE_UNIT, MOE_UNIT)
                acc_scr[pl.ds(off, MOE_UNIT), :] = jnp.zeros((MOE_UNIT, D), F32)
                return c
            lax.fori_loop(0, nunit, clear_acc, 0)

            def drain_group(r8, c):
                base = pl.multiple_of(r8 * 8, 8)
                for k in range(8):
                    _moe_row_copy(h_hbm, x_scr, gsem, 0, base + k).wait()
                return c
            lax.fori_loop(0, nunit * (MOE_UNIT // 8), drain_group, 0)

        def chunk(off, rows):
            x = x_scr[pl.ds(off, rows), :]
            hid = _silu(_dot(x, wg_ref[...])) * _dot(x, wu_ref[...])
            acc_scr[pl.ds(off, rows), :] += _dot(hid, wd_ref[...])

        def quad(c, carry):
            chunk(pl.multiple_of(c * 4 * MOE_UNIT, 4 * MOE_UNIT), 4 * MOE_UNIT)
            return carry
        lax.fori_loop(0, nunit // 4, quad, 0)

        @pl.when(nunit % 4 >= 2)
        def _():
            chunk(pl.multiple_of((nunit // 4) * 4 * MOE_UNIT, 2 * MOE_UNIT), 2 * MOE_UNIT)

        @pl.when(nunit % 2 == 1)
        def _():
            chunk(pl.multiple_of((nunit - 1) * MOE_UNIT, MOE_UNIT), MOE_UNIT)

        @pl.when(f == nf - 1)
        def _():
            def start(u, carry):
                _moe_out_copy(acc_scr, y_hbm, osem, row0, u).start()
                return carry
            lax.fori_loop(0, nunit, start, 0)

            @pl.when(t == used_ref[0] - 1)
            def _():
                def finish(u, carry):
                    _moe_out_copy(acc_scr, y_hbm, osem, row0, u).wait()
                    return carry
                lax.fori_loop(0, nunit, finish, 0)

    @pl.when(jnp.logical_and(t == pl.num_programs(0) - 1, f == nf - 1))
    def _():
        acc_scr[0:MOE_UNIT, :] = jnp.zeros((MOE_UNIT, D), F32)

        def fill(u, carry):
            cp = _moe_out_copy(acc_scr, y_hbm, osem, u * MOE_UNIT, 0)
            cp.start()
            cp.wait()
            return carry
        lax.fori_loop(used_ref[1], MOE_CAP // MOE_UNIT, fill, 0)


def _moe_experts(h, wg, wu, wd, tile_e, tile_row0, tile_nunit, used, pad, dest):
    nf = FE // MOE_TF

    def fidx(t, f, us):
        return jnp.where(t < us[0], f, nf - 1)

    grid_spec = pltpu.PrefetchScalarGridSpec(
        num_scalar_prefetch=6,
        grid=(MOE_NT, nf),
        in_specs=[pl.BlockSpec(memory_space=pl.ANY),
                  pl.BlockSpec((None, D, MOE_TF), lambda t, f, te, r0, nu, us, pa, de: (te[t], 0, fidx(t, f, us))),
                  pl.BlockSpec((None, D, MOE_TF), lambda t, f, te, r0, nu, us, pa, de: (te[t], 0, fidx(t, f, us))),
                  pl.BlockSpec((None, MOE_TF, D), lambda t, f, te, r0, nu, us, pa, de: (te[t], fidx(t, f, us), 0))],
        out_specs=pl.BlockSpec(memory_space=pl.ANY),
        scratch_shapes=[pltpu.SMEM((MOE_CAP,), jnp.int32),
                        pltpu.VMEM((MOE_TMAX, D), F32),
                        pltpu.VMEM((MOE_TMAX, D), F32),
                        pltpu.SemaphoreType.DMA,
                        pltpu.SemaphoreType.DMA],
    )
    return pl.pallas_call(
        _moe_kernel,
        grid_spec=grid_spec,
        out_shape=jax.ShapeDtypeStruct((MOE_CAP, D), F32),
        compiler_params=_params(("arbitrary", "arbitrary"), 58),
        name="moe_experts",
    )(tile_e, tile_row0, tile_nunit, used, pad, dest, h, wg, wu, wd)


def _cmb_copy(y_hbm, ybuf, sem, buf, slot, k, r):
    return pltpu.make_async_copy(y_hbm.at[pl.ds(slot, 1), :], ybuf.at[buf, k, pl.ds(r, 1), :],
                                 sem.at[buf])


def _combine_kernel(dest_ref, h_ref, info_ref, y_hbm, g_ref, b_ref, o_ref, ybuf, sem):
    tm = h_ref.shape[0]
    i = pl.program_id(0)
    cur = lax.rem(i, 2)

    def gather(tile, buf):
        def issue_group(r8, c):
            base = pl.multiple_of(r8 * 8, 8)
            for k in range(8):
                tok = tile * tm + base + k
                _cmb_copy(y_hbm, ybuf, sem, buf, dest_ref[2 * tok], 0, base + k).start()
                _cmb_copy(y_hbm, ybuf, sem, buf, dest_ref[2 * tok + 1], 1, base + k).start()
            return c
        lax.fori_loop(0, tm // 8, issue_group, 0)

    @pl.when(i == 0)
    def _():
        gather(0, 0)

    @pl.when(i + 1 < pl.num_programs(0))
    def _():
        gather(i + 1, 1 - cur)

    def drain_group(r8, c):
        base = pl.multiple_of(r8 * 8, 8)
        for k in range(8):
            _cmb_copy(y_hbm, ybuf, sem, cur, 0, 0, base + k).wait()
            _cmb_copy(y_hbm, ybuf, sem, cur, 0, 1, base + k).wait()
        return c
    lax.fori_loop(0, tm // 8, drain_group, 0)
    info = info_ref[...]
    moe = info[:, 0:1] * ybuf[cur, 0] + info[:, 1:2] * ybuf[cur, 1]
    o_ref[...] = _layer_norm(ALPHA * h_ref[...] + moe, g_ref[...], b_ref[...])


def _combine(dest, h, info, y_rows, g, b):
    tm = CMB_TM
    grid_spec = pltpu.PrefetchScalarGridSpec(
        num_scalar_prefetch=1,
        grid=(T // tm,),
        in_specs=[pl.BlockSpec((tm, D), lambda i, de: (i, 0)),
                  pl.BlockSpec((tm, LANES_V7X), lambda i, de: (i, 0)),
                  pl.BlockSpec(memory_space=pl.ANY),
                  pl.BlockSpec((1, D), lambda i, de: (0, 0)),
                  pl.BlockSpec((1, D), lambda i, de: (0, 0))],
        out_specs=pl.BlockSpec((tm, D), lambda i, de: (i, 0)),
        scratch_shapes=[pltpu.VMEM((2, 2, tm, D), F32), pltpu.SemaphoreType.DMA((2,))],
    )
    return pl.pallas_call(
        _combine_kernel,
        grid_spec=grid_spec,
        out_shape=jax.ShapeDtypeStruct((T, D), F32),
        compiler_params=_params(("arbitrary",), 32),
        name="moe_combine_ln",
    )(dest, h, info, y_rows, g, b)


def _rope_tables(pos):
    inv_freq = THETA ** (-jnp.arange(0, ROPE, 2, dtype=F32) / ROPE)
    ang = pos[:, None] * inv_freq[None, :]
    cos = jnp.cos(ang).astype(F32)
    sin = jnp.sin(ang).astype(F32)
    zero = jnp.zeros((pos.shape[0], LANES_V7X - ROPE), F32)
    return (jnp.concatenate([cos, cos, zero], axis=1),
            jnp.concatenate([-sin, sin, zero], axis=1))


def _expert_tiles(counts):
    units = (counts + MOE_UNIT - 1) // MOE_UNIT
    seg_unit = jnp.cumsum(units) - units
    ntile = (units + MOE_TUNITS - 1) // MOE_TUNITS
    tile_end = jnp.cumsum(ntile)
    n_used = tile_end[-1]
    tid = jnp.arange(MOE_NT, dtype=jnp.int32)
    e = jnp.minimum(jnp.sum(tid[:, None] >= tile_end[None, :], axis=1), NE - 1).astype(jnp.int32)
    local = tid - (tile_end - ntile)[e]
    nt = jnp.maximum(ntile[e], 1)
    lo = (local * units[e]) // nt
    hi = ((local + 1) * units[e]) // nt
    used = tid < n_used
    last = jnp.maximum(n_used - 1, 0)
    e = jnp.where(used, e, e[last])
    row0 = jnp.where(used, (seg_unit[e] + lo) * MOE_UNIT, 0)
    nunit = jnp.where(used, hi - lo, 0)
    used_info = jnp.stack([n_used, jnp.sum(units)]).astype(jnp.int32)
    seg_row = seg_unit * MOE_UNIT
    pad = jnp.stack([seg_row + counts, seg_row + units * MOE_UNIT], axis=1).reshape(2 * NE)
    return (e.astype(jnp.int32), row0.astype(jnp.int32), nunit.astype(jnp.int32),
            used_info, pad.astype(jnp.int32), seg_row.astype(jnp.int32))


def kernel(x, meta_tokens, attn_w_in, fox_forget_bias, mla_q_norm, mla_kv_norm, mla_w_uq, mla_w_ukv, attn_w_out, pool_w_in, pool_w_group, pool_scale, pool_w_out, ffn_w_gate, ffn_w_up, ffn_w_down, moe_w_router, moe_b_router, moe_w_gate, moe_w_up, moe_w_down, ln_mix_g, ln_mix_b, ln_ffn_g, ln_ffn_b):
    assert x.shape == (1, T, D) and meta_tokens.shape == (NM, D)
    xt = x[0]
    meta = meta_tokens.astype(x.dtype)

    w_in = attn_w_in[0]
    o_kr, o_qf, o_f = QL + KVL, QL + KVL + ROPE, QL + KVL + ROPE + 3 * H * HD
    kr1, kr2 = w_in[:, o_kr:o_kr + ROPE // 2], w_in[:, o_kr + ROPE // 2:o_qf]
    zpad = lambda n: jnp.zeros((D, n), F32)
    w_small = jnp.concatenate([kr2, kr1, zpad(LANES_V7X - ROPE),
                               w_in[:, o_f:o_f + H], zpad(LANES_V7X - H)], axis=1)
    w_fox = w_in[:, o_qf:o_f].astype(BF16)
    w_in = w_in[:, :LAT_COLS]
    wuq = mla_w_uq[0].reshape(QL, H, NOPE + ROPE)
    uq_n, uq_1, uq_2 = wuq[..., :NOPE], wuq[..., NOPE:NOPE + ROPE // 2], wuq[..., NOPE + ROPE // 2:]
    zq = jnp.zeros((QL, H, LANES_V7X - ROPE), F32)
    w_uq = jnp.concatenate([uq_n, uq_1, uq_2, zq], axis=-1).reshape(QL, H * DQ_MLA).astype(BF16)
    w_uqs = jnp.concatenate([uq_2, uq_1, zq], axis=-1).reshape(QL, H * HD).astype(BF16)
    wukv = mla_w_ukv[0].reshape(KVL, H, NOPE + HD)
    w_ukv = jnp.concatenate([wukv[..., :NOPE].reshape(KVL, H * NOPE),
                             wukv[..., NOPE:].reshape(KVL, H * HD)], axis=1).astype(BF16)
    w_ao = attn_w_out[0]
    w_fg, w_fu, w_fd = ffn_w_gate[0], ffn_w_up[0], ffn_w_down[0]
    w_pi, w_pg, w_po = pool_w_in[0], pool_w_group[0], pool_w_out[0]
    qn, kvn = mla_q_norm[0][None, :], mla_kv_norm[0][None, :]
    row = lambda v: v[None, :]

    ctab_t, stab_t = _rope_tables(jnp.arange(NM, NM + T, dtype=F32))
    ctab_m, stab_m = _rope_tables(jnp.arange(NM, dtype=F32))

    fox_scale = HD ** -0.5 * LOG2E
    q_t, k_t, vt_t, fl_t = _mla_proj(xt, w_in, w_small, qn, kvn, w_uq, w_uqs, w_ukv, ctab_t, stab_t, TK)
    q_m, k_m, v_m, _ = _mla_proj(meta, w_in, w_small, qn, kvn, w_uq, w_uqs, w_ukv, ctab_m, stab_m, NM)
    fox_qk, fox_vt = _fox_proj(xt, w_fox, fox_scale)
    fox_m = _matmul(meta, w_fox, NM, H * HD, BF16, scaled_blocks=1, scale=fox_scale, name="fox_proj_meta")

    fl = fl_t[:, :H].T.reshape(H, T // LANES_V7X, LANES_V7X)
    bias_b = jnp.broadcast_to(fox_forget_bias[0][:, None, None], (H, 1, LANES_V7X))
    parts = _forget_cumsum(fl, bias_b).reshape(H, 6, T)
    pos = parts[:, 0:3].transpose(2, 0, 1)
    neg = parts[:, 3:6].transpose(2, 0, 1)
    lanes = lambda pieces: jnp.concatenate(pieces, axis=2).reshape(-1, H * AUG_STRIDE)
    cst = lambda rows, n, v: jnp.full((rows, H, n), v, BF16)
    aug = (lanes([cst(T, 3, 1.0), pos, cst(T, AUG_STRIDE - 6, 0.0)]),
           lanes([neg, cst(T, AUG_STRIDE - 3, 0.0)]),
           lanes([cst(NM, 3, 0.0), cst(NM, 3, -1.0), cst(NM, AUG_STRIDE - 6, 0.0)]))

    ones_rows = jnp.concatenate([jnp.ones((H, 1, NM), BF16), jnp.zeros((H, VROWS - HD - 1, NM), BF16)], axis=1)
    to_vmt = lambda v: jnp.concatenate([v.reshape(NM, H, HD).transpose(1, 2, 0), ones_rows], axis=1)
    o_mla = _attention(q_t, k_t, vt_t, k_m, to_vmt(v_m), 0, DQ_MLA, name="mla_attn")
    o_fox = _attention(fox_qk, fox_qk, fox_vt, fox_m, to_vmt(fox_m[:, 2 * H * HD:]),
                       H, HD, aug, name="fox_attn")
    om_mla = _meta_attention(q_m, k_m, v_m, 0, 0, 0, DQ_MLA, "mla_attn_meta")
    om_fox = _meta_attention(fox_m, fox_m, fox_m, 0, H, 2 * H, HD, "fox_attn_meta")

    g0, b0, g1, b1 = row(ln_mix_g[0]), row(ln_mix_b[0]), row(ln_ffn_g[0]), row(ln_ffn_b[0])
    h_t = _attn_out(o_mla, o_fox, w_ao, xt, g0, b0, TM)
    h_m = _attn_out(om_mla, om_fox, w_ao, meta, g0, b0, NM)
    h_t, h_m = _ffn(h_t, h_m, w_fg, w_fu, w_fd, g1, b1)

    hp_t = _matmul(h_t, w_pi, TM, D, F32, name="pool_in")
    hp_m = _matmul(h_m, w_pi, NM, D, F32, name="pool_in_meta")
    h_t = _pool_mixer(hp_t, hp_m, h_t, w_pg, row(pool_scale[0]), w_po,
                      row(ln_mix_g[1]), row(ln_mix_b[1]))

    w_r = jnp.concatenate([moe_w_router[0], jnp.zeros((D, LANES_V7X - NE), F32)], axis=1)
    b_r = jnp.concatenate([moe_b_router[0], jnp.full((LANES_V7X - NE,), NEG, F32)])[None, :]
    info, cnt = _router(h_t, w_r, b_r)
    counts = cnt[0, :NE].astype(jnp.int32)
    tile_e, tile_row0, tile_nunit, used, pad, seg_row = _expert_tiles(counts)
    experts = info[:, 2:4].astype(jnp.int32)
    ranks = info[:, 4:6].astype(jnp.int32)
    dest = (seg_row[experts] + ranks).reshape(2 * T)
    y_rows = _moe_experts(h_t, moe_w_gate[0], moe_w_up[0], moe_w_down[0],
                          tile_e, tile_row0, tile_nunit, used, pad, dest)
    out = _combine(dest, h_t, info, y_rows, row(ln_ffn_g[1]), row(ln_ffn_b[1]))
    return out[None]
```

```python
import functools
import math

import jax
import jax.numpy as jnp
import numpy as np
from jax import lax
from jax.experimental import pallas as pl
from jax.experimental.pallas import tpu as pltpu

D = 2048
T = 8192
NM = 16
H = 8
QL = 512
KVL = 512
NOPE = 128
ROPE = 64
HD = 128
DQ_MLA = 256
LAT_COLS = QL + KVL + 128
VROWS = HD + 16
THETA = 10000.0
POOL_W = (2, 4, 8, 16)
PG = D // 4
FD = 5632
NE = 8
FE = 7168
ALPHA = 4.0 ** 0.25
LN_EPS = 1e-5
RMS_EPS = 1e-6
NEG = -1e30
LOG2E = 1.4426950408889634

F32 = jnp.float32
BF16 = jnp.bfloat16

LANES_V7X = 128
VMEM_BYTES_V7X = 64 * 1024 * 1024
MIB = 1024 * 1024

TM = 512
TK = 512
ATT_TQ = 2 * TK
FFN_TM = 1024
FFN_CHUNK = 512
FFN_TF = 512
POOL_TM = 256
MOE_UNIT = 128
MOE_TUNITS = 18
MOE_TMAX = MOE_TUNITS * MOE_UNIT
MOE_TF = 512
MOE_CAP = 2 * T + NE * MOE_UNIT
MOE_NT = MOE_CAP // MOE_TMAX + NE + 1
CMB_TM = 256


def _params(sem, vmem_mib):
    return pltpu.CompilerParams(dimension_semantics=sem, vmem_limit_bytes=vmem_mib * MIB)


def _full(shape):
    n = len(shape)
    return pl.BlockSpec(shape, lambda *_: (0,) * n)


def _resident(shape):
    n = len(shape)
    return pl.BlockSpec(shape, lambda *_: (0,) * n, pipeline_mode=pl.Buffered(1))


def _layer_norm(x, g, b):
    mu = jnp.mean(x, axis=-1, keepdims=True)
    xc = x - mu
    var = jnp.mean(xc * xc, axis=-1, keepdims=True)
    return xc * lax.rsqrt(var + LN_EPS) * g + b


def _rms_norm(x, g):
    return x * lax.rsqrt(jnp.mean(x * x, axis=-1, keepdims=True) + RMS_EPS) * g


def _dot(a, b):
    return jnp.dot(a, b, preferred_element_type=F32)


def _dot_mixed(a, b):
    return lax.dot_general(a, b, (((1,), (0,)), ((), ())), preferred_element_type=F32)


def _dot_nt(a, b):
    return lax.dot_general(a, b, (((1,), (1,)), ((), ())), preferred_element_type=F32)


def _vt_store(v_ref, h, v):
    v_ref[h, 0, 0:HD, :] = v.T.astype(BF16)
    rows = lax.broadcasted_iota(jnp.int32, (VROWS - HD, v.shape[0]), 0)
    v_ref[h, 0, HD:VROWS, :] = (rows == 0).astype(BF16)


def _mla_proj_kernel(x_ref, win_ref, wsm_ref, qn_ref, kvn_ref, wuq_ref, wuqs_ref, wukv_ref,
                     c_ref, s_ref, q_ref, k_ref, v_ref, fl_ref, *, qscale, transposed_v):
    x = x_ref[...]
    lat = _dot(x, win_ref[...])
    small = _dot(x, wsm_ref[...])
    cq = lat[:, 0:QL]
    ckv = lat[:, QL:QL + KVL]
    kr = lat[:, QL + KVL:LAT_COLS]
    krs = small[:, 0:LANES_V7X]
    fl_ref[...] = small[:, LANES_V7X:]
    cos = c_ref[...]
    sin = s_ref[...]
    qn = _rms_norm(cq, qn_ref[...]).astype(BF16)
    qpre = _dot(qn, wuq_ref[...])
    qsw = _dot(qn, wuqs_ref[...])
    for h in range(H):
        nope = qpre[:, DQ_MLA * h:DQ_MLA * h + NOPE]
        rot = (qpre[:, DQ_MLA * h + NOPE:DQ_MLA * (h + 1)] * cos
               + qsw[:, HD * h:HD * (h + 1)] * sin)
        q_ref[:, DQ_MLA * h:DQ_MLA * h + NOPE] = (nope * qscale).astype(BF16)
        q_ref[:, DQ_MLA * h + NOPE:DQ_MLA * (h + 1)] = (rot * qscale).astype(BF16)
    kvn = _rms_norm(ckv, kvn_ref[...]).astype(BF16)
    kv = _dot(kvn, wukv_ref[...])
    krot = (kr * cos + krs * sin).astype(BF16)
    for h in range(H):
        k_ref[:, DQ_MLA * h:DQ_MLA * h + NOPE] = kv[:, HD * h:HD * (h + 1)].astype(BF16)
        k_ref[:, DQ_MLA * h + NOPE:DQ_MLA * (h + 1)] = krot
    if transposed_v:
        for h in range(H):
            _vt_store(v_ref, h, kv[:, (H + h) * HD:(H + h + 1) * HD])
    else:
        v_ref[...] = kv[:, H * HD:].astype(BF16)


def _mla_proj(x, w_in, wsm, qn, kvn, wuq, wuqs, wukv, ctab, stab, tm):
    m = x.shape[0]
    qscale = (NOPE + ROPE) ** -0.5 * LOG2E
    row = lambda w: pl.BlockSpec((tm, w), lambda i: (i, 0))
    transposed_v = tm == TK
    if transposed_v:
        v_spec = pl.BlockSpec((H, 1, VROWS, TK), lambda i: (0, i, 0, 0))
        v_shape = jax.ShapeDtypeStruct((H, m // TK, VROWS, TK), BF16)
    else:
        v_spec, v_shape = row(H * HD), jax.ShapeDtypeStruct((m, H * HD), BF16)
    lat_spec = pl.BlockSpec((D, LAT_COLS), lambda i: (0, 0), pipeline_mode=pl.Buffered(1))
    return pl.pallas_call(
        functools.partial(_mla_proj_kernel, qscale=qscale, transposed_v=transposed_v),
        grid=(m // tm,),
        in_specs=[row(D), lat_spec, _full(wsm.shape), _full(qn.shape), _full(kvn.shape),
                  _resident(wuq.shape), _resident(wuqs.shape), _resident(wukv.shape),
                  row(LANES_V7X), row(LANES_V7X)],
        out_specs=[row(H * DQ_MLA), row(H * DQ_MLA), v_spec, row(LANES_V7X)],
        out_shape=[jax.ShapeDtypeStruct((m, H * DQ_MLA), BF16),
                   jax.ShapeDtypeStruct((m, H * DQ_MLA), BF16),
                   v_shape,
                   jax.ShapeDtypeStruct((m, LANES_V7X), F32)],
        compiler_params=_params(("parallel",), 56),
        name="mla_proj",
    )(x, w_in, wsm, qn, kvn, wuq, wuqs, wukv, ctab, stab)


def _mm_kernel(x_ref, w_ref, o_ref, *, scaled_blocks, scale):
    y = _dot(x_ref[...], w_ref[...])
    if scaled_blocks:
        y = jnp.where(pl.program_id(1) < scaled_blocks, y * scale, y)
    o_ref[...] = y.astype(o_ref.dtype)


def _matmul(x, w, tm, tn, out_dtype, scaled_blocks=0, scale=1.0, name="matmul"):
    m, k = x.shape
    n = w.shape[1]
    return pl.pallas_call(
        functools.partial(_mm_kernel, scaled_blocks=scaled_blocks, scale=scale),
        grid=(m // tm, n // tn),
        in_specs=[pl.BlockSpec((tm, k), lambda i, j: (i, 0)),
                  _resident((k, n)) if tn == n else pl.BlockSpec((k, tn), lambda i, j: (0, j))],
        out_specs=pl.BlockSpec((tm, tn), lambda i, j: (i, j)),
        out_shape=jax.ShapeDtypeStruct((m, n), out_dtype),
        compiler_params=_params(("parallel", "parallel"), 48),
        name=name,
    )(x, w)


def _fox_proj_kernel(x_ref, w_ref, qk_ref, vt_ref, *, scale):
    j = pl.program_id(1)
    n = H * HD
    part = lambda c: _dot(x_ref[...].astype(BF16), w_ref[:, c * n:(c + 1) * n])

    @pl.when(j == 0)
    def _():
        qk_ref[...] = (part(0) * scale).astype(BF16)

    @pl.when(j == 1)
    def _():
        qk_ref[...] = part(1).astype(BF16)

    @pl.when(j == 2)
    def _():
        y = part(2)
        for h in range(H):
            _vt_store(vt_ref, h, y[:, h * HD:(h + 1) * HD])


def _fox_proj(x, w, scale):
    n = H * HD
    return pl.pallas_call(
        functools.partial(_fox_proj_kernel, scale=scale),
        grid=(T // TK, 3),
        in_specs=[pl.BlockSpec((TK, D), lambda i, j: (i, 0)),
                  _resident((D, 3 * n))],
        out_specs=[pl.BlockSpec((TK, n), lambda i, j: (i, jnp.minimum(j, 1))),
                   pl.BlockSpec((H, 1, VROWS, TK), lambda i, j: (0, i, 0, 0))],
        out_shape=[jax.ShapeDtypeStruct((T, 2 * n), BF16),
                   jax.ShapeDtypeStruct((H, T // TK, VROWS, TK), BF16)],
        compiler_params=_params(("parallel", "arbitrary"), 48),
        name="fox_proj",
    )(x, w)


def _split3(a):
    hi = a.astype(BF16)
    r = a - hi.astype(F32)
    mid = r.astype(BF16)
    lo = (r - mid.astype(F32)).astype(BF16)
    return hi, mid, lo


def _forget_cumsum_kernel(fl_ref, b_ref, o_ref):
    z = fl_ref[...] + b_ref[...]
    logf = jnp.minimum(z, 0.0) - jnp.log(1.0 + jnp.exp(-jnp.abs(z)))
    n = LANES_V7X
    ri = lax.broadcasted_iota(jnp.int32, (n, n), 0)
    ci = lax.broadcasted_iota(jnp.int32, (n, n), 1)
    upper = (ri <= ci).astype(BF16)
    ones = jnp.ones((n, n), BF16)
    within = jnp.zeros(logf.shape, F32)
    total = jnp.zeros(logf.shape, F32)
    for part in _split3(logf):
        within += _dot(part, upper)
        total += _dot(part, ones)
    r = logf.shape[0]
    lower = (lax.broadcasted_iota(jnp.int32, (r, r), 1)
             < lax.broadcasted_iota(jnp.int32, (r, r), 0)).astype(BF16)
    before = jnp.zeros(logf.shape, F32)
    for part in _split3(total):
        before += _dot(lower, part)
    c2 = (within + before) * LOG2E
    for p, part in enumerate(_split3(c2)):
        o_ref[p] = part
        o_ref[3 + p] = -part


def _forget_cumsum(fl_t, bias_b):
    r = T // LANES_V7X
    return pl.pallas_call(
        _forget_cumsum_kernel,
        grid=(H,),
        in_specs=[pl.BlockSpec((None, r, LANES_V7X), lambda h: (h, 0, 0)),
                  pl.BlockSpec((None, 1, LANES_V7X), lambda h: (h, 0, 0))],
        out_specs=pl.BlockSpec((None, 6, r, LANES_V7X), lambda h: (h, 0, 0, 0)),
        out_shape=jax.ShapeDtypeStruct((H, 6, r, LANES_V7X), BF16),
        compiler_params=_params(("parallel",), 32),
        name="forget_cumsum",
    )(fl_t, bias_b)


ATT_NH = 2
AUG_STRIDE = 16


def _attn_kernel(*refs, fox, dq):
    if fox:
        (q_ref, k_ref, vt_ref, km_ref, vmt_ref, qa_ref, ka_ref, kam_ref, o_ref,
         m_scr, acc_scr, s0_scr, s1_scr, mb0_scr, mb1_scr, qf_scr) = refs
    else:
        (q_ref, k_ref, vt_ref, km_ref, vmt_ref, o_ref,
         m_scr, acc_scr, s0_scr, s1_scr, mb0_scr, mb1_scr) = refs
    hp = pl.program_id(0)
    i = pl.program_id(1)
    m_scr[...] = jnp.full(m_scr.shape, -jnp.inf, F32)
    acc_scr[...] = jnp.zeros(acc_scr.shape, F32)
    if fox:
        lane = lax.broadcasted_iota(jnp.int32, qa_ref.shape, 1)
        for g in range(ATT_NH):
            lo = (hp * ATT_NH + g) * AUG_STRIDE
            own = jnp.logical_and(lane >= lo, lane < lo + AUG_STRIDE)
            qf_scr[g, :, 0:dq] = q_ref[:, g * dq:(g + 1) * dq]
            qf_scr[g, :, dq:2 * dq] = jnp.where(own, qa_ref[...], jnp.zeros_like(qa_ref))

    def query(g):
        return qf_scr[g] if fox else q_ref[:, g * dq:(g + 1) * dq]

    def update(g, st, mb, vt):
        m_prev = m_scr[g]
        m_new = jnp.maximum(m_prev, mb)
        alpha = jnp.exp2(m_prev - m_new)
        p = jnp.exp2(st - m_new)
        acc_scr[g] = alpha * acc_scr[g] + _dot(vt, p.astype(BF16))
        m_scr[g] = m_new

    def stage1(slot, j):
        s_ref, mb_ref = slots[slot]
        off = pl.multiple_of(j * TK, TK)
        for g in range(ATT_NH):
            kb = k_ref[pl.ds(off, TK), g * dq:(g + 1) * dq]
            if fox:
                kb = jnp.concatenate([kb, ka_ref[pl.ds(off, TK), :]], axis=1)
            st = _dot_nt(kb, query(g))
            s_ref[g] = st
            mb_ref[g] = jnp.max(st, axis=0, keepdims=True)

    def stage2(slot, j, diag_key0=None):
        s_ref, mb_ref = slots[slot]
        for g in range(ATT_NH):
            st = s_ref[g]
            mb = mb_ref[g]
            if diag_key0 is not None:
                keys = lax.broadcasted_iota(jnp.int32, st.shape, 0) + diag_key0
                qrys = lax.broadcasted_iota(jnp.int32, st.shape, 1)
                st = jnp.where(keys <= qrys, st, NEG)
                mb = jnp.max(st, axis=0, keepdims=True)
            update(g, st, mb, vt_ref[g, j])

    slots = ((s0_scr, mb0_scr), (s1_scr, mb1_scr))
    stage1(0, 0)

    def body(jj, carry):
        j = 2 * jj
        stage1(1, j + 1)
        stage2(0, j)
        stage1(0, j + 2)
        stage2(1, j + 1)
        return carry

    lax.fori_loop(0, i, body, 0)
    stage1(1, 2 * i + 1)
    stage2(0, 2 * i, diag_key0=0)
    stage2(1, 2 * i + 1, diag_key0=TK)

    for g in range(ATT_NH):
        kb = km_ref[:, g * dq:(g + 1) * dq]
        if fox:
            kb = jnp.concatenate([kb, kam_ref[...]], axis=1)
        st = _dot_nt(kb, query(g))
        update(g, st, jnp.max(st, axis=0, keepdims=True), vmt_ref[g])
        acc = acc_scr[g]
        o = (acc[0:HD] / acc[HD:HD + 1]).T
        o_ref[:, g * HD:(g + 1) * HD] = o.astype(o_ref.dtype)


def _attention(q, k, vt, km, vmt, k_col0, dq, aug=None, name="attn"):
    fox = aug is not None
    nh, tq = ATT_NH, ATT_TQ
    in_specs = [pl.BlockSpec((tq, nh * dq), lambda h, i: (i, h)),
                pl.BlockSpec((T, nh * dq), lambda h, i: (0, k_col0 // nh + h)),
                pl.BlockSpec((nh, T // TK, VROWS, TK), lambda h, i: (h, 0, 0, 0)),
                pl.BlockSpec((NM, nh * dq), lambda h, i: (0, k_col0 // nh + h)),
                pl.BlockSpec((nh, VROWS, NM), lambda h, i: (h, 0, 0))]
    args = [q, k, vt, km, vmt]
    scratch = [pltpu.VMEM((nh, 1, tq), F32), pltpu.VMEM((nh, VROWS, tq), F32),
               pltpu.VMEM((nh, TK, tq), F32), pltpu.VMEM((nh, TK, tq), F32),
               pltpu.VMEM((nh, 1, tq), F32), pltpu.VMEM((nh, 1, tq), F32)]
    if fox:
        in_specs += [pl.BlockSpec((tq, LANES_V7X), lambda h, i: (i, 0)),
                     _full((T, LANES_V7X)), _full((NM, LANES_V7X))]
        args += list(aug)
        scratch.append(pltpu.VMEM((nh, tq, 2 * dq), BF16))
    return pl.pallas_call(
        functools.partial(_attn_kernel, fox=fox, dq=dq),
        grid=(H // nh, T // tq),
        in_specs=in_specs,
        out_specs=pl.BlockSpec((tq, nh * HD), lambda h, i: (i, h)),
        out_shape=jax.ShapeDtypeStruct((T, H * HD), BF16),
        scratch_shapes=scratch,
        compiler_params=_params(("parallel", "arbitrary"), 56),
        name=name,
    )(*args)


def _meta_attn_kernel(q_ref, k_ref, v_ref, o_ref):
    s = _dot_nt(q_ref[...], k_ref[...])
    rows = lax.broadcasted_iota(jnp.int32, s.shape, 0)
    cols = lax.broadcasted_iota(jnp.int32, s.shape, 1)
    s = jnp.where(cols <= rows, s, NEG)
    p = jnp.exp2(s - jnp.max(s, axis=1, keepdims=True))
    o = _dot(p.astype(BF16), v_ref[...]) / jnp.sum(p, axis=1, keepdims=True)
    o_ref[...] = o.astype(o_ref.dtype)


def _meta_attention(q, k, v, q_col0, k_col0, v_col0, dq, name):
    return pl.pallas_call(
        _meta_attn_kernel,
        grid=(H,),
        in_specs=[pl.BlockSpec((NM, dq), lambda h: (0, q_col0 + h)),
                  pl.BlockSpec((NM, dq), lambda h: (0, k_col0 + h)),
                  pl.BlockSpec((NM, HD), lambda h: (0, v_col0 + h))],
        out_specs=pl.BlockSpec((NM, HD), lambda h: (0, h)),
        out_shape=jax.ShapeDtypeStruct((NM, H * HD), BF16),
        compiler_params=_params(("parallel",), 32),
        name=name,
    )(q, k, v)


def _attn_out_kernel(om_ref, of_ref, w_ref, h_ref, g_ref, b_ref, o_ref):
    m = _dot(om_ref[...], w_ref[0:H * HD, :]) + _dot(of_ref[...], w_ref[H * HD:, :])
    o_ref[...] = _layer_norm(ALPHA * h_ref[...] + m, g_ref[...], b_ref[...])


def _attn_out(om, of, w, h, g, b, tm):
    m = h.shape[0]
    row = lambda w_: pl.BlockSpec((tm, w_), lambda i: (i, 0))
    return pl.pallas_call(
        _attn_out_kernel,
        grid=(m // tm,),
        in_specs=[row(H * HD), row(H * HD), _resident(w.shape), row(D), _full(g.shape), _full(b.shape)],
        out_specs=row(D),
        out_shape=jax.ShapeDtypeStruct((m, D), F32),
        compiler_params=_params(("parallel",), 48),
        name="attn_out_ln",
    )(om, of, w, h, g, b)


def _silu(x):
    return x / (1.0 + jnp.exp(-x))


def _ffn_kernel(h_ref, hm_ref, wg_ref, wu_ref, wd_ref, g_ref, b_ref, o_ref, om_ref):
    i = pl.program_id(0)
    f = pl.program_id(1)
    last = pl.num_programs(1) - 1
    rows = h_ref.shape[0]
    chunk = FFN_CHUNK

    def chain(x):
        hid = _silu(_dot(x, wg_ref[...])) * _dot(x, wu_ref[...])
        return _dot(hid, wd_ref[...])

    @pl.when(f == 0)
    def _():
        o_ref[...] = jnp.zeros(o_ref.shape, F32)

    @pl.when(i == 0)
    def _():
        @pl.when(f == 0)
        def _():
            om_ref[...] = jnp.zeros(om_ref.shape, F32)
        part = chain(jnp.concatenate([h_ref[0:chunk, :], hm_ref[...]], axis=0))
        o_ref[0:chunk, :] += part[0:chunk]
        om_ref[...] += part[chunk:]

        @pl.when(f == last)
        def _():
            om_ref[...] = _layer_norm(ALPHA * hm_ref[...] + om_ref[...], g_ref[...], b_ref[...])

    @pl.when(i != 0)
    def _():
        o_ref[0:chunk, :] += chain(h_ref[0:chunk, :])

    for c in range(1, rows // chunk):
        sl = slice(c * chunk, (c + 1) * chunk)
        o_ref[sl, :] += chain(h_ref[sl, :])

    @pl.when(f == last)
    def _():
        o_ref[...] = _layer_norm(ALPHA * h_ref[...] + o_ref[...], g_ref[...], b_ref[...])


def _ffn(h, hm, wg, wu, wd, g, b):
    tm = FFN_TM
    once = pl.Buffered(1)
    return pl.pallas_call(
        _ffn_kernel,
        grid=(T // tm, FD // FFN_TF),
        in_specs=[pl.BlockSpec((tm, D), lambda i, f: (i, 0), pipeline_mode=once),
                  pl.BlockSpec((NM, D), lambda i, f: (0, 0)),
                  pl.BlockSpec((D, FFN_TF), lambda i, f: (0, f)),
                  pl.BlockSpec((D, FFN_TF), lambda i, f: (0, f)),
                  pl.BlockSpec((FFN_TF, D), lambda i, f: (f, 0)),
                  pl.BlockSpec((1, D), lambda i, f: (0, 0)),
                  pl.BlockSpec((1, D), lambda i, f: (0, 0))],
        out_specs=[pl.BlockSpec((tm, D), lambda i, f: (i, 0)),
                   pl.BlockSpec((NM, D), lambda i, f: (0, 0))],
        out_shape=[jax.ShapeDtypeStruct((T, D), F32), jax.ShapeDtypeStruct((NM, D), F32)],
        compiler_params=_params(("arbitrary", "arbitrary"), 58),
        name="ffn_ln",
    )(h, hm, wg, wu, wd, g, b)


def _pool_kernel(hp_ref, prev_ref, meta_ref, h_ref, wg_ref, sc_ref, wo_ref, g_ref, b_ref, o_ref,
                 ext_scr, z_scr):
    tm = hp_ref.shape[0]
    halo = jnp.where(pl.program_id(0) == 0, meta_ref[...], prev_ref[...])
    ext_scr[0:NM, :] = halo
    ext_scr[NM:NM + tm, :] = hp_ref[...]
    for gi, w in enumerate(POOL_W):
        cols = slice(PG * gi, PG * (gi + 1))
        cur = ext_scr[NM:NM + tm, cols]
        tot = cur
        for j in range(1, w):
            tot = tot + ext_scr[NM - j:NM - j + tm, cols]
        diff = tot * (1.0 / w) - cur
        y = _dot(diff, wg_ref[gi]) * sc_ref[:, cols]
        z_scr[:, cols] = y
    m = _dot(z_scr[...], wo_ref[...])
    o_ref[...] = _layer_norm(ALPHA * h_ref[...] + m, g_ref[...], b_ref[...])


def _pool_mixer(hp, hp_meta, h, wgroup, scale, wout, g, b):
    tm = POOL_TM
    row = pl.BlockSpec((tm, D), lambda i: (i, 0))
    prev = pl.BlockSpec((NM, D), lambda i: (jnp.maximum(i * (tm // NM) - 1, 0), 0))
    return pl.pallas_call(
        _pool_kernel,
        grid=(T // tm,),
        in_specs=[row, prev, _full(hp_meta.shape), row, _resident(wgroup.shape), _full(scale.shape),
                  _resident(wout.shape), _full(g.shape), _full(b.shape)],
        out_specs=row,
        out_shape=jax.ShapeDtypeStruct((T, D), F32),
        scratch_shapes=[pltpu.VMEM((NM + tm, D), F32), pltpu.VMEM((tm, D), F32)],
        compiler_params=_params(("parallel",), 48),
        name="pool_ln",
    )(hp, hp, hp_meta, h, wgroup, scale, wout, g, b)


def _router_kernel(h_ref, w_ref, b_ref, info_ref, cnt_ref, carry_scr):
    tm = h_ref.shape[0]

    @pl.when(pl.program_id(0) == 0)
    def _():
        carry_scr[...] = jnp.zeros(carry_scr.shape, F32)

    x = h_ref[...]
    w = w_ref[...]
    xh, wh = x.astype(BF16), w.astype(BF16)
    xl = (x - xh.astype(F32)).astype(BF16)
    wl = (w - wh.astype(F32)).astype(BF16)
    logits = _dot(xh, wh) + _dot(xh, wl) + _dot(xl, wh) + b_ref[...]
    lane = lax.broadcasted_iota(jnp.int32, logits.shape, 1).astype(F32)
    nolane = float(LANES_V7X)
    m1 = jnp.max(logits, axis=1, keepdims=True)
    i1 = jnp.min(jnp.where(logits == m1, lane, nolane), axis=1, keepdims=True)
    rest = jnp.where(lane == i1, -jnp.inf, logits)
    m2 = jnp.max(rest, axis=1, keepdims=True)
    i2 = jnp.min(jnp.where(rest == m2, lane, nolane), axis=1, keepdims=True)
    e = jnp.exp(m2 - m1)
    g1 = 1.0 / (1.0 + e)
    g2 = e / (1.0 + e)
    hit1 = lane == i1
    hit2 = lane == i2
    member = jnp.logical_or(hit1, hit2).astype(BF16)
    earlier = (lax.broadcasted_iota(jnp.int32, (tm, tm), 1)
               < lax.broadcasted_iota(jnp.int32, (tm, tm), 0)).astype(BF16)
    before = _dot(earlier, member) + carry_scr[0:1, :]
    r1 = jnp.sum(jnp.where(hit1, before, 0.0), axis=1, keepdims=True)
    r2 = jnp.sum(jnp.where(hit2, before, 0.0), axis=1, keepdims=True)
    carry = carry_scr[0:1, :] + jnp.sum(member.astype(F32), axis=0, keepdims=True)
    carry_scr[...] = jnp.broadcast_to(carry, carry_scr.shape)
    cnt_ref[...] = jnp.broadcast_to(carry, cnt_ref.shape)
    info = jnp.where(lane == 0, g1, 0.0)
    info = jnp.where(lane == 1, g2, info)
    info = jnp.where(lane == 2, i1, info)
    info = jnp.where(lane == 3, i2, info)
    info = jnp.where(lane == 4, r1, info)
    info = jnp.where(lane == 5, r2, info)
    info_ref[...] = info


def _router(h, w_pad, b_pad):
    tm = TM
    return pl.pallas_call(
        _router_kernel,
        grid=(T // tm,),
        in_specs=[pl.BlockSpec((tm, D), lambda i: (i, 0)), _full(w_pad.shape), _full(b_pad.shape)],
        out_specs=[pl.BlockSpec((tm, LANES_V7X), lambda i: (i, 0)), _full((8, LANES_V7X))],
        out_shape=[jax.ShapeDtypeStruct((T, LANES_V7X), F32),
                   jax.ShapeDtypeStruct((8, LANES_V7X), F32)],
        scratch_shapes=[pltpu.VMEM((8, LANES_V7X), F32)],
        compiler_params=_params(("arbitrary",), 32),
        name="router",
    )(h, w_pad, b_pad)


def _moe_row_copy(h_hbm, stage, sem, buf, tok, r):
    return pltpu.make_async_copy(h_hbm.at[pl.ds(tok, 1), :], stage.at[buf, pl.ds(r, 1), :],
                                 sem.at[buf])


def _moe_out_copy(acc_scr, y_hbm, sem, row0, u):
    off = pl.multiple_of(u * MOE_UNIT, MOE_UNIT)
    dst = pl.multiple_of(row0 + off, MOE_UNIT)
    return pltpu.make_async_copy(acc_scr.at[pl.ds(off, MOE_UNIT), :],
                                 y_hbm.at[pl.ds(dst, MOE_UNIT), :], sem)


def _moe_kernel(te_ref, row0_ref, nunit_ref, used_ref, pad_ref, dest_ref,
                h_hbm, wg_ref, wu_ref, wd_ref, y_hbm,
                slot_tok, stage, xb_scr, acc_scr, gsem, osem):
    t = pl.program_id(0)
    f = pl.program_id(1)
    nf = pl.num_programs(1)

    @pl.when(jnp.logical_and(t == 0, f == 0))
    def _():
        for e in range(NE):
            def clear(s, c):
                slot_tok[s] = 0
                return c
            lax.fori_loop(pad_ref[2 * e], pad_ref[2 * e + 1], clear, 0)

        def place(tok, c):
            slot_tok[dest_ref[2 * tok]] = tok
            slot_tok[dest_ref[2 * tok + 1]] = tok
            return c
        lax.fori_loop(0, T, place, 0, unroll=8)

    @pl.when(t < used_ref[0])
    def _():
        row0 = row0_ref[t]
        nunit = nunit_ref[t]

        @pl.when(f == 0)
        def _():
            def fetch(u, buf):
                first = row0 + u * MOE_UNIT

                def issue_group(r8, c):
                    base = pl.multiple_of(r8 * 8, 8)
                    for k in range(8):
                        _moe_row_copy(h_hbm, stage, gsem, buf, slot_tok[first + base + k],
                                      base + k).start()
                    return c
                lax.fori_loop(0, MOE_UNIT // 8, issue_group, 0)

            fetch(0, 0)

            @pl.when(t > 0)
            def _():
                def finish(u, c):
                    _moe_out_copy(acc_scr, y_hbm, osem, row0_ref[t - 1], u).wait()
                    return c
                lax.fori_loop(0, nunit_ref[t - 1], finish, 0)

            def clear_acc(u, c):
                off = pl.multiple_of(u * MOE_UNIT, MOE_UNIT)
                acc_scr[pl.ds(off, MOE_UNIT), :] = jnp.zeros((MOE_UNIT, D), F32)
                return c
            lax.fori_loop(0, nunit, clear_acc, 0)

            def unit(u, c):
                buf = lax.rem(u, 2)

                @pl.when(u + 1 < nunit)
                def _():
                    fetch(u + 1, 1 - buf)

                def drain_group(r8, cc):
                    base = pl.multiple_of(r8 * 8, 8)
                    for k in range(8):
                        _moe_row_copy(h_hbm, stage, gsem, buf, 0, base + k).wait()
                    return cc
                lax.fori_loop(0, MOE_UNIT // 8, drain_group, 0)
                off = pl.multiple_of(u * MOE_UNIT, MOE_UNIT)
                xb_scr[pl.ds(off, MOE_UNIT), :] = stage[buf].astype(BF16)
                return c
            lax.fori_loop(0, nunit, unit, 0)

        def chunk(off, rows):
            x = xb_scr[pl.ds(off, rows), :]
            hid = _silu(_dot_mixed(x, wg_ref[...])) * _dot_mixed(x, wu_ref[...])
            acc_scr[pl.ds(off, rows), :] += _dot(hid, wd_ref[...])

        def quad(c, carry):
            chunk(pl.multiple_of(c * 4 * MOE_UNIT, 4 * MOE_UNIT), 4 * MOE_UNIT)
            return carry
        lax.fori_loop(0, nunit // 4, quad, 0)

        @pl.when(nunit % 4 >= 2)
        def _():
            chunk(pl.multiple_of((nunit // 4) * 4 * MOE_UNIT, 2 * MOE_UNIT), 2 * MOE_UNIT)

        @pl.when(nunit % 2 == 1)
        def _():
            chunk(pl.multiple_of((nunit - 1) * MOE_UNIT, MOE_UNIT), MOE_UNIT)

        @pl.when(f == nf - 1)
        def _():
            def start(u, carry):
                _moe_out_copy(acc_scr, y_hbm, osem, row0, u).start()
                return carry
            lax.fori_loop(0, nunit, start, 0)

            @pl.when(t == used_ref[0] - 1)
            def _():
                def finish(u, carry):
                    _moe_out_copy(acc_scr, y_hbm, osem, row0, u).wait()
                    return carry
                lax.fori_loop(0, nunit, finish, 0)

    @pl.when(jnp.logical_and(t == pl.num_programs(0) - 1, f == nf - 1))
    def _():
        acc_scr[0:MOE_UNIT, :] = jnp.zeros((MOE_UNIT, D), F32)

        def fill(u, carry):
            cp = _moe_out_copy(acc_scr, y_hbm, osem, u * MOE_UNIT, 0)
            cp.start()
            cp.wait()
            return carry
        lax.fori_loop(used_ref[1], MOE_CAP // MOE_UNIT, fill, 0)


def _moe_experts(h, wg, wu, wd, tile_e, tile_row0, tile_nunit, used, pad, dest):
    nf = FE // MOE_TF

    def fidx(t, f, us):
        return jnp.where(t < us[0], f, nf - 1)

    grid_spec = pltpu.PrefetchScalarGridSpec(
        num_scalar_prefetch=6,
        grid=(MOE_NT, nf),
        in_specs=[pl.BlockSpec(memory_space=pl.ANY),
                  pl.BlockSpec((None, D, MOE_TF), lambda t, f, te, r0, nu, us, pa, de: (te[t], 0, fidx(t, f, us))),
                  pl.BlockSpec((None, D, MOE_TF), lambda t, f, te, r0, nu, us, pa, de: (te[t], 0, fidx(t, f, us))),
                  pl.BlockSpec((None, MOE_TF, D), lambda t, f, te, r0, nu, us, pa, de: (te[t], fidx(t, f, us), 0))],
        out_specs=pl.BlockSpec(memory_space=pl.ANY),
        scratch_shapes=[pltpu.SMEM((MOE_CAP,), jnp.int32),
                        pltpu.VMEM((2, MOE_UNIT, D), F32),
                        pltpu.VMEM((MOE_TMAX, D), BF16),
                        pltpu.VMEM((MOE_TMAX, D), F32),
                        pltpu.SemaphoreType.DMA((2,)),
                        pltpu.SemaphoreType.DMA],
    )
    return pl.pallas_call(
        _moe_kernel,
        grid_spec=grid_spec,
        out_shape=jax.ShapeDtypeStruct((MOE_CAP, D), F32),
        compiler_params=_params(("arbitrary", "arbitrary"), 58),
        name="moe_experts",
    )(tile_e, tile_row0, tile_nunit, used, pad, dest, h, wg, wu, wd)


def _cmb_copy(y_hbm, ybuf, sem, buf, slot, k, r):
    return pltpu.make_async_copy(y_hbm.at[pl.ds(slot, 1), :], ybuf.at[buf, k, pl.ds(r, 1), :],
                                 sem.at[buf])


def _combine_kernel(dest_ref, h_ref, info_ref, y_hbm, g_ref, b_ref, o_ref, ybuf, sem):
    tm = h_ref.shape[0]
    i = pl.program_id(0)
    cur = lax.rem(i, 2)

    def gather(tile, buf):
        def issue_group(r8, c):
            base = pl.multiple_of(r8 * 8, 8)
            for k in range(8):
                tok = tile * tm + base + k
                _cmb_copy(y_hbm, ybuf, sem, buf, dest_ref[2 * tok], 0, base + k).start()
                _cmb_copy(y_hbm, ybuf, sem, buf, dest_ref[2 * tok + 1], 1, base + k).start()
            return c
        lax.fori_loop(0, tm // 8, issue_group, 0)

    @pl.when(i == 0)
    def _():
        gather(0, 0)

    @pl.when(i + 1 < pl.num_programs(0))
    def _():
        gather(i + 1, 1 - cur)

    def drain_group(r8, c):
        base = pl.multiple_of(r8 * 8, 8)
        for k in range(8):
            _cmb_copy(y_hbm, ybuf, sem, cur, 0, 0, base + k).wait()
            _cmb_copy(y_hbm, ybuf, sem, cur, 0, 1, base + k).wait()
        return c
    lax.fori_loop(0, tm // 8, drain_group, 0)
    info = info_ref[...]
    moe = info[:, 0:1] * ybuf[cur, 0] + info[:, 1:2] * ybuf[cur, 1]
    o_ref[...] = _layer_norm(ALPHA * h_ref[...] + moe, g_ref[...], b_ref[...])


def _combine(dest, h, info, y_rows, g, b):
    tm = CMB_TM
    grid_spec = pltpu.PrefetchScalarGridSpec(
        num_scalar_prefetch=1,
        grid=(T // tm,),
        in_specs=[pl.BlockSpec((tm, D), lambda i, de: (i, 0)),
                  pl.BlockSpec((tm, LANES_V7X), lambda i, de: (i, 0)),
                  pl.BlockSpec(memory_space=pl.ANY),
                  pl.BlockSpec((1, D), lambda i, de: (0, 0)),
                  pl.BlockSpec((1, D), lambda i, de: (0, 0))],
        out_specs=pl.BlockSpec((tm, D), lambda i, de: (i, 0)),
        scratch_shapes=[pltpu.VMEM((2, 2, tm, D), F32), pltpu.SemaphoreType.DMA((2,))],
    )
    return pl.pallas_call(
        _combine_kernel,
        grid_spec=grid_spec,
        out_shape=jax.ShapeDtypeStruct((T, D), F32),
        compiler_params=_params(("arbitrary",), 32),
        name="moe_combine_ln",
    )(dest, h, info, y_rows, g, b)


def _rope_tables(pos):
    inv_freq = THETA ** (-jnp.arange(0, ROPE, 2, dtype=F32) / ROPE)
    ang = pos[:, None] * inv_freq[None, :]
    cos = jnp.cos(ang).astype(F32)
    sin = jnp.sin(ang).astype(F32)
    zero = jnp.zeros((pos.shape[0], LANES_V7X - ROPE), F32)
    return (jnp.concatenate([cos, cos, zero], axis=1),
            jnp.concatenate([-sin, sin, zero], axis=1))


def _expert_tiles(counts):
    units = (counts + MOE_UNIT - 1) // MOE_UNIT
    seg_unit = jnp.cumsum(units) - units
    ntile = (units + MOE_TUNITS - 1) // MOE_TUNITS
    tile_end = jnp.cumsum(ntile)
    n_used = tile_end[-1]
    tid = jnp.arange(MOE_NT, dtype=jnp.int32)
    e = jnp.minimum(jnp.sum(tid[:, None] >= tile_end[None, :], axis=1), NE - 1).astype(jnp.int32)
    local = tid - (tile_end - ntile)[e]
    nt = jnp.maximum(ntile[e], 1)
    lo = (local * units[e]) // nt
    hi = ((local + 1) * units[e]) // nt
    used = tid < n_used
    last = jnp.maximum(n_used - 1, 0)
    e = jnp.where(used, e, e[last])
    row0 = jnp.where(used, (seg_unit[e] + lo) * MOE_UNIT, 0)
    nunit = jnp.where(used, hi - lo, 0)
    used_info = jnp.stack([n_used, jnp.sum(units)]).astype(jnp.int32)
    seg_row = seg_unit * MOE_UNIT
    pad = jnp.stack([seg_row + counts, seg_row + units * MOE_UNIT], axis=1).reshape(2 * NE)
    return (e.astype(jnp.int32), row0.astype(jnp.int32), nunit.astype(jnp.int32),
            used_info, pad.astype(jnp.int32), seg_row.astype(jnp.int32))


def kernel(x, meta_tokens, attn_w_in, fox_forget_bias, mla_q_norm, mla_kv_norm, mla_w_uq, mla_w_ukv, attn_w_out, pool_w_in, pool_w_group, pool_scale, pool_w_out, ffn_w_gate, ffn_w_up, ffn_w_down, moe_w_router, moe_b_router, moe_w_gate, moe_w_up, moe_w_down, ln_mix_g, ln_mix_b, ln_ffn_g, ln_ffn_b):
    assert x.shape == (1, T, D) and meta_tokens.shape == (NM, D)
    xt = x[0]
    meta = meta_tokens.astype(x.dtype)

    w_in = attn_w_in[0]
    o_kr, o_qf, o_f = QL + KVL, QL + KVL + ROPE, QL + KVL + ROPE + 3 * H * HD
    kr1, kr2 = w_in[:, o_kr:o_kr + ROPE // 2], w_in[:, o_kr + ROPE // 2:o_qf]
    zpad = lambda n: jnp.zeros((D, n), F32)
    w_small = jnp.concatenate([kr2, kr1, zpad(LANES_V7X - ROPE),
                               w_in[:, o_f:o_f + H], zpad(LANES_V7X - H)], axis=1)
    w_fox = w_in[:, o_qf:o_f].astype(BF16)
    w_in = w_in[:, :LAT_COLS]
    wuq = mla_w_uq[0].reshape(QL, H, NOPE + ROPE)
    uq_n, uq_1, uq_2 = wuq[..., :NOPE], wuq[..., NOPE:NOPE + ROPE // 2], wuq[..., NOPE + ROPE // 2:]
    zq = jnp.zeros((QL, H, LANES_V7X - ROPE), F32)
    w_uq = jnp.concatenate([uq_n, uq_1, uq_2, zq], axis=-1).reshape(QL, H * DQ_MLA).astype(BF16)
    w_uqs = jnp.concatenate([uq_2, uq_1, zq], axis=-1).reshape(QL, H * HD).astype(BF16)
    wukv = mla_w_ukv[0].reshape(KVL, H, NOPE + HD)
    w_ukv = jnp.concatenate([wukv[..., :NOPE].reshape(KVL, H * NOPE),
                             wukv[..., NOPE:].reshape(KVL, H * HD)], axis=1).astype(BF16)
    w_ao = attn_w_out[0]
    w_fg, w_fu, w_fd = ffn_w_gate[0], ffn_w_up[0], ffn_w_down[0]
    w_pi, w_pg, w_po = pool_w_in[0], pool_w_group[0], pool_w_out[0]
    qn, kvn = mla_q_norm[0][None, :], mla_kv_norm[0][None, :]
    row = lambda v: v[None, :]

    ctab_t, stab_t = _rope_tables(jnp.arange(NM, NM + T, dtype=F32))
    ctab_m, stab_m = _rope_tables(jnp.arange(NM, dtype=F32))

    fox_scale = HD ** -0.5 * LOG2E
    q_t, k_t, vt_t, fl_t = _mla_proj(xt, w_in, w_small, qn, kvn, w_uq, w_uqs, w_ukv, ctab_t, stab_t, TK)
    q_m, k_m, v_m, _ = _mla_proj(meta, w_in, w_small, qn, kvn, w_uq, w_uqs, w_ukv, ctab_m, stab_m, NM)
    fox_qk, fox_vt = _fox_proj(xt, w_fox, fox_scale)
    fox_m = _matmul(meta, w_fox, NM, H * HD, BF16, scaled_blocks=1, scale=fox_scale, name="fox_proj_meta")

    fl = fl_t[:, :H].T.reshape(H, T // LANES_V7X, LANES_V7X)
    bias_b = jnp.broadcast_to(fox_forget_bias[0][:, None, None], (H, 1, LANES_V7X))
    parts = _forget_cumsum(fl, bias_b).reshape(H, 6, T)
    pos = parts[:, 0:3].transpose(2, 0, 1)
    neg = parts[:, 3:6].transpose(2, 0, 1)
    lanes = lambda pieces: jnp.concatenate(pieces, axis=2).reshape(-1, H * AUG_STRIDE)
    cst = lambda rows, n, v: jnp.full((rows, H, n), v, BF16)
    aug = (lanes([cst(T, 3, 1.0), pos, cst(T, AUG_STRIDE - 6, 0.0)]),
           lanes([neg, cst(T, AUG_STRIDE - 3, 0.0)]),
           lanes([cst(NM, 3, 0.0), cst(NM, 3, -1.0), cst(NM, AUG_STRIDE - 6, 0.0)]))

    ones_rows = jnp.concatenate([jnp.ones((H, 1, NM), BF16), jnp.zeros((H, VROWS - HD - 1, NM), BF16)], axis=1)
    to_vmt = lambda v: jnp.concatenate([v.reshape(NM, H, HD).transpose(1, 2, 0), ones_rows], axis=1)
    o_mla = _attention(q_t, k_t, vt_t, k_m, to_vmt(v_m), 0, DQ_MLA, name="mla_attn")
    o_fox = _attention(fox_qk, fox_qk, fox_vt, fox_m, to_vmt(fox_m[:, 2 * H * HD:]),
                       H, HD, aug, name="fox_attn")
    om_mla = _meta_attention(q_m, k_m, v_m, 0, 0, 0, DQ_MLA, "mla_attn_meta")
    om_fox = _meta_attention(fox_m, fox_m, fox_m, 0, H, 2 * H, HD, "fox_attn_meta")

    g0, b0, g1, b1 = row(ln_mix_g[0]), row(ln_mix_b[0]), row(ln_ffn_g[0]), row(ln_ffn_b[0])
    h_t = _attn_out(o_mla, o_fox, w_ao, xt, g0, b0, TM)
    h_m = _attn_out(om_mla, om_fox, w_ao, meta, g0, b0, NM)
    h_t, h_m = _ffn(h_t, h_m, w_fg, w_fu, w_fd, g1, b1)

    hp_t = _matmul(h_t, w_pi, TM, D, F32, name="pool_in")
    hp_m = _matmul(h_m, w_pi, NM, D, F32, name="pool_in_meta")
    h_t = _pool_mixer(hp_t, hp_m, h_t, w_pg, row(pool_scale[0]), w_po,
                      row(ln_mix_g[1]), row(ln_mix_b[1]))

    w_r = jnp.concatenate([moe_w_router[0], jnp.zeros((D, LANES_V7X - NE), F32)], axis=1)
    b_r = jnp.concatenate([moe_b_router[0], jnp.full((LANES_V7X - NE,), NEG, F32)])[None, :]
    info, cnt = _router(h_t, w_r, b_r)
    counts = cnt[0, :NE].astype(jnp.int32)
    tile_e, tile_row0, tile_nunit, used, pad, seg_row = _expert_tiles(counts)
    experts = info[:, 2:4].astype(jnp.int32)
    ranks = info[:, 4:6].astype(jnp.int32)
    dest = (seg_row[experts] + ranks).reshape(2 * T)
    y_rows = _moe_experts(h_t, moe_w_gate[0], moe_w_up[0], moe_w_down[0],
                          tile_e, tile_row0, tile_nunit, used, pad, dest)
    out = _combine(dest, h_t, info, y_rows, row(ln_ffn_g[1]), row(ln_ffn_b[1]))
    return out[None]
```

```python
import functools
import math

import jax
import jax.numpy as jnp
import numpy as np
from jax import lax
from jax.experimental import pallas as pl
from jax.experimental.pallas import tpu as pltpu

D = 2048
T = 8192
NM = 16
H = 8
QL = 512
KVL = 512
NOPE = 128
ROPE = 64
HD = 128
DQ_MLA = 256
LAT_COLS = QL + KVL + 128
VROWS = HD + 16
THETA = 10000.0
POOL_W = (2, 4, 8, 16)
PG = D // 4
FD = 5632
NE = 8
FE = 7168
ALPHA = 4.0 ** 0.25
LN_EPS = 1e-5
RMS_EPS = 1e-6
NEG = -1e30
LOG2E = 1.4426950408889634

F32 = jnp.float32
BF16 = jnp.bfloat16

LANES_V7X = 128
VMEM_BYTES_V7X = 64 * 1024 * 1024
MIB = 1024 * 1024

TM = 512
TK = 512
ATT_TQ = 2 * TK
FFN_TM = 1024
FFN_CHUNK = 1024
FFN_TF = 512
POOL_TM = 256
MOE_UNIT = 128
MOE_TUNITS = 10
MOE_TMAX = MOE_TUNITS * MOE_UNIT
MOE_TF = 512
MOE_CAP = 2 * T + NE * MOE_UNIT
MOE_NT = MOE_CAP // MOE_TMAX + NE + 1
CMB_TM = 256


def _params(sem, vmem_mib):
    return pltpu.CompilerParams(dimension_semantics=sem, vmem_limit_bytes=vmem_mib * MIB)


def _full(shape):
    n = len(shape)
    return pl.BlockSpec(shape, lambda *_: (0,) * n)


def _resident(shape):
    n = len(shape)
    return pl.BlockSpec(shape, lambda *_: (0,) * n, pipeline_mode=pl.Buffered(1))


def _layer_norm(x, g, b):
    mu = jnp.mean(x, axis=-1, keepdims=True)
    xc = x - mu
    var = jnp.mean(xc * xc, axis=-1, keepdims=True)
    return xc * lax.rsqrt(var + LN_EPS) * g + b


def _rms_norm(x, g):
    return x * lax.rsqrt(jnp.mean(x * x, axis=-1, keepdims=True) + RMS_EPS) * g


def _dot(a, b):
    return jnp.dot(a, b, preferred_element_type=F32)


def _dot_nt(a, b):
    return lax.dot_general(a, b, (((1,), (1,)), ((), ())), preferred_element_type=F32)


def _vt_store(v_ref, h, v):
    v_ref[h, 0, 0:HD, :] = v.T.astype(BF16)
    rows = lax.broadcasted_iota(jnp.int32, (VROWS - HD, v.shape[0]), 0)
    v_ref[h, 0, HD:VROWS, :] = (rows == 0).astype(BF16)


def _mla_proj_kernel(x_ref, win_ref, wsm_ref, qn_ref, kvn_ref, wuq_ref, wuqs_ref, wukv_ref,
                     c_ref, s_ref, q_ref, k_ref, v_ref, fl_ref, *, qscale, transposed_v):
    x = x_ref[...]
    lat = _dot(x, win_ref[...])
    small = _dot(x, wsm_ref[...])
    cq = lat[:, 0:QL]
    ckv = lat[:, QL:QL + KVL]
    kr = lat[:, QL + KVL:LAT_COLS]
    krs = small[:, 0:LANES_V7X]
    fl_ref[...] = small[:, LANES_V7X:]
    cos = c_ref[...]
    sin = s_ref[...]
    qn = _rms_norm(cq, qn_ref[...]).astype(BF16)
    qpre = _dot(qn, wuq_ref[...])
    qsw = _dot(qn, wuqs_ref[...])
    for h in range(H):
        nope = qpre[:, DQ_MLA * h:DQ_MLA * h + NOPE]
        rot = (qpre[:, DQ_MLA * h + NOPE:DQ_MLA * (h + 1)] * cos
               + qsw[:, HD * h:HD * (h + 1)] * sin)
        q_ref[:, DQ_MLA * h:DQ_MLA * h + NOPE] = (nope * qscale).astype(BF16)
        q_ref[:, DQ_MLA * h + NOPE:DQ_MLA * (h + 1)] = (rot * qscale).astype(BF16)
    kvn = _rms_norm(ckv, kvn_ref[...]).astype(BF16)
    kv = _dot(kvn, wukv_ref[...])
    krot = (kr * cos + krs * sin).astype(BF16)
    for h in range(H):
        k_ref[:, DQ_MLA * h:DQ_MLA * h + NOPE] = kv[:, HD * h:HD * (h + 1)].astype(BF16)
        k_ref[:, DQ_MLA * h + NOPE:DQ_MLA * (h + 1)] = krot
    if transposed_v:
        for h in range(H):
            _vt_store(v_ref, h, kv[:, (H + h) * HD:(H + h + 1) * HD])
    else:
        v_ref[...] = kv[:, H * HD:].astype(BF16)


def _mla_proj(x, w_in, wsm, qn, kvn, wuq, wuqs, wukv, ctab, stab, tm):
    m = x.shape[0]
    qscale = (NOPE + ROPE) ** -0.5 * LOG2E
    row = lambda w: pl.BlockSpec((tm, w), lambda i: (i, 0))
    transposed_v = tm == TK
    if transposed_v:
        v_spec = pl.BlockSpec((H, 1, VROWS, TK), lambda i: (0, i, 0, 0))
        v_shape = jax.ShapeDtypeStruct((H, m // TK, VROWS, TK), BF16)
    else:
        v_spec, v_shape = row(H * HD), jax.ShapeDtypeStruct((m, H * HD), BF16)
    lat_spec = pl.BlockSpec((D, LAT_COLS), lambda i: (0, 0), pipeline_mode=pl.Buffered(1))
    return pl.pallas_call(
        functools.partial(_mla_proj_kernel, qscale=qscale, transposed_v=transposed_v),
        grid=(m // tm,),
        in_specs=[row(D), lat_spec, _full(wsm.shape), _full(qn.shape), _full(kvn.shape),
                  _resident(wuq.shape), _resident(wuqs.shape), _resident(wukv.shape),
                  row(LANES_V7X), row(LANES_V7X)],
        out_specs=[row(H * DQ_MLA), row(H * DQ_MLA), v_spec, row(LANES_V7X)],
        out_shape=[jax.ShapeDtypeStruct((m, H * DQ_MLA), BF16),
                   jax.ShapeDtypeStruct((m, H * DQ_MLA), BF16),
                   v_shape,
                   jax.ShapeDtypeStruct((m, LANES_V7X), F32)],
        compiler_params=_params(("parallel",), 56),
        name="mla_proj",
    )(x, w_in, wsm, qn, kvn, wuq, wuqs, wukv, ctab, stab)


def _mm_kernel(x_ref, w_ref, o_ref, *, scaled_blocks, scale):
    y = _dot(x_ref[...], w_ref[...])
    if scaled_blocks:
        y = jnp.where(pl.program_id(1) < scaled_blocks, y * scale, y)
    o_ref[...] = y.astype(o_ref.dtype)


def _matmul(x, w, tm, tn, out_dtype, scaled_blocks=0, scale=1.0, name="matmul"):
    m, k = x.shape
    n = w.shape[1]
    return pl.pallas_call(
        functools.partial(_mm_kernel, scaled_blocks=scaled_blocks, scale=scale),
        grid=(m // tm, n // tn),
        in_specs=[pl.BlockSpec((tm, k), lambda i, j: (i, 0)),
                  _resident((k, n)) if tn == n else pl.BlockSpec((k, tn), lambda i, j: (0, j))],
        out_specs=pl.BlockSpec((tm, tn), lambda i, j: (i, j)),
        out_shape=jax.ShapeDtypeStruct((m, n), out_dtype),
        compiler_params=_params(("parallel", "parallel"), 48),
        name=name,
    )(x, w)


def _fox_proj_kernel(x_ref, w_ref, qk_ref, vt_ref, *, scale):
    j = pl.program_id(1)
    n = H * HD
    part = lambda c: _dot(x_ref[...].astype(BF16), w_ref[:, c * n:(c + 1) * n])

    @pl.when(j == 0)
    def _():
        qk_ref[...] = (part(0) * scale).astype(BF16)

    @pl.when(j == 1)
    def _():
        qk_ref[...] = part(1).astype(BF16)

    @pl.when(j == 2)
    def _():
        y = part(2)
        for h in range(H):
            _vt_store(vt_ref, h, y[:, h * HD:(h + 1) * HD])


def _fox_proj(x, w, scale):
    n = H * HD
    return pl.pallas_call(
        functools.partial(_fox_proj_kernel, scale=scale),
        grid=(T // TK, 3),
        in_specs=[pl.BlockSpec((TK, D), lambda i, j: (i, 0)),
                  _resident((D, 3 * n))],
        out_specs=[pl.BlockSpec((TK, n), lambda i, j: (i, jnp.minimum(j, 1))),
                   pl.BlockSpec((H, 1, VROWS, TK), lambda i, j: (0, i, 0, 0))],
        out_shape=[jax.ShapeDtypeStruct((T, 2 * n), BF16),
                   jax.ShapeDtypeStruct((H, T // TK, VROWS, TK), BF16)],
        compiler_params=_params(("parallel", "arbitrary"), 48),
        name="fox_proj",
    )(x, w)


def _split3(a):
    hi = a.astype(BF16)
    r = a - hi.astype(F32)
    mid = r.astype(BF16)
    lo = (r - mid.astype(F32)).astype(BF16)
    return hi, mid, lo


def _forget_cumsum_kernel(fl_ref, b_ref, o_ref):
    z = fl_ref[...] + b_ref[...]
    logf = jnp.minimum(z, 0.0) - jnp.log(1.0 + jnp.exp(-jnp.abs(z)))
    n = LANES_V7X
    ri = lax.broadcasted_iota(jnp.int32, (n, n), 0)
    ci = lax.broadcasted_iota(jnp.int32, (n, n), 1)
    upper = (ri <= ci).astype(BF16)
    ones = jnp.ones((n, n), BF16)
    within = jnp.zeros(logf.shape, F32)
    total = jnp.zeros(logf.shape, F32)
    for part in _split3(logf):
        within += _dot(part, upper)
        total += _dot(part, ones)
    r = logf.shape[0]
    lower = (lax.broadcasted_iota(jnp.int32, (r, r), 1)
             < lax.broadcasted_iota(jnp.int32, (r, r), 0)).astype(BF16)
    before = jnp.zeros(logf.shape, F32)
    for part in _split3(total):
        before += _dot(lower, part)
    c2 = (within + before) * LOG2E
    for p, part in enumerate(_split3(c2)):
        o_ref[p] = part
        o_ref[3 + p] = -part


def _forget_cumsum(fl_t, bias_b):
    r = T // LANES_V7X
    return pl.pallas_call(
        _forget_cumsum_kernel,
        grid=(H,),
        in_specs=[pl.BlockSpec((None, r, LANES_V7X), lambda h: (h, 0, 0)),
                  pl.BlockSpec((None, 1, LANES_V7X), lambda h: (h, 0, 0))],
        out_specs=pl.BlockSpec((None, 6, r, LANES_V7X), lambda h: (h, 0, 0, 0)),
        out_shape=jax.ShapeDtypeStruct((H, 6, r, LANES_V7X), BF16),
        compiler_params=_params(("parallel",), 32),
        name="forget_cumsum",
    )(fl_t, bias_b)


ATT_NH = 2
AUG_STRIDE = 16


def _attn_kernel(*refs, fox, dq):
    if fox:
        (q_ref, k_ref, vt_ref, km_ref, vmt_ref, qa_ref, ka_ref, kam_ref, o_ref,
         m_scr, acc_scr, s0_scr, s1_scr, mb0_scr, mb1_scr, qf_scr) = refs
    else:
        (q_ref, k_ref, vt_ref, km_ref, vmt_ref, o_ref,
         m_scr, acc_scr, s0_scr, s1_scr, mb0_scr, mb1_scr) = refs
    hp = pl.program_id(0)
    i = pl.program_id(1)
    m_scr[...] = jnp.full(m_scr.shape, -jnp.inf, F32)
    acc_scr[...] = jnp.zeros(acc_scr.shape, F32)
    if fox:
        lane = lax.broadcasted_iota(jnp.int32, qa_ref.shape, 1)
        for g in range(ATT_NH):
            lo = (hp * ATT_NH + g) * AUG_STRIDE
            own = jnp.logical_and(lane >= lo, lane < lo + AUG_STRIDE)
            qf_scr[g, :, 0:dq] = q_ref[:, g * dq:(g + 1) * dq]
            qf_scr[g, :, dq:2 * dq] = jnp.where(own, qa_ref[...], jnp.zeros_like(qa_ref))

    def query(g):
        return qf_scr[g] if fox else q_ref[:, g * dq:(g + 1) * dq]

    def update(g, st, mb, vt):
        m_prev = m_scr[g]
        m_new = jnp.maximum(m_prev, mb)
        alpha = jnp.exp2(m_prev - m_new)
        p = jnp.exp2(st - m_new)
        acc_scr[g] = alpha * acc_scr[g] + _dot(vt, p.astype(BF16))
        m_scr[g] = m_new

    def stage1(slot, j, heads=range(ATT_NH)):
        s_ref, mb_ref = slots[slot]
        off = pl.multiple_of(j * TK, TK)
        for g in heads:
            kb = k_ref[pl.ds(off, TK), g * dq:(g + 1) * dq]
            if fox:
                kb = jnp.concatenate([kb, ka_ref[pl.ds(off, TK), :]], axis=1)
            st = _dot_nt(kb, query(g))
            s_ref[g] = st
            mb_ref[g] = jnp.max(st, axis=0, keepdims=True)

    def stage2(slot, j, diag_key0=None, heads=range(ATT_NH)):
        s_ref, mb_ref = slots[slot]
        for g in heads:
            st = s_ref[g]
            mb = mb_ref[g]
            if diag_key0 is not None:
                keys = lax.broadcasted_iota(jnp.int32, st.shape, 0) + diag_key0
                qrys = lax.broadcasted_iota(jnp.int32, st.shape, 1)
                st = jnp.where(keys <= qrys, st, NEG)
                mb = jnp.max(st, axis=0, keepdims=True)
            update(g, st, mb, vt_ref[g, j])

    slots = ((s0_scr, mb0_scr), (s1_scr, mb1_scr))
    stage1(0, 0)

    def body(jj, carry):
        j = 2 * jj
        for g in range(ATT_NH):
            stage1(1, j + 1, heads=(g,))
            stage2(0, j, heads=(g,))
        for g in range(ATT_NH):
            stage1(0, j + 2, heads=(g,))
            stage2(1, j + 1, heads=(g,))
        return carry

    lax.fori_loop(0, i, body, 0)
    stage1(1, 2 * i + 1)
    stage2(0, 2 * i, diag_key0=0)
    stage2(1, 2 * i + 1, diag_key0=TK)

    for g in range(ATT_NH):
        kb = km_ref[:, g * dq:(g + 1) * dq]
        if fox:
            kb = jnp.concatenate([kb, kam_ref[...]], axis=1)
        st = _dot_nt(kb, query(g))
        update(g, st, jnp.max(st, axis=0, keepdims=True), vmt_ref[g])
        acc = acc_scr[g]
        o = (acc[0:HD] / acc[HD:HD + 1]).T
        o_ref[:, g * HD:(g + 1) * HD] = o.astype(o_ref.dtype)


def _attention(q, k, vt, km, vmt, k_col0, dq, aug=None, name="attn"):
    fox = aug is not None
    nh, tq = ATT_NH, ATT_TQ
    in_specs = [pl.BlockSpec((tq, nh * dq), lambda h, i: (i, h)),
                pl.BlockSpec((T, nh * dq), lambda h, i: (0, k_col0 // nh + h)),
                pl.BlockSpec((nh, T // TK, VROWS, TK), lambda h, i: (h, 0, 0, 0)),
                pl.BlockSpec((NM, nh * dq), lambda h, i: (0, k_col0 // nh + h)),
                pl.BlockSpec((nh, VROWS, NM), lambda h, i: (h, 0, 0))]
    args = [q, k, vt, km, vmt]
    scratch = [pltpu.VMEM((nh, 1, tq), F32), pltpu.VMEM((nh, VROWS, tq), F32),
               pltpu.VMEM((nh, TK, tq), F32), pltpu.VMEM((nh, TK, tq), F32),
               pltpu.VMEM((nh, 1, tq), F32), pltpu.VMEM((nh, 1, tq), F32)]
    if fox:
        in_specs += [pl.BlockSpec((tq, LANES_V7X), lambda h, i: (i, 0)),
                     _full((T, LANES_V7X)), _full((NM, LANES_V7X))]
        args += list(aug)
        scratch.append(pltpu.VMEM((nh, tq, 2 * dq), BF16))
    return pl.pallas_call(
        functools.partial(_attn_kernel, fox=fox, dq=dq),
        grid=(H // nh, T // tq),
        in_specs=in_specs,
        out_specs=pl.BlockSpec((tq, nh * HD), lambda h, i: (i, h)),
        out_shape=jax.ShapeDtypeStruct((T, H * HD), BF16),
        scratch_shapes=scratch,
        compiler_params=_params(("parallel", "arbitrary"), 56),
        name=name,
    )(*args)


def _meta_attn_kernel(q_ref, k_ref, v_ref, o_ref):
    s = _dot_nt(q_ref[...], k_ref[...])
    rows = lax.broadcasted_iota(jnp.int32, s.shape, 0)
    cols = lax.broadcasted_iota(jnp.int32, s.shape, 1)
    s = jnp.where(cols <= rows, s, NEG)
    p = jnp.exp2(s - jnp.max(s, axis=1, keepdims=True))
    o = _dot(p.astype(BF16), v_ref[...]) / jnp.sum(p, axis=1, keepdims=True)
    o_ref[...] = o.astype(o_ref.dtype)


def _meta_attention(q, k, v, q_col0, k_col0, v_col0, dq, name):
    return pl.pallas_call(
        _meta_attn_kernel,
        grid=(H,),
        in_specs=[pl.BlockSpec((NM, dq), lambda h: (0, q_col0 + h)),
                  pl.BlockSpec((NM, dq), lambda h: (0, k_col0 + h)),
                  pl.BlockSpec((NM, HD), lambda h: (0, v_col0 + h))],
        out_specs=pl.BlockSpec((NM, HD), lambda h: (0, h)),
        out_shape=jax.ShapeDtypeStruct((NM, H * HD), BF16),
        compiler_params=_params(("parallel",), 32),
        name=name,
    )(q, k, v)


def _attn_out_kernel(om_ref, of_ref, w_ref, h_ref, g_ref, b_ref, o_ref):
    m = _dot(om_ref[...], w_ref[0:H * HD, :]) + _dot(of_ref[...], w_ref[H * HD:, :])
    o_ref[...] = _layer_norm(ALPHA * h_ref[...] + m, g_ref[...], b_ref[...])


def _attn_out(om, of, w, h, g, b, tm):
    m = h.shape[0]
    row = lambda w_: pl.BlockSpec((tm, w_), lambda i: (i, 0))
    return pl.pallas_call(
        _attn_out_kernel,
        grid=(m // tm,),
        in_specs=[row(H * HD), row(H * HD), _resident(w.shape), row(D), _full(g.shape), _full(b.shape)],
        out_specs=row(D),
        out_shape=jax.ShapeDtypeStruct((m, D), F32),
        compiler_params=_params(("parallel",), 48),
        name="attn_out_ln",
    )(om, of, w, h, g, b)


def _silu(x):
    return x / (1.0 + jnp.exp(-x))


def _ffn_kernel(h_ref, hm_ref, wg_ref, wu_ref, wd_ref, g_ref, b_ref, o_ref, om_ref):
    i = pl.program_id(0)
    f = pl.program_id(1)
    last = pl.num_programs(1) - 1
    rows = h_ref.shape[0]
    chunk = FFN_CHUNK

    def chain(x):
        hid = _silu(_dot(x, wg_ref[...])) * _dot(x, wu_ref[...])
        return _dot(hid, wd_ref[...])

    @pl.when(f == 0)
    def _():
        o_ref[...] = jnp.zeros(o_ref.shape, F32)

    @pl.when(i == 0)
    def _():
        @pl.when(f == 0)
        def _():
            om_ref[...] = jnp.zeros(om_ref.shape, F32)
        part = chain(jnp.concatenate([h_ref[0:chunk, :], hm_ref[...]], axis=0))
        o_ref[0:chunk, :] += part[0:chunk]
        om_ref[...] += part[chunk:]

        @pl.when(f == last)
        def _():
            om_ref[...] = _layer_norm(ALPHA * hm_ref[...] + om_ref[...], g_ref[...], b_ref[...])

    @pl.when(i != 0)
    def _():
        o_ref[0:chunk, :] += chain(h_ref[0:chunk, :])

    for c in range(1, rows // chunk):
        sl = slice(c * chunk, (c + 1) * chunk)
        o_ref[sl, :] += chain(h_ref[sl, :])

    @pl.when(f == last)
    def _():
        o_ref[...] = _layer_norm(ALPHA * h_ref[...] + o_ref[...], g_ref[...], b_ref[...])


def _ffn(h, hm, wg, wu, wd, g, b):
    tm = FFN_TM
    once = pl.Buffered(1)
    return pl.pallas_call(
        _ffn_kernel,
        grid=(T // tm, FD // FFN_TF),
        in_specs=[pl.BlockSpec((tm, D), lambda i, f: (i, 0), pipeline_mode=once),
                  pl.BlockSpec((NM, D), lambda i, f: (0, 0)),
                  pl.BlockSpec((D, FFN_TF), lambda i, f: (0, f)),
                  pl.BlockSpec((D, FFN_TF), lambda i, f: (0, f)),
                  pl.BlockSpec((FFN_TF, D), lambda i, f: (f, 0)),
                  pl.BlockSpec((1, D), lambda i, f: (0, 0)),
                  pl.BlockSpec((1, D), lambda i, f: (0, 0))],
        out_specs=[pl.BlockSpec((tm, D), lambda i, f: (i, 0)),
                   pl.BlockSpec((NM, D), lambda i, f: (0, 0))],
        out_shape=[jax.ShapeDtypeStruct((T, D), F32), jax.ShapeDtypeStruct((NM, D), F32)],
        compiler_params=_params(("arbitrary", "arbitrary"), 58),
        name="ffn_ln",
    )(h, hm, wg, wu, wd, g, b)


def _pool_kernel(hp_ref, prev_ref, meta_ref, h_ref, wg_ref, sc_ref, wo_ref, g_ref, b_ref, o_ref,
                 ext_scr, z_scr):
    tm = hp_ref.shape[0]
    halo = jnp.where(pl.program_id(0) == 0, meta_ref[...], prev_ref[...])
    ext_scr[0:NM, :] = halo
    ext_scr[NM:NM + tm, :] = hp_ref[...]
    for gi, w in enumerate(POOL_W):
        cols = slice(PG * gi, PG * (gi + 1))
        cur = ext_scr[NM:NM + tm, cols]
        tot = cur
        for j in range(1, w):
            tot = tot + ext_scr[NM - j:NM - j + tm, cols]
        diff = tot * (1.0 / w) - cur
        y = _dot(diff, wg_ref[gi]) * sc_ref[:, cols]
        z_scr[:, cols] = y
    m = _dot(z_scr[...], wo_ref[...])
    o_ref[...] = _layer_norm(ALPHA * h_ref[...] + m, g_ref[...], b_ref[...])


def _pool_mixer(hp, hp_meta, h, wgroup, scale, wout, g, b):
    tm = POOL_TM
    row = pl.BlockSpec((tm, D), lambda i: (i, 0))
    prev = pl.BlockSpec((NM, D), lambda i: (jnp.maximum(i * (tm // NM) - 1, 0), 0))
    return pl.pallas_call(
        _pool_kernel,
        grid=(T // tm,),
        in_specs=[row, prev, _full(hp_meta.shape), row, _resident(wgroup.shape), _full(scale.shape),
                  _resident(wout.shape), _full(g.shape), _full(b.shape)],
        out_specs=row,
        out_shape=jax.ShapeDtypeStruct((T, D), F32),
        scratch_shapes=[pltpu.VMEM((NM + tm, D), F32), pltpu.VMEM((tm, D), F32)],
        compiler_params=_params(("parallel",), 48),
        name="pool_ln",
    )(hp, hp, hp_meta, h, wgroup, scale, wout, g, b)


def _router_kernel(h_ref, w_ref, b_ref, info_ref, cnt_ref, carry_scr):
    tm = h_ref.shape[0]

    @pl.when(pl.program_id(0) == 0)
    def _():
        carry_scr[...] = jnp.zeros(carry_scr.shape, F32)

    x = h_ref[...]
    w = w_ref[...]
    xh, wh = x.astype(BF16), w.astype(BF16)
    xl = (x - xh.astype(F32)).astype(BF16)
    wl = (w - wh.astype(F32)).astype(BF16)
    logits = _dot(xh, wh) + _dot(xh, wl) + _dot(xl, wh) + b_ref[...]
    lane = lax.broadcasted_iota(jnp.int32, logits.shape, 1).astype(F32)
    nolane = float(LANES_V7X)
    m1 = jnp.max(logits, axis=1, keepdims=True)
    i1 = jnp.min(jnp.where(logits == m1, lane, nolane), axis=1, keepdims=True)
    rest = jnp.where(lane == i1, -jnp.inf, logits)
    m2 = jnp.max(rest, axis=1, keepdims=True)
    i2 = jnp.min(jnp.where(rest == m2, lane, nolane), axis=1, keepdims=True)
    e = jnp.exp(m2 - m1)
    g1 = 1.0 / (1.0 + e)
    g2 = e / (1.0 + e)
    hit1 = lane == i1
    hit2 = lane == i2
    member = jnp.logical_or(hit1, hit2).astype(BF16)
    earlier = (lax.broadcasted_iota(jnp.int32, (tm, tm), 1)
               < lax.broadcasted_iota(jnp.int32, (tm, tm), 0)).astype(BF16)
    before = _dot(earlier, member) + carry_scr[0:1, :]
    r1 = jnp.sum(jnp.where(hit1, before, 0.0), axis=1, keepdims=True)
    r2 = jnp.sum(jnp.where(hit2, before, 0.0), axis=1, keepdims=True)
    carry = carry_scr[0:1, :] + jnp.sum(member.astype(F32), axis=0, keepdims=True)
    carry_scr[...] = jnp.broadcast_to(carry, carry_scr.shape)
    cnt_ref[...] = jnp.broadcast_to(carry, cnt_ref.shape)
    info = jnp.where(lane == 0, g1, 0.0)
    info = jnp.where(lane == 1, g2, info)
    info = jnp.where(lane == 2, i1, info)
    info = jnp.where(lane == 3, i2, info)
    info = jnp.where(lane == 4, r1, info)
    info = jnp.where(lane == 5, r2, info)
    info_ref[...] = info


def _router(h, w_pad, b_pad):
    tm = TM
    return pl.pallas_call(
        _router_kernel,
        grid=(T // tm,),
        in_specs=[pl.BlockSpec((tm, D), lambda i: (i, 0)), _full(w_pad.shape), _full(b_pad.shape)],
        out_specs=[pl.BlockSpec((tm, LANES_V7X), lambda i: (i, 0)), _full((8, LANES_V7X))],
        out_shape=[jax.ShapeDtypeStruct((T, LANES_V7X), F32),
                   jax.ShapeDtypeStruct((8, LANES_V7X), F32)],
        scratch_shapes=[pltpu.VMEM((8, LANES_V7X), F32)],
        compiler_params=_params(("arbitrary",), 32),
        name="router",
    )(h, w_pad, b_pad)


def _moe_row_copy(h_hbm, x_scr, sem, tok, r):
    return pltpu.make_async_copy(h_hbm.at[pl.ds(tok, 1), :], x_scr.at[pl.ds(r, 1), :], sem)


def _moe_out_copy(acc_scr, y_hbm, sem, row0, u):
    off = pl.multiple_of(u * MOE_UNIT, MOE_UNIT)
    dst = pl.multiple_of(row0 + off, MOE_UNIT)
    return pltpu.make_async_copy(acc_scr.at[pl.ds(off, MOE_UNIT), :],
                                 y_hbm.at[pl.ds(dst, MOE_UNIT), :], sem)


def _moe_kernel(te_ref, row0_ref, nunit_ref, used_ref, pad_ref, dest_ref,
                h_hbm, wg_ref, wu_ref, wd_ref, y_hbm,
                slot_tok, x_scr, acc_scr, gsem, osem):
    t = pl.program_id(0)
    f = pl.program_id(1)
    nf = pl.num_programs(1)

    @pl.when(jnp.logical_and(t == 0, f == 0))
    def _():
        for e in range(NE):
            def clear(s, c):
                slot_tok[s] = 0
                return c
            lax.fori_loop(pad_ref[2 * e], pad_ref[2 * e + 1], clear, 0)

        def place(tok, c):
            slot_tok[dest_ref[2 * tok]] = tok
            slot_tok[dest_ref[2 * tok + 1]] = tok
            return c
        lax.fori_loop(0, T, place, 0, unroll=8)

    @pl.when(t < used_ref[0])
    def _():
        row0 = row0_ref[t]
        nunit = nunit_ref[t]
        nrows = nunit * MOE_UNIT

        @pl.when(f == 0)
        def _():
            def issue_group(r8, c):
                base = pl.multiple_of(r8 * 8, 8)
                for k in range(8):
                    _moe_row_copy(h_hbm, x_scr, gsem, slot_tok[row0 + base + k], base + k).start()
                return c
            lax.fori_loop(0, nunit * (MOE_UNIT // 8), issue_group, 0)

            @pl.when(t > 0)
            def _():
                def finish(u, c):
                    _moe_out_copy(acc_scr, y_hbm, osem, row0_ref[t - 1], u).wait()
                    return c
                lax.fori_loop(0, nunit_ref[t - 1], finish, 0)

            def clear_acc(u, c):
                off = pl.multiple_of(u * MOE_UNIT, MOE_UNIT)
                acc_scr[pl.ds(off, MOE_UNIT), :] = jnp.zeros((MOE_UNIT, D), F32)
                return c
            lax.fori_loop(0, nunit, clear_acc, 0)

            def drain_group(r8, c):
                base = pl.multiple_of(r8 * 8, 8)
                for k in range(8):
                    _moe_row_copy(h_hbm, x_scr, gsem, 0, base + k).wait()
                return c
            lax.fori_loop(0, nunit * (MOE_UNIT // 8), drain_group, 0)

        def chunk(off, rows):
            x = x_scr[pl.ds(off, rows), :]
            hid = _silu(_dot(x, wg_ref[...])) * _dot(x, wu_ref[...])
            acc_scr[pl.ds(off, rows), :] += _dot(hid, wd_ref[...])

        def oct_(c, carry):
            chunk(pl.multiple_of(c * 8 * MOE_UNIT, 8 * MOE_UNIT), 8 * MOE_UNIT)
            return carry
        lax.fori_loop(0, nunit // 8, oct_, 0)

        @pl.when(nunit % 8 >= 4)
        def _():
            chunk(pl.multiple_of((nunit // 8) * 8 * MOE_UNIT, 4 * MOE_UNIT), 4 * MOE_UNIT)

        @pl.when(nunit % 4 >= 2)
        def _():
            chunk(pl.multiple_of((nunit // 4) * 4 * MOE_UNIT, 2 * MOE_UNIT), 2 * MOE_UNIT)

        @pl.when(nunit % 2 == 1)
        def _():
            chunk(pl.multiple_of((nunit - 1) * MOE_UNIT, MOE_UNIT), MOE_UNIT)

        @pl.when(f == nf - 1)
        def _():
            def start(u, carry):
                _moe_out_copy(acc_scr, y_hbm, osem, row0, u).start()
                return carry
            lax.fori_loop(0, nunit, start, 0)

            @pl.when(t == used_ref[0] - 1)
            def _():
                def finish(u, carry):
                    _moe_out_copy(acc_scr, y_hbm, osem, row0, u).wait()
                    return carry
                lax.fori_loop(0, nunit, finish, 0)

    @pl.when(jnp.logical_and(t == pl.num_programs(0) - 1, f == nf - 1))
    def _():
        acc_scr[0:MOE_UNIT, :] = jnp.zeros((MOE_UNIT, D), F32)

        def fill(u, carry):
            cp = _moe_out_copy(acc_scr, y_hbm, osem, u * MOE_UNIT, 0)
            cp.start()
            cp.wait()
            return carry
        lax.fori_loop(used_ref[1], MOE_CAP // MOE_UNIT, fill, 0)


def _moe_experts(h, wg, wu, wd, tile_e, tile_row0, tile_nunit, used, pad, dest):
    nf = FE // MOE_TF

    def fidx(t, f, us):
        return jnp.where(t < us[0], f, nf - 1)

    grid_spec = pltpu.PrefetchScalarGridSpec(
        num_scalar_prefetch=6,
        grid=(MOE_NT, nf),
        in_specs=[pl.BlockSpec(memory_space=pl.ANY),
                  pl.BlockSpec((None, D, MOE_TF), lambda t, f, te, r0, nu, us, pa, de: (te[t], 0, fidx(t, f, us))),
                  pl.BlockSpec((None, D, MOE_TF), lambda t, f, te, r0, nu, us, pa, de: (te[t], 0, fidx(t, f, us))),
                  pl.BlockSpec((None, MOE_TF, D), lambda t, f, te, r0, nu, us, pa, de: (te[t], fidx(t, f, us), 0))],
        out_specs=pl.BlockSpec(memory_space=pl.ANY),
        scratch_shapes=[pltpu.SMEM((MOE_CAP,), jnp.int32),
                        pltpu.VMEM((MOE_TMAX, D), F32),
                        pltpu.VMEM((MOE_TMAX, D), F32),
                        pltpu.SemaphoreType.DMA,
                        pltpu.SemaphoreType.DMA],
    )
    return pl.pallas_call(
        _moe_kernel,
        grid_spec=grid_spec,
        out_shape=jax.ShapeDtypeStruct((MOE_CAP, D), F32),
        compiler_params=_params(("arbitrary", "arbitrary"), 58),
        name="moe_experts",
    )(tile_e, tile_row0, tile_nunit, used, pad, dest, h, wg, wu, wd)


def _cmb_copy(y_hbm, ybuf, sem, buf, slot, k, r):
    return pltpu.make_async_copy(y_hbm.at[pl.ds(slot, 1), :], ybuf.at[buf, k, pl.ds(r, 1), :],
                                 sem.at[buf])


def _combine_kernel(dest_ref, h_ref, info_ref, y_hbm, g_ref, b_ref, o_ref, ybuf, sem):
    tm = h_ref.shape[0]
    i = pl.program_id(0)
    cur = lax.rem(i, 2)

    def gather(tile, buf):
        def issue_group(r8, c):
            base = pl.multiple_of(r8 * 8, 8)
            for k in range(8):
                tok = tile * tm + base + k
                _cmb_copy(y_hbm, ybuf, sem, buf, dest_ref[2 * tok], 0, base + k).start()
                _cmb_copy(y_hbm, ybuf, sem, buf, dest_ref[2 * tok + 1], 1, base + k).start()
            return c
        lax.fori_loop(0, tm // 8, issue_group, 0)

    @pl.when(i == 0)
    def _():
        gather(0, 0)

    @pl.when(i + 1 < pl.num_programs(0))
    def _():
        gather(i + 1, 1 - cur)

    def drain_group(r8, c):
        base = pl.multiple_of(r8 * 8, 8)
        for k in range(8):
            _cmb_copy(y_hbm, ybuf, sem, cur, 0, 0, base + k).wait()
            _cmb_copy(y_hbm, ybuf, sem, cur, 0, 1, base + k).wait()
        return c
    lax.fori_loop(0, tm // 8, drain_group, 0)
    info = info_ref[...]
    moe = info[:, 0:1] * ybuf[cur, 0] + info[:, 1:2] * ybuf[cur, 1]
    o_ref[...] = _layer_norm(ALPHA * h_ref[...] + moe, g_ref[...], b_ref[...])


def _combine(dest, h, info, y_rows, g, b):
    tm = CMB_TM
    grid_spec = pltpu.PrefetchScalarGridSpec(
        num_scalar_prefetch=1,
        grid=(T // tm,),
        in_specs=[pl.BlockSpec((tm, D), lambda i, de: (i, 0)),
                  pl.BlockSpec((tm, LANES_V7X), lambda i, de: (i, 0)),
                  pl.BlockSpec(memory_space=pl.ANY),
                  pl.BlockSpec((1, D), lambda i, de: (0, 0)),
                  pl.BlockSpec((1, D), lambda i, de: (0, 0))],
        out_specs=pl.BlockSpec((tm, D), lambda i, de: (i, 0)),
        scratch_shapes=[pltpu.VMEM((2, 2, tm, D), F32), pltpu.SemaphoreType.DMA((2,))],
    )
    return pl.pallas_call(
        _combine_kernel,
        grid_spec=grid_spec,
        out_shape=jax.ShapeDtypeStruct((T, D), F32),
        compiler_params=_params(("arbitrary",), 32),
        name="moe_combine_ln",
    )(dest, h, info, y_rows, g, b)


def _rope_tables(pos):
    inv_freq = THETA ** (-jnp.arange(0, ROPE, 2, dtype=F32) / ROPE)
    ang = pos[:, None] * inv_freq[None, :]
    cos = jnp.cos(ang).astype(F32)
    sin = jnp.sin(ang).astype(F32)
    zero = jnp.zeros((pos.shape[0], LANES_V7X - ROPE), F32)
    return (jnp.concatenate([cos, cos, zero], axis=1),
            jnp.concatenate([-sin, sin, zero], axis=1))


def _expert_tiles(counts):
    units = (counts + MOE_UNIT - 1) // MOE_UNIT
    seg_unit = jnp.cumsum(units) - units
    ntile = (units + MOE_TUNITS - 1) // MOE_TUNITS
    tile_end = jnp.cumsum(ntile)
    n_used = tile_end[-1]
    tid = jnp.arange(MOE_NT, dtype=jnp.int32)
    e = jnp.minimum(jnp.sum(tid[:, None] >= tile_end[None, :], axis=1), NE - 1).astype(jnp.int32)
    local = tid - (tile_end - ntile)[e]
    nt = jnp.maximum(ntile[e], 1)
    lo = (local * units[e]) // nt
    hi = ((local + 1) * units[e]) // nt
    used = tid < n_used
    last = jnp.maximum(n_used - 1, 0)
    e = jnp.where(used, e, e[last])
    row0 = jnp.where(used, (seg_unit[e] + lo) * MOE_UNIT, 0)
    nunit = jnp.where(used, hi - lo, 0)
    used_info = jnp.stack([n_used, jnp.sum(units)]).astype(jnp.int32)
    seg_row = seg_unit * MOE_UNIT
    pad = jnp.stack([seg_row + counts, seg_row + units * MOE_UNIT], axis=1).reshape(2 * NE)
    return (e.astype(jnp.int32), row0.astype(jnp.int32), nunit.astype(jnp.int32),
            used_info, pad.astype(jnp.int32), seg_row.astype(jnp.int32))


def kernel(x, meta_tokens, attn_w_in, fox_forget_bias, mla_q_norm, mla_kv_norm, mla_w_uq, mla_w_ukv, attn_w_out, pool_w_in, pool_w_group, pool_scale, pool_w_out, ffn_w_gate, ffn_w_up, ffn_w_down, moe_w_router, moe_b_router, moe_w_gate, moe_w_up, moe_w_down, ln_mix_g, ln_mix_b, ln_ffn_g, ln_ffn_b):
    assert x.shape == (1, T, D) and meta_tokens.shape == (NM, D)
    xt = x[0]
    meta = meta_tokens.astype(x.dtype)

    w_in = attn_w_in[0]
    o_kr, o_qf, o_f = QL + KVL, QL + KVL + ROPE, QL + KVL + ROPE + 3 * H * HD
    kr1, kr2 = w_in[:, o_kr:o_kr + ROPE // 2], w_in[:, o_kr + ROPE // 2:o_qf]
    zpad = lambda n: jnp.zeros((D, n), F32)
    w_small = jnp.concatenate([kr2, kr1, zpad(LANES_V7X - ROPE),
                               w_in[:, o_f:o_f + H], zpad(LANES_V7X - H)], axis=1)
    w_fox = w_in[:, o_qf:o_f].astype(BF16)
    w_in = w_in[:, :LAT_COLS]
    wuq = mla_w_uq[0].reshape(QL, H, NOPE + ROPE)
    uq_n, uq_1, uq_2 = wuq[..., :NOPE], wuq[..., NOPE:NOPE + ROPE // 2], wuq[..., NOPE + ROPE // 2:]
    zq = jnp.zeros((QL, H, LANES_V7X - ROPE), F32)
    w_uq = jnp.concatenate([uq_n, uq_1, uq_2, zq], axis=-1).reshape(QL, H * DQ_MLA).astype(BF16)
    w_uqs = jnp.concatenate([uq_2, uq_1, zq], axis=-1).reshape(QL, H * HD).astype(BF16)
    wukv = mla_w_ukv[0].reshape(KVL, H, NOPE + HD)
    w_ukv = jnp.concatenate([wukv[..., :NOPE].reshape(KVL, H * NOPE),
                             wukv[..., NOPE:].reshape(KVL, H * HD)], axis=1).astype(BF16)
    w_ao = attn_w_out[0]
    w_fg, w_fu, w_fd = ffn_w_gate[0], ffn_w_up[0], ffn_w_down[0]
    w_pi, w_pg, w_po = pool_w_in[0], pool_w_group[0], pool_w_out[0]
    qn, kvn = mla_q_norm[0][None, :], mla_kv_norm[0][None, :]
    row = lambda v: v[None, :]

    ctab_t, stab_t = _rope_tables(jnp.arange(NM, NM + T, dtype=F32))
    ctab_m, stab_m = _rope_tables(jnp.arange(NM, dtype=F32))

    fox_scale = HD ** -0.5 * LOG2E
    q_t, k_t, vt_t, fl_t = _mla_proj(xt, w_in, w_small, qn, kvn, w_uq, w_uqs, w_ukv, ctab_t, stab_t, TK)
    q_m, k_m, v_m, _ = _mla_proj(meta, w_in, w_small, qn, kvn, w_uq, w_uqs, w_ukv, ctab_m, stab_m, NM)
    fox_qk, fox_vt = _fox_proj(xt, w_fox, fox_scale)
    fox_m = _matmul(meta, w_fox, NM, H * HD, BF16, scaled_blocks=1, scale=fox_scale, name="fox_proj_meta")

    fl = fl_t[:, :H].T.reshape(H, T // LANES_V7X, LANES_V7X)
    bias_b = jnp.broadcast_to(fox_forget_bias[0][:, None, None], (H, 1, LANES_V7X))
    parts = _forget_cumsum(fl, bias_b).reshape(H, 6, T)
    pos = parts[:, 0:3].transpose(2, 0, 1)
    neg = parts[:, 3:6].transpose(2, 0, 1)
    lanes = lambda pieces: jnp.concatenate(pieces, axis=2).reshape(-1, H * AUG_STRIDE)
    cst = lambda rows, n, v: jnp.full((rows, H, n), v, BF16)
    aug = (lanes([cst(T, 3, 1.0), pos, cst(T, AUG_STRIDE - 6, 0.0)]),
           lanes([neg, cst(T, AUG_STRIDE - 3, 0.0)]),
           lanes([cst(NM, 3, 0.0), cst(NM, 3, -1.0), cst(NM, AUG_STRIDE - 6, 0.0)]))

    ones_rows = jnp.concatenate([jnp.ones((H, 1, NM), BF16), jnp.zeros((H, VROWS - HD - 1, NM), BF16)], axis=1)
    to_vmt = lambda v: jnp.concatenate([v.reshape(NM, H, HD).transpose(1, 2, 0), ones_rows], axis=1)
    o_mla = _attention(q_t, k_t, vt_t, k_m, to_vmt(v_m), 0, DQ_MLA, name="mla_attn")
    o_fox = _attention(fox_qk, fox_qk, fox_vt, fox_m, to_vmt(fox_m[:, 2 * H * HD:]),
                       H, HD, aug, name="fox_attn")
    om_mla = _meta_attention(q_m, k_m, v_m, 0, 0, 0, DQ_MLA, "mla_attn_meta")
    om_fox = _meta_attention(fox_m, fox_m, fox_m, 0, H, 2 * H, HD, "fox_attn_meta")

    g0, b0, g1, b1 = row(ln_mix_g[0]), row(ln_mix_b[0]), row(ln_ffn_g[0]), row(ln_ffn_b[0])
    h_t = _attn_out(o_mla, o_fox, w_ao, xt, g0, b0, TM)
    h_m = _attn_out(om_mla, om_fox, w_ao, meta, g0, b0, NM)
    h_t, h_m = _ffn(h_t, h_m, w_fg, w_fu, w_fd, g1, b1)

    hp_t = _matmul(h_t, w_pi, TM, D, F32, name="pool_in")
    hp_m = _matmul(h_m, w_pi, NM, D, F32, name="pool_in_meta")
    h_t = _pool_mixer(hp_t, hp_m, h_t, w_pg, row(pool_scale[0]), w_po,
                      row(ln_mix_g[1]), row(ln_mix_b[1]))

    w_r = jnp.concatenate([moe_w_router[0], jnp.zeros((D, LANES_V7X - NE), F32)], axis=1)
    b_r = jnp.concatenate([moe_b_router[0], jnp.full((LANES_V7X - NE,), NEG, F32)])[None, :]
    info, cnt = _router(h_t, w_r, b_r)
    counts = cnt[0, :NE].astype(jnp.int32)
    tile_e, tile_row0, tile_nunit, used, pad, seg_row = _expert_tiles(counts)
    experts = info[:, 2:4].astype(jnp.int32)
    ranks = info[:, 4:6].astype(jnp.int32)
    dest = (seg_row[experts] + ranks).reshape(2 * T)
    y_rows = _moe_experts(h_t, moe_w_gate[0], moe_w_up[0], moe_w_down[0],
                          tile_e, tile_row0, tile_nunit, used, pad, dest)
    out = _combine(dest, h_t, info, y_rows, row(ln_ffn_g[1]), row(ln_ffn_b[1]))
    return out[None]
```

```python
import functools
import math

import jax
import jax.numpy as jnp
import numpy as np
from jax import lax
from jax.experimental import pallas as pl
from jax.experimental.pallas import tpu as pltpu

D = 2048
T = 8192
NM = 16
H = 8
QL = 512
KVL = 512
NOPE = 128
ROPE = 64
HD = 128
DQ_MLA = 256
LAT_COLS = QL + KVL + 128
VROWS = HD + 16
THETA = 10000.0
POOL_W = (2, 4, 8, 16)
PG = D // 4
FD = 5632
NE = 8
FE = 7168
ALPHA = 4.0 ** 0.25
LN_EPS = 1e-5
RMS_EPS = 1e-6
NEG = -1e30
LOG2E = 1.4426950408889634

F32 = jnp.float32
BF16 = jnp.bfloat16

LANES_V7X = 128
VMEM_BYTES_V7X = 64 * 1024 * 1024
MIB = 1024 * 1024

TM = 512
TK = 512
ATT_TQ = 2 * TK
FFN_TM = 1024
FFN_CHUNK = 1024
FFN_TF = 512
POOL_TM = 256
MOE_UNIT = 128
MOE_TUNITS = 10
MOE_TMAX = MOE_TUNITS * MOE_UNIT
MOE_TF = 512
MOE_CAP = 2 * T + NE * MOE_UNIT
MOE_NT = MOE_CAP // MOE_TMAX + NE + 1
CMB_TM = 256


def _params(sem, vmem_mib):
    return pltpu.CompilerParams(dimension_semantics=sem, vmem_limit_bytes=vmem_mib * MIB)


def _full(shape):
    n = len(shape)
    return pl.BlockSpec(shape, lambda *_: (0,) * n)


def _resident(shape):
    n = len(shape)
    return pl.BlockSpec(shape, lambda *_: (0,) * n, pipeline_mode=pl.Buffered(1))


def _layer_norm(x, g, b):
    mu = jnp.mean(x, axis=-1, keepdims=True)
    xc = x - mu
    var = jnp.mean(xc * xc, axis=-1, keepdims=True)
    return xc * lax.rsqrt(var + LN_EPS) * g + b


def _rms_norm(x, g):
    return x * lax.rsqrt(jnp.mean(x * x, axis=-1, keepdims=True) + RMS_EPS) * g


def _dot(a, b):
    return jnp.dot(a, b, preferred_element_type=F32)


def _dot_nt(a, b):
    return lax.dot_general(a, b, (((1,), (1,)), ((), ())), preferred_element_type=F32)


def _vt_store(v_ref, h, v):
    v_ref[h, 0, 0:HD, :] = v.T.astype(BF16)
    rows = lax.broadcasted_iota(jnp.int32, (VROWS - HD, v.shape[0]), 0)
    v_ref[h, 0, HD:VROWS, :] = (rows == 0).astype(BF16)


def _mla_proj_kernel(x_ref, win_ref, wsm_ref, qn_ref, kvn_ref, wuq_ref, wuqs_ref, wukv_ref,
                     c_ref, s_ref, q_ref, k_ref, v_ref, fl_ref, *, qscale, transposed_v):
    x = x_ref[...]
    lat = _dot(x, win_ref[...])
    small = _dot(x, wsm_ref[...])
    cq = lat[:, 0:QL]
    ckv = lat[:, QL:QL + KVL]
    kr = lat[:, QL + KVL:LAT_COLS]
    krs = small[:, 0:LANES_V7X]
    fl_ref[...] = small[:, LANES_V7X:]
    cos = c_ref[...]
    sin = s_ref[...]
    qn = _rms_norm(cq, qn_ref[...]).astype(BF16)
    qpre = _dot(qn, wuq_ref[...])
    qsw = _dot(qn, wuqs_ref[...])
    for h in range(H):
        nope = qpre[:, DQ_MLA * h:DQ_MLA * h + NOPE]
        rot = (qpre[:, DQ_MLA * h + NOPE:DQ_MLA * (h + 1)] * cos
               + qsw[:, HD * h:HD * (h + 1)] * sin)
        q_ref[:, DQ_MLA * h:DQ_MLA * h + NOPE] = (nope * qscale).astype(BF16)
        q_ref[:, DQ_MLA * h + NOPE:DQ_MLA * (h + 1)] = (rot * qscale).astype(BF16)
    kvn = _rms_norm(ckv, kvn_ref[...]).astype(BF16)
    kv = _dot(kvn, wukv_ref[...])
    krot = (kr * cos + krs * sin).astype(BF16)
    for h in range(H):
        k_ref[:, DQ_MLA * h:DQ_MLA * h + NOPE] = kv[:, HD * h:HD * (h + 1)].astype(BF16)
        k_ref[:, DQ_MLA * h + NOPE:DQ_MLA * (h + 1)] = krot
    if transposed_v:
        for h in range(H):
            _vt_store(v_ref, h, kv[:, (H + h) * HD:(H + h + 1) * HD])
    else:
        v_ref[...] = kv[:, H * HD:].astype(BF16)


def _mla_proj(x, w_in, wsm, qn, kvn, wuq, wuqs, wukv, ctab, stab, tm):
    m = x.shape[0]
    qscale = (NOPE + ROPE) ** -0.5 * LOG2E
    row = lambda w: pl.BlockSpec((tm, w), lambda i: (i, 0))
    transposed_v = tm == TK
    if transposed_v:
        v_spec = pl.BlockSpec((H, 1, VROWS, TK), lambda i: (0, i, 0, 0))
        v_shape = jax.ShapeDtypeStruct((H, m // TK, VROWS, TK), BF16)
    else:
        v_spec, v_shape = row(H * HD), jax.ShapeDtypeStruct((m, H * HD), BF16)
    lat_spec = pl.BlockSpec((D, LAT_COLS), lambda i: (0, 0), pipeline_mode=pl.Buffered(1))
    return pl.pallas_call(
        functools.partial(_mla_proj_kernel, qscale=qscale, transposed_v=transposed_v),
        grid=(m // tm,),
        in_specs=[row(D), lat_spec, _full(wsm.shape), _full(qn.shape), _full(kvn.shape),
                  _resident(wuq.shape), _resident(wuqs.shape), _resident(wukv.shape),
                  row(LANES_V7X), row(LANES_V7X)],
        out_specs=[row(H * DQ_MLA), row(H * DQ_MLA), v_spec, row(LANES_V7X)],
        out_shape=[jax.ShapeDtypeStruct((m, H * DQ_MLA), BF16),
                   jax.ShapeDtypeStruct((m, H * DQ_MLA), BF16),
                   v_shape,
                   jax.ShapeDtypeStruct((m, LANES_V7X), F32)],
        compiler_params=_params(("parallel",), 56),
        name="mla_proj",
    )(x, w_in, wsm, qn, kvn, wuq, wuqs, wukv, ctab, stab)


def _mm_kernel(x_ref, w_ref, o_ref, *, scaled_blocks, scale):
    y = _dot(x_ref[...], w_ref[...])
    if scaled_blocks:
        y = jnp.where(pl.program_id(1) < scaled_blocks, y * scale, y)
    o_ref[...] = y.astype(o_ref.dtype)


def _matmul(x, w, tm, tn, out_dtype, scaled_blocks=0, scale=1.0, name="matmul"):
    m, k = x.shape
    n = w.shape[1]
    return pl.pallas_call(
        functools.partial(_mm_kernel, scaled_blocks=scaled_blocks, scale=scale),
        grid=(m // tm, n // tn),
        in_specs=[pl.BlockSpec((tm, k), lambda i, j: (i, 0)),
                  _resident((k, n)) if tn == n else pl.BlockSpec((k, tn), lambda i, j: (0, j))],
        out_specs=pl.BlockSpec((tm, tn), lambda i, j: (i, j)),
        out_shape=jax.ShapeDtypeStruct((m, n), out_dtype),
        compiler_params=_params(("parallel", "parallel"), 48),
        name=name,
    )(x, w)


def _fox_proj_kernel(x_ref, w_ref, qk_ref, vt_ref, *, scale):
    j = pl.program_id(1)
    n = H * HD
    part = lambda c: _dot(x_ref[...].astype(BF16), w_ref[:, c * n:(c + 1) * n])

    @pl.when(j == 0)
    def _():
        qk_ref[...] = (part(0) * scale).astype(BF16)

    @pl.when(j == 1)
    def _():
        qk_ref[...] = part(1).astype(BF16)

    @pl.when(j == 2)
    def _():
        y = part(2)
        for h in range(H):
            _vt_store(vt_ref, h, y[:, h * HD:(h + 1) * HD])


def _fox_proj(x, w, scale):
    n = H * HD
    return pl.pallas_call(
        functools.partial(_fox_proj_kernel, scale=scale),
        grid=(T // TK, 3),
        in_specs=[pl.BlockSpec((TK, D), lambda i, j: (i, 0)),
                  _resident((D, 3 * n))],
        out_specs=[pl.BlockSpec((TK, n), lambda i, j: (i, jnp.minimum(j, 1))),
                   pl.BlockSpec((H, 1, VROWS, TK), lambda i, j: (0, i, 0, 0))],
        out_shape=[jax.ShapeDtypeStruct((T, 2 * n), BF16),
                   jax.ShapeDtypeStruct((H, T // TK, VROWS, TK), BF16)],
        compiler_params=_params(("parallel", "arbitrary"), 48),
        name="fox_proj",
    )(x, w)


def _split3(a):
    hi = a.astype(BF16)
    r = a - hi.astype(F32)
    mid = r.astype(BF16)
    lo = (r - mid.astype(F32)).astype(BF16)
    return hi, mid, lo


def _forget_cumsum_kernel(fl_ref, b_ref, o_ref):
    z = fl_ref[...] + b_ref[...]
    logf = jnp.minimum(z, 0.0) - jnp.log(1.0 + jnp.exp(-jnp.abs(z)))
    n = LANES_V7X
    ri = lax.broadcasted_iota(jnp.int32, (n, n), 0)
    ci = lax.broadcasted_iota(jnp.int32, (n, n), 1)
    upper = (ri <= ci).astype(BF16)
    ones = jnp.ones((n, n), BF16)
    within = jnp.zeros(logf.shape, F32)
    total = jnp.zeros(logf.shape, F32)
    for part in _split3(logf):
        within += _dot(part, upper)
        total += _dot(part, ones)
    r = logf.shape[0]
    lower = (lax.broadcasted_iota(jnp.int32, (r, r), 1)
             < lax.broadcasted_iota(jnp.int32, (r, r), 0)).astype(BF16)
    before = jnp.zeros(logf.shape, F32)
    for part in _split3(total):
        before += _dot(lower, part)
    c2 = (within + before) * LOG2E
    for p, part in enumerate(_split3(c2)):
        o_ref[p] = part
        o_ref[3 + p] = -part


def _forget_cumsum(fl_t, bias_b):
    r = T // LANES_V7X
    return pl.pallas_call(
        _forget_cumsum_kernel,
        grid=(H,),
        in_specs=[pl.BlockSpec((None, r, LANES_V7X), lambda h: (h, 0, 0)),
                  pl.BlockSpec((None, 1, LANES_V7X), lambda h: (h, 0, 0))],
        out_specs=pl.BlockSpec((None, 6, r, LANES_V7X), lambda h: (h, 0, 0, 0)),
        out_shape=jax.ShapeDtypeStruct((H, 6, r, LANES_V7X), BF16),
        compiler_params=_params(("parallel",), 32),
        name="forget_cumsum",
    )(fl_t, bias_b)


ATT_NH = 2
AUG_STRIDE = 16


def _attn_kernel(*refs, fox, dq):
    if fox:
        (q_ref, k_ref, vt_ref, km_ref, vmt_ref, qa_ref, ka_ref, kam_ref, o_ref,
         m_scr, acc_scr, s0_scr, s1_scr, mb0_scr, mb1_scr, qf_scr) = refs
    else:
        (q_ref, k_ref, vt_ref, km_ref, vmt_ref, o_ref,
         m_scr, acc_scr, s0_scr, s1_scr, mb0_scr, mb1_scr) = refs
    hp = pl.program_id(0)
    i = pl.program_id(1)
    m_scr[...] = jnp.full(m_scr.shape, -jnp.inf, F32)
    acc_scr[...] = jnp.zeros(acc_scr.shape, F32)
    if fox:
        lane = lax.broadcasted_iota(jnp.int32, qa_ref.shape, 1)
        for g in range(ATT_NH):
            lo = (hp * ATT_NH + g) * AUG_STRIDE
            own = jnp.logical_and(lane >= lo, lane < lo + AUG_STRIDE)
            qf_scr[g, :, 0:dq] = q_ref[:, g * dq:(g + 1) * dq]
            qf_scr[g, :, dq:2 * dq] = jnp.where(own, qa_ref[...], jnp.zeros_like(qa_ref))

    def query(g, rows=slice(None)):
        return qf_scr[g, rows, :] if fox else q_ref[rows, g * dq:(g + 1) * dq]

    def update(g, st, mb, vt, qs=None):
        m_ref = m_scr.at[g] if qs is None else m_scr.at[g, :, qs]
        a_ref = acc_scr.at[g] if qs is None else acc_scr.at[g, :, qs]
        m_prev = m_ref[...]
        m_new = jnp.maximum(m_prev, mb)
        alpha = jnp.exp2(m_prev - m_new)
        p = jnp.exp2(st - m_new)
        a_ref[...] = alpha * a_ref[...] + _dot(vt, p.astype(BF16))
        m_ref[...] = m_new

    def stage1(slot, j, heads=range(ATT_NH)):
        s_ref, mb_ref = slots[slot]
        off = pl.multiple_of(j * TK, TK)
        for g in heads:
            kb = k_ref[pl.ds(off, TK), g * dq:(g + 1) * dq]
            if fox:
                kb = jnp.concatenate([kb, ka_ref[pl.ds(off, TK), :]], axis=1)
            st = _dot_nt(kb, query(g))
            s_ref[g] = st
            mb_ref[g] = jnp.max(st, axis=0, keepdims=True)

    def stage2(slot, j, diag_key0=None, heads=range(ATT_NH)):
        s_ref, mb_ref = slots[slot]
        for g in heads:
            st = s_ref[g]
            mb = mb_ref[g]
            if diag_key0 is not None:
                keys = lax.broadcasted_iota(jnp.int32, st.shape, 0) + diag_key0
                qrys = lax.broadcasted_iota(jnp.int32, st.shape, 1)
                st = jnp.where(keys <= qrys, st, NEG)
                mb = jnp.max(st, axis=0, keepdims=True)
            update(g, st, mb, vt_ref[g, j])

    slots = ((s0_scr, mb0_scr), (s1_scr, mb1_scr))
    stage1(0, 0)

    def body(jj, carry):
        j = 2 * jj
        for g in range(ATT_NH):
            stage1(1, j + 1, heads=(g,))
            stage2(0, j, heads=(g,))
        for g in range(ATT_NH):
            stage1(0, j + 2, heads=(g,))
            stage2(1, j + 1, heads=(g,))
        return carry

    lax.fori_loop(0, i, body, 0)
    stage2(0, 2 * i, diag_key0=0)
    late = slice(TK, ATT_TQ)
    off = pl.multiple_of((2 * i + 1) * TK, TK)
    for g in range(ATT_NH):
        kb = k_ref[pl.ds(off, TK), g * dq:(g + 1) * dq]
        if fox:
            kb = jnp.concatenate([kb, ka_ref[pl.ds(off, TK), :]], axis=1)
        st = _dot_nt(kb, query(g, late))
        keys = lax.broadcasted_iota(jnp.int32, st.shape, 0)
        qrys = lax.broadcasted_iota(jnp.int32, st.shape, 1)
        st = jnp.where(keys <= qrys, st, NEG)
        update(g, st, jnp.max(st, axis=0, keepdims=True), vt_ref[g, 2 * i + 1], late)

    for g in range(ATT_NH):
        kb = km_ref[:, g * dq:(g + 1) * dq]
        if fox:
            kb = jnp.concatenate([kb, kam_ref[...]], axis=1)
        st = _dot_nt(kb, query(g))
        update(g, st, jnp.max(st, axis=0, keepdims=True), vmt_ref[g])
        acc = acc_scr[g]
        o = (acc[0:HD] / acc[HD:HD + 1]).T
        o_ref[:, g * HD:(g + 1) * HD] = o.astype(o_ref.dtype)


def _attention(q, k, vt, km, vmt, k_col0, dq, aug=None, name="attn"):
    fox = aug is not None
    nh, tq = ATT_NH, ATT_TQ
    in_specs = [pl.BlockSpec((tq, nh * dq), lambda h, i: (i, h)),
                pl.BlockSpec((T, nh * dq), lambda h, i: (0, k_col0 // nh + h)),
                pl.BlockSpec((nh, T // TK, VROWS, TK), lambda h, i: (h, 0, 0, 0)),
                pl.BlockSpec((NM, nh * dq), lambda h, i: (0, k_col0 // nh + h)),
                pl.BlockSpec((nh, VROWS, NM), lambda h, i: (h, 0, 0))]
    args = [q, k, vt, km, vmt]
    scratch = [pltpu.VMEM((nh, 1, tq), F32), pltpu.VMEM((nh, VROWS, tq), F32),
               pltpu.VMEM((nh, TK, tq), F32), pltpu.VMEM((nh, TK, tq), F32),
               pltpu.VMEM((nh, 1, tq), F32), pltpu.VMEM((nh, 1, tq), F32)]
    if fox:
        in_specs += [pl.BlockSpec((tq, LANES_V7X), lambda h, i: (i, 0)),
                     _full((T, LANES_V7X)), _full((NM, LANES_V7X))]
        args += list(aug)
        scratch.append(pltpu.VMEM((nh, tq, 2 * dq), BF16))
    return pl.pallas_call(
        functools.partial(_attn_kernel, fox=fox, dq=dq),
        grid=(H // nh, T // tq),
        in_specs=in_specs,
        out_specs=pl.BlockSpec((tq, nh * HD), lambda h, i: (i, h)),
        out_shape=jax.ShapeDtypeStruct((T, H * HD), BF16),
        scratch_shapes=scratch,
        compiler_params=_params(("parallel", "arbitrary"), 56),
        name=name,
    )(*args)


def _meta_attn_kernel(q_ref, k_ref, v_ref, o_ref):
    s = _dot_nt(q_ref[...], k_ref[...])
    rows = lax.broadcasted_iota(jnp.int32, s.shape, 0)
    cols = lax.broadcasted_iota(jnp.int32, s.shape, 1)
    s = jnp.where(cols <= rows, s, NEG)
    p = jnp.exp2(s - jnp.max(s, axis=1, keepdims=True))
    o = _dot(p.astype(BF16), v_ref[...]) / jnp.sum(p, axis=1, keepdims=True)
    o_ref[...] = o.astype(o_ref.dtype)


def _meta_attention(q, k, v, q_col0, k_col0, v_col0, dq, name):
    return pl.pallas_call(
        _meta_attn_kernel,
        grid=(H,),
        in_specs=[pl.BlockSpec((NM, dq), lambda h: (0, q_col0 + h)),
                  pl.BlockSpec((NM, dq), lambda h: (0, k_col0 + h)),
                  pl.BlockSpec((NM, HD), lambda h: (0, v_col0 + h))],
        out_specs=pl.BlockSpec((NM, HD), lambda h: (0, h)),
        out_shape=jax.ShapeDtypeStruct((NM, H * HD), BF16),
        compiler_params=_params(("parallel",), 32),
        name=name,
    )(q, k, v)


def _attn_out_kernel(om_ref, of_ref, w_ref, h_ref, g_ref, b_ref, o_ref):
    m = _dot(om_ref[...], w_ref[0:H * HD, :]) + _dot(of_ref[...], w_ref[H * HD:, :])
    o_ref[...] = _layer_norm(ALPHA * h_ref[...] + m, g_ref[...], b_ref[...])


def _attn_out(om, of, w, h, g, b, tm):
    m = h.shape[0]
    row = lambda w_: pl.BlockSpec((tm, w_), lambda i: (i, 0))
    return pl.pallas_call(
        _attn_out_kernel,
        grid=(m // tm,),
        in_specs=[row(H * HD), row(H * HD), _resident(w.shape), row(D), _full(g.shape), _full(b.shape)],
        out_specs=row(D),
        out_shape=jax.ShapeDtypeStruct((m, D), F32),
        compiler_params=_params(("parallel",), 48),
        name="attn_out_ln",
    )(om, of, w, h, g, b)


def _silu(x):
    return x / (1.0 + jnp.exp(-x))


def _ffn_kernel(h_ref, hm_ref, wg_ref, wu_ref, wd_ref, g_ref, b_ref, o_ref, om_ref):
    i = pl.program_id(0)
    f = pl.program_id(1)
    last = pl.num_programs(1) - 1
    rows = h_ref.shape[0]
    chunk = FFN_CHUNK

    def chain(x):
        hid = _silu(_dot(x, wg_ref[...])) * _dot(x, wu_ref[...])
        return _dot(hid, wd_ref[...])

    @pl.when(f == 0)
    def _():
        o_ref[...] = jnp.zeros(o_ref.shape, F32)

    @pl.when(i == 0)
    def _():
        @pl.when(f == 0)
        def _():
            om_ref[...] = jnp.zeros(om_ref.shape, F32)
        part = chain(jnp.concatenate([h_ref[0:chunk, :], hm_ref[...]], axis=0))
        o_ref[0:chunk, :] += part[0:chunk]
        om_ref[...] += part[chunk:]

        @pl.when(f == last)
        def _():
            om_ref[...] = _layer_norm(ALPHA * hm_ref[...] + om_ref[...], g_ref[...], b_ref[...])

    @pl.when(i != 0)
    def _():
        o_ref[0:chunk, :] += chain(h_ref[0:chunk, :])

    for c in range(1, rows // chunk):
        sl = slice(c * chunk, (c + 1) * chunk)
        o_ref[sl, :] += chain(h_ref[sl, :])

    @pl.when(f == last)
    def _():
        o_ref[...] = _layer_norm(ALPHA * h_ref[...] + o_ref[...], g_ref[...], b_ref[...])


def _ffn(h, hm, wg, wu, wd, g, b):
    tm = FFN_TM
    once = pl.Buffered(1)
    return pl.pallas_call(
        _ffn_kernel,
        grid=(T // tm, FD // FFN_TF),
        in_specs=[pl.BlockSpec((tm, D), lambda i, f: (i, 0), pipeline_mode=once),
                  pl.BlockSpec((NM, D), lambda i, f: (0, 0)),
                  pl.BlockSpec((D, FFN_TF), lambda i, f: (0, f)),
                  pl.BlockSpec((D, FFN_TF), lambda i, f: (0, f)),
                  pl.BlockSpec((FFN_TF, D), lambda i, f: (f, 0)),
                  pl.BlockSpec((1, D), lambda i, f: (0, 0)),
                  pl.BlockSpec((1, D), lambda i, f: (0, 0))],
        out_specs=[pl.BlockSpec((tm, D), lambda i, f: (i, 0)),
                   pl.BlockSpec((NM, D), lambda i, f: (0, 0))],
        out_shape=[jax.ShapeDtypeStruct((T, D), F32), jax.ShapeDtypeStruct((NM, D), F32)],
        compiler_params=_params(("arbitrary", "arbitrary"), 58),
        name="ffn_ln",
    )(h, hm, wg, wu, wd, g, b)


def _pool_kernel(hp_ref, prev_ref, meta_ref, h_ref, wg_ref, sc_ref, wo_ref, g_ref, b_ref, o_ref,
                 ext_scr, z_scr):
    tm = hp_ref.shape[0]
    halo = jnp.where(pl.program_id(0) == 0, meta_ref[...], prev_ref[...])
    ext_scr[0:NM, :] = halo
    ext_scr[NM:NM + tm, :] = hp_ref[...]
    for gi, w in enumerate(POOL_W):
        cols = slice(PG * gi, PG * (gi + 1))
        cur = ext_scr[NM:NM + tm, cols]
        tot = cur
        for j in range(1, w):
            tot = tot + ext_scr[NM - j:NM - j + tm, cols]
        diff = tot * (1.0 / w) - cur
        y = _dot(diff, wg_ref[gi]) * sc_ref[:, cols]
        z_scr[:, cols] = y
    m = _dot(z_scr[...], wo_ref[...])
    o_ref[...] = _layer_norm(ALPHA * h_ref[...] + m, g_ref[...], b_ref[...])


def _pool_mixer(hp, hp_meta, h, wgroup, scale, wout, g, b):
    tm = POOL_TM
    row = pl.BlockSpec((tm, D), lambda i: (i, 0))
    prev = pl.BlockSpec((NM, D), lambda i: (jnp.maximum(i * (tm // NM) - 1, 0), 0))
    return pl.pallas_call(
        _pool_kernel,
        grid=(T // tm,),
        in_specs=[row, prev, _full(hp_meta.shape), row, _resident(wgroup.shape), _full(scale.shape),
                  _resident(wout.shape), _full(g.shape), _full(b.shape)],
        out_specs=row,
        out_shape=jax.ShapeDtypeStruct((T, D), F32),
        scratch_shapes=[pltpu.VMEM((NM + tm, D), F32), pltpu.VMEM((tm, D), F32)],
        compiler_params=_params(("parallel",), 48),
        name="pool_ln",
    )(hp, hp, hp_meta, h, wgroup, scale, wout, g, b)


def _router_kernel(h_ref, w_ref, b_ref, info_ref, cnt_ref, carry_scr):
    tm = h_ref.shape[0]

    @pl.when(pl.program_id(0) == 0)
    def _():
        carry_scr[...] = jnp.zeros(carry_scr.shape, F32)

    x = h_ref[...]
    w = w_ref[...]
    xh, wh = x.astype(BF16), w.astype(BF16)
    xl = (x - xh.astype(F32)).astype(BF16)
    wl = (w - wh.astype(F32)).astype(BF16)
    logits = _dot(xh, wh) + _dot(xh, wl) + _dot(xl, wh) + b_ref[...]
    lane = lax.broadcasted_iota(jnp.int32, logits.shape, 1).astype(F32)
    nolane = float(LANES_V7X)
    m1 = jnp.max(logits, axis=1, keepdims=True)
    i1 = jnp.min(jnp.where(logits == m1, lane, nolane), axis=1, keepdims=True)
    rest = jnp.where(lane == i1, -jnp.inf, logits)
    m2 = jnp.max(rest, axis=1, keepdims=True)
    i2 = jnp.min(jnp.where(rest == m2, lane, nolane), axis=1, keepdims=True)
    e = jnp.exp(m2 - m1)
    g1 = 1.0 / (1.0 + e)
    g2 = e / (1.0 + e)
    hit1 = lane == i1
    hit2 = lane == i2
    member = jnp.logical_or(hit1, hit2).astype(BF16)
    earlier = (lax.broadcasted_iota(jnp.int32, (tm, tm), 1)
               < lax.broadcasted_iota(jnp.int32, (tm, tm), 0)).astype(BF16)
    before = _dot(earlier, member) + carry_scr[0:1, :]
    r1 = jnp.sum(jnp.where(hit1, before, 0.0), axis=1, keepdims=True)
    r2 = jnp.sum(jnp.where(hit2, before, 0.0), axis=1, keepdims=True)
    carry = carry_scr[0:1, :] + jnp.sum(member.astype(F32), axis=0, keepdims=True)
    carry_scr[...] = jnp.broadcast_to(carry, carry_scr.shape)
    cnt_ref[...] = jnp.broadcast_to(carry, cnt_ref.shape)
    info = jnp.where(lane == 0, g1, 0.0)
    info = jnp.where(lane == 1, g2, info)
    info = jnp.where(lane == 2, i1, info)
    info = jnp.where(lane == 3, i2, info)
    info = jnp.where(lane == 4, r1, info)
    info = jnp.where(lane == 5, r2, info)
    info_ref[...] = info


def _router(h, w_pad, b_pad):
    tm = TM
    return pl.pallas_call(
        _router_kernel,
        grid=(T // tm,),
        in_specs=[pl.BlockSpec((tm, D), lambda i: (i, 0)), _full(w_pad.shape), _full(b_pad.shape)],
        out_specs=[pl.BlockSpec((tm, LANES_V7X), lambda i: (i, 0)), _full((8, LANES_V7X))],
        out_shape=[jax.ShapeDtypeStruct((T, LANES_V7X), F32),
                   jax.ShapeDtypeStruct((8, LANES_V7X), F32)],
        scratch_shapes=[pltpu.VMEM((8, LANES_V7X), F32)],
        compiler_params=_params(("arbitrary",), 32),
        name="router",
    )(h, w_pad, b_pad)


def _moe_row_copy(h_hbm, x_scr, sem, tok, r):
    return pltpu.make_async_copy(h_hbm.at[pl.ds(tok, 1), :], x_scr.at[pl.ds(r, 1), :], sem)


def _moe_out_copy(acc_scr, y_hbm, sem, row0, u):
    off = pl.multiple_of(u * MOE_UNIT, MOE_UNIT)
    dst = pl.multiple_of(row0 + off, MOE_UNIT)
    return pltpu.make_async_copy(acc_scr.at[pl.ds(off, MOE_UNIT), :],
                                 y_hbm.at[pl.ds(dst, MOE_UNIT), :], sem)


def _moe_kernel(te_ref, row0_ref, nunit_ref, used_ref, pad_ref, dest_ref,
                h_hbm, wg_ref, wu_ref, wd_ref, y_hbm,
                slot_tok, x_scr, acc_scr, gsem, osem):
    t = pl.program_id(0)
    f = pl.program_id(1)
    nf = pl.num_programs(1)

    @pl.when(jnp.logical_and(t == 0, f == 0))
    def _():
        for e in range(NE):
            def clear(s, c):
                slot_tok[s] = 0
                return c
            lax.fori_loop(pad_ref[2 * e], pad_ref[2 * e + 1], clear, 0)

        def place(tok, c):
            slot_tok[dest_ref[2 * tok]] = tok
            slot_tok[dest_ref[2 * tok + 1]] = tok
            return c
        lax.fori_loop(0, T, place, 0, unroll=8)

    @pl.when(t < used_ref[0])
    def _():
        row0 = row0_ref[t]
        nunit = nunit_ref[t]
        nrows = nunit * MOE_UNIT

        @pl.when(f == 0)
        def _():
            def issue_group(r8, c):
                base = pl.multiple_of(r8 * 8, 8)
                for k in range(8):
                    _moe_row_copy(h_hbm, x_scr, gsem, slot_tok[row0 + base + k], base + k).start()
                return c
            lax.fori_loop(0, nunit * (MOE_UNIT // 8), issue_group, 0)

            @pl.when(t > 0)
            def _():
                def finish(u, c):
                    _moe_out_copy(acc_scr, y_hbm, osem, row0_ref[t - 1], u).wait()
                    return c
                lax.fori_loop(0, nunit_ref[t - 1], finish, 0)

            def clear_acc(u, c):
                off = pl.multiple_of(u * MOE_UNIT, MOE_UNIT)
                acc_scr[pl.ds(off, MOE_UNIT), :] = jnp.zeros((MOE_UNIT, D), F32)
                return c
            lax.fori_loop(0, nunit, clear_acc, 0)

            def drain_group(r8, c):
                base = pl.multiple_of(r8 * 8, 8)
                for k in range(8):
                    _moe_row_copy(h_hbm, x_scr, gsem, 0, base + k).wait()
                return c
            lax.fori_loop(0, nunit * (MOE_UNIT // 8), drain_group, 0)

        def chunk(off, rows):
            x = x_scr[pl.ds(off, rows), :]
            hid = _silu(_dot(x, wg_ref[...])) * _dot(x, wu_ref[...])
            acc_scr[pl.ds(off, rows), :] += _dot(hid, wd_ref[...])

        def oct_(c, carry):
            chunk(pl.multiple_of(c * 8 * MOE_UNIT, 8 * MOE_UNIT), 8 * MOE_UNIT)
            return carry
        lax.fori_loop(0, nunit // 8, oct_, 0)

        @pl.when(nunit % 8 >= 4)
        def _():
            chunk(pl.multiple_of((nunit // 8) * 8 * MOE_UNIT, 4 * MOE_UNIT), 4 * MOE_UNIT)

        @pl.when(nunit % 4 >= 2)
        def _():
            chunk(pl.multiple_of((nunit // 4) * 4 * MOE_UNIT, 2 * MOE_UNIT), 2 * MOE_UNIT)

        @pl.when(nunit % 2 == 1)
        def _():
            chunk(pl.multiple_of((nunit - 1) * MOE_UNIT, MOE_UNIT), MOE_UNIT)

        @pl.when(f == nf - 1)
        def _():
            def start(u, carry):
                _moe_out_copy(acc_scr, y_hbm, osem, row0, u).start()
                return carry
            lax.fori_loop(0, nunit, start, 0)

            @pl.when(t == used_ref[0] - 1)
            def _():
                def finish(u, carry):
                    _moe_out_copy(acc_scr, y_hbm, osem, row0, u).wait()
                    return carry
                lax.fori_loop(0, nunit, finish, 0)

    @pl.when(jnp.logical_and(t == pl.num_programs(0) - 1, f == nf - 1))
    def _():
        acc_scr[0:MOE_UNIT, :] = jnp.zeros((MOE_UNIT, D), F32)

        def fill(u, carry):
            cp = _moe_out_copy(acc_scr, y_hbm, osem, u * MOE_UNIT, 0)
            cp.start()
            cp.wait()
            return carry
        lax.fori_loop(used_ref[1], MOE_CAP // MOE_UNIT, fill, 0)


def _moe_experts(h, wg, wu, wd, tile_e, tile_row0, tile_nunit, used, pad, dest):
    nf = FE // MOE_TF

    def fidx(t, f, us):
        return jnp.where(t < us[0], f, nf - 1)

    grid_spec = pltpu.PrefetchScalarGridSpec(
        num_scalar_prefetch=6,
        grid=(MOE_NT, nf),
        in_specs=[pl.BlockSpec(memory_space=pl.ANY),
                  pl.BlockSpec((None, D, MOE_TF), lambda t, f, te, r0, nu, us, pa, de: (te[t], 0, fidx(t, f, us))),
                  pl.BlockSpec((None, D, MOE_TF), lambda t, f, te, r0, nu, us, pa, de: (te[t], 0, fidx(t, f, us))),
                  pl.BlockSpec((None, MOE_TF, D), lambda t, f, te, r0, nu, us, pa, de: (te[t], fidx(t, f, us), 0))],
        out_specs=pl.BlockSpec(memory_space=pl.ANY),
        scratch_shapes=[pltpu.SMEM((MOE_CAP,), jnp.int32),
                        pltpu.VMEM((MOE_TMAX, D), F32),
                        pltpu.VMEM((MOE_TMAX, D), F32),
                        pltpu.SemaphoreType.DMA,
                        pltpu.SemaphoreType.DMA],
    )
    return pl.pallas_call(
        _moe_kernel,
        grid_spec=grid_spec,
        out_shape=jax.ShapeDtypeStruct((MOE_CAP, D), F32),
        compiler_params=_params(("arbitrary", "arbitrary"), 58),
        name="moe_experts",
    )(tile_e, tile_row0, tile_nunit, used, pad, dest, h, wg, wu, wd)


def _cmb_copy(y_hbm, ybuf, sem, buf, slot, k, r):
    return pltpu.make_async_copy(y_hbm.at[pl.ds(slot, 1), :], ybuf.at[buf, k, pl.ds(r, 1), :],
                                 sem.at[buf])


def _combine_kernel(dest_ref, h_ref, info_ref, y_hbm, g_ref, b_ref, o_ref, ybuf, sem):
    tm = h_ref.shape[0]
    i = pl.program_id(0)
    cur = lax.rem(i, 2)

    def gather(tile, buf):
        def issue_group(r8, c):
            base = pl.multiple_of(r8 * 8, 8)
            for k in range(8):
                tok = tile * tm + base + k
                _cmb_copy(y_hbm, ybuf, sem, buf, dest_ref[2 * tok], 0, base + k).start()
                _cmb_copy(y_hbm, ybuf, sem, buf, dest_ref[2 * tok + 1], 1, base + k).start()
            return c
        lax.fori_loop(0, tm // 8, issue_group, 0)

    @pl.when(i == 0)
    def _():
        gather(0, 0)

    @pl.when(i + 1 < pl.num_programs(0))
    def _():
        gather(i + 1, 1 - cur)

    def drain_group(r8, c):
        base = pl.multiple_of(r8 * 8, 8)
        for k in range(8):
            _cmb_copy(y_hbm, ybuf, sem, cur, 0, 0, base + k).wait()
            _cmb_copy(y_hbm, ybuf, sem, cur, 0, 1, base + k).wait()
        return c
    lax.fori_loop(0, tm // 8, drain_group, 0)
    info = info_ref[...]
    moe = info[:, 0:1] * ybuf[cur, 0] + info[:, 1:2] * ybuf[cur, 1]
    o_ref[...] = _layer_norm(ALPHA * h_ref[...] + moe, g_ref[...], b_ref[...])


def _combine(dest, h, info, y_rows, g, b):
    tm = CMB_TM
    grid_spec = pltpu.PrefetchScalarGridSpec(
        num_scalar_prefetch=1,
        grid=(T // tm,),
        in_specs=[pl.BlockSpec((tm, D), lambda i, de: (i, 0)),
                  pl.BlockSpec((tm, LANES_V7X), lambda i, de: (i, 0)),
                  pl.BlockSpec(memory_space=pl.ANY),
                  pl.BlockSpec((1, D), lambda i, de: (0, 0)),
                  pl.BlockSpec((1, D), lambda i, de: (0, 0))],
        out_specs=pl.BlockSpec((tm, D), lambda i, de: (i, 0)),
        scratch_shapes=[pltpu.VMEM((2, 2, tm, D), F32), pltpu.SemaphoreType.DMA((2,))],
    )
    return pl.pallas_call(
        _combine_kernel,
        grid_spec=grid_spec,
        out_shape=jax.ShapeDtypeStruct((T, D), F32),
        compiler_params=_params(("arbitrary",), 32),
        name="moe_combine_ln",
    )(dest, h, info, y_rows, g, b)


def _rope_tables(pos):
    inv_freq = THETA ** (-jnp.arange(0, ROPE, 2, dtype=F32) / ROPE)
    ang = pos[:, None] * inv_freq[None, :]
    cos = jnp.cos(ang).astype(F32)
    sin = jnp.sin(ang).astype(F32)
    zero = jnp.zeros((pos.shape[0], LANES_V7X - ROPE), F32)
    return (jnp.concatenate([cos, cos, zero], axis=1),
            jnp.concatenate([-sin, sin, zero], axis=1))


def _expert_tiles(counts):
    units = (counts + MOE_UNIT - 1) // MOE_UNIT
    seg_unit = jnp.cumsum(units) - units
    ntile = (units + MOE_TUNITS - 1) // MOE_TUNITS
    tile_end = jnp.cumsum(ntile)
    n_used = tile_end[-1]
    tid = jnp.arange(MOE_NT, dtype=jnp.int32)
    e = jnp.minimum(jnp.sum(tid[:, None] >= tile_end[None, :], axis=1), NE - 1).astype(jnp.int32)
    local = tid - (tile_end - ntile)[e]
    nt = jnp.maximum(ntile[e], 1)
    lo = (local * units[e]) // nt
    hi = ((local + 1) * units[e]) // nt
    used = tid < n_used
    last = jnp.maximum(n_used - 1, 0)
    e = jnp.where(used, e, e[last])
    row0 = jnp.where(used, (seg_unit[e] + lo) * MOE_UNIT, 0)
    nunit = jnp.where(used, hi - lo, 0)
    used_info = jnp.stack([n_used, jnp.sum(units)]).astype(jnp.int32)
    seg_row = seg_unit * MOE_UNIT
    pad = jnp.stack([seg_row + counts, seg_row + units * MOE_UNIT], axis=1).reshape(2 * NE)
    return (e.astype(jnp.int32), row0.astype(jnp.int32), nunit.astype(jnp.int32),
            used_info, pad.astype(jnp.int32), seg_row.astype(jnp.int32))


def kernel(x, meta_tokens, attn_w_in, fox_forget_bias, mla_q_norm, mla_kv_norm, mla_w_uq, mla_w_ukv, attn_w_out, pool_w_in, pool_w_group, pool_scale, pool_w_out, ffn_w_gate, ffn_w_up, ffn_w_down, moe_w_router, moe_b_router, moe_w_gate, moe_w_up, moe_w_down, ln_mix_g, ln_mix_b, ln_ffn_g, ln_ffn_b):
    assert x.shape == (1, T, D) and meta_tokens.shape == (NM, D)
    xt = x[0]
    meta = meta_tokens.astype(x.dtype)

    w_in = attn_w_in[0]
    o_kr, o_qf, o_f = QL + KVL, QL + KVL + ROPE, QL + KVL + ROPE + 3 * H * HD
    kr1, kr2 = w_in[:, o_kr:o_kr + ROPE // 2], w_in[:, o_kr + ROPE // 2:o_qf]
    zpad = lambda n: jnp.zeros((D, n), F32)
    w_small = jnp.concatenate([kr2, kr1, zpad(LANES_V7X - ROPE),
                               w_in[:, o_f:o_f + H], zpad(LANES_V7X - H)], axis=1)
    w_fox = w_in[:, o_qf:o_f].astype(BF16)
    w_in = w_in[:, :LAT_COLS]
    wuq = mla_w_uq[0].reshape(QL, H, NOPE + ROPE)
    uq_n, uq_1, uq_2 = wuq[..., :NOPE], wuq[..., NOPE:NOPE + ROPE // 2], wuq[..., NOPE + ROPE // 2:]
    zq = jnp.zeros((QL, H, LANES_V7X - ROPE), F32)
    w_uq = jnp.concatenate([uq_n, uq_1, uq_2, zq], axis=-1).reshape(QL, H * DQ_MLA).astype(BF16)
    w_uqs = jnp.concatenate([uq_2, uq_1, zq], axis=-1).reshape(QL, H * HD).astype(BF16)
    wukv = mla_w_ukv[0].reshape(KVL, H, NOPE + HD)
    w_ukv = jnp.concatenate([wukv[..., :NOPE].reshape(KVL, H * NOPE),
                             wukv[..., NOPE:].reshape(KVL, H * HD)], axis=1).astype(BF16)
    w_ao = attn_w_out[0]
    w_fg, w_fu, w_fd = ffn_w_gate[0], ffn_w_up[0], ffn_w_down[0]
    w_pi, w_pg, w_po = pool_w_in[0], pool_w_group[0], pool_w_out[0]
    qn, kvn = mla_q_norm[0][None, :], mla_kv_norm[0][None, :]
    row = lambda v: v[None, :]

    ctab_t, stab_t = _rope_tables(jnp.arange(NM, NM + T, dtype=F32))
    ctab_m, stab_m = _rope_tables(jnp.arange(NM, dtype=F32))

    fox_scale = HD ** -0.5 * LOG2E
    q_t, k_t, vt_t, fl_t = _mla_proj(xt, w_in, w_small, qn, kvn, w_uq, w_uqs, w_ukv, ctab_t, stab_t, TK)
    q_m, k_m, v_m, _ = _mla_proj(meta, w_in, w_small, qn, kvn, w_uq, w_uqs, w_ukv, ctab_m, stab_m, NM)
    fox_qk, fox_vt = _fox_proj(xt, w_fox, fox_scale)
    fox_m = _matmul(meta, w_fox, NM, H * HD, BF16, scaled_blocks=1, scale=fox_scale, name="fox_proj_meta")

    fl = fl_t[:, :H].T.reshape(H, T // LANES_V7X, LANES_V7X)
    bias_b = jnp.broadcast_to(fox_forget_bias[0][:, None, None], (H, 1, LANES_V7X))
    parts = _forget_cumsum(fl, bias_b).reshape(H, 6, T)
    pos = parts[:, 0:3].transpose(2, 0, 1)
    neg = parts[:, 3:6].transpose(2, 0, 1)
    lanes = lambda pieces: jnp.concatenate(pieces, axis=2).reshape(-1, H * AUG_STRIDE)
    cst = lambda rows, n, v: jnp.full((rows, H, n), v, BF16)
    aug = (lanes([cst(T, 3, 1.0), pos, cst(T, AUG_STRIDE - 6, 0.0)]),
           lanes([neg, cst(T, AUG_STRIDE - 3, 0.0)]),
           lanes([cst(NM, 3, 0.0), cst(NM, 3, -1.0), cst(NM, AUG_STRIDE - 6, 0.0)]))

    ones_rows = jnp.concatenate([jnp.ones((H, 1, NM), BF16), jnp.zeros((H, VROWS - HD - 1, NM), BF16)], axis=1)
    to_vmt = lambda v: jnp.concatenate([v.reshape(NM, H, HD).transpose(1, 2, 0), ones_rows], axis=1)
    o_mla = _attention(q_t, k_t, vt_t, k_m, to_vmt(v_m), 0, DQ_MLA, name="mla_attn")
    o_fox = _attention(fox_qk, fox_qk, fox_vt, fox_m, to_vmt(fox_m[:, 2 * H * HD:]),
                       H, HD, aug, name="fox_attn")
    om_mla = _meta_attention(q_m, k_m, v_m, 0, 0, 0, DQ_MLA, "mla_attn_meta")
    om_fox = _meta_attention(fox_m, fox_m, fox_m, 0, H, 2 * H, HD, "fox_attn_meta")

    g0, b0, g1, b1 = row(ln_mix_g[0]), row(ln_mix_b[0]), row(ln_ffn_g[0]), row(ln_ffn_b[0])
    h_t = _attn_out(o_mla, o_fox, w_ao, xt, g0, b0, TM)
    h_m = _attn_out(om_mla, om_fox, w_ao, meta, g0, b0, NM)
    h_t, h_m = _ffn(h_t, h_m, w_fg, w_fu, w_fd, g1, b1)

    hp_t = _matmul(h_t, w_pi, TM, D, F32, name="pool_in")
    hp_m = _matmul(h_m, w_pi, NM, D, F32, name="pool_in_meta")
    h_t = _pool_mixer(hp_t, hp_m, h_t, w_pg, row(pool_scale[0]), w_po,
                      row(ln_mix_g[1]), row(ln_mix_b[1]))

    w_r = jnp.concatenate([moe_w_router[0], jnp.zeros((D, LANES_V7X - NE), F32)], axis=1)
    b_r = jnp.concatenate([moe_b_router[0], jnp.full((LANES_V7X - NE,), NEG, F32)])[None, :]
    info, cnt = _router(h_t, w_r, b_r)
    counts = cnt[0, :NE].astype(jnp.int32)
    tile_e, tile_row0, tile_nunit, used, pad, seg_row = _expert_tiles(counts)
    experts = info[:, 2:4].astype(jnp.int32)
    ranks = info[:, 4:6].astype(jnp.int32)
    dest = (seg_row[experts] + ranks).reshape(2 * T)
    y_rows = _moe_experts(h_t, moe_w_gate[0], moe_w_up[0], moe_w_down[0],
                          tile_e, tile_row0, tile_nunit, used, pad, dest)
    out = _combine(dest, h_t, info, y_rows, row(ln_ffn_g[1]), row(ln_ffn_b[1]))
    return out[None]
```

```python
import functools

import jax
import jax.numpy as jnp
from jax import lax
from jax.experimental import pallas as pl
from jax.experimental.pallas import tpu as pltpu

D = 2048
T = 8192
NM = 16
H = 8
QL = 512
KVL = 512
NOPE = 128
ROPE = 64
HD = 128
DQ_MLA = 256
LAT_COLS = QL + KVL + 2 * ROPE
VROWS = HD + 16
THETA = 10000.0
POOL_W = (2, 4, 8, 16)
PG = D // 4
FD = 5632
NE = 8
FE = 7168
ALPHA = 4.0 ** 0.25
LN_EPS = 1e-5
RMS_EPS = 1e-6
NEG = -1e30
LOG2E = 1.4426950408889634

F32 = jnp.float32
BF16 = jnp.bfloat16

LANES_V7X = 128
MIB = 1024 * 1024

TM = 512
TK = 512
ATT_TQ = 2 * TK
FFN_TM = 1024
FFN_CHUNK = 1024
FFN_TF = 512
POOL_TM = 256
MOE_UNIT = 128
MOE_TUNITS = 10
MOE_TMAX = MOE_TUNITS * MOE_UNIT
MOE_TF = 512
MOE_CAP = 2 * T + NE * MOE_UNIT
MOE_NT = MOE_CAP // MOE_TMAX + NE + 1
CMB_TM = 256


def _params(sem, vmem_mib):
    return pltpu.CompilerParams(dimension_semantics=sem, vmem_limit_bytes=vmem_mib * MIB)


def _full(shape):
    n = len(shape)
    return pl.BlockSpec(shape, lambda *_: (0,) * n)


def _resident(shape):
    n = len(shape)
    return pl.BlockSpec(shape, lambda *_: (0,) * n, pipeline_mode=pl.Buffered(1))


def _layer_norm(x, g, b):
    mu = jnp.mean(x, axis=-1, keepdims=True)
    xc = x - mu
    var = jnp.mean(xc * xc, axis=-1, keepdims=True)
    return xc * lax.rsqrt(var + LN_EPS) * g + b


def _rms_norm(x, g):
    return x * lax.rsqrt(jnp.mean(x * x, axis=-1, keepdims=True) + RMS_EPS) * g


def _dot(a, b):
    return jnp.dot(a, b, preferred_element_type=F32)


def _dot_nt(a, b):
    return lax.dot_general(a, b, (((1,), (1,)), ((), ())), preferred_element_type=F32)


def _vt_store(v_ref, h, v):
    v_ref[h, 0, 0:HD, :] = v.T.astype(BF16)
    rows = lax.broadcasted_iota(jnp.int32, (VROWS - HD, v.shape[0]), 0)
    v_ref[h, 0, HD:VROWS, :] = (rows == 0).astype(BF16)


def _mla_proj_kernel(x_ref, win_ref, wsm_ref, qn_ref, kvn_ref, wuq_ref, wuqs_ref, wukv_ref,
                     c_ref, s_ref, q_ref, k_ref, v_ref, fl_ref, *, qscale, transposed_v):
    x = x_ref[...]
    lat = _dot(x, win_ref[...])
    small = _dot(x, wsm_ref[...])
    cq = lat[:, 0:QL]
    ckv = lat[:, QL:QL + KVL]
    kr = lat[:, QL + KVL:LAT_COLS]
    krs = small[:, 0:LANES_V7X]
    fl_ref[...] = small[:, LANES_V7X:]
    cos = c_ref[...]
    sin = s_ref[...]
    qn = _rms_norm(cq, qn_ref[...]).astype(BF16)
    qpre = _dot(qn, wuq_ref[...])
    qsw = _dot(qn, wuqs_ref[...])
    for h in range(H):
        nope = qpre[:, DQ_MLA * h:DQ_MLA * h + NOPE]
        rot = (qpre[:, DQ_MLA * h + NOPE:DQ_MLA * (h + 1)] * cos
               + qsw[:, HD * h:HD * (h + 1)] * sin)
        q_ref[:, DQ_MLA * h:DQ_MLA * h + NOPE] = (nope * qscale).astype(BF16)
        q_ref[:, DQ_MLA * h + NOPE:DQ_MLA * (h + 1)] = (rot * qscale).astype(BF16)
    kvn = _rms_norm(ckv, kvn_ref[...]).astype(BF16)
    kv = _dot(kvn, wukv_ref[...])
    krot = (kr * cos + krs * sin).astype(BF16)
    for h in range(H):
        k_ref[:, DQ_MLA * h:DQ_MLA * h + NOPE] = kv[:, HD * h:HD * (h + 1)].astype(BF16)
        k_ref[:, DQ_MLA * h + NOPE:DQ_MLA * (h + 1)] = krot
    if transposed_v:
        for h in range(H):
            _vt_store(v_ref, h, kv[:, (H + h) * HD:(H + h + 1) * HD])
    else:
        v_ref[...] = kv[:, H * HD:].astype(BF16)


def _mla_proj(x, w_in, wsm, qn, kvn, wuq, wuqs, wukv, ctab, stab, tm):
    m = x.shape[0]
    qscale = (NOPE + ROPE) ** -0.5 * LOG2E
    row = lambda w: pl.BlockSpec((tm, w), lambda i: (i, 0))
    transposed_v = tm == TK
    if transposed_v:
        v_spec = pl.BlockSpec((H, 1, VROWS, TK), lambda i: (0, i, 0, 0))
        v_shape = jax.ShapeDtypeStruct((H, m // TK, VROWS, TK), BF16)
    else:
        v_spec, v_shape = row(H * HD), jax.ShapeDtypeStruct((m, H * HD), BF16)
    lat_spec = pl.BlockSpec((D, LAT_COLS), lambda i: (0, 0), pipeline_mode=pl.Buffered(1))
    return pl.pallas_call(
        functools.partial(_mla_proj_kernel, qscale=qscale, transposed_v=transposed_v),
        grid=(m // tm,),
        in_specs=[row(D), lat_spec, _full(wsm.shape), _full(qn.shape), _full(kvn.shape),
                  _resident(wuq.shape), _resident(wuqs.shape), _resident(wukv.shape),
                  row(LANES_V7X), row(LANES_V7X)],
        out_specs=[row(H * DQ_MLA), row(H * DQ_MLA), v_spec, row(LANES_V7X)],
        out_shape=[jax.ShapeDtypeStruct((m, H * DQ_MLA), BF16),
                   jax.ShapeDtypeStruct((m, H * DQ_MLA), BF16),
                   v_shape,
                   jax.ShapeDtypeStruct((m, LANES_V7X), F32)],
        compiler_params=_params(("parallel",), 56),
        name="mla_proj",
    )(x, w_in, wsm, qn, kvn, wuq, wuqs, wukv, ctab, stab)


def _mm_kernel(x_ref, w_ref, o_ref, *, scaled_blocks, scale):
    y = _dot(x_ref[...], w_ref[...])
    if scaled_blocks:
        y = jnp.where(pl.program_id(1) < scaled_blocks, y * scale, y)
    o_ref[...] = y.astype(o_ref.dtype)


def _matmul(x, w, tm, tn, out_dtype, scaled_blocks=0, scale=1.0, name="matmul"):
    m, k = x.shape
    n = w.shape[1]
    return pl.pallas_call(
        functools.partial(_mm_kernel, scaled_blocks=scaled_blocks, scale=scale),
        grid=(m // tm, n // tn),
        in_specs=[pl.BlockSpec((tm, k), lambda i, j: (i, 0)),
                  _resident((k, n)) if tn == n else pl.BlockSpec((k, tn), lambda i, j: (0, j))],
        out_specs=pl.BlockSpec((tm, tn), lambda i, j: (i, j)),
        out_shape=jax.ShapeDtypeStruct((m, n), out_dtype),
        compiler_params=_params(("parallel", "parallel"), 48),
        name=name,
    )(x, w)


def _fox_proj_kernel(x_ref, w_ref, qk_ref, vt_ref, *, scale):
    j = pl.program_id(1)
    n = H * HD
    part = lambda c: _dot(x_ref[...].astype(BF16), w_ref[:, c * n:(c + 1) * n])

    @pl.when(j == 0)
    def _():
        qk_ref[...] = (part(0) * scale).astype(BF16)

    @pl.when(j == 1)
    def _():
        qk_ref[...] = part(1).astype(BF16)

    @pl.when(j == 2)
    def _():
        y = part(2)
        for h in range(H):
            _vt_store(vt_ref, h, y[:, h * HD:(h + 1) * HD])


def _fox_proj(x, w, scale):
    n = H * HD
    return pl.pallas_call(
        functools.partial(_fox_proj_kernel, scale=scale),
        grid=(T // TK, 3),
        in_specs=[pl.BlockSpec((TK, D), lambda i, j: (i, 0)),
                  _resident((D, 3 * n))],
        out_specs=[pl.BlockSpec((TK, n), lambda i, j: (i, jnp.minimum(j, 1))),
                   pl.BlockSpec((H, 1, VROWS, TK), lambda i, j: (0, i, 0, 0))],
        out_shape=[jax.ShapeDtypeStruct((T, 2 * n), BF16),
                   jax.ShapeDtypeStruct((H, T // TK, VROWS, TK), BF16)],
        compiler_params=_params(("parallel", "arbitrary"), 48),
        name="fox_proj",
    )(x, w)


def _split3(a):
    hi = a.astype(BF16)
    r = a - hi.astype(F32)
    mid = r.astype(BF16)
    lo = (r - mid.astype(F32)).astype(BF16)
    return hi, mid, lo


def _forget_cumsum_kernel(fl_ref, b_ref, o_ref):
    z = fl_ref[...] + b_ref[...]
    logf = jnp.minimum(z, 0.0) - jnp.log(1.0 + jnp.exp(-jnp.abs(z)))
    n = LANES_V7X
    ri = lax.broadcasted_iota(jnp.int32, (n, n), 0)
    ci = lax.broadcasted_iota(jnp.int32, (n, n), 1)
    upper = (ri <= ci).astype(BF16)
    ones = jnp.ones((n, n), BF16)
    within = jnp.zeros(logf.shape, F32)
    total = jnp.zeros(logf.shape, F32)
    for part in _split3(logf):
        within += _dot(part, upper)
        total += _dot(part, ones)
    r = logf.shape[0]
    lower = (lax.broadcasted_iota(jnp.int32, (r, r), 1)
             < lax.broadcasted_iota(jnp.int32, (r, r), 0)).astype(BF16)
    before = jnp.zeros(logf.shape, F32)
    for part in _split3(total):
        before += _dot(lower, part)
    c2 = (within + before) * LOG2E
    for p, part in enumerate(_split3(c2)):
        o_ref[p] = part
        o_ref[3 + p] = -part


def _forget_cumsum(fl_t, bias_b):
    r = T // LANES_V7X
    return pl.pallas_call(
        _forget_cumsum_kernel,
        grid=(H,),
        in_specs=[pl.BlockSpec((None, r, LANES_V7X), lambda h: (h, 0, 0)),
                  pl.BlockSpec((None, 1, LANES_V7X), lambda h: (h, 0, 0))],
        out_specs=pl.BlockSpec((None, 6, r, LANES_V7X), lambda h: (h, 0, 0, 0)),
        out_shape=jax.ShapeDtypeStruct((H, 6, r, LANES_V7X), BF16),
        compiler_params=_params(("parallel",), 32),
        name="forget_cumsum",
    )(fl_t, bias_b)


ATT_NH = 2
AUG_STRIDE = 16


def _attn_kernel(*refs, fox, dq):
    if fox:
        (q_ref, k_ref, vt_ref, km_ref, vmt_ref, qa_ref, ka_ref, kam_ref, o_ref,
         m_scr, acc_scr, s0_scr, s1_scr, mb0_scr, mb1_scr, qf_scr) = refs
    else:
        (q_ref, k_ref, vt_ref, km_ref, vmt_ref, o_ref,
         m_scr, acc_scr, s0_scr, s1_scr, mb0_scr, mb1_scr) = refs
    hp = pl.program_id(0)
    i = pl.program_id(1)
    m_scr[...] = jnp.full(m_scr.shape, -jnp.inf, F32)
    acc_scr[...] = jnp.zeros(acc_scr.shape, F32)
    if fox:
        lane = lax.broadcasted_iota(jnp.int32, qa_ref.shape, 1)
        for g in range(ATT_NH):
            lo = (hp * ATT_NH + g) * AUG_STRIDE
            own = jnp.logical_and(lane >= lo, lane < lo + AUG_STRIDE)
            qf_scr[g, :, 0:dq] = q_ref[:, g * dq:(g + 1) * dq]
            qf_scr[g, :, dq:2 * dq] = jnp.where(own, qa_ref[...], jnp.zeros_like(qa_ref))

    def query(g):
        return qf_scr[g] if fox else q_ref[:, g * dq:(g + 1) * dq]

    def update(g, st, mb, vt):
        m_prev = m_scr[g]
        m_new = jnp.maximum(m_prev, mb)
        alpha = jnp.exp2(m_prev - m_new)
        p = jnp.exp2(st - m_new)
        acc_scr[g] = alpha * acc_scr[g] + _dot(vt, p.astype(BF16))
        m_scr[g] = m_new

    def stage1(slot, j, heads=range(ATT_NH)):
        s_ref, mb_ref = slots[slot]
        off = pl.multiple_of(j * TK, TK)
        for g in heads:
            kb = k_ref[pl.ds(off, TK), g * dq:(g + 1) * dq]
            if fox:
                kb = jnp.concatenate([kb, ka_ref[pl.ds(off, TK), :]], axis=1)
            st = _dot_nt(kb, query(g))
            s_ref[g] = st
            mb_ref[g] = jnp.max(st, axis=0, keepdims=True)

    def stage2(slot, j, diag_key0=None, heads=range(ATT_NH)):
        s_ref, mb_ref = slots[slot]
        for g in heads:
            st = s_ref[g]
            mb = mb_ref[g]
            if diag_key0 is not None:
                keys = lax.broadcasted_iota(jnp.int32, st.shape, 0) + diag_key0
                qrys = lax.broadcasted_iota(jnp.int32, st.shape, 1)
                st = jnp.where(keys <= qrys, st, NEG)
                mb = jnp.max(st, axis=0, keepdims=True)
            update(g, st, mb, vt_ref[g, j])

    slots = ((s0_scr, mb0_scr), (s1_scr, mb1_scr))
    stage1(0, 0)

    def body(jj, carry):
        j = 2 * jj
        for g in range(ATT_NH):
            stage1(1, j + 1, heads=(g,))
            stage2(0, j, heads=(g,))
        for g in range(ATT_NH):
            stage1(0, j + 2, heads=(g,))
            stage2(1, j + 1, heads=(g,))
        return carry

    lax.fori_loop(0, i, body, 0)
    stage1(1, 2 * i + 1)
    stage2(0, 2 * i, diag_key0=0)
    stage2(1, 2 * i + 1, diag_key0=TK)

    for g in range(ATT_NH):
        kb = km_ref[:, g * dq:(g + 1) * dq]
        if fox:
            kb = jnp.concatenate([kb, kam_ref[...]], axis=1)
        st = _dot_nt(kb, query(g))
        update(g, st, jnp.max(st, axis=0, keepdims=True), vmt_ref[g])
        acc = acc_scr[g]
        o = (acc[0:HD] / acc[HD:HD + 1]).T
        o_ref[:, g * HD:(g + 1) * HD] = o.astype(o_ref.dtype)


def _attention(q, k, vt, km, vmt, k_col0, dq, aug=None, name="attn"):
    fox = aug is not None
    nh, tq = ATT_NH, ATT_TQ
    in_specs = [pl.BlockSpec((tq, nh * dq), lambda h, i: (i, h)),
                pl.BlockSpec((T, nh * dq), lambda h, i: (0, k_col0 // nh + h)),
                pl.BlockSpec((nh, T // TK, VROWS, TK), lambda h, i: (h, 0, 0, 0)),
                pl.BlockSpec((NM, nh * dq), lambda h, i: (0, k_col0 // nh + h)),
                pl.BlockSpec((nh, VROWS, NM), lambda h, i: (h, 0, 0))]
    args = [q, k, vt, km, vmt]
    scratch = [pltpu.VMEM((nh, 1, tq), F32), pltpu.VMEM((nh, VROWS, tq), F32),
               pltpu.VMEM((nh, TK, tq), F32), pltpu.VMEM((nh, TK, tq), F32),
               pltpu.VMEM((nh, 1, tq), F32), pltpu.VMEM((nh, 1, tq), F32)]
    if fox:
        in_specs += [pl.BlockSpec((tq, LANES_V7X), lambda h, i: (i, 0)),
                     _full((T, LANES_V7X)), _full((NM, LANES_V7X))]
        args += list(aug)
        scratch.append(pltpu.VMEM((nh, tq, 2 * dq), BF16))
    return pl.pallas_call(
        functools.partial(_attn_kernel, fox=fox, dq=dq),
        grid=(H // nh, T // tq),
        in_specs=in_specs,
        out_specs=pl.BlockSpec((tq, nh * HD), lambda h, i: (i, h)),
        out_shape=jax.ShapeDtypeStruct((T, H * HD), BF16),
        scratch_shapes=scratch,
        compiler_params=_params(("parallel", "arbitrary"), 56),
        name=name,
    )(*args)


def _meta_attn_kernel(q_ref, k_ref, v_ref, o_ref):
    s = _dot_nt(q_ref[...], k_ref[...])
    rows = lax.broadcasted_iota(jnp.int32, s.shape, 0)
    cols = lax.broadcasted_iota(jnp.int32, s.shape, 1)
    s = jnp.where(cols <= rows, s, NEG)
    p = jnp.exp2(s - jnp.max(s, axis=1, keepdims=True))
    o = _dot(p.astype(BF16), v_ref[...]) / jnp.sum(p, axis=1, keepdims=True)
    o_ref[...] = o.astype(o_ref.dtype)


def _meta_attention(q, k, v, q_col0, k_col0, v_col0, dq, name):
    return pl.pallas_call(
        _meta_attn_kernel,
        grid=(H,),
        in_specs=[pl.BlockSpec((NM, dq), lambda h: (0, q_col0 + h)),
                  pl.BlockSpec((NM, dq), lambda h: (0, k_col0 + h)),
                  pl.BlockSpec((NM, HD), lambda h: (0, v_col0 + h))],
        out_specs=pl.BlockSpec((NM, HD), lambda h: (0, h)),
        out_shape=jax.ShapeDtypeStruct((NM, H * HD), BF16),
        compiler_params=_params(("parallel",), 32),
        name=name,
    )(q, k, v)


def _attn_out_kernel(om_ref, of_ref, w_ref, h_ref, g_ref, b_ref, o_ref):
    m = _dot(om_ref[...], w_ref[0:H * HD, :]) + _dot(of_ref[...], w_ref[H * HD:, :])
    o_ref[...] = _layer_norm(ALPHA * h_ref[...] + m, g_ref[...], b_ref[...])


def _attn_out(om, of, w, h, g, b, tm):
    m = h.shape[0]
    row = lambda w_: pl.BlockSpec((tm, w_), lambda i: (i, 0))
    return pl.pallas_call(
        _attn_out_kernel,
        grid=(m // tm,),
        in_specs=[row(H * HD), row(H * HD), _resident(w.shape), row(D), _full(g.shape), _full(b.shape)],
        out_specs=row(D),
        out_shape=jax.ShapeDtypeStruct((m, D), F32),
        compiler_params=_params(("parallel",), 48),
        name="attn_out_ln",
    )(om, of, w, h, g, b)


def _silu(x):
    return x / (1.0 + jnp.exp(-x))


def _ffn_kernel(h_ref, hm_ref, wg_ref, wu_ref, wd_ref, g_ref, b_ref, o_ref, om_ref):
    i = pl.program_id(0)
    f = pl.program_id(1)
    last = pl.num_programs(1) - 1
    rows = h_ref.shape[0]
    chunk = FFN_CHUNK

    def chain(x):
        hid = _silu(_dot(x, wg_ref[...])) * _dot(x, wu_ref[...])
        return _dot(hid, wd_ref[...])

    @pl.when(f == 0)
    def _():
        o_ref[...] = jnp.zeros(o_ref.shape, F32)

    @pl.when(i == 0)
    def _():
        @pl.when(f == 0)
        def _():
            om_ref[...] = jnp.zeros(om_ref.shape, F32)
        part = chain(jnp.concatenate([h_ref[0:chunk, :], hm_ref[...]], axis=0))
        o_ref[0:chunk, :] += part[0:chunk]
        om_ref[...] += part[chunk:]

        @pl.when(f == last)
        def _():
            om_ref[...] = _layer_norm(ALPHA * hm_ref[...] + om_ref[...], g_ref[...], b_ref[...])

    @pl.when(i != 0)
    def _():
        o_ref[0:chunk, :] += chain(h_ref[0:chunk, :])

    for c in range(1, rows // chunk):
        sl = slice(c * chunk, (c + 1) * chunk)
        o_ref[sl, :] += chain(h_ref[sl, :])

    @pl.when(f == last)
    def _():
        o_ref[...] = _layer_norm(ALPHA * h_ref[...] + o_ref[...], g_ref[...], b_ref[...])


def _ffn(h, hm, wg, wu, wd, g, b):
    tm = FFN_TM
    once = pl.Buffered(1)
    return pl.pallas_call(
        _ffn_kernel,
        grid=(T // tm, FD // FFN_TF),
        in_specs=[pl.BlockSpec((tm, D), lambda i, f: (i, 0), pipeline_mode=once),
                  pl.BlockSpec((NM, D), lambda i, f: (0, 0)),
                  pl.BlockSpec((D, FFN_TF), lambda i, f: (0, f)),
                  pl.BlockSpec((D, FFN_TF), lambda i, f: (0, f)),
                  pl.BlockSpec((FFN_TF, D), lambda i, f: (f, 0)),
                  pl.BlockSpec((1, D), lambda i, f: (0, 0)),
                  pl.BlockSpec((1, D), lambda i, f: (0, 0))],
        out_specs=[pl.BlockSpec((tm, D), lambda i, f: (i, 0)),
                   pl.BlockSpec((NM, D), lambda i, f: (0, 0))],
        out_shape=[jax.ShapeDtypeStruct((T, D), F32), jax.ShapeDtypeStruct((NM, D), F32)],
        compiler_params=_params(("arbitrary", "arbitrary"), 58),
        name="ffn_ln",
    )(h, hm, wg, wu, wd, g, b)


def _pool_kernel(hp_ref, prev_ref, meta_ref, h_ref, wg_ref, sc_ref, wo_ref, g_ref, b_ref, o_ref,
                 ext_scr, z_scr):
    tm = hp_ref.shape[0]
    halo = jnp.where(pl.program_id(0) == 0, meta_ref[...], prev_ref[...])
    ext_scr[0:NM, :] = halo
    ext_scr[NM:NM + tm, :] = hp_ref[...]
    for gi, w in enumerate(POOL_W):
        cols = slice(PG * gi, PG * (gi + 1))
        cur = ext_scr[NM:NM + tm, cols]
        tot = cur
        for j in range(1, w):
            tot = tot + ext_scr[NM - j:NM - j + tm, cols]
        diff = tot * (1.0 / w) - cur
        y = _dot(diff, wg_ref[gi]) * sc_ref[:, cols]
        z_scr[:, cols] = y
    m = _dot(z_scr[...], wo_ref[...])
    o_ref[...] = _layer_norm(ALPHA * h_ref[...] + m, g_ref[...], b_ref[...])


def _pool_mixer(hp, hp_meta, h, wgroup, scale, wout, g, b):
    tm = POOL_TM
    row = pl.BlockSpec((tm, D), lambda i: (i, 0))
    prev = pl.BlockSpec((NM, D), lambda i: (jnp.maximum(i * (tm // NM) - 1, 0), 0))
    return pl.pallas_call(
        _pool_kernel,
        grid=(T // tm,),
        in_specs=[row, prev, _full(hp_meta.shape), row, _resident(wgroup.shape), _full(scale.shape),
                  _resident(wout.shape), _full(g.shape), _full(b.shape)],
        out_specs=row,
        out_shape=jax.ShapeDtypeStruct((T, D), F32),
        scratch_shapes=[pltpu.VMEM((NM + tm, D), F32), pltpu.VMEM((tm, D), F32)],
        compiler_params=_params(("parallel",), 48),
        name="pool_ln",
    )(hp, hp, hp_meta, h, wgroup, scale, wout, g, b)


def _router_kernel(h_ref, w_ref, b_ref, info_ref, cnt_ref, carry_scr):
    tm = h_ref.shape[0]

    @pl.when(pl.program_id(0) == 0)
    def _():
        carry_scr[...] = jnp.zeros(carry_scr.shape, F32)

    x = h_ref[...]
    w = w_ref[...]
    xh, wh = x.astype(BF16), w.astype(BF16)
    xl = (x - xh.astype(F32)).astype(BF16)
    wl = (w - wh.astype(F32)).astype(BF16)
    logits = _dot(xh, wh) + _dot(xh, wl) + _dot(xl, wh) + b_ref[...]
    lane = lax.broadcasted_iota(jnp.int32, logits.shape, 1).astype(F32)
    nolane = float(LANES_V7X)
    m1 = jnp.max(logits, axis=1, keepdims=True)
    i1 = jnp.min(jnp.where(logits == m1, lane, nolane), axis=1, keepdims=True)
    rest = jnp.where(lane == i1, -jnp.inf, logits)
    m2 = jnp.max(rest, axis=1, keepdims=True)
    i2 = jnp.min(jnp.where(rest == m2, lane, nolane), axis=1, keepdims=True)
    e = jnp.exp(m2 - m1)
    g1 = 1.0 / (1.0 + e)
    g2 = e / (1.0 + e)
    hit1 = lane == i1
    hit2 = lane == i2
    member = jnp.logical_or(hit1, hit2).astype(BF16)
    earlier = (lax.broadcasted_iota(jnp.int32, (tm, tm), 1)
               < lax.broadcasted_iota(jnp.int32, (tm, tm), 0)).astype(BF16)
    before = _dot(earlier, member) + carry_scr[0:1, :]
    r1 = jnp.sum(jnp.where(hit1, before, 0.0), axis=1, keepdims=True)
    r2 = jnp.sum(jnp.where(hit2, before, 0.0), axis=1, keepdims=True)
    carry = carry_scr[0:1, :] + jnp.sum(member.astype(F32), axis=0, keepdims=True)
    carry_scr[...] = jnp.broadcast_to(carry, carry_scr.shape)
    cnt_ref[...] = jnp.broadcast_to(carry, cnt_ref.shape)
    info = jnp.where(lane == 0, g1, 0.0)
    info = jnp.where(lane == 1, g2, info)
    info = jnp.where(lane == 2, i1, info)
    info = jnp.where(lane == 3, i2, info)
    info = jnp.where(lane == 4, r1, info)
    info = jnp.where(lane == 5, r2, info)
    info_ref[...] = info


def _router(h, w_pad, b_pad):
    tm = TM
    return pl.pallas_call(
        _router_kernel,
        grid=(T // tm,),
        in_specs=[pl.BlockSpec((tm, D), lambda i: (i, 0)), _full(w_pad.shape), _full(b_pad.shape)],
        out_specs=[pl.BlockSpec((tm, LANES_V7X), lambda i: (i, 0)), _full((8, LANES_V7X))],
        out_shape=[jax.ShapeDtypeStruct((T, LANES_V7X), F32),
                   jax.ShapeDtypeStruct((8, LANES_V7X), F32)],
        scratch_shapes=[pltpu.VMEM((8, LANES_V7X), F32)],
        compiler_params=_params(("arbitrary",), 32),
        name="router",
    )(h, w_pad, b_pad)


def _moe_row_copy(h_hbm, x_scr, sem, tok, r):
    return pltpu.make_async_copy(h_hbm.at[pl.ds(tok, 1), :], x_scr.at[pl.ds(r, 1), :], sem)


def _moe_out_copy(acc_scr, y_hbm, sem, row0, u):
    off = pl.multiple_of(u * MOE_UNIT, MOE_UNIT)
    dst = pl.multiple_of(row0 + off, MOE_UNIT)
    return pltpu.make_async_copy(acc_scr.at[pl.ds(off, MOE_UNIT), :],
                                 y_hbm.at[pl.ds(dst, MOE_UNIT), :], sem)


def _moe_kernel(te_ref, row0_ref, nunit_ref, used_ref, pad_ref, dest_ref,
                h_hbm, wg_ref, wu_ref, wd_ref, y_hbm,
                slot_tok, x_scr, acc_scr, gsem, osem):
    t = pl.program_id(0)
    f = pl.program_id(1)
    nf = pl.num_programs(1)

    @pl.when(jnp.logical_and(t == 0, f == 0))
    def _():
        for e in range(NE):
            def clear(s, c):
                slot_tok[s] = 0
                return c
            lax.fori_loop(pad_ref[2 * e], pad_ref[2 * e + 1], clear, 0)

        def place(tok, c):
            slot_tok[dest_ref[2 * tok]] = tok
            slot_tok[dest_ref[2 * tok + 1]] = tok
            return c
        lax.fori_loop(0, T, place, 0, unroll=8)

    @pl.when(t < used_ref[0])
    def _():
        row0 = row0_ref[t]
        nunit = nunit_ref[t]

        @pl.when(f == 0)
        def _():
            def issue_group(r8, c):
                base = pl.multiple_of(r8 * 8, 8)
                for k in range(8):
                    _moe_row_copy(h_hbm, x_scr, gsem, slot_tok[row0 + base + k], base + k).start()
                return c
            lax.fori_loop(0, nunit * (MOE_UNIT // 8), issue_group, 0)

            @pl.when(t > 0)
            def _():
                def finish(u, c):
                    _moe_out_copy(acc_scr, y_hbm, osem, row0_ref[t - 1], u).wait()
                    return c
                lax.fori_loop(0, nunit_ref[t - 1], finish, 0)

            def clear_acc(u, c):
                off = pl.multiple_of(u * MOE_UNIT, MOE_UNIT)
                acc_scr[pl.ds(off, MOE_UNIT), :] = jnp.zeros((MOE_UNIT, D), F32)
                return c
            lax.fori_loop(0, nunit, clear_acc, 0)

            def drain_group(r8, c):
                base = pl.multiple_of(r8 * 8, 8)
                for k in range(8):
                    _moe_row_copy(h_hbm, x_scr, gsem, 0, base + k).wait()
                return c
            lax.fori_loop(0, nunit * (MOE_UNIT // 8), drain_group, 0)

        def chunk(off, rows):
            x = x_scr[pl.ds(off, rows), :]
            hid = _silu(_dot(x, wg_ref[...])) * _dot(x, wu_ref[...])
            acc_scr[pl.ds(off, rows), :] += _dot(hid, wd_ref[...])

        def oct_(c, carry):
            chunk(pl.multiple_of(c * 8 * MOE_UNIT, 8 * MOE_UNIT), 8 * MOE_UNIT)
            return carry
        lax.fori_loop(0, nunit // 8, oct_, 0)

        @pl.when(nunit % 8 >= 4)
        def _():
            chunk(pl.multiple_of((nunit // 8) * 8 * MOE_UNIT, 4 * MOE_UNIT), 4 * MOE_UNIT)

        @pl.when(nunit % 4 >= 2)
        def _():
            chunk(pl.multiple_of((nunit // 4) * 4 * MOE_UNIT, 2 * MOE_UNIT), 2 * MOE_UNIT)

        @pl.when(nunit % 2 == 1)
        def _():
            chunk(pl.multiple_of((nunit - 1) * MOE_UNIT, MOE_UNIT), MOE_UNIT)

        @pl.when(f == nf - 1)
        def _():
            def start(u, carry):
                _moe_out_copy(acc_scr, y_hbm, osem, row0, u).start()
                return carry
            lax.fori_loop(0, nunit, start, 0)

            @pl.when(t == used_ref[0] - 1)
            def _():
                def finish(u, carry):
                    _moe_out_copy(acc_scr, y_hbm, osem, row0, u).wait()
                    return carry
                lax.fori_loop(0, nunit, finish, 0)

    @pl.when(jnp.logical_and(t == pl.num_programs(0) - 1, f == nf - 1))
    def _():
        acc_scr[0:MOE_UNIT, :] = jnp.zeros((MOE_UNIT, D), F32)

        def fill(u, carry):
            cp = _moe_out_copy(acc_scr, y_hbm, osem, u * MOE_UNIT, 0)
            cp.start()
            cp.wait()
            return carry
        lax.fori_loop(used_ref[1], MOE_CAP // MOE_UNIT, fill, 0)


def _moe_experts(h, wg, wu, wd, tile_e, tile_row0, tile_nunit, used, pad, dest):
    nf = FE // MOE_TF

    def fidx(t, f, us):
        return jnp.where(t < us[0], f, nf - 1)

    grid_spec = pltpu.PrefetchScalarGridSpec(
        num_scalar_prefetch=6,
        grid=(MOE_NT, nf),
        in_specs=[pl.BlockSpec(memory_space=pl.ANY),
                  pl.BlockSpec((None, D, MOE_TF), lambda t, f, te, r0, nu, us, pa, de: (te[t], 0, fidx(t, f, us))),
                  pl.BlockSpec((None, D, MOE_TF), lambda t, f, te, r0, nu, us, pa, de: (te[t], 0, fidx(t, f, us))),
                  pl.BlockSpec((None, MOE_TF, D), lambda t, f, te, r0, nu, us, pa, de: (te[t], fidx(t, f, us), 0))],
        out_specs=pl.BlockSpec(memory_space=pl.ANY),
        scratch_shapes=[pltpu.SMEM((MOE_CAP,), jnp.int32),
                        pltpu.VMEM((MOE_TMAX, D), F32),
                        pltpu.VMEM((MOE_TMAX, D), F32),
                        pltpu.SemaphoreType.DMA,
                        pltpu.SemaphoreType.DMA],
    )
    return pl.pallas_call(
        _moe_kernel,
        grid_spec=grid_spec,
        out_shape=jax.ShapeDtypeStruct((MOE_CAP, D), F32),
        compiler_params=_params(("arbitrary", "arbitrary"), 58),
        name="moe_experts",
    )(tile_e, tile_row0, tile_nunit, used, pad, dest, h, wg, wu, wd)


def _cmb_copy(y_hbm, ybuf, sem, buf, slot, k, r):
    return pltpu.make_async_copy(y_hbm.at[pl.ds(slot, 1), :], ybuf.at[buf, k, pl.ds(r, 1), :],
                                 sem.at[buf])


def _combine_kernel(dest_ref, h_ref, info_ref, y_hbm, g_ref, b_ref, o_ref, ybuf, sem):
    tm = h_ref.shape[0]
    i = pl.program_id(0)
    cur = lax.rem(i, 2)

    def gather(tile, buf):
        def issue_group(r8, c):
            base = pl.multiple_of(r8 * 8, 8)
            for k in range(8):
                tok = tile * tm + base + k
                _cmb_copy(y_hbm, ybuf, sem, buf, dest_ref[2 * tok], 0, base + k).start()
                _cmb_copy(y_hbm, ybuf, sem, buf, dest_ref[2 * tok + 1], 1, base + k).start()
            return c
        lax.fori_loop(0, tm // 8, issue_group, 0)

    @pl.when(i == 0)
    def _():
        gather(0, 0)

    @pl.when(i + 1 < pl.num_programs(0))
    def _():
        gather(i + 1, 1 - cur)

    def drain_group(r8, c):
        base = pl.multiple_of(r8 * 8, 8)
        for k in range(8):
            _cmb_copy(y_hbm, ybuf, sem, cur, 0, 0, base + k).wait()
            _cmb_copy(y_hbm, ybuf, sem, cur, 0, 1, base + k).wait()
        return c
    lax.fori_loop(0, tm // 8, drain_group, 0)
    info = info_ref[...]
    moe = info[:, 0:1] * ybuf[cur, 0] + info[:, 1:2] * ybuf[cur, 1]
    o_ref[...] = _layer_norm(ALPHA * h_ref[...] + moe, g_ref[...], b_ref[...])


def _combine(dest, h, info, y_rows, g, b):
    tm = CMB_TM
    grid_spec = pltpu.PrefetchScalarGridSpec(
        num_scalar_prefetch=1,
        grid=(T // tm,),
        in_specs=[pl.BlockSpec((tm, D), lambda i, de: (i, 0)),
                  pl.BlockSpec((tm, LANES_V7X), lambda i, de: (i, 0)),
                  pl.BlockSpec(memory_space=pl.ANY),
                  pl.BlockSpec((1, D), lambda i, de: (0, 0)),
                  pl.BlockSpec((1, D), lambda i, de: (0, 0))],
        out_specs=pl.BlockSpec((tm, D), lambda i, de: (i, 0)),
        scratch_shapes=[pltpu.VMEM((2, 2, tm, D), F32), pltpu.SemaphoreType.DMA((2,))],
    )
    return pl.pallas_call(
        _combine_kernel,
        grid_spec=grid_spec,
        out_shape=jax.ShapeDtypeStruct((T, D), F32),
        compiler_params=_params(("arbitrary",), 32),
        name="moe_combine_ln",
    )(dest, h, info, y_rows, g, b)


def _rope_tables(pos):
    inv_freq = THETA ** (-jnp.arange(0, ROPE, 2, dtype=F32) / ROPE)
    ang = pos[:, None] * inv_freq[None, :]
    cos = jnp.cos(ang).astype(F32)
    sin = jnp.sin(ang).astype(F32)
    zero = jnp.zeros((pos.shape[0], LANES_V7X - ROPE), F32)
    return (jnp.concatenate([cos, cos, zero], axis=1),
            jnp.concatenate([-sin, sin, zero], axis=1))


def _expert_tiles(counts):
    units = (counts + MOE_UNIT - 1) // MOE_UNIT
    seg_unit = jnp.cumsum(units) - units
    ntile = (units + MOE_TUNITS - 1) // MOE_TUNITS
    tile_end = jnp.cumsum(ntile)
    n_used = tile_end[-1]
    tid = jnp.arange(MOE_NT, dtype=jnp.int32)
    e = jnp.minimum(jnp.sum(tid[:, None] >= tile_end[None, :], axis=1), NE - 1).astype(jnp.int32)
    local = tid - (tile_end - ntile)[e]
    nt = jnp.maximum(ntile[e], 1)
    lo = (local * units[e]) // nt
    hi = ((local + 1) * units[e]) // nt
    used = tid < n_used
    last = jnp.maximum(n_used - 1, 0)
    e = jnp.where(used, e, e[last])
    row0 = jnp.where(used, (seg_unit[e] + lo) * MOE_UNIT, 0)
    nunit = jnp.where(used, hi - lo, 0)
    used_info = jnp.stack([n_used, jnp.sum(units)]).astype(jnp.int32)
    seg_row = seg_unit * MOE_UNIT
    pad = jnp.stack([seg_row + counts, seg_row + units * MOE_UNIT], axis=1).reshape(2 * NE)
    return (e.astype(jnp.int32), row0.astype(jnp.int32), nunit.astype(jnp.int32),
            used_info, pad.astype(jnp.int32), seg_row.astype(jnp.int32))


def kernel(x, meta_tokens, attn_w_in, fox_forget_bias, mla_q_norm, mla_kv_norm, mla_w_uq, mla_w_ukv, attn_w_out, pool_w_in, pool_w_group, pool_scale, pool_w_out, ffn_w_gate, ffn_w_up, ffn_w_down, moe_w_router, moe_b_router, moe_w_gate, moe_w_up, moe_w_down, ln_mix_g, ln_mix_b, ln_ffn_g, ln_ffn_b):
    assert x.shape == (1, T, D) and meta_tokens.shape == (NM, D)
    xt = x[0]
    meta = meta_tokens.astype(x.dtype)

    w_in = attn_w_in[0]
    o_kr, o_qf, o_f = QL + KVL, QL + KVL + ROPE, QL + KVL + ROPE + 3 * H * HD
    kr1, kr2 = w_in[:, o_kr:o_kr + ROPE // 2], w_in[:, o_kr + ROPE // 2:o_qf]
    zpad = lambda n: jnp.zeros((D, n), F32)
    w_small = jnp.concatenate([kr2, kr1, zpad(LANES_V7X - ROPE),
                               w_in[:, o_f:o_f + H], zpad(LANES_V7X - H)], axis=1)
    w_fox = w_in[:, o_qf:o_f].astype(BF16)
    w_in = w_in[:, :LAT_COLS]
    wuq = mla_w_uq[0].reshape(QL, H, NOPE + ROPE)
    uq_n, uq_1, uq_2 = wuq[..., :NOPE], wuq[..., NOPE:NOPE + ROPE // 2], wuq[..., NOPE + ROPE // 2:]
    zq = jnp.zeros((QL, H, LANES_V7X - ROPE), F32)
    w_uq = jnp.concatenate([uq_n, uq_1, uq_2, zq], axis=-1).reshape(QL, H * DQ_MLA).astype(BF16)
    w_uqs = jnp.concatenate([uq_2, uq_1, zq], axis=-1).reshape(QL, H * HD).astype(BF16)
    wukv = mla_w_ukv[0].reshape(KVL, H, NOPE + HD)
    w_ukv = jnp.concatenate([wukv[..., :NOPE].reshape(KVL, H * NOPE),
                             wukv[..., NOPE:].reshape(KVL, H * HD)], axis=1).astype(BF16)
    w_ao = attn_w_out[0]
    w_fg, w_fu, w_fd = ffn_w_gate[0], ffn_w_up[0], ffn_w_down[0]
    w_pi, w_pg, w_po = pool_w_in[0], pool_w_group[0], pool_w_out[0]
    qn, kvn = mla_q_norm[0][None, :], mla_kv_norm[0][None, :]
    row = lambda v: v[None, :]

    ctab_t, stab_t = _rope_tables(jnp.arange(NM, NM + T, dtype=F32))
    ctab_m, stab_m = _rope_tables(jnp.arange(NM, dtype=F32))

    fox_scale = HD ** -0.5 * LOG2E
    q_t, k_t, vt_t, fl_t = _mla_proj(xt, w_in, w_small, qn, kvn, w_uq, w_uqs, w_ukv, ctab_t, stab_t, TK)
    q_m, k_m, v_m, _ = _mla_proj(meta, w_in, w_small, qn, kvn, w_uq, w_uqs, w_ukv, ctab_m, stab_m, NM)
    fox_qk, fox_vt = _fox_proj(xt, w_fox, fox_scale)
    fox_m = _matmul(meta, w_fox, NM, H * HD, BF16, scaled_blocks=1, scale=fox_scale, name="fox_proj_meta")

    fl = fl_t[:, :H].T.reshape(H, T // LANES_V7X, LANES_V7X)
    bias_b = jnp.broadcast_to(fox_forget_bias[0][:, None, None], (H, 1, LANES_V7X))
    parts = _forget_cumsum(fl, bias_b).reshape(H, 6, T)
    pos = parts[:, 0:3].transpose(2, 0, 1)
    neg = parts[:, 3:6].transpose(2, 0, 1)
    lanes = lambda pieces: jnp.concatenate(pieces, axis=2).reshape(-1, H * AUG_STRIDE)
    cst = lambda rows, n, v: jnp.full((rows, H, n), v, BF16)
    aug = (lanes([cst(T, 3, 1.0), pos, cst(T, AUG_STRIDE - 6, 0.0)]),
           lanes([neg, cst(T, AUG_STRIDE - 3, 0.0)]),
           lanes([cst(NM, 3, 0.0), cst(NM, 3, -1.0), cst(NM, AUG_STRIDE - 6, 0.0)]))

    ones_rows = jnp.concatenate([jnp.ones((H, 1, NM), BF16), jnp.zeros((H, VROWS - HD - 1, NM), BF16)], axis=1)
    to_vmt = lambda v: jnp.concatenate([v.reshape(NM, H, HD).transpose(1, 2, 0), ones_rows], axis=1)
    o_mla = _attention(q_t, k_t, vt_t, k_m, to_vmt(v_m), 0, DQ_MLA, name="mla_attn")
    o_fox = _attention(fox_qk, fox_qk, fox_vt, fox_m, to_vmt(fox_m[:, 2 * H * HD:]),
                       H, HD, aug, name="fox_attn")
    om_mla = _meta_attention(q_m, k_m, v_m, 0, 0, 0, DQ_MLA, "mla_attn_meta")
    om_fox = _meta_attention(fox_m, fox_m, fox_m, 0, H, 2 * H, HD, "fox_attn_meta")

    g0, b0, g1, b1 = row(ln_mix_g[0]), row(ln_mix_b[0]), row(ln_ffn_g[0]), row(ln_ffn_b[0])
    h_t = _attn_out(o_mla, o_fox, w_ao, xt, g0, b0, TM)
    h_m = _attn_out(om_mla, om_fox, w_ao, meta, g0, b0, NM)
    h_t, h_m = _ffn(h_t, h_m, w_fg, w_fu, w_fd, g1, b1)

    hp_t = _matmul(h_t, w_pi, TM, D, F32, name="pool_in")
    hp_m = _matmul(h_m, w_pi, NM, D, F32, name="pool_in_meta")
    h_t = _pool_mixer(hp_t, hp_m, h_t, w_pg, row(pool_scale[0]), w_po,
                      row(ln_mix_g[1]), row(ln_mix_b[1]))

    w_r = jnp.concatenate([moe_w_router[0], jnp.zeros((D, LANES_V7X - NE), F32)], axis=1)
    b_r = jnp.concatenate([moe_b_router[0], jnp.full((LANES_V7X - NE,), NEG, F32)])[None, :]
    info, cnt = _router(h_t, w_r, b_r)
    counts = cnt[0, :NE].astype(jnp.int32)
    tile_e, tile_row0, tile_nunit, used, pad, seg_row = _expert_tiles(counts)
    experts = info[:, 2:4].astype(jnp.int32)
    ranks = info[:, 4:6].astype(jnp.int32)
    dest = (seg_row[experts] + ranks).reshape(2 * T)
    y_rows = _moe_experts(h_t, moe_w_gate[0], moe_w_up[0], moe_w_down[0],
                          tile_e, tile_row0, tile_nunit, used, pad, dest)
    out = _combine(dest, h_t, info, y_rows, row(ln_ffn_g[1]), row(ln_ffn_b[1]))
    return out[None]
```

```python
import functools

import jax
import jax.numpy as jnp
from jax import lax
from jax.experimental import pallas as pl
from jax.experimental.pallas import tpu as pltpu

D = 2048
T = 8192
NM = 16
H = 8
QL = 512
KVL = 512
NOPE = 128
ROPE = 64
HD = 128
DQ_MLA = 256
LAT_COLS = QL + KVL + 2 * ROPE
VROWS = HD + 16
THETA = 10000.0
POOL_W = (2, 4, 8, 16)
PG = D // 4
FD = 5632
NE = 8
FE = 7168
ALPHA = 4.0 ** 0.25
LN_EPS = 1e-5
RMS_EPS = 1e-6
NEG = -1e30
LOG2E = 1.4426950408889634

F32 = jnp.float32
BF16 = jnp.bfloat16

LANES_V7X = 128
MIB = 1024 * 1024

TM = 512
TK = 512
ATT_TQ = 2 * TK
FFN_TM = 1024
FFN_CHUNK = 1024
FFN_TF = 512
POOL_TM = 256
MOE_UNIT = 128
MOE_TUNITS = 10
MOE_TMAX = MOE_TUNITS * MOE_UNIT
MOE_TF = 512
MOE_CAP = 2 * T + NE * MOE_UNIT
MOE_NT = MOE_CAP // MOE_TMAX + NE + 1
CMB_TM = 256


def _params(sem, vmem_mib):
    return pltpu.CompilerParams(dimension_semantics=sem, vmem_limit_bytes=vmem_mib * MIB)


def _full(shape):
    n = len(shape)
    return pl.BlockSpec(shape, lambda *_: (0,) * n)


def _resident(shape):
    n = len(shape)
    return pl.BlockSpec(shape, lambda *_: (0,) * n, pipeline_mode=pl.Buffered(1))


def _layer_norm(x, g, b):
    mu = jnp.mean(x, axis=-1, keepdims=True)
    xc = x - mu
    var = jnp.mean(xc * xc, axis=-1, keepdims=True)
    return xc * lax.rsqrt(var + LN_EPS) * g + b


def _rms_norm(x, g):
    return x * lax.rsqrt(jnp.mean(x * x, axis=-1, keepdims=True) + RMS_EPS) * g


def _dot(a, b):
    return jnp.dot(a, b, preferred_element_type=F32)


def _dot_nt(a, b):
    return lax.dot_general(a, b, (((1,), (1,)), ((), ())), preferred_element_type=F32)


def _vt_store(v_ref, h, v):
    v_ref[h, 0, 0:HD, :] = v.T.astype(BF16)
    rows = lax.broadcasted_iota(jnp.int32, (VROWS - HD, v.shape[0]), 0)
    v_ref[h, 0, HD:VROWS, :] = (rows == 0).astype(BF16)


def _mla_proj_kernel(x_ref, win_ref, wsm_ref, qn_ref, kvn_ref, wuq_ref, wuqs_ref, wukv_ref,
                     c_ref, s_ref, q_ref, k_ref, v_ref, fl_ref, *, qscale, transposed_v):
    x = x_ref[...]
    lat = _dot(x, win_ref[...])
    small = _dot(x, wsm_ref[...])
    cq = lat[:, 0:QL]
    ckv = lat[:, QL:QL + KVL]
    kr = lat[:, QL + KVL:LAT_COLS]
    krs = small[:, 0:LANES_V7X]
    fl_ref[...] = small[:, LANES_V7X:]
    cos = c_ref[...]
    sin = s_ref[...]
    qn = _rms_norm(cq, qn_ref[...]).astype(BF16)
    qpre = _dot(qn, wuq_ref[...])
    qsw = _dot(qn, wuqs_ref[...])
    for h in range(H):
        nope = qpre[:, DQ_MLA * h:DQ_MLA * h + NOPE]
        rot = (qpre[:, DQ_MLA * h + NOPE:DQ_MLA * (h + 1)] * cos
               + qsw[:, HD * h:HD * (h + 1)] * sin)
        q_ref[:, DQ_MLA * h:DQ_MLA * h + NOPE] = (nope * qscale).astype(BF16)
        q_ref[:, DQ_MLA * h + NOPE:DQ_MLA * (h + 1)] = (rot * qscale).astype(BF16)
    kvn = _rms_norm(ckv, kvn_ref[...]).astype(BF16)
    kv = _dot(kvn, wukv_ref[...])
    krot = (kr * cos + krs * sin).astype(BF16)
    for h in range(H):
        k_ref[:, DQ_MLA * h:DQ_MLA * h + NOPE] = kv[:, HD * h:HD * (h + 1)].astype(BF16)
        k_ref[:, DQ_MLA * h + NOPE:DQ_MLA * (h + 1)] = krot
    if transposed_v:
        for h in range(H):
            _vt_store(v_ref, h, kv[:, (H + h) * HD:(H + h + 1) * HD])
    else:
        v_ref[...] = kv[:, H * HD:].astype(BF16)


def _mla_proj(x, w_in, wsm, qn, kvn, wuq, wuqs, wukv, ctab, stab, tm):
    m = x.shape[0]
    qscale = (NOPE + ROPE) ** -0.5 * LOG2E
    row = lambda w: pl.BlockSpec((tm, w), lambda i: (i, 0))
    transposed_v = tm == TK
    if transposed_v:
        v_spec = pl.BlockSpec((H, 1, VROWS, TK), lambda i: (0, i, 0, 0))
        v_shape = jax.ShapeDtypeStruct((H, m // TK, VROWS, TK), BF16)
    else:
        v_spec, v_shape = row(H * HD), jax.ShapeDtypeStruct((m, H * HD), BF16)
    lat_spec = pl.BlockSpec((D, LAT_COLS), lambda i: (0, 0), pipeline_mode=pl.Buffered(1))
    return pl.pallas_call(
        functools.partial(_mla_proj_kernel, qscale=qscale, transposed_v=transposed_v),
        grid=(m // tm,),
        in_specs=[row(D), lat_spec, _full(wsm.shape), _full(qn.shape), _full(kvn.shape),
                  _resident(wuq.shape), _resident(wuqs.shape), _resident(wukv.shape),
                  row(LANES_V7X), row(LANES_V7X)],
        out_specs=[row(H * DQ_MLA), row(H * DQ_MLA), v_spec, row(LANES_V7X)],
        out_shape=[jax.ShapeDtypeStruct((m, H * DQ_MLA), BF16),
                   jax.ShapeDtypeStruct((m, H * DQ_MLA), BF16),
                   v_shape,
                   jax.ShapeDtypeStruct((m, LANES_V7X), F32)],
        compiler_params=_params(("parallel",), 56),
        name="mla_proj",
    )(x, w_in, wsm, qn, kvn, wuq, wuqs, wukv, ctab, stab)


def _mm_kernel(x_ref, w_ref, o_ref, *, scaled_blocks, scale):
    y = _dot(x_ref[...], w_ref[...])
    if scaled_blocks:
        y = jnp.where(pl.program_id(1) < scaled_blocks, y * scale, y)
    o_ref[...] = y.astype(o_ref.dtype)


def _matmul(x, w, tm, tn, out_dtype, scaled_blocks=0, scale=1.0, name="matmul"):
    m, k = x.shape
    n = w.shape[1]
    return pl.pallas_call(
        functools.partial(_mm_kernel, scaled_blocks=scaled_blocks, scale=scale),
        grid=(m // tm, n // tn),
        in_specs=[pl.BlockSpec((tm, k), lambda i, j: (i, 0)),
                  _resident((k, n)) if tn == n else pl.BlockSpec((k, tn), lambda i, j: (0, j))],
        out_specs=pl.BlockSpec((tm, tn), lambda i, j: (i, j)),
        out_shape=jax.ShapeDtypeStruct((m, n), out_dtype),
        compiler_params=_params(("parallel", "parallel"), 48),
        name=name,
    )(x, w)


def _fox_proj_kernel(x_ref, w_ref, qk_ref, vt_ref, *, scale):
    j = pl.program_id(1)
    n = H * HD
    part = lambda c: _dot(x_ref[...].astype(BF16), w_ref[:, c * n:(c + 1) * n])

    @pl.when(j == 0)
    def _():
        qk_ref[...] = (part(0) * scale).astype(BF16)

    @pl.when(j == 1)
    def _():
        qk_ref[...] = part(1).astype(BF16)

    @pl.when(j == 2)
    def _():
        y = part(2)
        for h in range(H):
            _vt_store(vt_ref, h, y[:, h * HD:(h + 1) * HD])


def _fox_proj(x, w, scale):
    n = H * HD
    return pl.pallas_call(
        functools.partial(_fox_proj_kernel, scale=scale),
        grid=(T // TK, 3),
        in_specs=[pl.BlockSpec((TK, D), lambda i, j: (i, 0)),
                  _resident((D, 3 * n))],
        out_specs=[pl.BlockSpec((TK, n), lambda i, j: (i, jnp.minimum(j, 1))),
                   pl.BlockSpec((H, 1, VROWS, TK), lambda i, j: (0, i, 0, 0))],
        out_shape=[jax.ShapeDtypeStruct((T, 2 * n), BF16),
                   jax.ShapeDtypeStruct((H, T // TK, VROWS, TK), BF16)],
        compiler_params=_params(("parallel", "arbitrary"), 48),
        name="fox_proj",
    )(x, w)


def _split3(a):
    hi = a.astype(BF16)
    r = a - hi.astype(F32)
    mid = r.astype(BF16)
    lo = (r - mid.astype(F32)).astype(BF16)
    return hi, mid, lo


def _forget_cumsum_kernel(fl_ref, b_ref, o_ref):
    z = fl_ref[...] + b_ref[...]
    logf = jnp.minimum(z, 0.0) - jnp.log(1.0 + jnp.exp(-jnp.abs(z)))
    n = LANES_V7X
    ri = lax.broadcasted_iota(jnp.int32, (n, n), 0)
    ci = lax.broadcasted_iota(jnp.int32, (n, n), 1)
    upper = (ri <= ci).astype(BF16)
    ones = jnp.ones((n, n), BF16)
    within = jnp.zeros(logf.shape, F32)
    total = jnp.zeros(logf.shape, F32)
    for part in _split3(logf):
        within += _dot(part, upper)
        total += _dot(part, ones)
    r = logf.shape[0]
    lower = (lax.broadcasted_iota(jnp.int32, (r, r), 1)
             < lax.broadcasted_iota(jnp.int32, (r, r), 0)).astype(BF16)
    before = jnp.zeros(logf.shape, F32)
    for part in _split3(total):
        before += _dot(lower, part)
    c2 = (within + before) * LOG2E
    for p, part in enumerate(_split3(c2)):
        o_ref[p] = part
        o_ref[3 + p] = -part


def _forget_cumsum(fl_t, bias_b):
    r = T // LANES_V7X
    return pl.pallas_call(
        _forget_cumsum_kernel,
        grid=(H,),
        in_specs=[pl.BlockSpec((None, r, LANES_V7X), lambda h: (h, 0, 0)),
                  pl.BlockSpec((None, 1, LANES_V7X), lambda h: (h, 0, 0))],
        out_specs=pl.BlockSpec((None, 6, r, LANES_V7X), lambda h: (h, 0, 0, 0)),
        out_shape=jax.ShapeDtypeStruct((H, 6, r, LANES_V7X), BF16),
        compiler_params=_params(("parallel",), 32),
        name="forget_cumsum",
    )(fl_t, bias_b)


ATT_NH = 2
AUG_STRIDE = 16


def _attn_kernel(*refs, fox, dq):
    if fox:
        (q_ref, k_ref, vt_ref, km_ref, vmt_ref, qa_ref, ka_ref, kam_ref, o_ref,
         m_scr, acc_scr, s0_scr, s1_scr, mb0_scr, mb1_scr, qf_scr) = refs
    else:
        (q_ref, k_ref, vt_ref, km_ref, vmt_ref, o_ref,
         m_scr, acc_scr, s0_scr, s1_scr, mb0_scr, mb1_scr) = refs
    hp = pl.program_id(0)
    i = pl.program_id(1)
    m_scr[...] = jnp.full(m_scr.shape, -jnp.inf, F32)
    acc_scr[...] = jnp.zeros(acc_scr.shape, F32)
    if fox:
        lane = lax.broadcasted_iota(jnp.int32, qa_ref.shape, 1)
        for g in range(ATT_NH):
            lo = (hp * ATT_NH + g) * AUG_STRIDE
            own = jnp.logical_and(lane >= lo, lane < lo + AUG_STRIDE)
            qf_scr[g, :, 0:dq] = q_ref[:, g * dq:(g + 1) * dq]
            qf_scr[g, :, dq:2 * dq] = jnp.where(own, qa_ref[...], jnp.zeros_like(qa_ref))

    def query(g):
        return qf_scr[g] if fox else q_ref[:, g * dq:(g + 1) * dq]

    def update(g, st, mb, vt):
        m_prev = m_scr[g]
        m_new = jnp.maximum(m_prev, mb)
        alpha = jnp.exp2(m_prev - m_new)
        p = jnp.exp2(st - m_new)
        acc_scr[g] = alpha * acc_scr[g] + _dot(vt, p.astype(BF16))
        m_scr[g] = m_new

    def stage1(slot, j, heads=range(ATT_NH)):
        s_ref, mb_ref = slots[slot]
        off = pl.multiple_of(j * TK, TK)
        for g in heads:
            kb = k_ref[pl.ds(off, TK), g * dq:(g + 1) * dq]
            if fox:
                kb = jnp.concatenate([kb, ka_ref[pl.ds(off, TK), :]], axis=1)
            st = _dot_nt(kb, query(g))
            s_ref[g] = st
            mb_ref[g] = jnp.max(st, axis=0, keepdims=True)

    def stage2(slot, j, diag_key0=None, heads=range(ATT_NH)):
        s_ref, mb_ref = slots[slot]
        for g in heads:
            st = s_ref[g]
            mb = mb_ref[g]
            if diag_key0 is not None:
                keys = lax.broadcasted_iota(jnp.int32, st.shape, 0) + diag_key0
                qrys = lax.broadcasted_iota(jnp.int32, st.shape, 1)
                st = jnp.where(keys <= qrys, st, NEG)
                mb = jnp.max(st, axis=0, keepdims=True)
            update(g, st, mb, vt_ref[g, j])

    slots = ((s0_scr, mb0_scr), (s1_scr, mb1_scr))
    stage1(0, 0)

    def body(jj, carry):
        j = 2 * jj
        for g in range(ATT_NH):
            stage1(1, j + 1, heads=(g,))
            stage2(0, j, heads=(g,))
        for g in range(ATT_NH):
            stage1(0, j + 2, heads=(g,))
            stage2(1, j + 1, heads=(g,))
        return carry

    lax.fori_loop(0, i, body, 0)
    stage1(1, 2 * i + 1)
    stage2(0, 2 * i, diag_key0=0)
    stage2(1, 2 * i + 1, diag_key0=TK)

    for g in range(ATT_NH):
        kb = km_ref[:, g * dq:(g + 1) * dq]
        if fox:
            kb = jnp.concatenate([kb, kam_ref[...]], axis=1)
        st = _dot_nt(kb, query(g))
        update(g, st, jnp.max(st, axis=0, keepdims=True), vmt_ref[g])
        acc = acc_scr[g]
        o = (acc[0:HD] / acc[HD:HD + 1]).T
        o_ref[:, g * HD:(g + 1) * HD] = o.astype(o_ref.dtype)


def _attention(q, k, vt, km, vmt, k_col0, dq, aug=None, name="attn"):
    fox = aug is not None
    nh, tq = ATT_NH, ATT_TQ
    in_specs = [pl.BlockSpec((tq, nh * dq), lambda h, i: (i, h)),
                pl.BlockSpec((T, nh * dq), lambda h, i: (0, k_col0 // nh + h)),
                pl.BlockSpec((nh, T // TK, VROWS, TK), lambda h, i: (h, 0, 0, 0)),
                pl.BlockSpec((NM, nh * dq), lambda h, i: (0, k_col0 // nh + h)),
                pl.BlockSpec((nh, VROWS, NM), lambda h, i: (h, 0, 0))]
    args = [q, k, vt, km, vmt]
    scratch = [pltpu.VMEM((nh, 1, tq), F32), pltpu.VMEM((nh, VROWS, tq), F32),
               pltpu.VMEM((nh, TK, tq), F32), pltpu.VMEM((nh, TK, tq), F32),
               pltpu.VMEM((nh, 1, tq), F32), pltpu.VMEM((nh, 1, tq), F32)]
    if fox:
        in_specs += [pl.BlockSpec((tq, LANES_V7X), lambda h, i: (i, 0)),
                     _full((T, LANES_V7X)), _full((NM, LANES_V7X))]
        args += list(aug)
        scratch.append(pltpu.VMEM((nh, tq, 2 * dq), BF16))
    return pl.pallas_call(
        functools.partial(_attn_kernel, fox=fox, dq=dq),
        grid=(H // nh, T // tq),
        in_specs=in_specs,
        out_specs=pl.BlockSpec((tq, nh * HD), lambda h, i: (i, h)),
        out_shape=jax.ShapeDtypeStruct((T, H * HD), BF16),
        scratch_shapes=scratch,
        compiler_params=_params(("parallel", "arbitrary"), 56),
        name=name,
    )(*args)


def _meta_attn_kernel(q_ref, k_ref, v_ref, o_ref):
    s = _dot_nt(q_ref[...], k_ref[...])
    rows = lax.broadcasted_iota(jnp.int32, s.shape, 0)
    cols = lax.broadcasted_iota(jnp.int32, s.shape, 1)
    s = jnp.where(cols <= rows, s, NEG)
    p = jnp.exp2(s - jnp.max(s, axis=1, keepdims=True))
    o = _dot(p.astype(BF16), v_ref[...]) / jnp.sum(p, axis=1, keepdims=True)
    o_ref[...] = o.astype(o_ref.dtype)


def _meta_attention(q, k, v, q_col0, k_col0, v_col0, dq, name):
    return pl.pallas_call(
        _meta_attn_kernel,
        grid=(H,),
        in_specs=[pl.BlockSpec((NM, dq), lambda h: (0, q_col0 + h)),
                  pl.BlockSpec((NM, dq), lambda h: (0, k_col0 + h)),
                  pl.BlockSpec((NM, HD), lambda h: (0, v_col0 + h))],
        out_specs=pl.BlockSpec((NM, HD), lambda h: (0, h)),
        out_shape=jax.ShapeDtypeStruct((NM, H * HD), BF16),
        compiler_params=_params(("parallel",), 32),
        name=name,
    )(q, k, v)


def _attn_out_kernel(om_ref, of_ref, w_ref, h_ref, g_ref, b_ref, o_ref):
    m = _dot(om_ref[...], w_ref[0:H * HD, :]) + _dot(of_ref[...], w_ref[H * HD:, :])
    o_ref[...] = _layer_norm(ALPHA * h_ref[...] + m, g_ref[...], b_ref[...])


def _attn_out(om, of, w, h, g, b, tm):
    m = h.shape[0]
    row = lambda w_: pl.BlockSpec((tm, w_), lambda i: (i, 0))
    return pl.pallas_call(
        _attn_out_kernel,
        grid=(m // tm,),
        in_specs=[row(H * HD), row(H * HD), _resident(w.shape), row(D), _full(g.shape), _full(b.shape)],
        out_specs=row(D),
        out_shape=jax.ShapeDtypeStruct((m, D), F32),
        compiler_params=_params(("parallel",), 48),
        name="attn_out_ln",
    )(om, of, w, h, g, b)


def _silu(x):
    return x / (1.0 + jnp.exp(-x))


def _ffn_kernel(h_ref, hm_ref, wg_ref, wu_ref, wd_ref, g_ref, b_ref, o_ref, om_ref):
    i = pl.program_id(0)
    f = pl.program_id(1)
    last = pl.num_programs(1) - 1
    rows = h_ref.shape[0]
    chunk = FFN_CHUNK

    def chain(x):
        hid = _silu(_dot(x, wg_ref[...])) * _dot(x, wu_ref[...])
        return _dot(hid, wd_ref[...])

    @pl.when(f == 0)
    def _():
        o_ref[...] = jnp.zeros(o_ref.shape, F32)

    @pl.when(i == 0)
    def _():
        @pl.when(f == 0)
        def _():
            om_ref[...] = jnp.zeros(om_ref.shape, F32)
        part = chain(jnp.concatenate([h_ref[0:chunk, :], hm_ref[...]], axis=0))
        o_ref[0:chunk, :] += part[0:chunk]
        om_ref[...] += part[chunk:]

        @pl.when(f == last)
        def _():
            om_ref[...] = _layer_norm(ALPHA * hm_ref[...] + om_ref[...], g_ref[...], b_ref[...])

    @pl.when(i != 0)
    def _():
        o_ref[0:chunk, :] += chain(h_ref[0:chunk, :])

    for c in range(1, rows // chunk):
        sl = slice(c * chunk, (c + 1) * chunk)
        o_ref[sl, :] += chain(h_ref[sl, :])

    @pl.when(f == last)
    def _():
        o_ref[...] = _layer_norm(ALPHA * h_ref[...] + o_ref[...], g_ref[...], b_ref[...])


def _ffn(h, hm, wg, wu, wd, g, b):
    tm = FFN_TM
    once = pl.Buffered(1)
    return pl.pallas_call(
        _ffn_kernel,
        grid=(T // tm, FD // FFN_TF),
        in_specs=[pl.BlockSpec((tm, D), lambda i, f: (i, 0), pipeline_mode=once),
                  pl.BlockSpec((NM, D), lambda i, f: (0, 0)),
                  pl.BlockSpec((D, FFN_TF), lambda i, f: (0, f)),
                  pl.BlockSpec((D, FFN_TF), lambda i, f: (0, f)),
                  pl.BlockSpec((FFN_TF, D), lambda i, f: (f, 0)),
                  pl.BlockSpec((1, D), lambda i, f: (0, 0)),
                  pl.BlockSpec((1, D), lambda i, f: (0, 0))],
        out_specs=[pl.BlockSpec((tm, D), lambda i, f: (i, 0)),
                   pl.BlockSpec((NM, D), lambda i, f: (0, 0))],
        out_shape=[jax.ShapeDtypeStruct((T, D), F32), jax.ShapeDtypeStruct((NM, D), F32)],
        compiler_params=_params(("arbitrary", "arbitrary"), 58),
        name="ffn_ln",
    )(h, hm, wg, wu, wd, g, b)


def _pool_kernel(hp_ref, prev_ref, meta_ref, h_ref, wg_ref, sc_ref, wo_ref, g_ref, b_ref, o_ref,
                 ext_scr, z_scr):
    tm = hp_ref.shape[0]
    halo = jnp.where(pl.program_id(0) == 0, meta_ref[...], prev_ref[...])
    ext_scr[0:NM, :] = halo
    ext_scr[NM:NM + tm, :] = hp_ref[...]
    for gi, w in enumerate(POOL_W):
        cols = slice(PG * gi, PG * (gi + 1))
        cur = ext_scr[NM:NM + tm, cols]
        tot = cur
        for j in range(1, w):
            tot = tot + ext_scr[NM - j:NM - j + tm, cols]
        diff = tot * (1.0 / w) - cur
        y = _dot(diff, wg_ref[gi]) * sc_ref[:, cols]
        z_scr[:, cols] = y
    m = _dot(z_scr[...], wo_ref[...])
    o_ref[...] = _layer_norm(ALPHA * h_ref[...] + m, g_ref[...], b_ref[...])


def _pool_mixer(hp, hp_meta, h, wgroup, scale, wout, g, b):
    tm = POOL_TM
    row = pl.BlockSpec((tm, D), lambda i: (i, 0))
    prev = pl.BlockSpec((NM, D), lambda i: (jnp.maximum(i * (tm // NM) - 1, 0), 0))
    return pl.pallas_call(
        _pool_kernel,
        grid=(T // tm,),
        in_specs=[row, prev, _full(hp_meta.shape), row, _resident(wgroup.shape), _full(scale.shape),
                  _resident(wout.shape), _full(g.shape), _full(b.shape)],
        out_specs=row,
        out_shape=jax.ShapeDtypeStruct((T, D), F32),
        scratch_shapes=[pltpu.VMEM((NM + tm, D), F32), pltpu.VMEM((tm, D), F32)],
        compiler_params=_params(("parallel",), 48),
        name="pool_ln",
    )(hp, hp, hp_meta, h, wgroup, scale, wout, g, b)


def _router_kernel(h_ref, w_ref, b_ref, info_ref, cnt_ref, carry_scr):
    tm = h_ref.shape[0]

    @pl.when(pl.program_id(0) == 0)
    def _():
        carry_scr[...] = jnp.zeros(carry_scr.shape, F32)

    x = h_ref[...]
    w = w_ref[...]
    xh, wh = x.astype(BF16), w.astype(BF16)
    xl = (x - xh.astype(F32)).astype(BF16)
    wl = (w - wh.astype(F32)).astype(BF16)
    logits = _dot(xh, wh) + _dot(xh, wl) + _dot(xl, wh) + b_ref[...]
    lane = lax.broadcasted_iota(jnp.int32, logits.shape, 1).astype(F32)
    nolane = float(LANES_V7X)
    m1 = jnp.max(logits, axis=1, keepdims=True)
    i1 = jnp.min(jnp.where(logits == m1, lane, nolane), axis=1, keepdims=True)
    rest = jnp.where(lane == i1, -jnp.inf, logits)
    m2 = jnp.max(rest, axis=1, keepdims=True)
    i2 = jnp.min(jnp.where(rest == m2, lane, nolane), axis=1, keepdims=True)
    e = jnp.exp(m2 - m1)
    g1 = 1.0 / (1.0 + e)
    g2 = e / (1.0 + e)
    hit1 = lane == i1
    hit2 = lane == i2
    member = jnp.logical_or(hit1, hit2).astype(BF16)
    earlier = (lax.broadcasted_iota(jnp.int32, (tm, tm), 1)
               < lax.broadcasted_iota(jnp.int32, (tm, tm), 0)).astype(BF16)
    before = _dot(earlier, member) + carry_scr[0:1, :]
    r1 = jnp.sum(jnp.where(hit1, before, 0.0), axis=1, keepdims=True)
    r2 = jnp.sum(jnp.where(hit2, before, 0.0), axis=1, keepdims=True)
    carry = carry_scr[0:1, :] + jnp.sum(member.astype(F32), axis=0, keepdims=True)
    carry_scr[...] = jnp.broadcast_to(carry, carry_scr.shape)
    cnt_ref[...] = jnp.broadcast_to(carry, cnt_ref.shape)
    info = jnp.where(lane == 0, g1, 0.0)
    info = jnp.where(lane == 1, g2, info)
    info = jnp.where(lane == 2, i1, info)
    info = jnp.where(lane == 3, i2, info)
    info = jnp.where(lane == 4, r1, info)
    info = jnp.where(lane == 5, r2, info)
    info_ref[...] = info


def _router(h, w_pad, b_pad):
    tm = TM
    return pl.pallas_call(
        _router_kernel,
        grid=(T // tm,),
        in_specs=[pl.BlockSpec((tm, D), lambda i: (i, 0)), _full(w_pad.shape), _full(b_pad.shape)],
        out_specs=[pl.BlockSpec((tm, LANES_V7X), lambda i: (i, 0)), _full((8, LANES_V7X))],
        out_shape=[jax.ShapeDtypeStruct((T, LANES_V7X), F32),
                   jax.ShapeDtypeStruct((8, LANES_V7X), F32)],
        scratch_shapes=[pltpu.VMEM((8, LANES_V7X), F32)],
        compiler_params=_params(("arbitrary",), 32),
        name="router",
    )(h, w_pad, b_pad)


def _moe_row_copy(h_hbm, x_scr, sem, tok, r):
    return pltpu.make_async_copy(h_hbm.at[pl.ds(tok, 1), :], x_scr.at[pl.ds(r, 1), :], sem)


def _moe_out_copy(acc_scr, y_hbm, sem, row0, u):
    off = pl.multiple_of(u * MOE_UNIT, MOE_UNIT)
    dst = pl.multiple_of(row0 + off, MOE_UNIT)
    return pltpu.make_async_copy(acc_scr.at[pl.ds(off, MOE_UNIT), :],
                                 y_hbm.at[pl.ds(dst, MOE_UNIT), :], sem)


def _moe_kernel(te_ref, row0_ref, nunit_ref, used_ref, pad_ref, dest_ref,
                h_hbm, wg_ref, wu_ref, wd_ref, y_hbm,
                slot_tok, x_scr, acc_scr, gsem, osem):
    t = pl.program_id(0)
    f = pl.program_id(1)
    nf = pl.num_programs(1)

    @pl.when(jnp.logical_and(t == 0, f == 0))
    def _():
        for e in range(NE):
            def clear(s, c):
                slot_tok[s] = 0
                return c
            lax.fori_loop(pad_ref[2 * e], pad_ref[2 * e + 1], clear, 0)

        def place(tok, c):
            slot_tok[dest_ref[2 * tok]] = tok
            slot_tok[dest_ref[2 * tok + 1]] = tok
            return c
        lax.fori_loop(0, T, place, 0, unroll=8)

    @pl.when(t < used_ref[0])
    def _():
        row0 = row0_ref[t]
        nunit = nunit_ref[t]

        @pl.when(f == 0)
        def _():
            def issue_group(r8, c):
                base = pl.multiple_of(r8 * 8, 8)
                for k in range(8):
                    _moe_row_copy(h_hbm, x_scr, gsem, slot_tok[row0 + base + k],
                                  base + k).start(priority=k % 2)
                return c
            lax.fori_loop(0, nunit * (MOE_UNIT // 8), issue_group, 0)

            @pl.when(t > 0)
            def _():
                def finish(u, c):
                    _moe_out_copy(acc_scr, y_hbm, osem, row0_ref[t - 1], u).wait()
                    return c
                lax.fori_loop(0, nunit_ref[t - 1], finish, 0)

            def clear_acc(u, c):
                off = pl.multiple_of(u * MOE_UNIT, MOE_UNIT)
                acc_scr[pl.ds(off, MOE_UNIT), :] = jnp.zeros((MOE_UNIT, D), F32)
                return c
            lax.fori_loop(0, nunit, clear_acc, 0)

            def drain_group(r8, c):
                base = pl.multiple_of(r8 * 8, 8)
                for k in range(8):
                    _moe_row_copy(h_hbm, x_scr, gsem, 0, base + k).wait()
                return c
            lax.fori_loop(0, nunit * (MOE_UNIT // 8), drain_group, 0)

        def chunk(off, rows):
            x = x_scr[pl.ds(off, rows), :]
            hid = _silu(_dot(x, wg_ref[...])) * _dot(x, wu_ref[...])
            acc_scr[pl.ds(off, rows), :] += _dot(hid, wd_ref[...])

        def oct_(c, carry):
            chunk(pl.multiple_of(c * 8 * MOE_UNIT, 8 * MOE_UNIT), 8 * MOE_UNIT)
            return carry
        lax.fori_loop(0, nunit // 8, oct_, 0)

        @pl.when(nunit % 8 >= 4)
        def _():
            chunk(pl.multiple_of((nunit // 8) * 8 * MOE_UNIT, 4 * MOE_UNIT), 4 * MOE_UNIT)

        @pl.when(nunit % 4 >= 2)
        def _():
            chunk(pl.multiple_of((nunit // 4) * 4 * MOE_UNIT, 2 * MOE_UNIT), 2 * MOE_UNIT)

        @pl.when(nunit % 2 == 1)
        def _():
            chunk(pl.multiple_of((nunit - 1) * MOE_UNIT, MOE_UNIT), MOE_UNIT)

        @pl.when(f == nf - 1)
        def _():
            def start(u, carry):
                _moe_out_copy(acc_scr, y_hbm, osem, row0, u).start()
                return carry
            lax.fori_loop(0, nunit, start, 0)

            @pl.when(t == used_ref[0] - 1)
            def _():
                def finish(u, carry):
                    _moe_out_copy(acc_scr, y_hbm, osem, row0, u).wait()
                    return carry
                lax.fori_loop(0, nunit, finish, 0)

    @pl.when(jnp.logical_and(t == pl.num_programs(0) - 1, f == nf - 1))
    def _():
        acc_scr[0:MOE_UNIT, :] = jnp.zeros((MOE_UNIT, D), F32)

        def fill(u, carry):
            cp = _moe_out_copy(acc_scr, y_hbm, osem, u * MOE_UNIT, 0)
            cp.start()
            cp.wait()
            return carry
        lax.fori_loop(used_ref[1], MOE_CAP // MOE_UNIT, fill, 0)


def _moe_experts(h, wg, wu, wd, tile_e, tile_row0, tile_nunit, used, pad, dest):
    nf = FE // MOE_TF

    def fidx(t, f, us):
        return jnp.where(t < us[0], f, nf - 1)

    grid_spec = pltpu.PrefetchScalarGridSpec(
        num_scalar_prefetch=6,
        grid=(MOE_NT, nf),
        in_specs=[pl.BlockSpec(memory_space=pl.ANY),
                  pl.BlockSpec((None, D, MOE_TF), lambda t, f, te, r0, nu, us, pa, de: (te[t], 0, fidx(t, f, us))),
                  pl.BlockSpec((None, D, MOE_TF), lambda t, f, te, r0, nu, us, pa, de: (te[t], 0, fidx(t, f, us))),
                  pl.BlockSpec((None, MOE_TF, D), lambda t, f, te, r0, nu, us, pa, de: (te[t], fidx(t, f, us), 0))],
        out_specs=pl.BlockSpec(memory_space=pl.ANY),
        scratch_shapes=[pltpu.SMEM((MOE_CAP,), jnp.int32),
                        pltpu.VMEM((MOE_TMAX, D), F32),
                        pltpu.VMEM((MOE_TMAX, D), F32),
                        pltpu.SemaphoreType.DMA,
                        pltpu.SemaphoreType.DMA],
    )
    return pl.pallas_call(
        _moe_kernel,
        grid_spec=grid_spec,
        out_shape=jax.ShapeDtypeStruct((MOE_CAP, D), F32),
        compiler_params=_params(("arbitrary", "arbitrary"), 58),
        name="moe_experts",
    )(tile_e, tile_row0, tile_nunit, used, pad, dest, h, wg, wu, wd)


def _cmb_copy(y_hbm, ybuf, sem, buf, slot, k, r):
    return pltpu.make_async_copy(y_hbm.at[pl.ds(slot, 1), :], ybuf.at[buf, k, pl.ds(r, 1), :],
                                 sem.at[buf])


def _combine_kernel(dest_ref, h_ref, info_ref, y_hbm, g_ref, b_ref, o_ref, ybuf, sem):
    tm = h_ref.shape[0]
    i = pl.program_id(0)
    cur = lax.rem(i, 2)

    def gather(tile, buf):
        def issue_group(r8, c):
            base = pl.multiple_of(r8 * 8, 8)
            for k in range(8):
                tok = tile * tm + base + k
                _cmb_copy(y_hbm, ybuf, sem, buf, dest_ref[2 * tok], 0, base + k).start(priority=0)
                _cmb_copy(y_hbm, ybuf, sem, buf, dest_ref[2 * tok + 1], 1, base + k).start(priority=1)
            return c
        lax.fori_loop(0, tm // 8, issue_group, 0)

    @pl.when(i == 0)
    def _():
        gather(0, 0)

    @pl.when(i + 1 < pl.num_programs(0))
    def _():
        gather(i + 1, 1 - cur)

    def drain_group(r8, c):
        base = pl.multiple_of(r8 * 8, 8)
        for k in range(8):
            _cmb_copy(y_hbm, ybuf, sem, cur, 0, 0, base + k).wait()
            _cmb_copy(y_hbm, ybuf, sem, cur, 0, 1, base + k).wait()
        return c
    lax.fori_loop(0, tm // 8, drain_group, 0)
    info = info_ref[...]
    moe = info[:, 0:1] * ybuf[cur, 0] + info[:, 1:2] * ybuf[cur, 1]
    o_ref[...] = _layer_norm(ALPHA * h_ref[...] + moe, g_ref[...], b_ref[...])


def _combine(dest, h, info, y_rows, g, b):
    tm = CMB_TM
    grid_spec = pltpu.PrefetchScalarGridSpec(
        num_scalar_prefetch=1,
        grid=(T // tm,),
        in_specs=[pl.BlockSpec((tm, D), lambda i, de: (i, 0)),
                  pl.BlockSpec((tm, LANES_V7X), lambda i, de: (i, 0)),
                  pl.BlockSpec(memory_space=pl.ANY),
                  pl.BlockSpec((1, D), lambda i, de: (0, 0)),
                  pl.BlockSpec((1, D), lambda i, de: (0, 0))],
        out_specs=pl.BlockSpec((tm, D), lambda i, de: (i, 0)),
        scratch_shapes=[pltpu.VMEM((2, 2, tm, D), F32), pltpu.SemaphoreType.DMA((2,))],
    )
    return pl.pallas_call(
        _combine_kernel,
        grid_spec=grid_spec,
        out_shape=jax.ShapeDtypeStruct((T, D), F32),
        compiler_params=_params(("arbitrary",), 32),
        name="moe_combine_ln",
    )(dest, h, info, y_rows, g, b)


def _rope_tables(pos):
    inv_freq = THETA ** (-jnp.arange(0, ROPE, 2, dtype=F32) / ROPE)
    ang = pos[:, None] * inv_freq[None, :]
    cos = jnp.cos(ang).astype(F32)
    sin = jnp.sin(ang).astype(F32)
    zero = jnp.zeros((pos.shape[0], LANES_V7X - ROPE), F32)
    return (jnp.concatenate([cos, cos, zero], axis=1),
            jnp.concatenate([-sin, sin, zero], axis=1))


def _expert_tiles(counts):
    units = (counts + MOE_UNIT - 1) // MOE_UNIT
    seg_unit = jnp.cumsum(units) - units
    ntile = (units + MOE_TUNITS - 1) // MOE_TUNITS
    tile_end = jnp.cumsum(ntile)
    n_used = tile_end[-1]
    tid = jnp.arange(MOE_NT, dtype=jnp.int32)
    e = jnp.minimum(jnp.sum(tid[:, None] >= tile_end[None, :], axis=1), NE - 1).astype(jnp.int32)
    local = tid - (tile_end - ntile)[e]
    nt = jnp.maximum(ntile[e], 1)
    lo = (local * units[e]) // nt
    hi = ((local + 1) * units[e]) // nt
    used = tid < n_used
    last = jnp.maximum(n_used - 1, 0)
    e = jnp.where(used, e, e[last])
    row0 = jnp.where(used, (seg_unit[e] + lo) * MOE_UNIT, 0)
    nunit = jnp.where(used, hi - lo, 0)
    used_info = jnp.stack([n_used, jnp.sum(units)]).astype(jnp.int32)
    seg_row = seg_unit * MOE_UNIT
    pad = jnp.stack([seg_row + counts, seg_row + units * MOE_UNIT], axis=1).reshape(2 * NE)
    return (e.astype(jnp.int32), row0.astype(jnp.int32), nunit.astype(jnp.int32),
            used_info, pad.astype(jnp.int32), seg_row.astype(jnp.int32))


def kernel(x, meta_tokens, attn_w_in, fox_forget_bias, mla_q_norm, mla_kv_norm, mla_w_uq, mla_w_ukv, attn_w_out, pool_w_in, pool_w_group, pool_scale, pool_w_out, ffn_w_gate, ffn_w_up, ffn_w_down, moe_w_router, moe_b_router, moe_w_gate, moe_w_up, moe_w_down, ln_mix_g, ln_mix_b, ln_ffn_g, ln_ffn_b):
    assert x.shape == (1, T, D) and meta_tokens.shape == (NM, D)
    xt = x[0]
    meta = meta_tokens.astype(x.dtype)

    w_in = attn_w_in[0]
    o_kr, o_qf, o_f = QL + KVL, QL + KVL + ROPE, QL + KVL + ROPE + 3 * H * HD
    kr1, kr2 = w_in[:, o_kr:o_kr + ROPE // 2], w_in[:, o_kr + ROPE // 2:o_qf]
    zpad = lambda n: jnp.zeros((D, n), F32)
    w_small = jnp.concatenate([kr2, kr1, zpad(LANES_V7X - ROPE),
                               w_in[:, o_f:o_f + H], zpad(LANES_V7X - H)], axis=1)
    w_fox = w_in[:, o_qf:o_f].astype(BF16)
    w_in = w_in[:, :LAT_COLS]
    wuq = mla_w_uq[0].reshape(QL, H, NOPE + ROPE)
    uq_n, uq_1, uq_2 = wuq[..., :NOPE], wuq[..., NOPE:NOPE + ROPE // 2], wuq[..., NOPE + ROPE // 2:]
    zq = jnp.zeros((QL, H, LANES_V7X - ROPE), F32)
    w_uq = jnp.concatenate([uq_n, uq_1, uq_2, zq], axis=-1).reshape(QL, H * DQ_MLA).astype(BF16)
    w_uqs = jnp.concatenate([uq_2, uq_1, zq], axis=-1).reshape(QL, H * HD).astype(BF16)
    wukv = mla_w_ukv[0].reshape(KVL, H, NOPE + HD)
    w_ukv = jnp.concatenate([wukv[..., :NOPE].reshape(KVL, H * NOPE),
                             wukv[..., NOPE:].reshape(KVL, H * HD)], axis=1).astype(BF16)
    w_ao = attn_w_out[0]
    w_fg, w_fu, w_fd = ffn_w_gate[0], ffn_w_up[0], ffn_w_down[0]
    w_pi, w_pg, w_po = pool_w_in[0], pool_w_group[0], pool_w_out[0]
    qn, kvn = mla_q_norm[0][None, :], mla_kv_norm[0][None, :]
    row = lambda v: v[None, :]

    ctab_t, stab_t = _rope_tables(jnp.arange(NM, NM + T, dtype=F32))
    ctab_m, stab_m = _rope_tables(jnp.arange(NM, dtype=F32))

    fox_scale = HD ** -0.5 * LOG2E
    q_t, k_t, vt_t, fl_t = _mla_proj(xt, w_in, w_small, qn, kvn, w_uq, w_uqs, w_ukv, ctab_t, stab_t, TK)
    q_m, k_m, v_m, _ = _mla_proj(meta, w_in, w_small, qn, kvn, w_uq, w_uqs, w_ukv, ctab_m, stab_m, NM)
    fox_qk, fox_vt = _fox_proj(xt, w_fox, fox_scale)
    fox_m = _matmul(meta, w_fox, NM, H * HD, BF16, scaled_blocks=1, scale=fox_scale, name="fox_proj_meta")

    fl = fl_t[:, :H].T.reshape(H, T // LANES_V7X, LANES_V7X)
    bias_b = jnp.broadcast_to(fox_forget_bias[0][:, None, None], (H, 1, LANES_V7X))
    parts = _forget_cumsum(fl, bias_b).reshape(H, 6, T)
    pos = parts[:, 0:3].transpose(2, 0, 1)
    neg = parts[:, 3:6].transpose(2, 0, 1)
    lanes = lambda pieces: jnp.concatenate(pieces, axis=2).reshape(-1, H * AUG_STRIDE)
    cst = lambda rows, n, v: jnp.full((rows, H, n), v, BF16)
    aug = (lanes([cst(T, 3, 1.0), pos, cst(T, AUG_STRIDE - 6, 0.0)]),
           lanes([neg, cst(T, AUG_STRIDE - 3, 0.0)]),
           lanes([cst(NM, 3, 0.0), cst(NM, 3, -1.0), cst(NM, AUG_STRIDE - 6, 0.0)]))

    ones_rows = jnp.concatenate([jnp.ones((H, 1, NM), BF16), jnp.zeros((H, VROWS - HD - 1, NM), BF16)], axis=1)
    to_vmt = lambda v: jnp.concatenate([v.reshape(NM, H, HD).transpose(1, 2, 0), ones_rows], axis=1)
    o_mla = _attention(q_t, k_t, vt_t, k_m, to_vmt(v_m), 0, DQ_MLA, name="mla_attn")
    o_fox = _attention(fox_qk, fox_qk, fox_vt, fox_m, to_vmt(fox_m[:, 2 * H * HD:]),
                       H, HD, aug, name="fox_attn")
    om_mla = _meta_attention(q_m, k_m, v_m, 0, 0, 0, DQ_MLA, "mla_attn_meta")
    om_fox = _meta_attention(fox_m, fox_m, fox_m, 0, H, 2 * H, HD, "fox_attn_meta")

    g0, b0, g1, b1 = row(ln_mix_g[0]), row(ln_mix_b[0]), row(ln_ffn_g[0]), row(ln_ffn_b[0])
    h_t = _attn_out(o_mla, o_fox, w_ao, xt, g0, b0, TM)
    h_m = _attn_out(om_mla, om_fox, w_ao, meta, g0, b0, NM)
    h_t, h_m = _ffn(h_t, h_m, w_fg, w_fu, w_fd, g1, b1)

    hp_t = _matmul(h_t, w_pi, TM, D, F32, name="pool_in")
    hp_m = _matmul(h_m, w_pi, NM, D, F32, name="pool_in_meta")
    h_t = _pool_mixer(hp_t, hp_m, h_t, w_pg, row(pool_scale[0]), w_po,
                      row(ln_mix_g[1]), row(ln_mix_b[1]))

    w_r = jnp.concatenate([moe_w_router[0], jnp.zeros((D, LANES_V7X - NE), F32)], axis=1)
    b_r = jnp.concatenate([moe_b_router[0], jnp.full((LANES_V7X - NE,), NEG, F32)])[None, :]
    info, cnt = _router(h_t, w_r, b_r)
    counts = cnt[0, :NE].astype(jnp.int32)
    tile_e, tile_row0, tile_nunit, used, pad, seg_row = _expert_tiles(counts)
    experts = info[:, 2:4].astype(jnp.int32)
    ranks = info[:, 4:6].astype(jnp.int32)
    dest = (seg_row[experts] + ranks).reshape(2 * T)
    y_rows = _moe_experts(h_t, moe_w_gate[0], moe_w_up[0], moe_w_down[0],
                          tile_e, tile_row0, tile_nunit, used, pad, dest)
    out = _combine(dest, h_t, info, y_rows, row(ln_ffn_g[1]), row(ln_ffn_b[1]))
    return out[None]
```
